```python
import math
import jax, jax.numpy as jnp
from jax import lax
import numpy as np

D_MODEL = 1024
BATCH = 16
SEQ = 2048
DEPTH = 4

PLE_DIM = 256
EPS = 1e-6
BLOCK = 128
SB_HEADS = 8
SB_DIM = 64
SB_WIDTH = SB_HEADS * SB_DIM
HG_HEADS = 4
HG_DK = 128
HG_DV = 128
HG_QK = HG_HEADS * HG_DK
HG_V = HG_HEADS * HG_DV
HG_CHUNK = 64
AB_SPLITS = [SB_WIDTH] * 3 + [HG_QK] * 2 + [HG_V] * 2
AB_IN = sum(AB_SPLITS)
AB_OUT = SB_WIDTH + HG_V
SW_HEADS = 16
SW_KV_HEADS = 4
SW_DIM = 64
SW_GROUP = SW_HEADS // SW_KV_HEADS
WINDOW = 128
C_IN = (SW_HEADS + 2 * SW_KV_HEADS) * SW_DIM
C_OUT = SW_HEADS * SW_DIM
N_BUCKETS = 32
MAX_DISTANCE = 128
D_FF = 2816
CONV_W = 3
N_EVEN = (DEPTH + 1) // 2
N_ODD = DEPTH // 2

kernel_name = 'hybrid_sb_hgrn2_swa_convffn'


def rmsnorm(x, g):
    xf = x.astype(jnp.float32)
    y = xf * lax.rsqrt(jnp.mean(xf * xf, axis=-1, keepdims=True) + EPS)
    return (y * g.astype(jnp.float32)).astype(x.dtype)


def stick_breaking_attention(q, k, v):
    S = q.shape[1]
    qf = jnp.swapaxes(q, 1, 2).astype(jnp.float32)
    kf = jnp.swapaxes(k, 1, 2).astype(jnp.float32)
    vf = jnp.swapaxes(v, 1, 2).astype(jnp.float32)
    scale = SB_DIM ** -0.5
    outs = []
    for n in range(S // BLOCK):
        t0 = n * BLOCK
        kn = t0 + BLOCK
        z = jnp.einsum('bhtd,bhsd->bhts', qf[:, :, t0:kn], kf[:, :, :kn]) * scale
        t_pos = t0 + jnp.arange(BLOCK)[:, None]
        s_pos = jnp.arange(kn)[None, :]
        mask = s_pos < t_pos
        log_keep = jnp.where(mask, jax.nn.log_sigmoid(-z), 0.0)
        later = lax.cumsum(log_keep, axis=3, reverse=True) - log_keep
        w = jnp.where(mask, jnp.exp(jax.nn.log_sigmoid(z) + later), 0.0)
        outs.append(jnp.einsum('bhts,bhsd->bhtd', w, vf[:, :, :kn]))
    o = jnp.concatenate(outs, axis=2)
    return jnp.swapaxes(o, 1, 2)


def hgrn2(q, f_pre, i, lb):
    B, S = q.shape[:2]
    lb = lb.reshape(HG_HEADS, HG_DK).astype(jnp.float32)
    fp = f_pre.astype(jnp.float32)
    log_f = jnp.log(lb + (1.0 - lb) * jax.nn.sigmoid(fp))
    kk = (1.0 - lb) * jax.nn.sigmoid(-fp)
    qf = jax.nn.silu(q.astype(jnp.float32))
    nc = S // HG_CHUNK

    def to_chunks(a):
        return a.reshape(B, nc, HG_CHUNK, HG_HEADS, a.shape[-1]).transpose(1, 0, 3, 2, 4)

    causal = jnp.tril(jnp.ones((HG_CHUNK, HG_CHUNK), dtype=bool))

    def step(state, xs):
        qc, kc, lfc, ic = xs
        b = jnp.cumsum(lfc, axis=2)
        o_inter = jnp.einsum('bhtk,bhkv->bhtv', qc * jnp.exp(b), state)
        rel = jnp.where(causal[:, :, None], b[:, :, :, None, :] - b[:, :, None, :, :], -jnp.inf)
        scores = jnp.einsum('bhtk,bhsk,bhtsk->bhts', qc, kc, jnp.exp(rel))
        o = o_inter + jnp.einsum('bhts,bhsv->bhtv', scores, ic)
        b_last = b[:, :, -1:, :]
        state = jnp.exp(b_last[:, :, 0, :, None]) * state + jnp.einsum('bhsk,bhsv->bhkv', kc * jnp.exp(b_last - b), ic)
        return state, o

    s0 = jnp.zeros((B, HG_HEADS, HG_DK, HG_DV), jnp.float32)
    _, o = lax.scan(step, s0, (to_chunks(qf), to_chunks(kk), to_chunks(log_f), to_chunks(i.astype(jnp.float32))))
    return o.transpose(1, 0, 3, 2, 4).reshape(B, S, HG_HEADS, HG_DV)


def t5_band_buckets():
    t = np.arange(WINDOW)[:, None]
    s = np.arange(2 * WINDOW)[None, :]
    dist = t + WINDOW - s
    band = (dist >= 0) & (dist < WINDOW)
    max_exact = N_BUCKETS // 2
    large = max_exact + (np.log(np.maximum(dist, max_exact) / max_exact) / math.log(MAX_DISTANCE / max_exact) * (N_BUCKETS - max_exact)).astype(np.int32)
    large = np.minimum(large, N_BUCKETS - 1)
    bucket = np.where(dist < max_exact, np.maximum(dist, 0), large).astype(np.int32)
    return bucket, band


def sliding_window_attention(q, k, v, sinks, rel_bias):
    B, S = q.shape[:2]
    nb = S // WINDOW
    bucket, band = t5_band_buckets()
    bias = rel_bias.astype(jnp.float32)[bucket]
    bias = bias.transpose(2, 0, 1).reshape(SW_KV_HEADS, SW_GROUP, WINDOW, 2 * WINDOW)
    key_pos = np.arange(nb)[:, None] * WINDOW - WINDOW + np.arange(2 * WINDOW)[None, :]
    mask = jnp.asarray(band[None] & (key_pos >= 0)[:, None, :])
    qb = q.astype(jnp.float32).reshape(B, nb, WINDOW, SW_KV_HEADS, SW_GROUP, SW_DIM).transpose(1, 0, 2, 3, 4, 5)

    def band_keys(a):
        ap = jnp.pad(a.astype(jnp.float32), ((0, 0), (WINDOW, 0), (0, 0), (0, 0)))
        ap = ap.reshape(B, nb + 1, WINDOW, SW_KV_HEADS, SW_DIM)
        return jnp.concatenate([ap[:, :-1], ap[:, 1:]], axis=2).transpose(1, 0, 2, 3, 4)

    kb, vb = band_keys(k), band_keys(v)
    sink = sinks.astype(jnp.float32).reshape(SW_KV_HEADS, SW_GROUP, 1, 1)
    scale = SW_DIM ** -0.5

    def block_attn(args):
        qn, kn, vn, mn = args
        logits = jnp.einsum('bqhgd,bkhd->bhgqk', qn, kn) * scale + bias
        logits = jnp.where(mn, logits, -jnp.inf)
        m = jnp.maximum(jnp.max(logits, axis=-1, keepdims=True), sink)
        e = jnp.exp(logits - m)
        w = e / (jnp.sum(e, axis=-1, keepdims=True) + jnp.exp(sink - m))
        return jnp.einsum('bhgqk,bkhd->bqhgd', w, vn)

    o = lax.map(block_attn, (qb, kb, vb, mask))
    return o.transpose(1, 0, 2, 3, 4, 5).reshape(B, S, SW_HEADS, SW_DIM)


def mixer_ab(h, w_in, lb, hg_norm, w_out):
    B, S, _ = h.shape
    proj = h @ w_in
    qa, ka, va, qb, fb, ib, gb = jnp.split(proj, np.cumsum(AB_SPLITS)[:-1].tolist(), axis=-1)
    sb_shape = (B, S, SB_HEADS, SB_DIM)
    o_a = stick_breaking_attention(qa.reshape(sb_shape), ka.reshape(sb_shape), va.reshape(sb_shape))
    o_a = o_a.astype(h.dtype).reshape(B, S, SB_WIDTH)
    o_b = hgrn2(qb.reshape(B, S, HG_HEADS, HG_DK), fb.reshape(B, S, HG_HEADS, HG_DK), ib.reshape(B, S, HG_HEADS, HG_DV), lb)
    o_b = rmsnorm(o_b.astype(h.dtype), hg_norm) * jax.nn.silu(gb.reshape(B, S, HG_HEADS, HG_DV))
    o_b = o_b.reshape(B, S, HG_V)
    return jnp.concatenate([o_a, o_b], axis=-1) @ w_out


def mixer_c(h, w_in, q_norm, k_norm, sinks, rel_bias, w_out):
    B, S, _ = h.shape
    proj = h @ w_in
    q, k, v = jnp.split(proj, [SW_HEADS * SW_DIM, (SW_HEADS + SW_KV_HEADS) * SW_DIM], axis=-1)
    q = rmsnorm(q.reshape(B, S, SW_HEADS, SW_DIM), q_norm)
    k = rmsnorm(k.reshape(B, S, SW_KV_HEADS, SW_DIM), k_norm)
    v = v.reshape(B, S, SW_KV_HEADS, SW_DIM)
    o = sliding_window_attention(q, k, v, sinks, rel_bias).astype(h.dtype)
    return o.reshape(B, S, C_OUT) @ w_out


def conv_glu_ffn(h, w_up, conv_w, conv_b, w_down):
    u = h @ w_up
    u = lax.conv_general_dilated(u, conv_w[:, None, :], window_strides=(1,), padding=[(CONV_W - 1, 0)],
                                 dimension_numbers=('NWC', 'WIO', 'NWC'), feature_group_count=2 * D_FF) + conv_b
    gate, up = jnp.split(u, 2, axis=-1)
    return (jax.nn.silu(gate) * up) @ w_down


def _fwd_setup_inputs(seed: int = 0) -> dict:
    key = jax.random.key(seed)
    ks = jax.random.split(key, 21)

    def nrm(k, shape):
        return jax.random.normal(k, shape, jnp.float32)

    def w(k, shape, fan_in):
        return nrm(k, shape) * fan_in ** -0.5

    def gain(k, shape):
        return 1.0 + 0.02 * nrm(k, shape)

    F2 = 2 * D_FF
    return {
        'x': nrm(ks[0], (BATCH, SEQ, D_MODEL)),
        'p': nrm(ks[1], (DEPTH, BATCH, SEQ, PLE_DIM)),
        'mix_norm': gain(ks[2], (DEPTH, D_MODEL)),
        'ab_w_in': w(ks[3], (N_EVEN, D_MODEL, AB_IN), D_MODEL),
        'hg_lb_logits': 0.5 * nrm(ks[4], (N_EVEN, HG_QK)),
        'hg_out_norm': gain(ks[5], (N_EVEN, HG_DV)),
        'ab_w_out': w(ks[6], (N_EVEN, AB_OUT, D_MODEL), AB_OUT),
        'c_w_in': w(ks[7], (N_ODD, D_MODEL, C_IN), D_MODEL),
        'q_norm': gain(ks[8], (N_ODD, SW_DIM)),
        'k_norm': gain(ks[9], (N_ODD, SW_DIM)),
        'sinks': 0.5 * nrm(ks[10], (N_ODD, SW_HEADS)),
        'rel_bias': 0.5 * nrm(ks[11], (N_BUCKETS, SW_HEADS)),
        'c_w_out': w(ks[12], (N_ODD, C_OUT, D_MODEL), C_OUT),
        'ffn_norm': gain(ks[13], (DEPTH, D_MODEL)),
        'ffn_up': w(ks[14], (DEPTH, D_MODEL, F2), D_MODEL),
        'ffn_conv': w(ks[15], (DEPTH, CONV_W, F2), CONV_W),
        'ffn_conv_b': 0.02 * nrm(ks[16], (DEPTH, F2)),
        'ffn_down': w(ks[17], (DEPTH, D_FF, D_MODEL), D_FF),
        'ple_norm': gain(ks[18], (DEPTH, D_MODEL)),
        'ple_gate': w(ks[19], (DEPTH, D_MODEL, D_MODEL), D_MODEL),
        'ple_proj': w(ks[20], (DEPTH, PLE_DIM, D_MODEL), PLE_DIM),
    }


def _fwd_reference(x, p, mix_norm, ab_w_in, hg_lb_logits, hg_out_norm, ab_w_out, c_w_in, q_norm, k_norm,
              sinks, rel_bias, c_w_out, ffn_norm, ffn_up, ffn_conv, ffn_conv_b, ffn_down,
              ple_norm, ple_gate, ple_proj):
    lb_cum = jnp.cumsum(jax.nn.softmax(hg_lb_logits.astype(jnp.float32), axis=0), axis=0)
    lower_bounds = lb_cum - lb_cum[0]
    h = x
    for i in range(DEPTH):
        j = i // 2
        hn = rmsnorm(h, mix_norm[i])
        if i % 2 == 0:
            h = h + mixer_ab(hn, ab_w_in[j], lower_bounds[j], hg_out_norm[j], ab_w_out[j])
        else:
            h = h + mixer_c(hn, c_w_in[j], q_norm[j], k_norm[j], sinks[j], rel_bias, c_w_out[j])
        h = h + conv_glu_ffn(rmsnorm(h, ffn_norm[i]), ffn_up[i], ffn_conv[i], ffn_conv_b[i], ffn_down[i])
        gate = jax.nn.sigmoid(rmsnorm(h, ple_norm[i]) @ ple_gate[i])
        h = h + gate * (p[i] @ ple_proj[i])
    return h


import jax as _jax
import jax.numpy as _jnp

TWIN_FORMAT = 'train_step'
FWD_PARAMS = ['x', 'p', 'mix_norm', 'ab_w_in', 'hg_lb_logits', 'hg_out_norm', 'ab_w_out', 'c_w_in', 'q_norm', 'k_norm', 'sinks', 'rel_bias', 'c_w_out', 'ffn_norm', 'ffn_up', 'ffn_conv', 'ffn_conv_b', 'ffn_down', 'ple_norm', 'ple_gate', 'ple_proj']
TWIN_WEIGHTS = ['mix_norm', 'ab_w_in', 'hg_lb_logits', 'hg_out_norm', 'ab_w_out', 'c_w_in', 'q_norm', 'k_norm', 'sinks', 'rel_bias', 'c_w_out', 'ffn_norm', 'ffn_up', 'ffn_conv', 'ffn_conv_b', 'ffn_down', 'ple_norm', 'ple_gate', 'ple_proj']
TWIN_DIFF_INPUT = 'x'
TWIN_INPUTS = ['x', 'p', 'mix_norm', 'ab_w_in', 'hg_lb_logits', 'hg_out_norm', 'ab_w_out', 'c_w_in', 'q_norm', 'k_norm', 'sinks', 'rel_bias', 'c_w_out', 'ffn_norm', 'ffn_up', 'ffn_conv', 'ffn_conv_b', 'ffn_down', 'ple_norm', 'ple_gate', 'ple_proj', 'loss_target', 'm_mix_norm', 'm_ab_w_in', 'm_hg_lb_logits', 'm_hg_out_norm', 'm_ab_w_out', 'm_c_w_in', 'm_q_norm', 'm_k_norm', 'm_sinks', 'm_rel_bias', 'm_c_w_out', 'm_ffn_norm', 'm_ffn_up', 'm_ffn_conv', 'm_ffn_conv_b', 'm_ffn_down', 'm_ple_norm', 'm_ple_gate', 'm_ple_proj', 'v_mix_norm', 'v_ab_w_in', 'v_hg_lb_logits', 'v_hg_out_norm', 'v_ab_w_out', 'v_c_w_in', 'v_q_norm', 'v_k_norm', 'v_sinks', 'v_rel_bias', 'v_c_w_out', 'v_ffn_norm', 'v_ffn_up', 'v_ffn_conv', 'v_ffn_conv_b', 'v_ffn_down', 'v_ple_norm', 'v_ple_gate', 'v_ple_proj']
TWIN_OUTPUTS = ['loss', 'grad_x', 'grad_mix_norm', 'grad_ab_w_in', 'grad_hg_lb_logits', 'grad_hg_out_norm', 'grad_ab_w_out', 'grad_c_w_in', 'grad_q_norm', 'grad_k_norm', 'grad_sinks', 'grad_rel_bias', 'grad_c_w_out', 'grad_ffn_norm', 'grad_ffn_up', 'grad_ffn_conv', 'grad_ffn_conv_b', 'grad_ffn_down', 'grad_ple_norm', 'grad_ple_gate', 'grad_ple_proj', 'delta_mix_norm', 'delta_ab_w_in', 'delta_hg_lb_logits', 'delta_hg_out_norm', 'delta_ab_w_out', 'delta_c_w_in', 'delta_q_norm', 'delta_k_norm', 'delta_sinks', 'delta_rel_bias', 'delta_c_w_out', 'delta_ffn_norm', 'delta_ffn_up', 'delta_ffn_conv', 'delta_ffn_conv_b', 'delta_ffn_down', 'delta_ple_norm', 'delta_ple_gate', 'delta_ple_proj', 'new_m_mix_norm', 'new_m_ab_w_in', 'new_m_hg_lb_logits', 'new_m_hg_out_norm', 'new_m_ab_w_out', 'new_m_c_w_in', 'new_m_q_norm', 'new_m_k_norm', 'new_m_sinks', 'new_m_rel_bias', 'new_m_c_w_out', 'new_m_ffn_norm', 'new_m_ffn_up', 'new_m_ffn_conv', 'new_m_ffn_conv_b', 'new_m_ffn_down', 'new_m_ple_norm', 'new_m_ple_gate', 'new_m_ple_proj', 'new_v_mix_norm', 'new_v_ab_w_in', 'new_v_hg_lb_logits', 'new_v_hg_out_norm', 'new_v_ab_w_out', 'new_v_c_w_in', 'new_v_q_norm', 'new_v_k_norm', 'new_v_sinks', 'new_v_rel_bias', 'new_v_c_w_out', 'new_v_ffn_norm', 'new_v_ffn_up', 'new_v_ffn_conv', 'new_v_ffn_conv_b', 'new_v_ffn_down', 'new_v_ple_norm', 'new_v_ple_gate', 'new_v_ple_proj']
TWIN_LEAF_KINDS = {'loss': 'loss', 'grad_x': 'grad_x', 'grad_mix_norm': 'grad_w', 'grad_ab_w_in': 'grad_w', 'grad_hg_lb_logits': 'grad_w', 'grad_hg_out_norm': 'grad_w', 'grad_ab_w_out': 'grad_w', 'grad_c_w_in': 'grad_w', 'grad_q_norm': 'grad_w', 'grad_k_norm': 'grad_w', 'grad_sinks': 'grad_w', 'grad_rel_bias': 'grad_w', 'grad_c_w_out': 'grad_w', 'grad_ffn_norm': 'grad_w', 'grad_ffn_up': 'grad_w', 'grad_ffn_conv': 'grad_w', 'grad_ffn_conv_b': 'grad_w', 'grad_ffn_down': 'grad_w', 'grad_ple_norm': 'grad_w', 'grad_ple_gate': 'grad_w', 'grad_ple_proj': 'grad_w', 'delta_mix_norm': 'delta_w', 'delta_ab_w_in': 'delta_w', 'delta_hg_lb_logits': 'delta_w', 'delta_hg_out_norm': 'delta_w', 'delta_ab_w_out': 'delta_w', 'delta_c_w_in': 'delta_w', 'delta_q_norm': 'delta_w', 'delta_k_norm': 'delta_w', 'delta_sinks': 'delta_w', 'delta_rel_bias': 'delta_w', 'delta_c_w_out': 'delta_w', 'delta_ffn_norm': 'delta_w', 'delta_ffn_up': 'delta_w', 'delta_ffn_conv': 'delta_w', 'delta_ffn_conv_b': 'delta_w', 'delta_ffn_down': 'delta_w', 'delta_ple_norm': 'delta_w', 'delta_ple_gate': 'delta_w', 'delta_ple_proj': 'delta_w', 'new_m_mix_norm': 'new_m', 'new_m_ab_w_in': 'new_m', 'new_m_hg_lb_logits': 'new_m', 'new_m_hg_out_norm': 'new_m', 'new_m_ab_w_out': 'new_m', 'new_m_c_w_in': 'new_m', 'new_m_q_norm': 'new_m', 'new_m_k_norm': 'new_m', 'new_m_sinks': 'new_m', 'new_m_rel_bias': 'new_m', 'new_m_c_w_out': 'new_m', 'new_m_ffn_norm': 'new_m', 'new_m_ffn_up': 'new_m', 'new_m_ffn_conv': 'new_m', 'new_m_ffn_conv_b': 'new_m', 'new_m_ffn_down': 'new_m', 'new_m_ple_norm': 'new_m', 'new_m_ple_gate': 'new_m', 'new_m_ple_proj': 'new_m', 'new_v_mix_norm': 'new_v', 'new_v_ab_w_in': 'new_v', 'new_v_hg_lb_logits': 'new_v', 'new_v_hg_out_norm': 'new_v', 'new_v_ab_w_out': 'new_v', 'new_v_c_w_in': 'new_v', 'new_v_q_norm': 'new_v', 'new_v_k_norm': 'new_v', 'new_v_sinks': 'new_v', 'new_v_rel_bias': 'new_v', 'new_v_c_w_out': 'new_v', 'new_v_ffn_norm': 'new_v', 'new_v_ffn_up': 'new_v', 'new_v_ffn_conv': 'new_v', 'new_v_ffn_conv_b': 'new_v', 'new_v_ffn_down': 'new_v', 'new_v_ple_norm': 'new_v', 'new_v_ple_gate': 'new_v', 'new_v_ple_proj': 'new_v'}


def _forward(args):
    return _fwd_reference(*[args[k] for k in FWD_PARAMS])


def _output_shape():
    out = _jax.eval_shape(lambda: _forward(_fwd_setup_inputs(0)))
    return out.shape, out.dtype

N_MICROBATCH = 1
ADAM_LR = 0.001
ADAM_B1 = 0.9
ADAM_B2 = 0.999
ADAM_EPS = 1e-08
ADAM_WD = 0.01
ADAM_STEP = 10
PER_EXAMPLE_BATCH_AXIS = {'x': 0, 'p': 1, 'loss_target': 0}
SHARED_INPUTS = []
_WEIGHT_DTYPES = {'mix_norm': _jnp.float32, 'ab_w_in': _jnp.float32, 'hg_lb_logits': _jnp.float32, 'hg_out_norm': _jnp.float32, 'ab_w_out': _jnp.float32, 'c_w_in': _jnp.float32, 'q_norm': _jnp.float32, 'k_norm': _jnp.float32, 'sinks': _jnp.float32, 'rel_bias': _jnp.float32, 'c_w_out': _jnp.float32, 'ffn_norm': _jnp.float32, 'ffn_up': _jnp.float32, 'ffn_conv': _jnp.float32, 'ffn_conv_b': _jnp.float32, 'ffn_down': _jnp.float32, 'ple_norm': _jnp.float32, 'ple_gate': _jnp.float32, 'ple_proj': _jnp.float32}
MOMENT_SCALE = {'mix_norm': 1.070403e+01, 'ab_w_in': 4.226628e-01, 'hg_lb_logits': 3.433341e-02, 'hg_out_norm': 4.557929e+01, 'ab_w_out': 6.107042e-01, 'c_w_in': 2.395350e-01, 'q_norm': 7.295522e+00, 'k_norm': 7.294055e+00, 'sinks': 8.717667e-01, 'rel_bias': 1.747635e+00, 'c_w_out': 2.082920e-01, 'ffn_norm': 2.529193e+01, 'ffn_up': 3.042142e-01, 'ffn_conv': 3.388825e+00, 'ffn_conv_b': 3.106548e+00, 'ffn_down': 4.543616e-01, 'ple_norm': 8.454502e-01, 'ple_gate': 1.339735e-01, 'ple_proj': 4.649833e-01}


def _to_microbatches(a, axis):
    t = _jnp.moveaxis(a, axis, 0)
    t = t.reshape((N_MICROBATCH, t.shape[0] // N_MICROBATCH) + t.shape[1:])
    return _jnp.moveaxis(t, 1, axis + 1)


def setup_inputs(seed: int = 0) -> dict:
    inp = _fwd_setup_inputs(seed)
    key = _jax.random.fold_in(_jax.random.key(seed), 7919)
    shape, _ = _output_shape()
    out = dict(inp)
    out["loss_target"] = _jax.random.normal(_jax.random.fold_in(key, 0), shape, _jnp.float32)
    for i, name in enumerate(TWIN_WEIGHTS):
        w = inp[name].astype(_jnp.float32)
        if MOMENT_SCALE is None:
            s = _jnp.sqrt(_jnp.mean(_jnp.square(w)) + 1e-30)
        else:
            s = MOMENT_SCALE[name]
        km, kv = _jax.random.split(_jax.random.fold_in(key, i + 1))
        out[name] = w
        out["m_" + name] = s * _jax.random.normal(km, w.shape, _jnp.float32)
        out["v_" + name] = (s * s) * _jax.random.uniform(kv, w.shape, _jnp.float32, 0.5, 1.5)
    if N_MICROBATCH > 1:
        for name, axis in PER_EXAMPLE_BATCH_AXIS.items():
            out[name] = _to_microbatches(out[name], axis)
    return {'x': out['x'], 'p': out['p'], 'mix_norm': out['mix_norm'], 'ab_w_in': out['ab_w_in'], 'hg_lb_logits': out['hg_lb_logits'], 'hg_out_norm': out['hg_out_norm'], 'ab_w_out': out['ab_w_out'], 'c_w_in': out['c_w_in'], 'q_norm': out['q_norm'], 'k_norm': out['k_norm'], 'sinks': out['sinks'], 'rel_bias': out['rel_bias'], 'c_w_out': out['c_w_out'], 'ffn_norm': out['ffn_norm'], 'ffn_up': out['ffn_up'], 'ffn_conv': out['ffn_conv'], 'ffn_conv_b': out['ffn_conv_b'], 'ffn_down': out['ffn_down'], 'ple_norm': out['ple_norm'], 'ple_gate': out['ple_gate'], 'ple_proj': out['ple_proj'], 'loss_target': out['loss_target'], 'm_mix_norm': out['m_mix_norm'], 'm_ab_w_in': out['m_ab_w_in'], 'm_hg_lb_logits': out['m_hg_lb_logits'], 'm_hg_out_norm': out['m_hg_out_norm'], 'm_ab_w_out': out['m_ab_w_out'], 'm_c_w_in': out['m_c_w_in'], 'm_q_norm': out['m_q_norm'], 'm_k_norm': out['m_k_norm'], 'm_sinks': out['m_sinks'], 'm_rel_bias': out['m_rel_bias'], 'm_c_w_out': out['m_c_w_out'], 'm_ffn_norm': out['m_ffn_norm'], 'm_ffn_up': out['m_ffn_up'], 'm_ffn_conv': out['m_ffn_conv'], 'm_ffn_conv_b': out['m_ffn_conv_b'], 'm_ffn_down': out['m_ffn_down'], 'm_ple_norm': out['m_ple_norm'], 'm_ple_gate': out['m_ple_gate'], 'm_ple_proj': out['m_ple_proj'], 'v_mix_norm': out['v_mix_norm'], 'v_ab_w_in': out['v_ab_w_in'], 'v_hg_lb_logits': out['v_hg_lb_logits'], 'v_hg_out_norm': out['v_hg_out_norm'], 'v_ab_w_out': out['v_ab_w_out'], 'v_c_w_in': out['v_c_w_in'], 'v_q_norm': out['v_q_norm'], 'v_k_norm': out['v_k_norm'], 'v_sinks': out['v_sinks'], 'v_rel_bias': out['v_rel_bias'], 'v_c_w_out': out['v_c_w_out'], 'v_ffn_norm': out['v_ffn_norm'], 'v_ffn_up': out['v_ffn_up'], 'v_ffn_conv': out['v_ffn_conv'], 'v_ffn_conv_b': out['v_ffn_conv_b'], 'v_ffn_down': out['v_ffn_down'], 'v_ple_norm': out['v_ple_norm'], 'v_ple_gate': out['v_ple_gate'], 'v_ple_proj': out['v_ple_proj']}


def _loss(weights, diff, rest, loss_target):
    with _jax.named_scope("forward"):
        args = {**rest, TWIN_DIFF_INPUT: diff, **{k: w.astype(_WEIGHT_DTYPES[k]) for k, w in weights.items()}}
        y = _forward(args)
    with _jax.named_scope("loss_head"):
        err = _jnp.square(y.astype(_jnp.float32) - loss_target)
        return 0.5 * _jnp.sum(_jnp.mean(err, axis=-1)) if err.ndim else 0.5 * err


def _adamw(w, g, m, v):
    m = ADAM_B1 * m + (1.0 - ADAM_B1) * g
    v = ADAM_B2 * v + (1.0 - ADAM_B2) * _jnp.square(g)
    m_hat = m / (1.0 - ADAM_B1 ** ADAM_STEP)
    v_hat = v / (1.0 - ADAM_B2 ** ADAM_STEP)
    delta = -ADAM_LR * (m_hat / (_jnp.sqrt(v_hat) + ADAM_EPS) + ADAM_WD * w)
    return delta, m, v


def reference(x, p, mix_norm, ab_w_in, hg_lb_logits, hg_out_norm, ab_w_out, c_w_in, q_norm, k_norm, sinks, rel_bias, c_w_out, ffn_norm, ffn_up, ffn_conv, ffn_conv_b, ffn_down, ple_norm, ple_gate, ple_proj, loss_target, m_mix_norm, m_ab_w_in, m_hg_lb_logits, m_hg_out_norm, m_ab_w_out, m_c_w_in, m_q_norm, m_k_norm, m_sinks, m_rel_bias, m_c_w_out, m_ffn_norm, m_ffn_up, m_ffn_conv, m_ffn_conv_b, m_ffn_down, m_ple_norm, m_ple_gate, m_ple_proj, v_mix_norm, v_ab_w_in, v_hg_lb_logits, v_hg_out_norm, v_ab_w_out, v_c_w_in, v_q_norm, v_k_norm, v_sinks, v_rel_bias, v_c_w_out, v_ffn_norm, v_ffn_up, v_ffn_conv, v_ffn_conv_b, v_ffn_down, v_ple_norm, v_ple_gate, v_ple_proj):
    given = dict(x=x, p=p, mix_norm=mix_norm, ab_w_in=ab_w_in, hg_lb_logits=hg_lb_logits, hg_out_norm=hg_out_norm, ab_w_out=ab_w_out, c_w_in=c_w_in, q_norm=q_norm, k_norm=k_norm, sinks=sinks, rel_bias=rel_bias, c_w_out=c_w_out, ffn_norm=ffn_norm, ffn_up=ffn_up, ffn_conv=ffn_conv, ffn_conv_b=ffn_conv_b, ffn_down=ffn_down, ple_norm=ple_norm, ple_gate=ple_gate, ple_proj=ple_proj, loss_target=loss_target, m_mix_norm=m_mix_norm, m_ab_w_in=m_ab_w_in, m_hg_lb_logits=m_hg_lb_logits, m_hg_out_norm=m_hg_out_norm, m_ab_w_out=m_ab_w_out, m_c_w_in=m_c_w_in, m_q_norm=m_q_norm, m_k_norm=m_k_norm, m_sinks=m_sinks, m_rel_bias=m_rel_bias, m_c_w_out=m_c_w_out, m_ffn_norm=m_ffn_norm, m_ffn_up=m_ffn_up, m_ffn_conv=m_ffn_conv, m_ffn_conv_b=m_ffn_conv_b, m_ffn_down=m_ffn_down, m_ple_norm=m_ple_norm, m_ple_gate=m_ple_gate, m_ple_proj=m_ple_proj, v_mix_norm=v_mix_norm, v_ab_w_in=v_ab_w_in, v_hg_lb_logits=v_hg_lb_logits, v_hg_out_norm=v_hg_out_norm, v_ab_w_out=v_ab_w_out, v_c_w_in=v_c_w_in, v_q_norm=v_q_norm, v_k_norm=v_k_norm, v_sinks=v_sinks, v_rel_bias=v_rel_bias, v_c_w_out=v_c_w_out, v_ffn_norm=v_ffn_norm, v_ffn_up=v_ffn_up, v_ffn_conv=v_ffn_conv, v_ffn_conv_b=v_ffn_conv_b, v_ffn_down=v_ffn_down, v_ple_norm=v_ple_norm, v_ple_gate=v_ple_gate, v_ple_proj=v_ple_proj)
    weights = {n: given[n] for n in TWIN_WEIGHTS}
    shared = {n: given[n] for n in SHARED_INPUTS}
    per_example = {n: given[n] for n in ['x', 'p']}
    grad_fn = _jax.value_and_grad(_loss, argnums=(0, 1))

    def one_microbatch(ex, loss_target):
        ex = dict(ex)
        diff = ex.pop(TWIN_DIFF_INPUT)
        return grad_fn(weights, diff, {**shared, **ex}, loss_target)

    if N_MICROBATCH == 1:
        loss, (grad_w, grad_x) = one_microbatch(per_example, given["loss_target"])
    else:
        def body(carry, xs):
            loss_sum, grad_sum = carry
            l_k, (gw_k, gx_k) = one_microbatch(xs[0], xs[1])
            with _jax.named_scope("update"):
                return (loss_sum + l_k, _jax.tree.map(_jnp.add, grad_sum, gw_k)), gx_k

        init = (_jnp.zeros((), _jnp.float32), _jax.tree.map(_jnp.zeros_like, weights))
        (loss, grad_w), grad_x = _jax.lax.scan(body, init, (per_example, given["loss_target"]))
    with _jax.named_scope("update"):
        delta_w, new_m, new_v = {}, {}, {}
        for n in TWIN_WEIGHTS:
            delta_w[n], new_m[n], new_v[n] = _adamw(weights[n], grad_w[n], given["m_" + n], given["v_" + n])
    return (loss, grad_x, *[grad_w[n] for n in TWIN_WEIGHTS], *[delta_w[n] for n in TWIN_WEIGHTS],
            *[new_m[n] for n in TWIN_WEIGHTS], *[new_v[n] for n in TWIN_WEIGHTS])
```

```python
import functools
import math

import numpy as np
import jax
import jax.numpy as jnp
from jax import lax
from jax.experimental import pallas as pl
from jax.experimental.pallas import tpu as pltpu

F32 = jnp.float32
BF16 = jnp.bfloat16

D_MODEL = 1024
DEPTH = 4
PLE_DIM = 256
EPS = 1e-6
SB_HEADS, SB_DIM = 8, 64
SB_WIDTH = SB_HEADS * SB_DIM
HG_HEADS, HG_DK, HG_DV = 4, 128, 128
HG_W = HG_HEADS * HG_DK
AB_IN = 3 * SB_WIDTH + 4 * HG_W
SW_HEADS, SW_KV_HEADS, SW_DIM = 16, 4, 64
SW_GROUP = SW_HEADS // SW_KV_HEADS
WINDOW = 128
C_IN = (SW_HEADS + 2 * SW_KV_HEADS) * SW_DIM
N_BUCKETS, MAX_DISTANCE = 32, 128
D_FF = 2816
N_DEV = 8

ADAM_LR, ADAM_B1, ADAM_B2, ADAM_EPS, ADAM_WD, ADAM_STEP = 0.001, 0.9, 0.999, 1e-08, 0.01, 10

LANES = 128
VMEM_LIMIT = 48 * 1024 * 1024

NN = (((1,), (0,)), ((), ()))
NT = (((1,), (1,)), ((), ()))
TN = (((0,), (0,)), ((), ()))


MXU_DTYPE = BF16


def _bf(x):
    return x.astype(MXU_DTYPE)


def _dot(a, b, dims=NN):
    return lax.dot_general(_bf(a), _bf(b), dims, preferred_element_type=F32)


def _split3(x):
    x1 = _bf(x)
    r = x - x1.astype(F32)
    x2 = _bf(r)
    x3 = _bf(r - x2.astype(F32))
    return x1, x2, x3


def _dot_exact_lhs01(m, x, terms=3):
    parts = _split3(x)[:terms]
    out = lax.dot_general(m, parts[0], NN, preferred_element_type=F32)
    for p_ in parts[1:]:
        out = out + lax.dot_general(m, p_, NN, preferred_element_type=F32)
    return out


def _dot_exact_rhs01(x, m, terms=2):
    parts = _split3(x)[:terms]
    out = lax.dot_general(parts[0], m, NN, preferred_element_type=F32)
    for p_ in parts[1:]:
        out = out + lax.dot_general(p_, m, NN, preferred_element_type=F32)
    return out


def _pick(n, target):
    best = None
    for t in range(LANES, target + 1, LANES):
        if n % t == 0:
            best = t
    return best or n


def _params(sem=None):
    return pltpu.CompilerParams(dimension_semantics=sem, vmem_limit_bytes=VMEM_LIMIT)


def _mm(a, b, mode, name, res=None, out_dtype=F32):
    if mode == "nn":
        (M, K), (K2, N) = a.shape, b.shape
    elif mode == "nt":
        (M, K), (N, K2) = a.shape, b.shape
    else:
        (K, M), (K2, N) = a.shape, b.shape
    assert K == K2, (a.shape, b.shape, mode)
    tm, tn, tk = _pick(M, 512), _pick(N, 512), _pick(K, 1024)
    nk = K // tk
    dims = {"nn": NN, "nt": NT, "tn": TN}[mode]
    a_spec = pl.BlockSpec((tk, tm), lambda i, j, k: (k, i)) if mode == "tn" else pl.BlockSpec((tm, tk), lambda i, j, k: (i, k))
    b_spec = pl.BlockSpec((tn, tk), lambda i, j, k: (j, k)) if mode == "nt" else pl.BlockSpec((tk, tn), lambda i, j, k: (k, j))
    o_spec = pl.BlockSpec((tm, tn), lambda i, j, k: (i, j))
    has_res = res is not None

    def body(*refs):
        a_ref, b_ref = refs[0], refs[1]
        r_ref = refs[2] if has_res else None
        o_ref, acc_ref = refs[-2], refs[-1]
        k = pl.program_id(2)

        @pl.when(k == 0)
        def _():
            acc_ref[...] = jnp.zeros_like(acc_ref)

        acc_ref[...] += _dot(a_ref[...], b_ref[...], dims)

        @pl.when(k == nk - 1)
        def _():
            out = acc_ref[...]
            if has_res:
                out = out + r_ref[...]
            o_ref[...] = out.astype(out_dtype)

    in_specs = [a_spec, b_spec] + ([o_spec] if has_res else [])
    args = (a, b) + ((res,) if has_res else ())
    return pl.pallas_call(
        body, name=name, grid=(M // tm, N // tn, nk), in_specs=in_specs, out_specs=o_spec,
        out_shape=jax.ShapeDtypeStruct((M, N), out_dtype), scratch_shapes=[pltpu.VMEM((tm, tn), F32)],
        compiler_params=_params(("parallel", "parallel", "arbitrary")),
    )(*args)


def _row_tile(n, d):
    if n % 8:
        return n
    t = 8
    while t * 2 <= min(n, (256 * 1024) // d) and n % (t * 2) == 0:
        t *= 2
    return t


def _rms_fwd(x, g, name, out_dtype=F32):
    n, d = x.shape
    tm = _row_tile(n, d)

    def body(x_ref, g_ref, o_ref):
        xf = x_ref[...]
        r = lax.rsqrt(jnp.mean(xf * xf, axis=-1, keepdims=True) + EPS)
        o_ref[...] = (xf * r * g_ref[...]).astype(out_dtype)

    return pl.pallas_call(
        body, name=name, grid=(n // tm,),
        in_specs=[pl.BlockSpec((tm, d), lambda i: (i, 0)), pl.BlockSpec((1, d), lambda i: (0, 0))],
        out_specs=pl.BlockSpec((tm, d), lambda i: (i, 0)),
        out_shape=jax.ShapeDtypeStruct((n, d), out_dtype), compiler_params=_params(("parallel",)),
    )(x, g.reshape(1, d))


def _rms_bwd(x, g, dy, name, res=None):
    n, d = x.shape
    tm = _row_tile(n, d)
    has_res = res is not None

    def body(*refs):
        x_ref, g_ref, dy_ref = refs[:3]
        r_ref = refs[3] if has_res else None
        dx_ref, dg_ref = refs[-2:]
        xf = x_ref[...]
        r = lax.rsqrt(jnp.mean(xf * xf, axis=-1, keepdims=True) + EPS)
        xh = xf * r
        dyf = dy_ref[...].astype(F32)
        dxh = dyf * g_ref[...]
        dx = r * (dxh - xh * jnp.mean(dxh * xh, axis=-1, keepdims=True))
        if has_res:
            dx = dx + r_ref[...]
        dx_ref[...] = dx

        @pl.when(pl.program_id(0) == 0)
        def _():
            dg_ref[...] = jnp.zeros_like(dg_ref)

        dg_ref[...] += jnp.sum(dyf * xh, axis=0, keepdims=True)

    row = pl.BlockSpec((tm, d), lambda i: (i, 0))
    vec = pl.BlockSpec((1, d), lambda i: (0, 0))
    dx, dg = pl.pallas_call(
        body, name=name, grid=(n // tm,),
        in_specs=[row, vec, row] + ([row] if has_res else []),
        out_specs=[row, vec],
        out_shape=[jax.ShapeDtypeStruct((n, d), F32), jax.ShapeDtypeStruct((1, d), F32)],
        compiler_params=_params(("arbitrary",)),
    )(x, g.reshape(1, d), dy, *((res,) if has_res else ()))
    return dx, dg.reshape(d)


def _silu(x):
    return x * jax.nn.sigmoid(x)


def _gnorm_fwd(o, gate, w, name):
    n, d = o.shape
    tm = _row_tile(n, d)

    def body(o_ref, g_ref, w_ref, y_ref):
        of = o_ref[...]
        r = lax.rsqrt(jnp.mean(of * of, axis=-1, keepdims=True) + EPS)
        y_ref[...] = of * r * w_ref[...] * _silu(g_ref[...])

    row = pl.BlockSpec((tm, d), lambda i: (i, 0))
    vec = pl.BlockSpec((1, d), lambda i: (0, 0))
    return pl.pallas_call(body, name=name, grid=(n // tm,), in_specs=[row, row, vec], out_specs=row,
                          out_shape=jax.ShapeDtypeStruct((n, d), F32), compiler_params=_params(("parallel",)))(o, gate, w.reshape(1, d))


def _gnorm_bwd(o, gate, w, dy, name):
    n, d = o.shape
    tm = _row_tile(n, d)

    def body(o_ref, g_ref, w_ref, dy_ref, do_ref, dgate_ref, dw_ref):
        of, gf, dyf = o_ref[...], g_ref[...], dy_ref[...]
        r = lax.rsqrt(jnp.mean(of * of, axis=-1, keepdims=True) + EPS)
        xh = of * r
        sg = jax.nn.sigmoid(gf)
        sil = gf * sg
        dnorm = dyf * sil
        dgate_ref[...] = dyf * xh * w_ref[...] * (sg * (1.0 + gf * (1.0 - sg)))
        dxh = dnorm * w_ref[...]
        do_ref[...] = r * (dxh - xh * jnp.mean(dxh * xh, axis=-1, keepdims=True))

        @pl.when(pl.program_id(0) == 0)
        def _():
            dw_ref[...] = jnp.zeros_like(dw_ref)

        dw_ref[...] += jnp.sum(dnorm * xh, axis=0, keepdims=True)

    row = pl.BlockSpec((tm, d), lambda i: (i, 0))
    vec = pl.BlockSpec((1, d), lambda i: (0, 0))
    do, dgate, dw = pl.pallas_call(
        body, name=name, grid=(n // tm,), in_specs=[row, row, vec, row], out_specs=[row, row, vec],
        out_shape=[jax.ShapeDtypeStruct((n, d), F32)] * 2 + [jax.ShapeDtypeStruct((1, d), F32)],
        compiler_params=_params(("arbitrary",)),
    )(o, gate, w.reshape(1, d), dy)
    return do, dgate, dw.reshape(d)


def _sigmul_fwd(z, e, res, name):
    n, d = z.shape
    tm = _row_tile(n, d)

    def body(z_ref, e_ref, r_ref, o_ref):
        o_ref[...] = r_ref[...] + jax.nn.sigmoid(z_ref[...]) * e_ref[...]

    row = pl.BlockSpec((tm, d), lambda i: (i, 0))
    return pl.pallas_call(body, name=name, grid=(n // tm,), in_specs=[row] * 3, out_specs=row,
                          out_shape=jax.ShapeDtypeStruct((n, d), F32), compiler_params=_params(("parallel",)))(z, e, res)


def _sigmul_bwd(z, e, dy, name):
    n, d = z.shape
    tm = _row_tile(n, d)

    def body(z_ref, e_ref, dy_ref, dz_ref, de_ref):
        s = jax.nn.sigmoid(z_ref[...])
        dyf = dy_ref[...]
        dz_ref[...] = dyf * e_ref[...] * s * (1.0 - s)
        de_ref[...] = dyf * s

    row = pl.BlockSpec((tm, d), lambda i: (i, 0))
    return pl.pallas_call(body, name=name, grid=(n // tm,), in_specs=[row] * 3, out_specs=[row] * 2,
                          out_shape=[jax.ShapeDtypeStruct((n, d), F32)] * 2, compiler_params=_params(("parallel",)))(z, e, dy)


def _loss_fwd(y, target, name):
    n, d = y.shape
    tm = _row_tile(n, d)

    def body(y_ref, t_ref, l_ref, dy_ref):
        diff = y_ref[...] - t_ref[...]
        dy_ref[...] = diff * (1.0 / d)

        @pl.when(pl.program_id(0) == 0)
        def _():
            l_ref[...] = jnp.zeros_like(l_ref)

        part = jnp.sum(jnp.mean(diff * diff, axis=-1, keepdims=True), axis=0, keepdims=True)
        l_ref[...] += 0.5 * jnp.broadcast_to(part, l_ref.shape)

    row = pl.BlockSpec((tm, d), lambda i: (i, 0))
    vec = pl.BlockSpec((1, LANES), lambda i: (0, 0))
    return pl.pallas_call(body, name=name, grid=(n // tm,), in_specs=[row, row], out_specs=[vec, row],
                          out_shape=[jax.ShapeDtypeStruct((1, LANES), F32), jax.ShapeDtypeStruct((n, d), F32)],
                          compiler_params=_params(("arbitrary",)))(y, target)


SB_BLK = 128


def _sb_logits(qb, kb, qi, kj, scale, row, col):
    z = _dot(qb, kb, NT) * scale
    mask = (kj * SB_BLK + col) < (qi * SB_BLK + row)
    sp = jnp.maximum(z, 0.0) + jnp.log1p(jnp.exp(-jnp.abs(z)))
    lk = jnp.where(mask, -sp, 0.0)
    return mask, lk, z - sp


def _sb_fwd(q, k, v, name):
    bh, s, d = q.shape
    nq = s // SB_BLK
    scale = d ** -0.5

    def body(q_ref, k_ref, v_ref, o_ref, lt_ref):
        row = lax.broadcasted_iota(jnp.int32, (SB_BLK, SB_BLK), 0)
        col = lax.broadcasted_iota(jnp.int32, (SB_BLK, SB_BLK), 1)
        u_after = _bf(row > col)

        def qloop(qi, _):
            q0 = pl.multiple_of(qi * SB_BLK, SB_BLK)
            qb = q_ref[0, pl.ds(q0, SB_BLK), :]

            def kloop(j, st):
                acc, carry = st
                kj = qi - j
                k0 = pl.multiple_of(kj * SB_BLK, SB_BLK)
                kb = k_ref[0, pl.ds(k0, SB_BLK), :]
                vb = v_ref[0, pl.ds(k0, SB_BLK), :]
                mask, lk, ls = _sb_logits(qb, kb, qi, kj, scale, row, col)
                later = carry + _dot_exact_rhs01(lk, u_after)
                w = jnp.where(mask, jnp.exp(ls + later), 0.0)
                acc = acc + _dot(w, vb)
                carry = carry + jnp.sum(lk, axis=1, keepdims=True)
                return acc, carry

            acc, carry = lax.fori_loop(0, qi + 1, kloop, (jnp.zeros((SB_BLK, d), F32), jnp.zeros((SB_BLK, 1), F32)))
            o_ref[0, pl.ds(q0, SB_BLK), :] = acc
            lt_ref[0, pl.ds(q0, SB_BLK), :] = carry
            return 0

        lax.fori_loop(0, nq, qloop, 0)

    blk = pl.BlockSpec((1, s, d), lambda i: (i, 0, 0))
    col1 = pl.BlockSpec((1, s, 1), lambda i: (i, 0, 0))
    return pl.pallas_call(body, name=name, grid=(bh,), in_specs=[blk] * 3, out_specs=[blk, col1],
                          out_shape=[jax.ShapeDtypeStruct((bh, s, d), F32), jax.ShapeDtypeStruct((bh, s, 1), F32)],
                          compiler_params=_params(("parallel",)))(q, k, v)


def _sb_bwd(q, k, v, ltot, do, name):
    bh, s, d = q.shape
    nq = s // SB_BLK
    scale = d ** -0.5

    def body(q_ref, k_ref, v_ref, lt_ref, do_ref, dq_ref, dk_ref, dv_ref):
        row = lax.broadcasted_iota(jnp.int32, (SB_BLK, SB_BLK), 0)
        col = lax.broadcasted_iota(jnp.int32, (SB_BLK, SB_BLK), 1)
        u_upto = _bf(row <= col)
        u_before = _bf(row < col)
        dk_ref[...] = jnp.zeros_like(dk_ref)
        dv_ref[...] = jnp.zeros_like(dv_ref)

        def qloop(qi, _):
            q0 = pl.multiple_of(qi * SB_BLK, SB_BLK)
            qb = q_ref[0, pl.ds(q0, SB_BLK), :]
            dob = do_ref[0, pl.ds(q0, SB_BLK), :]
            ltot_b = lt_ref[0, pl.ds(q0, SB_BLK), :]

            def kloop(kj, st):
                dq, cl, cg = st
                k0 = pl.multiple_of(kj * SB_BLK, SB_BLK)
                kb = k_ref[0, pl.ds(k0, SB_BLK), :]
                vb = v_ref[0, pl.ds(k0, SB_BLK), :]
                mask, lk, ls = _sb_logits(qb, kb, qi, kj, scale, row, col)
                later = ltot_b - (cl + _dot_exact_rhs01(lk, u_upto))
                w = jnp.where(mask, jnp.exp(ls + later), 0.0)
                g = _dot(dob, vb, NT) * w
                dv_ref[0, pl.ds(k0, SB_BLK), :] += _dot(w, dob, TN)
                g_before = cg + _dot_exact_rhs01(g, u_before)
                sig = jnp.exp(ls)
                dz = jnp.where(mask, g * (1.0 - sig) - sig * g_before, 0.0) * scale
                dq = dq + _dot(dz, kb)
                dk_ref[0, pl.ds(k0, SB_BLK), :] += _dot(dz, qb, TN)
                return dq, cl + jnp.sum(lk, axis=1, keepdims=True), cg + jnp.sum(g, axis=1, keepdims=True)

            z1 = jnp.zeros((SB_BLK, 1), F32)
            dq, _, _ = lax.fori_loop(0, qi + 1, kloop, (jnp.zeros((SB_BLK, d), F32), z1, z1))
            dq_ref[0, pl.ds(q0, SB_BLK), :] = dq
            return 0

        lax.fori_loop(0, nq, qloop, 0)

    blk = pl.BlockSpec((1, s, d), lambda i: (i, 0, 0))
    col1 = pl.BlockSpec((1, s, 1), lambda i: (i, 0, 0))
    return pl.pallas_call(body, name=name, grid=(bh,), in_specs=[blk, blk, blk, col1, blk], out_specs=[blk] * 3,
                          out_shape=[jax.ShapeDtypeStruct((bh, s, d), F32)] * 3, compiler_params=_params(("parallel",)))(q, k, v, ltot, do)


HG_CHUNK = 64


def _hg_consts(c):
    levels = int(math.log2(c))
    t = np.arange(c)
    tri = (t[:, None] >= t[None, :]).astype(np.float32)
    psel = np.zeros((levels, c, c), np.float32)
    masks = np.zeros((levels + 1, c, c), np.float32)
    for l in range(levels):
        n = c >> (l + 1)
        blk = t // (2 * n)
        psel[l, t, blk * 2 * n + n - 1] = 1.0
        upper = (t % (2 * n)) >= n
        masks[l] = (blk[:, None] == blk[None, :]) & upper[:, None] & (~upper)[None, :]
    masks[levels] = np.eye(c)
    psel = psel.reshape(levels * c, c)
    return levels, jnp.asarray(tri), jnp.asarray(psel), jnp.asarray(masks), jnp.asarray(tri.T.copy()), jnp.asarray(psel.T.copy())


def _hg_elem(qv, fv, lbv):
    sig = jax.nn.sigmoid(fv)
    lf = jnp.log(lbv + (1.0 - lbv) * sig)
    kk = (1.0 - lbv) * jax.nn.sigmoid(-fv)
    qf = qv * jax.nn.sigmoid(qv)
    return qf, kk, lf


def _col_bcast(rowvec):
    n = rowvec.shape[1]
    return jnp.transpose(jnp.broadcast_to(rowvec, (n, n)))


def _hg_chunk(qf, kk, lf, iv, state, tri, psel, m_ref, c, levels):
    b = _dot_exact_lhs01(tri, lf)
    bl = b[c - 1:c, :]
    eb = jnp.exp(b)
    qi = qf * eb
    bsel = _dot_exact_lhs01(psel, b)
    scores = jnp.where(m_ref[levels] > 0, _dot(qf, kk, NT), 0.0)
    lev = []
    for l in range(levels):
        bs = bsel[l * c:(l + 1) * c]
        eq = jnp.exp(jnp.minimum(b - bs, 0.0))
        ek = jnp.exp(jnp.minimum(bs - b, 0.0))
        ql, kl = qf * eq, kk * ek
        scores = scores + jnp.where(m_ref[l] > 0, _dot(ql, kl, NT), 0.0)
        lev.append((eq, ek, ql, kl))
    o = _dot(qi, state) + _dot(scores, iv)
    ebl = jnp.exp(bl - b)
    kd = kk * ebl
    decay = _col_bcast(jnp.exp(bl))
    new_state = decay * state + _dot(kd, iv, TN)
    return o, new_state, (eb, qi, scores, lev, ebl, kd, decay)


def _hg_fwd(q, f, i, lb, name):
    bh, s, d = q.shape
    c = HG_CHUNK
    nc = s // c
    levels, tri, psel, masks, _, _ = _hg_consts(c)
    nh = lb.shape[0]

    def body(q_ref, f_ref, i_ref, lb_ref, tri_ref, psel_ref, m_ref, o_ref, st_ref):
        lbv = jnp.broadcast_to(lb_ref[0], (c, d))
        tri_v, psel_v = _bf(tri_ref[...]), _bf(psel_ref[...])

        def chunk(ci, state):
            r0 = pl.multiple_of(ci * c, c)
            qf, kk, lf = _hg_elem(q_ref[0, pl.ds(r0, c), :], f_ref[0, pl.ds(r0, c), :], lbv)
            st_ref[0, ci] = state
            o, state, _ = _hg_chunk(qf, kk, lf, i_ref[0, pl.ds(r0, c), :], state, tri_v, psel_v, m_ref, c, levels)
            o_ref[0, pl.ds(r0, c), :] = o
            return state

        lax.fori_loop(0, nc, chunk, jnp.zeros((d, d), F32))

    seq = pl.BlockSpec((1, s, d), lambda b: (b, 0, 0))
    full = lambda a: pl.BlockSpec(a.shape, lambda b: (0,) * a.ndim)
    return pl.pallas_call(
        body, name=name, grid=(bh,),
        in_specs=[seq, seq, seq, pl.BlockSpec((1, 1, d), lambda b: (b % nh, 0, 0)), full(tri), full(psel), full(masks)],
        out_specs=[seq, pl.BlockSpec((1, nc, d, d), lambda b: (b, 0, 0, 0))],
        out_shape=[jax.ShapeDtypeStruct((bh, s, d), F32), jax.ShapeDtypeStruct((bh, nc, d, d), F32)],
        compiler_params=_params(("parallel",)),
    )(q, f, i, lb.reshape(nh, 1, d), tri, psel, masks)


def _hg_bwd(q, f, i, lb, states, do, name):
    bh, s, d = q.shape
    c = HG_CHUNK
    nc = s // c
    levels, tri, psel, masks, tri_t, psel_t = _hg_consts(c)
    nh = lb.shape[0]

    def body(q_ref, f_ref, i_ref, lb_ref, st_ref, do_ref, tri_ref, psel_ref, m_ref, trit_ref, pselt_ref,
             dq_ref, df_ref, di_ref, dlb_ref):
        lbv = jnp.broadcast_to(lb_ref[0], (c, d))
        tri_v, psel_v = _bf(tri_ref[...]), _bf(psel_ref[...])
        trit_v, pselt_v = _bf(trit_ref[...]), _bf(pselt_ref[...])
        last_row = lax.broadcasted_iota(jnp.int32, (c, d), 0) == c - 1

        def chunk(step, carry):
            ds_out, dlb = carry
            ci = nc - 1 - step
            r0 = pl.multiple_of(ci * c, c)
            qv, fv, iv = q_ref[0, pl.ds(r0, c), :], f_ref[0, pl.ds(r0, c), :], i_ref[0, pl.ds(r0, c), :]
            dov = do_ref[0, pl.ds(r0, c), :]
            state = st_ref[0, ci]
            (qf, kk, lf), elem_vjp = jax.vjp(_hg_elem, qv, fv, lbv)
            _, _, (eb, qi, scores, lev, ebl, kd, decay) = _hg_chunk(qf, kk, lf, iv, state, tri_v, psel_v, m_ref, c, levels)

            dscores = _dot(dov, iv, NT)
            di_ref[0, pl.ds(r0, c), :] = _dot(scores, dov, TN) + _dot(kd, ds_out)
            dqi = _dot(dov, state, NT)
            ds_in = decay * ds_out + _dot(qi, dov, TN)
            dkd = _dot(iv, ds_out, NT)
            dqf = dqi * eb
            dkk = dkd * ebl
            tkd = dkd * kd
            db = dqi * qi - tkd
            dbl = jnp.sum(tkd, axis=0, keepdims=True) + _col_bcast_t(jnp.sum(ds_out * decay * state, axis=1, keepdims=True))
            dsd = jnp.where(m_ref[levels] > 0, dscores, 0.0)
            dqf = dqf + _dot(dsd, kk)
            dkk = dkk + _dot(dsd, qf, TN)
            dbsel = []
            for l in range(levels):
                eq, ek, ql, kl = lev[l]
                dsl = jnp.where(m_ref[l] > 0, dscores, 0.0)
                dql = _dot(dsl, kl)
                dkl = _dot(dsl, ql, TN)
                dqf = dqf + dql * eq
                dkk = dkk + dkl * ek
                diff = dql * ql - dkl * kl
                db = db + diff
                dbsel.append(-diff)
            db = db + _dot_exact_lhs01(pselt_v, jnp.concatenate(dbsel, axis=0))
            db = db + jnp.where(last_row, dbl, 0.0)
            dlf = _dot_exact_lhs01(trit_v, db)
            dq, df, dlb_c = elem_vjp((dqf, dkk, dlf))
            dq_ref[0, pl.ds(r0, c), :] = dq
            df_ref[0, pl.ds(r0, c), :] = df
            return ds_in, dlb + jnp.sum(dlb_c, axis=0, keepdims=True)

        _, dlb = lax.fori_loop(0, nc, chunk, (jnp.zeros((d, d), F32), jnp.zeros((1, d), F32)))
        dlb_ref[0] = dlb

    seq = pl.BlockSpec((1, s, d), lambda b: (b, 0, 0))
    full = lambda a: pl.BlockSpec(a.shape, lambda b: (0,) * a.ndim)
    return pl.pallas_call(
        body, name=name, grid=(bh,),
        in_specs=[seq, seq, seq, pl.BlockSpec((1, 1, d), lambda b: (b % nh, 0, 0)),
                  pl.BlockSpec((1, nc, d, d), lambda b: (b, 0, 0, 0)), seq,
                  full(tri), full(psel), full(masks), full(tri_t), full(psel_t)],
        out_specs=[seq, seq, seq, pl.BlockSpec((1, 1, d), lambda b: (b, 0, 0))],
        out_shape=[jax.ShapeDtypeStruct((bh, s, d), F32)] * 3 + [jax.ShapeDtypeStruct((bh, 1, d), F32)],
        compiler_params=_params(("parallel",)),
    )(q, f, i, lb.reshape(nh, 1, d), states, do, tri, psel, masks, tri_t, psel_t)


def _col_bcast_t(colvec):
    n = colvec.shape[0]
    return jnp.transpose(jnp.broadcast_to(colvec, (n, n)))[0:1, :]


def _swa_probs(qg, kb, bias, sink, valid, scale):
    logits = _dot(qg, kb, NT) * scale + bias
    logits = jnp.where(valid, logits, -jnp.inf)
    m = jnp.maximum(jnp.max(logits, axis=-1, keepdims=True), sink)
    e = jnp.exp(logits - m)
    es = jnp.exp(sink - m)
    den = jnp.sum(e, axis=-1, keepdims=True) + es
    return e / den, es / den


def _swa_valid(n):
    w = WINDOW
    row = lax.broadcasted_iota(jnp.int32, (w, 2 * w), 0)
    col = lax.broadcasted_iota(jnp.int32, (w, 2 * w), 1)
    dist = row + w - col
    return (dist >= 0) & (dist < w) & ((col >= w) | (n > 0))


def _swa_specs(b, g, s, d):
    w = WINDOW
    q_spec = pl.BlockSpec((1, 1, g, w, d), lambda h, bi, n: (bi, h, 0, n, 0))
    kp_spec = pl.BlockSpec((1, 1, w, d), lambda h, bi, n: (bi, h, jnp.maximum(n - 1, 0), 0))
    kc_spec = pl.BlockSpec((1, 1, w, d), lambda h, bi, n: (bi, h, n, 0))
    bias_spec = pl.BlockSpec((1, g, w, 2 * w), lambda h, bi, n: (h, 0, 0, 0))
    sink_spec = pl.BlockSpec(memory_space=pltpu.SMEM)
    return q_spec, kp_spec, kc_spec, bias_spec, sink_spec


def _swa_fwd(q, k, v, sinks, bias, name):
    b, kvh, g, s, d = q.shape
    w = WINDOW
    scale = d ** -0.5
    q_spec, kp_spec, kc_spec, bias_spec, sink_spec = _swa_specs(b, g, s, d)

    def body(q_ref, kp_ref, kc_ref, vp_ref, vc_ref, bias_ref, sink_ref, o_ref):
        h, n = pl.program_id(0), pl.program_id(2)
        valid = _swa_valid(n)
        kb = jnp.concatenate([kp_ref[0, 0], kc_ref[0, 0]], axis=0)
        vb = jnp.concatenate([vp_ref[0, 0], vc_ref[0, 0]], axis=0)
        for gi in range(g):
            p, _ = _swa_probs(q_ref[0, 0, gi], kb, bias_ref[0, gi], sink_ref[h * g + gi], valid, scale)
            o_ref[0, 0, gi] = _dot(p, vb)

    return pl.pallas_call(
        body, name=name, grid=(kvh, b, s // w),
        in_specs=[q_spec, kp_spec, kc_spec, kp_spec, kc_spec, bias_spec, sink_spec], out_specs=q_spec,
        out_shape=jax.ShapeDtypeStruct(q.shape, F32), compiler_params=_params(("parallel", "parallel", "arbitrary")),
    )(q, k, k, v, v, bias, sinks)


def _swa_bwd(q, k, v, sinks, bias, do, name):
    b, kvh, g, s, d = q.shape
    w = WINDOW
    scale = d ** -0.5
    q_spec, kp_spec, kc_spec, bias_spec, sink_spec = _swa_specs(b, g, s, d)
    kv_acc = pl.BlockSpec((1, 1, s, d), lambda h, bi, n: (bi, h, 0, 0))
    dsink_spec = pl.BlockSpec((1, g, LANES), lambda h, bi, n: (h, 0, 0))

    def body(q_ref, kp_ref, kc_ref, vp_ref, vc_ref, bias_ref, sink_ref, do_ref, dq_ref, dk_ref, dv_ref, dbias_ref, dsink_ref):
        h, bi, n = pl.program_id(0), pl.program_id(1), pl.program_id(2)
        valid = _swa_valid(n)
        kb = jnp.concatenate([kp_ref[0, 0], kc_ref[0, 0]], axis=0)
        vb = jnp.concatenate([vp_ref[0, 0], vc_ref[0, 0]], axis=0)

        @pl.when(n == 0)
        def _():
            dk_ref[...] = jnp.zeros_like(dk_ref)
            dv_ref[...] = jnp.zeros_like(dv_ref)

        @pl.when((n == 0) & (bi == 0))
        def _():
            dbias_ref[...] = jnp.zeros_like(dbias_ref)
            dsink_ref[...] = jnp.zeros_like(dsink_ref)

        dkb = jnp.zeros((2 * w, d), F32)
        dvb = jnp.zeros((2 * w, d), F32)
        for gi in range(g):
            qg, dog = q_ref[0, 0, gi], do_ref[0, 0, gi]
            p, ps = _swa_probs(qg, kb, bias_ref[0, gi], sink_ref[h * g + gi], valid, scale)
            dp = _dot(dog, vb, NT)
            delta = jnp.sum(p * dp, axis=-1, keepdims=True)
            dl = p * (dp - delta)
            dq_ref[0, 0, gi] = _dot(dl, kb) * scale
            dkb = dkb + _dot(dl, qg, TN) * scale
            dvb = dvb + _dot(p, dog, TN)
            dbias_ref[0, gi] += dl
            dsink_ref[0, gi:gi + 1, :] += jnp.broadcast_to(jnp.sum(-ps * delta, axis=0, keepdims=True), (1, LANES))

        c0 = pl.multiple_of(n * w, w)
        dk_ref[0, 0, pl.ds(c0, w), :] += dkb[w:]
        dv_ref[0, 0, pl.ds(c0, w), :] += dvb[w:]

        @pl.when(n > 0)
        def _():
            p0 = pl.multiple_of((n - 1) * w, w)
            dk_ref[0, 0, pl.ds(p0, w), :] += dkb[:w]
            dv_ref[0, 0, pl.ds(p0, w), :] += dvb[:w]

    return pl.pallas_call(
        body, name=name, grid=(kvh, b, s // w),
        in_specs=[q_spec, kp_spec, kc_spec, kp_spec, kc_spec, bias_spec, sink_spec, q_spec],
        out_specs=[q_spec, kv_acc, kv_acc, bias_spec, dsink_spec],
        out_shape=[jax.ShapeDtypeStruct(q.shape, F32), jax.ShapeDtypeStruct(k.shape, F32), jax.ShapeDtypeStruct(k.shape, F32),
                   jax.ShapeDtypeStruct(bias.shape, F32), jax.ShapeDtypeStruct((kvh, g, LANES), F32)],
        compiler_params=_params(("arbitrary", "arbitrary", "arbitrary")),
    )(q, k, k, v, v, bias, sinks, do)


def _t5_bias(rel_bias):
    t = np.arange(WINDOW)[:, None]
    s = np.arange(2 * WINDOW)[None, :]
    dist = t + WINDOW - s
    max_exact = N_BUCKETS // 2
    large = max_exact + (np.log(np.maximum(dist, max_exact) / max_exact) / math.log(MAX_DISTANCE / max_exact)
                         * (N_BUCKETS - max_exact)).astype(np.int32)
    large = np.minimum(large, N_BUCKETS - 1)
    bucket = np.where(dist < max_exact, np.maximum(dist, 0), large).astype(np.int32)
    onehot = jnp.asarray(np.eye(N_BUCKETS, dtype=np.float32)[bucket])
    bias = jnp.einsum("tsb,bh->hts", onehot, rel_bias.astype(F32), precision=lax.Precision.HIGHEST)
    return bias.reshape(SW_KV_HEADS, SW_GROUP, WINDOW, 2 * WINDOW)


CONV_W = 3


def _shift_down(x, k):
    row = lax.broadcasted_iota(jnp.int32, x.shape, 0)
    return jnp.where(row >= k, pltpu.roll(x, k, axis=0), 0.0)


def _shift_up(x, k):
    n = x.shape[0]
    row = lax.broadcasted_iota(jnp.int32, x.shape, 0)
    return jnp.where(row < n - k, pltpu.roll(x, n - k, axis=0), 0.0)


def _conv3(u, w, bvec):
    return w[0:1] * _shift_down(u, 2) + w[1:2] * _shift_down(u, 1) + w[2:3] * u + bvec


def _convglu_fwd(u, w, bvec, name, out_dtype=F32):
    b, s, f2 = u.shape
    f = f2 // 2
    tc = _pick(f, 256)
    nt = f // tc

    def body(ug_ref, uu_ref, wg_ref, wu_ref, bg_ref, bu_ref, o_ref):
        cg = _conv3(ug_ref[0], wg_ref[...], bg_ref[...])
        cu = _conv3(uu_ref[0], wu_ref[...], bu_ref[...])
        o_ref[0] = (_silu(cg) * cu).astype(out_dtype)

    ug = pl.BlockSpec((1, s, tc), lambda j, bi: (bi, 0, j))
    uu = pl.BlockSpec((1, s, tc), lambda j, bi: (bi, 0, j + nt))
    wg = pl.BlockSpec((CONV_W, tc), lambda j, bi: (0, j))
    wu = pl.BlockSpec((CONV_W, tc), lambda j, bi: (0, j + nt))
    bg = pl.BlockSpec((1, tc), lambda j, bi: (0, j))
    bu = pl.BlockSpec((1, tc), lambda j, bi: (0, j + nt))
    bv = bvec.reshape(1, f2)
    return pl.pallas_call(body, name=name, grid=(nt, b), in_specs=[ug, uu, wg, wu, bg, bu], out_specs=ug,
                          out_shape=jax.ShapeDtypeStruct((b, s, f), out_dtype),
                          compiler_params=_params(("parallel", "parallel")))(u, u, w, w, bv, bv)


def _convglu_bwd(u, w, bvec, dact, name):
    b, s, f2 = u.shape
    f = f2 // 2
    tc = LANES
    nt = f // tc

    def taps(dc, uv):
        rows = [jnp.sum(dc * _shift_down(uv, 2), axis=0, keepdims=True), jnp.sum(dc * _shift_down(uv, 1), axis=0, keepdims=True),
                jnp.sum(dc * uv, axis=0, keepdims=True), jnp.sum(dc, axis=0, keepdims=True)]
        return jnp.concatenate(rows + [jnp.zeros((4, tc), F32)], axis=0)

    def back(dc, wv):
        return wv[2:3] * dc + wv[1:2] * _shift_up(dc, 1) + wv[0:1] * _shift_up(dc, 2)

    def body(ug_ref, uu_ref, wg_ref, wu_ref, bg_ref, bu_ref, da_ref, dug_ref, duu_ref, dwg_ref, dwu_ref):
        ugv, uuv, da = ug_ref[0], uu_ref[0], da_ref[0]
        cg = _conv3(ugv, wg_ref[...], bg_ref[...])
        cu = _conv3(uuv, wu_ref[...], bu_ref[...])
        sg = jax.nn.sigmoid(cg)
        dcu = da * (cg * sg)
        dcg = da * cu * (sg * (1.0 + cg * (1.0 - sg)))
        dug_ref[0] = back(dcg, wg_ref[...])
        duu_ref[0] = back(dcu, wu_ref[...])

        @pl.when(pl.program_id(1) == 0)
        def _():
            dwg_ref[...] = jnp.zeros_like(dwg_ref)
            dwu_ref[...] = jnp.zeros_like(dwu_ref)

        dwg_ref[...] += taps(dcg, ugv)
        dwu_ref[...] += taps(dcu, uuv)

    ug = pl.BlockSpec((1, s, tc), lambda j, bi: (bi, 0, j))
    uu = pl.BlockSpec((1, s, tc), lambda j, bi: (bi, 0, j + nt))
    wg = pl.BlockSpec((CONV_W, tc), lambda j, bi: (0, j))
    wu = pl.BlockSpec((CONV_W, tc), lambda j, bi: (0, j + nt))
    bg = pl.BlockSpec((1, tc), lambda j, bi: (0, j))
    bu = pl.BlockSpec((1, tc), lambda j, bi: (0, j + nt))
    acc = pl.BlockSpec((8, tc), lambda j, bi: (0, j))
    bv = bvec.reshape(1, f2)
    return pl.pallas_call(
        body, name=name, grid=(nt, b), in_specs=[ug, uu, wg, wu, bg, bu, ug], out_specs=[ug, ug, acc, acc],
        out_shape=[jax.ShapeDtypeStruct((b, s, f), F32)] * 2 + [jax.ShapeDtypeStruct((8, f), F32)] * 2,
        compiler_params=_params(("parallel", "arbitrary")),
    )(u, u, w, w, bv, bv, dact)


def _adamw(w, g, m, v, name):
    r, c = w.shape
    tr = _row_tile(r, c)
    c1 = 1.0 - ADAM_B1 ** ADAM_STEP
    c2 = 1.0 - ADAM_B2 ** ADAM_STEP

    def body(w_ref, g_ref, m_ref, v_ref, d_ref, mo_ref, vo_ref):
        gv = g_ref[...]
        mn = ADAM_B1 * m_ref[...] + (1.0 - ADAM_B1) * gv
        vn = ADAM_B2 * v_ref[...] + (1.0 - ADAM_B2) * (gv * gv)
        d_ref[...] = -ADAM_LR * ((mn / c1) / (jnp.sqrt(vn / c2) + ADAM_EPS) + ADAM_WD * w_ref[...])
        mo_ref[...] = mn
        vo_ref[...] = vn

    blk = pl.BlockSpec((tr, c), lambda i: (i, 0))
    return pl.pallas_call(body, name=name, grid=(r // tr,), in_specs=[blk] * 4, out_specs=[blk] * 3,
                          out_shape=[jax.ShapeDtypeStruct((r, c), F32)] * 3, compiler_params=_params(("parallel",)))(w, g, m, v)


MESH = pl.DeviceIdType.MESH
ANY = pl.BlockSpec(memory_space=pl.ANY)


def _position():
    return lax.axis_index("x"), lax.axis_index("y"), lax.axis_index("c")


def _all_gather(x, name):
    r, c = x.shape

    def body(x_ref, out_ref, send_sems, recv_sems, local_sem):
        mx, my, mc = _position()
        me, sibling = (mx, my, mc), (mx, my, 1 - mc)
        chips = [(1 - mx, my), (mx, 1 - my), (1 - mx, 1 - my)]

        def slot(px, py, pc):
            return out_ref.at[4 * px + 2 * py + pc]

        def copy(k, block, to, src=None):
            return pltpu.make_async_remote_copy(
                src_ref=slot(*block) if src is None else src, dst_ref=slot(*block),
                send_sem=send_sems.at[k], recv_sem=recv_sems.at[k], device_id=to, device_id_type=MESH)

        mine = pltpu.make_async_copy(x_ref, slot(*me), local_sem.at[0])
        mine.start()
        first = [copy(0, me, sibling, src=x_ref)]
        first += [copy(1 + j, me, (*chip, mc), src=x_ref) for j, chip in enumerate(chips)]
        for cp in first:
            cp.start()
        passed = [copy(4 + j, (*chip, mc), sibling) for j, chip in enumerate(chips)]
        for j, chip in enumerate(chips):
            copy(1 + j, (*chip, mc), me).wait_recv()
            passed[j].start()
        copy(0, sibling, me).wait_recv()
        for j, chip in enumerate(chips):
            copy(4 + j, (*chip, 1 - mc), me).wait_recv()
        for cp in first + passed:
            cp.wait_send()
        mine.wait()

    return pl.pallas_call(
        body, name=name, out_shape=jax.ShapeDtypeStruct((N_DEV, r, c), x.dtype), in_specs=[ANY], out_specs=ANY,
        scratch_shapes=[pltpu.SemaphoreType.DMA((7,)), pltpu.SemaphoreType.DMA((7,)), pltpu.SemaphoreType.DMA((1,))],
    )(x)


def _rs_pair(g, name):
    _, r, c = g.shape

    def body(g_ref, a_ref, send_sems, recv_sems):
        mx, my, mc = _position()
        copies = [pltpu.make_async_remote_copy(
            src_ref=g_ref.at[2 * j + 1 - mc], dst_ref=a_ref.at[j], send_sem=send_sems.at[j], recv_sem=recv_sems.at[j],
            device_id=(mx, my, 1 - mc), device_id_type=MESH) for j in range(4)]
        for cp in copies:
            cp.start()
        for cp in copies:
            cp.wait()

    return pl.pallas_call(
        body, name=name, out_shape=jax.ShapeDtypeStruct((4, r, c), g.dtype), in_specs=[ANY], out_specs=ANY,
        scratch_shapes=[pltpu.SemaphoreType.DMA((4,)), pltpu.SemaphoreType.DMA((4,))],
    )(g)


def _rs_chips(p, name):
    _, r, c = p.shape

    def body(p_ref, b_ref, send_sems, recv_sems):
        mx, my, mc = _position()
        chips = [(1 - mx, my), (mx, 1 - my), (1 - mx, 1 - my)]
        copies = [pltpu.make_async_remote_copy(
            src_ref=p_ref.at[2 * cx + cy], dst_ref=b_ref.at[k], send_sem=send_sems.at[k], recv_sem=recv_sems.at[k],
            device_id=(cx, cy, mc), device_id_type=MESH) for k, (cx, cy) in enumerate(chips)]
        for cp in copies:
            cp.start()
        for cp in copies:
            cp.wait()

    return pl.pallas_call(
        body, name=name, out_shape=jax.ShapeDtypeStruct((3, r, c), p.dtype), in_specs=[ANY], out_specs=ANY,
        scratch_shapes=[pltpu.SemaphoreType.DMA((3,)), pltpu.SemaphoreType.DMA((3,))],
    )(p)


def _rs_add_pair(g, a, core, name):
    _, r, c = g.shape
    tr = _row_tile(r, c)

    def body(core_ref, g_ref, a_ref, p_ref):
        p_ref[...] = g_ref[...] + a_ref[...]

    grid_spec = pltpu.PrefetchScalarGridSpec(
        num_scalar_prefetch=1, grid=(4, r // tr),
        in_specs=[pl.BlockSpec((1, tr, c), lambda j, i, core_ref: (2 * j + core_ref[0], i, 0)),
                  pl.BlockSpec((1, tr, c), lambda j, i, core_ref: (j, i, 0))],
        out_specs=pl.BlockSpec((1, tr, c), lambda j, i, core_ref: (j, i, 0)))
    return pl.pallas_call(body, name=name, grid_spec=grid_spec, out_shape=jax.ShapeDtypeStruct((4, r, c), F32),
                          compiler_params=_params(("parallel", "parallel")))(core, g, a)


def _rs_add_chips(p, b, chip, name):
    _, r, c = p.shape
    tr = _row_tile(r, c)

    def body(chip_ref, p_ref, b_ref, o_ref):
        o_ref[...] = ((p_ref[0] + b_ref[0]) + b_ref[1]) + b_ref[2]

    grid_spec = pltpu.PrefetchScalarGridSpec(
        num_scalar_prefetch=1, grid=(r // tr,),
        in_specs=[pl.BlockSpec((1, tr, c), lambda i, chip_ref: (chip_ref[0], i, 0)),
                  pl.BlockSpec((3, tr, c), lambda i, chip_ref: (0, i, 0))],
        out_specs=pl.BlockSpec((tr, c), lambda i, chip_ref: (i, 0)))
    return pl.pallas_call(body, name=name, grid_spec=grid_spec, out_shape=jax.ShapeDtypeStruct((r, c), F32),
                          compiler_params=_params(("parallel",)))(chip, p, b)


def _sum_devices(x, name):
    _, r, c = x.shape

    def body(x_ref, o_ref):
        acc = x_ref[0]
        for d in range(1, N_DEV):
            acc = acc + x_ref[d]
        o_ref[...] = acc

    return pl.pallas_call(body, name=name, out_shape=jax.ShapeDtypeStruct((r, c), F32))(x)


BIG = (("ab_w_in", "col"), ("ab_w_out", "row"), ("c_w_in", "col"), ("c_w_out", "row"),
       ("ffn_up", "col"), ("ffn_down", "row"), ("ple_gate", "row"), ("ple_proj", "col"), ("ffn_conv", "col"))
FLAT_COLS = 1024


def _flat_rows(shape):
    return -(-math.prod(shape) // FLAT_COLS)


def _flatten_shards(shards, dtype):
    parts = []
    for a in shards:
        flat = a.astype(dtype).reshape(-1)
        pad = _flat_rows(a.shape) * FLAT_COLS - flat.shape[0]
        parts.append(jnp.pad(flat, (0, pad)).reshape(-1, FLAT_COLS))
    return jnp.concatenate(parts, axis=0)


def _unflatten_shard(rows, shape):
    return rows.reshape(-1)[:math.prod(shape)].reshape(shape)


def _assemble(stacked, kind):
    _, l, a, b = stacked.shape
    if kind == "col":
        return stacked.transpose(1, 2, 0, 3).reshape(l, a, N_DEV * b)
    return stacked.transpose(1, 0, 2, 3).reshape(l, N_DEV * a, b)


def _disassemble(full, kind):
    l, a, b = full.shape
    if kind == "col":
        return full.reshape(l, a, N_DEV, b // N_DEV).transpose(2, 0, 1, 3)
    return full.reshape(l, N_DEV, a // N_DEV, b).transpose(1, 0, 2, 3)


def _heads_out(x, b, s, nh, d):
    return x.reshape(b, s, nh, d).transpose(0, 2, 1, 3).reshape(b * nh, s, d)


def _heads_in(x, b, s, nh, d):
    return x.reshape(b, nh, s, d).transpose(0, 2, 1, 3).reshape(b * s, nh * d)


def _lower_bounds(logits):
    c = jnp.cumsum(jax.nn.softmax(logits.astype(F32), axis=0), axis=0)
    return c - c[0]


def _forward_backward(x, p, target, W, P):
    b, s, dm = x.shape
    n = b * s
    h = x.reshape(n, dm)
    lbs, lb_vjp = jax.vjp(_lower_bounds, P["hg_lb_logits"])
    bias, bias_vjp = jax.vjp(_t5_bias, P["rel_bias"])
    saved = []

    for i in range(DEPTH):
        j = i // 2
        r = {"h0": h}
        hn = _rms_fwd(h, P["mix_norm"][i], f"mix_norm_f{i}", BF16)
        r["hn"] = hn
        if i % 2 == 0:
            proj = _mm(hn, W["ab_w_in"][j], "nn", f"ab_in_f{i}")
            cols = [proj[:, k * SB_WIDTH:(k + 1) * SB_WIDTH] for k in range(7)]
            qa, ka, va = [_heads_out(a, b, s, SB_HEADS, SB_DIM) for a in cols[:3]]
            qb, fb, ib, gb = [_heads_out(a, b, s, HG_HEADS, HG_DK) for a in cols[3:]]
            oa, lta = _sb_fwd(qa, ka, va, f"sb_f{i}")
            ob, st = _hg_fwd(qb, fb, ib, lbs[j].reshape(HG_HEADS, HG_DK), f"hg_f{i}")
            obg = _gnorm_fwd(ob.reshape(-1, HG_DV), gb.reshape(-1, HG_DV), P["hg_out_norm"][j], f"hg_norm_f{i}")
            cat = jnp.concatenate([_heads_in(oa, b, s, SB_HEADS, SB_DIM),
                                   _heads_in(obg.reshape(b * HG_HEADS, s, HG_DV), b, s, HG_HEADS, HG_DV)], axis=1).astype(BF16)
            h = _mm(cat, W["ab_w_out"][j], "nn", f"ab_out_f{i}", res=h)
            r.update(qa=qa, ka=ka, va=va, lta=lta, qb=qb, fb=fb, ib=ib, gb=gb, ob=ob, st=st, cat=cat)
        else:
            proj = _mm(hn, W["c_w_in"][j], "nn", f"c_in_f{i}")
            nq = SW_HEADS * SW_DIM
            nkv = SW_KV_HEADS * SW_DIM
            q = _heads_out(proj[:, :nq], b, s, SW_HEADS, SW_DIM).reshape(-1, SW_DIM)
            k = _heads_out(proj[:, nq:nq + nkv], b, s, SW_KV_HEADS, SW_DIM).reshape(-1, SW_DIM)
            v = _heads_out(proj[:, nq + nkv:], b, s, SW_KV_HEADS, SW_DIM).reshape(b, SW_KV_HEADS, s, SW_DIM)
            qn = _rms_fwd(q, P["q_norm"][j], f"q_norm_f{i}").reshape(b, SW_KV_HEADS, SW_GROUP, s, SW_DIM)
            kn = _rms_fwd(k, P["k_norm"][j], f"k_norm_f{i}").reshape(b, SW_KV_HEADS, s, SW_DIM)
            o = _swa_fwd(qn, kn, v, P["sinks"][j], bias, f"swa_f{i}")
            o2 = _heads_in(o.reshape(b * SW_HEADS, s, SW_DIM), b, s, SW_HEADS, SW_DIM).astype(BF16)
            h = _mm(o2, W["c_w_out"][j], "nn", f"c_out_f{i}", res=h)
            r.update(q=q, k=k, v=v, qn=qn, kn=kn, o2=o2)
        r["h1"] = h
        hn2 = _rms_fwd(h, P["ffn_norm"][i], f"ffn_norm_f{i}", BF16)
        u = _mm(hn2, W["ffn_up"][i], "nn", f"ffn_up_f{i}").reshape(b, s, 2 * D_FF)
        act = _convglu_fwd(u, W["ffn_conv"][i], P["ffn_conv_b"][i], f"conv_f{i}", BF16).reshape(n, D_FF)
        h = _mm(act, W["ffn_down"][i], "nn", f"ffn_down_f{i}", res=h)
        r.update(hn2=hn2, u=u, act=act, h2=h)
        hn3 = _rms_fwd(h, P["ple_norm"][i], f"ple_norm_f{i}", BF16)
        z = _mm(hn3, W["ple_gate"][i], "nn", f"ple_gate_f{i}")
        pi = p[i].reshape(n, PLE_DIM)
        e = _mm(pi, W["ple_proj"][i], "nn", f"ple_proj_f{i}")
        h = _sigmul_fwd(z, e, h, f"ple_f{i}")
        r.update(hn3=hn3, z=z, e=e, pi=pi)
        saved.append(r)

    loss, dh = _loss_fwd(h, target.reshape(n, dm), "loss")

    gw = {name: [None] * W[name].shape[0] for name, _ in BIG}
    gp = {name: [None] * P[name].shape[0] for name in ("mix_norm", "hg_out_norm", "q_norm", "k_norm", "sinks",
                                                        "ffn_norm", "ffn_conv_b", "ple_norm")}
    dlbs = [None] * (DEPTH // 2)
    dbias = jnp.zeros_like(bias)

    for i in reversed(range(DEPTH)):
        j = i // 2
        r = saved[i]
        dz, de = _sigmul_bwd(r["z"], r["e"], dh, f"ple_b{i}")
        gw["ple_proj"][i] = _mm(r["pi"], de, "tn", f"ple_proj_g{i}")
        gw["ple_gate"][i] = _mm(r["hn3"], dz, "tn", f"ple_gate_g{i}")
        dhn3 = _mm(dz, W["ple_gate"][i], "nt", f"ple_gate_b{i}")
        dh, gp["ple_norm"][i] = _rms_bwd(r["h2"], P["ple_norm"][i], dhn3, f"ple_norm_b{i}", res=dh)

        dact = _mm(dh, W["ffn_down"][i], "nt", f"ffn_down_b{i}").reshape(b, s, D_FF)
        gw["ffn_down"][i] = _mm(r["act"], dh, "tn", f"ffn_down_g{i}")
        dug, duu, ag, au = _convglu_bwd(r["u"], W["ffn_conv"][i], P["ffn_conv_b"][i], dact, f"conv_b{i}")
        du = jnp.concatenate([dug, duu], axis=-1).reshape(n, 2 * D_FF)
        gw["ffn_conv"][i] = jnp.concatenate([ag[:CONV_W], au[:CONV_W]], axis=-1)
        gp["ffn_conv_b"][i] = jnp.concatenate([ag[CONV_W], au[CONV_W]], axis=-1)
        gw["ffn_up"][i] = _mm(r["hn2"], du, "tn", f"ffn_up_g{i}")
        dhn2 = _mm(du, W["ffn_up"][i], "nt", f"ffn_up_b{i}")
        dh, gp["ffn_norm"][i] = _rms_bwd(r["h1"], P["ffn_norm"][i], dhn2, f"ffn_norm_b{i}", res=dh)

        if i % 2 == 0:
            dcat = _mm(dh, W["ab_w_out"][j], "nt", f"ab_out_b{i}")
            gw["ab_w_out"][j] = _mm(r["cat"], dh, "tn", f"ab_out_g{i}")
            doa = _heads_out(dcat[:, :SB_WIDTH], b, s, SB_HEADS, SB_DIM)
            dobg = _heads_out(dcat[:, SB_WIDTH:], b, s, HG_HEADS, HG_DV)
            dob, dgb, gp["hg_out_norm"][j] = _gnorm_bwd(r["ob"].reshape(-1, HG_DV), r["gb"].reshape(-1, HG_DV),
                                                        P["hg_out_norm"][j], dobg.reshape(-1, HG_DV), f"hg_norm_b{i}")
            shape_b = (b * HG_HEADS, s, HG_DV)
            dqb, dfb, dib, dlb = _hg_bwd(r["qb"], r["fb"], r["ib"], lbs[j].reshape(HG_HEADS, HG_DK), r["st"],
                                         dob.reshape(shape_b), f"hg_b{i}")
            dlbs[j] = dlb.reshape(b, HG_W).sum(axis=0)
            dqa, dka, dva = _sb_bwd(r["qa"], r["ka"], r["va"], r["lta"], doa, f"sb_b{i}")
            dproj = jnp.concatenate([_heads_in(a, b, s, SB_HEADS, SB_DIM) for a in (dqa, dka, dva)]
                                    + [_heads_in(a, b, s, HG_HEADS, HG_DK) for a in (dqb, dfb, dib, dgb.reshape(shape_b))], axis=1)
            gw["ab_w_in"][j] = _mm(r["hn"], dproj, "tn", f"ab_in_g{i}")
            dhn = _mm(dproj, W["ab_w_in"][j], "nt", f"ab_in_b{i}")
        else:
            do2 = _mm(dh, W["c_w_out"][j], "nt", f"c_out_b{i}")
            gw["c_w_out"][j] = _mm(r["o2"], dh, "tn", f"c_out_g{i}")
            do = _heads_out(do2, b, s, SW_HEADS, SW_DIM).reshape(b, SW_KV_HEADS, SW_GROUP, s, SW_DIM)
            dqn, dkn, dv, dbias_i, dsink = _swa_bwd(r["qn"], r["kn"], r["v"], P["sinks"][j], bias, do, f"swa_b{i}")
            dbias = dbias + dbias_i
            gp["sinks"][j] = dsink[:, :, 0].reshape(SW_HEADS)
            dq, gp["q_norm"][j] = _rms_bwd(r["q"], P["q_norm"][j], dqn.reshape(-1, SW_DIM), f"q_norm_b{i}")
            dk, gp["k_norm"][j] = _rms_bwd(r["k"], P["k_norm"][j], dkn.reshape(-1, SW_DIM), f"k_norm_b{i}")
            dproj = jnp.concatenate([_heads_in(dq.reshape(b * SW_HEADS, s, SW_DIM), b, s, SW_HEADS, SW_DIM),
                                     _heads_in(dk.reshape(b * SW_KV_HEADS, s, SW_DIM), b, s, SW_KV_HEADS, SW_DIM),
                                     _heads_in(dv.reshape(b * SW_KV_HEADS, s, SW_DIM), b, s, SW_KV_HEADS, SW_DIM)], axis=1)
            gw["c_w_in"][j] = _mm(r["hn"], dproj, "tn", f"c_in_g{i}")
            dhn = _mm(dproj, W["c_w_in"][j], "nt", f"c_in_b{i}")
        dh, gp["mix_norm"][i] = _rms_bwd(r["h0"], P["mix_norm"][i], dhn, f"mix_norm_b{i}", res=dh)

    gw = {name: jnp.stack(v) for name, v in gw.items()}
    gp = {name: jnp.stack(v) for name, v in gp.items()}
    gp["hg_lb_logits"] = lb_vjp(jnp.stack(dlbs))[0]
    gp["rel_bias"] = bias_vjp(dbias)[0]
    return loss[0, 0], dh.reshape(b, s, dm), gw, gp


WEIGHTS = ("mix_norm", "ab_w_in", "hg_lb_logits", "hg_out_norm", "ab_w_out", "c_w_in", "q_norm", "k_norm", "sinks", "rel_bias",
           "c_w_out", "ffn_norm", "ffn_up", "ffn_conv", "ffn_conv_b", "ffn_down", "ple_norm", "ple_gate", "ple_proj")
SMALL = ("mix_norm", "hg_lb_logits", "hg_out_norm", "q_norm", "k_norm", "sinks", "rel_bias", "ffn_norm", "ffn_conv_b", "ple_norm")


def _step(x, p, target, w, m, v):
    kinds = dict(BIG)
    shard_shapes = {name: w[name].shape for name, _ in BIG}
    seg_rows = {name: _flat_rows(shard_shapes[name]) for name, _ in BIG}

    mats = [name for name, _ in BIG if name != "ffn_conv"]
    gathered = _all_gather(_flatten_shards([w[name] for name in mats], BF16), "gather_weights")
    conv_rows = _flatten_shards([w["ffn_conv"]], F32)
    gathered_conv = _all_gather(conv_rows, "gather_conv")
    full = {}
    r0 = 0
    for name in mats:
        seg = gathered[:, r0:r0 + seg_rows[name]]
        r0 += seg_rows[name]
        full[name] = _assemble(seg.reshape((N_DEV,) + shard_shapes[name]), kinds[name])
    full["ffn_conv"] = _assemble(gathered_conv.reshape(N_DEV, -1)[:, :math.prod(shard_shapes["ffn_conv"])]
                                 .reshape((N_DEV,) + shard_shapes["ffn_conv"]), "col")

    small = {name: w[name] for name in SMALL}
    loss, grad_x, gw, gp = _forward_backward(x, p, target, full, small)

    parts = jnp.concatenate([_disassemble(gw[name], kinds[name]).reshape(N_DEV, -1) for name, _ in BIG], axis=1)
    pad = -parts.shape[1] % (256 * FLAT_COLS)
    parts = jnp.pad(parts, ((0, 0), (0, pad))).reshape(N_DEV, -1, FLAT_COLS)
    mx, my, mc = _position()
    core = jnp.reshape(mc, (1,)).astype(jnp.int32)
    chip = jnp.reshape(2 * mx + my, (1,)).astype(jnp.int32)
    from_sibling = _rs_pair(parts, "reduce_pair")
    chip_sums = _rs_add_pair(parts, from_sibling, core, "reduce_pair_add")
    from_chips = _rs_chips(chip_sums, "reduce_chips")
    shard_sum = _rs_add_chips(chip_sums, from_chips, chip, "reduce_chips_add").reshape(-1)
    grads = {}
    e0 = 0
    for name, _ in BIG:
        cnt = math.prod(shard_shapes[name])
        grads[name] = shard_sum[e0:e0 + cnt].reshape(shard_shapes[name])
        e0 += cnt

    flat_small = jnp.concatenate([gp[name].reshape(-1) for name in SMALL] + [loss.reshape(1)])
    pad = -flat_small.shape[0] % (8 * LANES)
    small_rows = jnp.pad(flat_small, (0, pad)).reshape(-1, LANES)
    small_sum = _sum_devices(_all_gather(small_rows, "gather_small"), "sum_small").reshape(-1)
    e0 = 0
    for name in SMALL:
        cnt = math.prod(w[name].shape)
        grads[name] = small_sum[e0:e0 + cnt].reshape(w[name].shape)
        e0 += cnt
    loss = small_sum[e0]

    deltas, new_m, new_v = {}, {}, {}
    for name in WEIGHTS:
        shape = w[name].shape
        view = (-1, shape[-1]) if len(shape) > 1 else (1, -1)
        d_, m_, v_ = _adamw(w[name].reshape(view), grads[name].reshape(view), m[name].reshape(view), v[name].reshape(view), f"adamw_{name}")
        deltas[name], new_m[name], new_v[name] = d_.reshape(shape), m_.reshape(shape), v_.reshape(shape)
    return (loss, grad_x, *[grads[k] for k in WEIGHTS], *[deltas[k] for k in WEIGHTS],
            *[new_m[k] for k in WEIGHTS], *[new_v[k] for k in WEIGHTS])


def kernel(x, p, mix_norm, ab_w_in, hg_lb_logits, hg_out_norm, ab_w_out, c_w_in, q_norm, k_norm, sinks, rel_bias, c_w_out, ffn_norm, ffn_up, ffn_conv, ffn_conv_b, ffn_down, ple_norm, ple_gate, ple_proj, loss_target, m_mix_norm, m_ab_w_in, m_hg_lb_logits, m_hg_out_norm, m_ab_w_out, m_c_w_in, m_q_norm, m_k_norm, m_sinks, m_rel_bias, m_c_w_out, m_ffn_norm, m_ffn_up, m_ffn_conv, m_ffn_conv_b, m_ffn_down, m_ple_norm, m_ple_gate, m_ple_proj, v_mix_norm, v_ab_w_in, v_hg_lb_logits, v_hg_out_norm, v_ab_w_out, v_c_w_in, v_q_norm, v_k_norm, v_sinks, v_rel_bias, v_c_w_out, v_ffn_norm, v_ffn_up, v_ffn_conv, v_ffn_conv_b, v_ffn_down, v_ple_norm, v_ple_gate, v_ple_proj):
    w = dict(zip(WEIGHTS, (mix_norm, ab_w_in, hg_lb_logits, hg_out_norm, ab_w_out, c_w_in, q_norm, k_norm, sinks, rel_bias, c_w_out,
                           ffn_norm, ffn_up, ffn_conv, ffn_conv_b, ffn_down, ple_norm, ple_gate, ple_proj)))
    m = dict(zip(WEIGHTS, (m_mix_norm, m_ab_w_in, m_hg_lb_logits, m_hg_out_norm, m_ab_w_out, m_c_w_in, m_q_norm, m_k_norm, m_sinks,
                           m_rel_bias, m_c_w_out, m_ffn_norm, m_ffn_up, m_ffn_conv, m_ffn_conv_b, m_ffn_down, m_ple_norm, m_ple_gate,
                           m_ple_proj)))
    v = dict(zip(WEIGHTS, (v_mix_norm, v_ab_w_in, v_hg_lb_logits, v_hg_out_norm, v_ab_w_out, v_c_w_in, v_q_norm, v_k_norm, v_sinks,
                           v_rel_bias, v_c_w_out, v_ffn_norm, v_ffn_up, v_ffn_conv, v_ffn_conv_b, v_ffn_down, v_ple_norm, v_ple_gate,
                           v_ple_proj)))
    return _step(x, p, loss_target, w, m, v)
```

```python
import functools
import math

import numpy as np
import jax
import jax.numpy as jnp
from jax import lax
from jax.experimental import pallas as pl
from jax.experimental.pallas import tpu as pltpu

F32 = jnp.float32
BF16 = jnp.bfloat16

D_MODEL = 1024
DEPTH = 4
PLE_DIM = 256
EPS = 1e-6
SB_HEADS, SB_DIM = 8, 64
SB_WIDTH = SB_HEADS * SB_DIM
HG_HEADS, HG_DK, HG_DV = 4, 128, 128
HG_W = HG_HEADS * HG_DK
AB_IN = 3 * SB_WIDTH + 4 * HG_W
SW_HEADS, SW_KV_HEADS, SW_DIM = 16, 4, 64
SW_GROUP = SW_HEADS // SW_KV_HEADS
WINDOW = 128
C_IN = (SW_HEADS + 2 * SW_KV_HEADS) * SW_DIM
N_BUCKETS, MAX_DISTANCE = 32, 128
D_FF = 2816
N_DEV = 8

ADAM_LR, ADAM_B1, ADAM_B2, ADAM_EPS, ADAM_WD, ADAM_STEP = 0.001, 0.9, 0.999, 1e-08, 0.01, 10

LANES = 128
VMEM_LIMIT = 48 * 1024 * 1024

NN = (((1,), (0,)), ((), ()))
NT = (((1,), (1,)), ((), ()))
TN = (((0,), (0,)), ((), ()))


MXU_DTYPE = BF16


def _bf(x):
    return x.astype(MXU_DTYPE)


def _dot(a, b, dims=NN):
    return lax.dot_general(_bf(a), _bf(b), dims, preferred_element_type=F32)


def _split3(x):
    x1 = _bf(x)
    r = x - x1.astype(F32)
    x2 = _bf(r)
    x3 = _bf(r - x2.astype(F32))
    return x1, x2, x3


def _dot_exact_lhs01(m, x, terms=3):
    parts = _split3(x)[:terms]
    out = lax.dot_general(m, parts[0], NN, preferred_element_type=F32)
    for p_ in parts[1:]:
        out = out + lax.dot_general(m, p_, NN, preferred_element_type=F32)
    return out


def _dot_exact_rhs01(x, m, terms=2):
    parts = _split3(x)[:terms]
    out = lax.dot_general(parts[0], m, NN, preferred_element_type=F32)
    for p_ in parts[1:]:
        out = out + lax.dot_general(p_, m, NN, preferred_element_type=F32)
    return out


def _pick(n, target):
    best = None
    for t in range(LANES, target + 1, LANES):
        if n % t == 0:
            best = t
    return best or n


def _params(sem=None):
    return pltpu.CompilerParams(dimension_semantics=sem, vmem_limit_bytes=VMEM_LIMIT)


def _mm(a, b, mode, name, res=None, out_dtype=F32, layer=None, into=None):
    bshape = b.shape if layer is None else b.shape[1:]
    if mode == "nn":
        (M, K), (K2, N) = a.shape, bshape
    elif mode == "nt":
        (M, K), (N, K2) = a.shape, bshape
    else:
        (K, M), (K2, N) = a.shape, bshape
    assert K == K2, (a.shape, b.shape, mode)
    tm, tn, tk = _pick(M, 1024), _pick(N, 1024), _pick(K, 1536)
    nk = K // tk
    dims = {"nn": NN, "nt": NT, "tn": TN}[mode]
    a_spec = pl.BlockSpec((tk, tm), lambda i, j, k: (k, i)) if mode == "tn" else pl.BlockSpec((tm, tk), lambda i, j, k: (i, k))
    if layer is None:
        b_spec = pl.BlockSpec((tn, tk), lambda i, j, k: (j, k)) if mode == "nt" else pl.BlockSpec((tk, tn), lambda i, j, k: (k, j))
    elif mode == "nt":
        b_spec = pl.BlockSpec((None, tn, tk), lambda i, j, k: (layer, j, k))
    else:
        b_spec = pl.BlockSpec((None, tk, tn), lambda i, j, k: (layer, k, j))
    o_spec = pl.BlockSpec((tm, tn), lambda i, j, k: (i, j))
    has_res = res is not None

    def body(*refs):
        a_ref, b_ref = refs[0], refs[1]
        r_ref = refs[2] if has_res else None
        o_ref, acc_ref = refs[-2], refs[-1]
        k = pl.program_id(2)

        @pl.when(k == 0)
        def _():
            acc_ref[...] = jnp.zeros_like(acc_ref)

        acc_ref[...] += _dot(a_ref[...], b_ref[...], dims)

        @pl.when(k == nk - 1)
        def _():
            out = acc_ref[...]
            if has_res:
                out = out + r_ref[...]
            o_ref[...] = out.astype(out_dtype)

    in_specs = [a_spec, b_spec] + ([o_spec] if has_res else [])
    args = (a, b) + ((res,) if has_res else ())
    out_shape, aliases = jax.ShapeDtypeStruct((M, N), out_dtype), {}
    if into is not None:
        stack, slot = into
        assert stack.shape[1:] == (M, N) and stack.dtype == out_dtype and not has_res
        in_specs = in_specs + [pl.BlockSpec(memory_space=pl.ANY)]
        args = args + (stack,)
        o_spec = pl.BlockSpec((None, tm, tn), lambda i, j, k: (slot, i, j))
        out_shape, aliases = jax.ShapeDtypeStruct(stack.shape, out_dtype), {2: 0}

    def body_into(a_ref, b_ref, stack_ref, o_ref, acc_ref):
        body(a_ref, b_ref, o_ref, acc_ref)

    return pl.pallas_call(
        body if into is None else body_into, name=name, grid=(M // tm, N // tn, nk), in_specs=in_specs, out_specs=o_spec,
        out_shape=out_shape, scratch_shapes=[pltpu.VMEM((tm, tn), F32)], input_output_aliases=aliases,
        compiler_params=_params(("parallel", "parallel", "arbitrary")),
    )(*args)


def _row_tile(n, d):
    if n % 8:
        return n
    t = 8
    while t * 2 <= min(n, (256 * 1024) // d) and n % (t * 2) == 0:
        t *= 2
    return t


def _rms_fwd(x, g, name, out_dtype=F32):
    n, d = x.shape
    tm = _row_tile(n, d)

    def body(x_ref, g_ref, o_ref):
        xf = x_ref[...]
        r = lax.rsqrt(jnp.mean(xf * xf, axis=-1, keepdims=True) + EPS)
        o_ref[...] = (xf * r * g_ref[...]).astype(out_dtype)

    return pl.pallas_call(
        body, name=name, grid=(n // tm,),
        in_specs=[pl.BlockSpec((tm, d), lambda i: (i, 0)), pl.BlockSpec((1, d), lambda i: (0, 0))],
        out_specs=pl.BlockSpec((tm, d), lambda i: (i, 0)),
        out_shape=jax.ShapeDtypeStruct((n, d), out_dtype), compiler_params=_params(("parallel",)),
    )(x, g.reshape(1, d))


def _rms_bwd(x, g, dy, name, res=None):
    n, d = x.shape
    tm = _row_tile(n, d)
    has_res = res is not None

    def body(*refs):
        x_ref, g_ref, dy_ref = refs[:3]
        r_ref = refs[3] if has_res else None
        dx_ref, dg_ref = refs[-2:]
        xf = x_ref[...]
        r = lax.rsqrt(jnp.mean(xf * xf, axis=-1, keepdims=True) + EPS)
        xh = xf * r
        dyf = dy_ref[...].astype(F32)
        dxh = dyf * g_ref[...]
        dx = r * (dxh - xh * jnp.mean(dxh * xh, axis=-1, keepdims=True))
        if has_res:
            dx = dx + r_ref[...]
        dx_ref[...] = dx

        @pl.when(pl.program_id(0) == 0)
        def _():
            dg_ref[...] = jnp.zeros_like(dg_ref)

        dg_ref[...] += jnp.sum(dyf * xh, axis=0, keepdims=True)

    row = pl.BlockSpec((tm, d), lambda i: (i, 0))
    vec = pl.BlockSpec((1, d), lambda i: (0, 0))
    dx, dg = pl.pallas_call(
        body, name=name, grid=(n // tm,),
        in_specs=[row, vec, row] + ([row] if has_res else []),
        out_specs=[row, vec],
        out_shape=[jax.ShapeDtypeStruct((n, d), F32), jax.ShapeDtypeStruct((1, d), F32)],
        compiler_params=_params(("arbitrary",)),
    )(x, g.reshape(1, d), dy, *((res,) if has_res else ()))
    return dx, dg.reshape(d)


def _silu(x):
    return x * jax.nn.sigmoid(x)


def _gnorm_fwd(o, gate, gate_col, w, name):
    n, width = o.shape
    d = w.shape[0]
    tm = _row_tile(n, d)

    def body(o_ref, g_ref, w_ref, y_ref):
        of = o_ref[...]
        r = lax.rsqrt(jnp.mean(of * of, axis=-1, keepdims=True) + EPS)
        y_ref[...] = of * r * w_ref[...] * _silu(g_ref[...])

    row = pl.BlockSpec((tm, d), lambda i, h: (i, h))
    vec = pl.BlockSpec((1, d), lambda i, h: (0, 0))
    return pl.pallas_call(body, name=name, grid=(n // tm, width // d),
                          in_specs=[row, pl.BlockSpec((tm, d), lambda i, h: (i, gate_col + h)), vec], out_specs=row,
                          out_shape=jax.ShapeDtypeStruct((n, width), F32), compiler_params=_params(("parallel", "parallel")))(o, gate, w.reshape(1, d))


def _gnorm_bwd(o, gate, gate_col, w, dy, dy_col, name):
    n, width = o.shape
    d = w.shape[0]
    tm = _row_tile(n, d)

    def body(o_ref, g_ref, w_ref, dy_ref, do_ref, dgate_ref, dw_ref):
        of, gf, dyf = o_ref[...], g_ref[...], dy_ref[...]
        r = lax.rsqrt(jnp.mean(of * of, axis=-1, keepdims=True) + EPS)
        xh = of * r
        sg = jax.nn.sigmoid(gf)
        sil = gf * sg
        dnorm = dyf * sil
        dgate_ref[...] = dyf * xh * w_ref[...] * (sg * (1.0 + gf * (1.0 - sg)))
        dxh = dnorm * w_ref[...]
        do_ref[...] = r * (dxh - xh * jnp.mean(dxh * xh, axis=-1, keepdims=True))

        @pl.when((pl.program_id(0) == 0) & (pl.program_id(1) == 0))
        def _():
            dw_ref[...] = jnp.zeros_like(dw_ref)

        dw_ref[...] += jnp.sum(dnorm * xh, axis=0, keepdims=True)

    row = pl.BlockSpec((tm, d), lambda i, h: (i, h))
    vec = pl.BlockSpec((1, d), lambda i, h: (0, 0))
    do, dgate, dw = pl.pallas_call(
        body, name=name, grid=(n // tm, width // d),
        in_specs=[row, pl.BlockSpec((tm, d), lambda i, h: (i, gate_col + h)), vec, pl.BlockSpec((tm, d), lambda i, h: (i, dy_col + h))],
        out_specs=[row, row, vec],
        out_shape=[jax.ShapeDtypeStruct((n, width), F32)] * 2 + [jax.ShapeDtypeStruct((1, d), F32)],
        compiler_params=_params(("arbitrary", "arbitrary")),
    )(o, gate, w.reshape(1, d), dy)
    return do, dgate, dw.reshape(d)


def _sigmul_fwd(z, e, res, name):
    n, d = z.shape
    tm = _row_tile(n, d)

    def body(z_ref, e_ref, r_ref, o_ref):
        o_ref[...] = r_ref[...] + jax.nn.sigmoid(z_ref[...]) * e_ref[...]

    row = pl.BlockSpec((tm, d), lambda i: (i, 0))
    return pl.pallas_call(body, name=name, grid=(n // tm,), in_specs=[row] * 3, out_specs=row,
                          out_shape=jax.ShapeDtypeStruct((n, d), F32), compiler_params=_params(("parallel",)))(z, e, res)


def _sigmul_bwd(z, e, dy, name):
    n, d = z.shape
    tm = _row_tile(n, d)

    def body(z_ref, e_ref, dy_ref, dz_ref, de_ref):
        s = jax.nn.sigmoid(z_ref[...])
        dyf = dy_ref[...]
        dz_ref[...] = dyf * e_ref[...] * s * (1.0 - s)
        de_ref[...] = dyf * s

    row = pl.BlockSpec((tm, d), lambda i: (i, 0))
    return pl.pallas_call(body, name=name, grid=(n // tm,), in_specs=[row] * 3, out_specs=[row] * 2,
                          out_shape=[jax.ShapeDtypeStruct((n, d), F32)] * 2, compiler_params=_params(("parallel",)))(z, e, dy)


def _loss_fwd(y, target, name):
    n, d = y.shape
    tm = _row_tile(n, d)

    def body(y_ref, t_ref, l_ref, dy_ref):
        diff = y_ref[...] - t_ref[...]
        dy_ref[...] = diff * (1.0 / d)

        @pl.when(pl.program_id(0) == 0)
        def _():
            l_ref[...] = jnp.zeros_like(l_ref)

        part = jnp.sum(jnp.mean(diff * diff, axis=-1, keepdims=True), axis=0, keepdims=True)
        l_ref[...] += 0.5 * jnp.broadcast_to(part, l_ref.shape)

    row = pl.BlockSpec((tm, d), lambda i: (i, 0))
    vec = pl.BlockSpec((1, LANES), lambda i: (0, 0))
    return pl.pallas_call(body, name=name, grid=(n // tm,), in_specs=[row, row], out_specs=[vec, row],
                          out_shape=[jax.ShapeDtypeStruct((1, LANES), F32), jax.ShapeDtypeStruct((n, d), F32)],
                          compiler_params=_params(("arbitrary",)))(y, target)


SB_BLK = 128


def _sb_logits(qb, kb, qi, kj, scale, row, col):
    z = _dot(qb, kb, NT) * scale
    mask = (kj * SB_BLK + col) < (qi * SB_BLK + row)
    sp = jnp.maximum(z, 0.0) + jnp.log1p(jnp.exp(-jnp.abs(z)))
    lk = jnp.where(mask, -sp, 0.0)
    return mask, lk, z - sp


SB_PAIRS = SB_WIDTH // LANES


def _sb_spec(s, col):
    return pl.BlockSpec((s, LANES), lambda e, pr: (e, col + pr))


def _sb_fwd(proj, cols, b, s, name):
    nq = s // SB_BLK
    scale = SB_DIM ** -0.5

    def body(q_ref, k_ref, v_ref, o_ref, lt_ref):
        row = lax.broadcasted_iota(jnp.int32, (SB_BLK, SB_BLK), 0)
        col = lax.broadcasted_iota(jnp.int32, (SB_BLK, SB_BLK), 1)
        u_after = _bf(row > col)
        heads = [col < SB_DIM, col >= SB_DIM]

        def qloop(qi, _):
            q0 = pl.multiple_of(qi * SB_BLK, SB_BLK)
            qb = q_ref[pl.ds(q0, SB_BLK), :]
            qh = [jnp.where(hm, qb, 0.0) for hm in heads]

            def kloop(j, st):
                acc, carries = st[0], st[1:]
                kj = qi - j
                k0 = pl.multiple_of(kj * SB_BLK, SB_BLK)
                kb = k_ref[pl.ds(k0, SB_BLK), :]
                vb = v_ref[pl.ds(k0, SB_BLK), :]
                new = []
                for hm, qm, carry in zip(heads, qh, carries):
                    mask, lk, ls = _sb_logits(qm, kb, qi, kj, scale, row, col)
                    later = carry + _dot_exact_rhs01(lk, u_after)
                    w = jnp.where(mask, jnp.exp(ls + later), 0.0)
                    acc = acc + _dot(w, jnp.where(hm, vb, 0.0))
                    new.append(carry + jnp.sum(lk, axis=1, keepdims=True))
                return (acc, *new)

            z1 = jnp.zeros((SB_BLK, 1), F32)
            acc, ca, cb = lax.fori_loop(0, qi + 1, kloop, (jnp.zeros((SB_BLK, LANES), F32), z1, z1))
            o_ref[pl.ds(q0, SB_BLK), :] = acc
            lt_ref[pl.ds(q0, SB_BLK), :] = jnp.where(heads[0], ca, cb)
            return 0

        lax.fori_loop(0, nq, qloop, 0)

    out = _sb_spec(s, 0)
    return pl.pallas_call(body, name=name, grid=(b, SB_PAIRS), in_specs=[_sb_spec(s, c) for c in cols], out_specs=[out, out],
                          out_shape=[jax.ShapeDtypeStruct((b * s, SB_WIDTH), F32)] * 2,
                          compiler_params=_params(("parallel", "parallel")))(proj, proj, proj)


def _sb_bwd(proj, cols, ltot, do, do_col, b, s, name):
    nq = s // SB_BLK
    scale = SB_DIM ** -0.5

    def body(q_ref, k_ref, v_ref, lt_ref, do_ref, dq_ref, dk_ref, dv_ref):
        row = lax.broadcasted_iota(jnp.int32, (SB_BLK, SB_BLK), 0)
        col = lax.broadcasted_iota(jnp.int32, (SB_BLK, SB_BLK), 1)
        u_upto = _bf(row <= col)
        u_before = _bf(row < col)
        heads = [col < SB_DIM, col >= SB_DIM]
        dk_ref[...] = jnp.zeros_like(dk_ref)
        dv_ref[...] = jnp.zeros_like(dv_ref)

        def qloop(qi, _):
            q0 = pl.multiple_of(qi * SB_BLK, SB_BLK)
            qb = q_ref[pl.ds(q0, SB_BLK), :]
            dob = do_ref[pl.ds(q0, SB_BLK), :]
            ltb = lt_ref[pl.ds(q0, SB_BLK), :]
            qh = [jnp.where(hm, qb, 0.0) for hm in heads]
            doh = [jnp.where(hm, dob, 0.0) for hm in heads]
            lth = [jnp.sum(jnp.where(col == hi * SB_DIM, ltb, 0.0), axis=1, keepdims=True) for hi in range(len(heads))]

            def kloop(kj, st):
                dq, sums = st[0], st[1:]
                k0 = pl.multiple_of(kj * SB_BLK, SB_BLK)
                kb = k_ref[pl.ds(k0, SB_BLK), :]
                vb = v_ref[pl.ds(k0, SB_BLK), :]
                dkb = jnp.zeros((SB_BLK, LANES), F32)
                dvb = jnp.zeros((SB_BLK, LANES), F32)
                new = []
                for hi, hm in enumerate(heads):
                    cl, cg = sums[2 * hi], sums[2 * hi + 1]
                    mask, lk, ls = _sb_logits(qh[hi], kb, qi, kj, scale, row, col)
                    later = lth[hi] - (cl + _dot_exact_rhs01(lk, u_upto))
                    w = jnp.where(mask, jnp.exp(ls + later), 0.0)
                    g = _dot(doh[hi], vb, NT) * w
                    dvb = dvb + _dot(w, doh[hi], TN)
                    g_before = cg + _dot_exact_rhs01(g, u_before)
                    sig = jnp.exp(ls)
                    dz = jnp.where(mask, g * (1.0 - sig) - sig * g_before, 0.0) * scale
                    dq = dq + _dot(dz, jnp.where(hm, kb, 0.0))
                    dkb = dkb + _dot(dz, qh[hi], TN)
                    new += [cl + jnp.sum(lk, axis=1, keepdims=True), cg + jnp.sum(g, axis=1, keepdims=True)]
                dk_ref[pl.ds(k0, SB_BLK), :] += dkb
                dv_ref[pl.ds(k0, SB_BLK), :] += dvb
                return (dq, *new)

            z1 = jnp.zeros((SB_BLK, 1), F32)
            dq = lax.fori_loop(0, qi + 1, kloop, (jnp.zeros((SB_BLK, LANES), F32), z1, z1, z1, z1))[0]
            dq_ref[pl.ds(q0, SB_BLK), :] = dq
            return 0

        lax.fori_loop(0, nq, qloop, 0)

    out = _sb_spec(s, 0)
    return pl.pallas_call(body, name=name, grid=(b, SB_PAIRS),
                          in_specs=[_sb_spec(s, c) for c in cols] + [out, _sb_spec(s, do_col)], out_specs=[out] * 3,
                          out_shape=[jax.ShapeDtypeStruct((b * s, SB_WIDTH), F32)] * 3,
                          compiler_params=_params(("parallel", "parallel")))(proj, proj, proj, ltot, do)


HG_CHUNK = 64


def _hg_consts(c):
    levels = int(math.log2(c))
    t = np.arange(c)
    tri = (t[:, None] >= t[None, :]).astype(np.float32)
    psel = np.zeros((levels, c, c), np.float32)
    masks = np.zeros((levels + 1, c, c), np.float32)
    for l in range(levels):
        n = c >> (l + 1)
        blk = t // (2 * n)
        psel[l, t, blk * 2 * n + n - 1] = 1.0
        upper = (t % (2 * n)) >= n
        masks[l] = (blk[:, None] == blk[None, :]) & upper[:, None] & (~upper)[None, :]
    masks[levels] = np.eye(c)
    psel = psel.reshape(levels * c, c)
    return levels, jnp.asarray(tri), jnp.asarray(psel), jnp.asarray(masks), jnp.asarray(tri.T.copy()), jnp.asarray(psel.T.copy())


def _hg_elem(qv, fv, lbv):
    sig = jax.nn.sigmoid(fv)
    lf = jnp.log(lbv + (1.0 - lbv) * sig)
    kk = (1.0 - lbv) * jax.nn.sigmoid(-fv)
    qf = qv * jax.nn.sigmoid(qv)
    return qf, kk, lf


def _col_bcast(rowvec):
    n = rowvec.shape[1]
    return jnp.transpose(jnp.broadcast_to(rowvec, (n, n)))


def _hg_chunk(qf, kk, lf, iv, state, tri, psel, m_ref, c, levels):
    b = _dot_exact_lhs01(tri, lf)
    bl = b[c - 1:c, :]
    eb = jnp.exp(b)
    qi = qf * eb
    bsel = _dot_exact_lhs01(psel, b)
    scores = jnp.where(m_ref[levels] > 0, _dot(qf, kk, NT), 0.0)
    lev = []
    for l in range(levels):
        bs = bsel[l * c:(l + 1) * c]
        eq = jnp.exp(jnp.minimum(b - bs, 0.0))
        ek = jnp.exp(jnp.minimum(bs - b, 0.0))
        ql, kl = qf * eq, kk * ek
        scores = scores + jnp.where(m_ref[l] > 0, _dot(ql, kl, NT), 0.0)
        lev.append((eq, ek, ql, kl))
    o = _dot(qi, state) + _dot(scores, iv)
    ebl = jnp.exp(bl - b)
    kd = kk * ebl
    decay = _col_bcast(jnp.exp(bl))
    new_state = decay * state + _dot(kd, iv, TN)
    return o, new_state, (eb, qi, scores, lev, ebl, kd, decay)


def _hg_fwd(proj, cols, lb, b, s, name):
    nh, d = lb.shape
    bh = b * nh
    c = HG_CHUNK
    nc = s // c
    levels, tri, psel, masks, _, _ = _hg_consts(c)

    def body(q_ref, f_ref, i_ref, lb_ref, tri_ref, psel_ref, m_ref, o_ref, st_ref):
        lbv = jnp.broadcast_to(lb_ref[0], (c, d))
        tri_v, psel_v = _bf(tri_ref[...]), _bf(psel_ref[...])

        def chunk(ci, state):
            r0 = pl.multiple_of(ci * c, c)
            qf, kk, lf = _hg_elem(q_ref[pl.ds(r0, c), :], f_ref[pl.ds(r0, c), :], lbv)
            st_ref[0, ci] = state
            o, state, _ = _hg_chunk(qf, kk, lf, i_ref[pl.ds(r0, c), :], state, tri_v, psel_v, m_ref, c, levels)
            o_ref[pl.ds(r0, c), :] = o
            return state

        lax.fori_loop(0, nc, chunk, jnp.zeros((d, d), F32))

    full = lambda a: pl.BlockSpec(a.shape, lambda e, hd: (0,) * a.ndim)
    return pl.pallas_call(
        body, name=name, grid=(b, nh),
        in_specs=[_hg_seq(s, d, cols[0]), _hg_seq(s, d, cols[1]), _hg_seq(s, d, cols[2]),
                  pl.BlockSpec((1, 1, d), lambda e, hd: (hd, 0, 0)), full(tri), full(psel), full(masks)],
        out_specs=[_hg_seq(s, d, 0), pl.BlockSpec((1, nc, d, d), lambda e, hd: (e * nh + hd, 0, 0, 0))],
        out_shape=[jax.ShapeDtypeStruct((b * s, nh * d), F32), jax.ShapeDtypeStruct((bh, nc, d, d), F32)],
        compiler_params=_params(("parallel", "parallel")),
    )(proj, proj, proj, lb.reshape(nh, 1, d), tri, psel, masks)


def _hg_seq(s, d, col):
    return pl.BlockSpec((s, d), lambda e, hd: (e, col + hd))


def _hg_bwd(proj, cols, lb, states, do, b, s, name):
    nh, d = lb.shape
    bh = b * nh
    c = HG_CHUNK
    nc = s // c
    levels, tri, psel, masks, tri_t, psel_t = _hg_consts(c)

    def body(q_ref, f_ref, i_ref, lb_ref, st_ref, do_ref, tri_ref, psel_ref, m_ref, trit_ref, pselt_ref,
             dq_ref, df_ref, di_ref, dlb_ref):
        lbv = jnp.broadcast_to(lb_ref[0], (c, d))
        tri_v, psel_v = _bf(tri_ref[...]), _bf(psel_ref[...])
        trit_v, pselt_v = _bf(trit_ref[...]), _bf(pselt_ref[...])
        last_row = lax.broadcasted_iota(jnp.int32, (c, d), 0) == c - 1

        def chunk(step, carry):
            ds_out, dlb = carry
            ci = nc - 1 - step
            r0 = pl.multiple_of(ci * c, c)
            qv, fv, iv = q_ref[pl.ds(r0, c), :], f_ref[pl.ds(r0, c), :], i_ref[pl.ds(r0, c), :]
            dov = do_ref[pl.ds(r0, c), :]
            state = st_ref[0, ci]
            (qf, kk, lf), elem_vjp = jax.vjp(_hg_elem, qv, fv, lbv)
            _, _, (eb, qi, scores, lev, ebl, kd, decay) = _hg_chunk(qf, kk, lf, iv, state, tri_v, psel_v, m_ref, c, levels)

            dscores = _dot(dov, iv, NT)
            di_ref[pl.ds(r0, c), :] = _dot(scores, dov, TN) + _dot(kd, ds_out)
            dqi = _dot(dov, state, NT)
            ds_in = decay * ds_out + _dot(qi, dov, TN)
            dkd = _dot(iv, ds_out, NT)
            dqf = dqi * eb
            dkk = dkd * ebl
            tkd = dkd * kd
            db = dqi * qi - tkd
            dbl = jnp.sum(tkd, axis=0, keepdims=True) + _col_bcast_t(jnp.sum(ds_out * decay * state, axis=1, keepdims=True))
            dsd = jnp.where(m_ref[levels] > 0, dscores, 0.0)
            dqf = dqf + _dot(dsd, kk)
            dkk = dkk + _dot(dsd, qf, TN)
            dbsel = []
            for l in range(levels):
                eq, ek, ql, kl = lev[l]
                dsl = jnp.where(m_ref[l] > 0, dscores, 0.0)
                dql = _dot(dsl, kl)
                dkl = _dot(dsl, ql, TN)
                dqf = dqf + dql * eq
                dkk = dkk + dkl * ek
                diff = dql * ql - dkl * kl
                db = db + diff
                dbsel.append(-diff)
            db = db + _dot_exact_lhs01(pselt_v, jnp.concatenate(dbsel, axis=0))
            db = db + jnp.where(last_row, dbl, 0.0)
            dlf = _dot_exact_lhs01(trit_v, db)
            dq, df, dlb_c = elem_vjp((dqf, dkk, dlf))
            dq_ref[pl.ds(r0, c), :] = dq
            df_ref[pl.ds(r0, c), :] = df
            return ds_in, dlb + jnp.sum(dlb_c, axis=0, keepdims=True)

        _, dlb = lax.fori_loop(0, nc, chunk, (jnp.zeros((d, d), F32), jnp.zeros((1, d), F32)))
        dlb_ref[0] = dlb

    seq = _hg_seq(s, d, 0)
    full = lambda a: pl.BlockSpec(a.shape, lambda e, hd: (0,) * a.ndim)
    return pl.pallas_call(
        body, name=name, grid=(b, nh),
        in_specs=[_hg_seq(s, d, cols[0]), _hg_seq(s, d, cols[1]), _hg_seq(s, d, cols[2]),
                  pl.BlockSpec((1, 1, d), lambda e, hd: (hd, 0, 0)),
                  pl.BlockSpec((1, nc, d, d), lambda e, hd: (e * nh + hd, 0, 0, 0)), seq,
                  full(tri), full(psel), full(masks), full(tri_t), full(psel_t)],
        out_specs=[seq, seq, seq, pl.BlockSpec((1, 1, d), lambda e, hd: (e * nh + hd, 0, 0))],
        out_shape=[jax.ShapeDtypeStruct((b * s, nh * d), F32)] * 3 + [jax.ShapeDtypeStruct((bh, 1, d), F32)],
        compiler_params=_params(("parallel", "parallel")),
    )(proj, proj, proj, lb.reshape(nh, 1, d), states, do, tri, psel, masks, tri_t, psel_t)


def _col_bcast_t(colvec):
    n = colvec.shape[0]
    return jnp.transpose(jnp.broadcast_to(colvec, (n, n)))[0:1, :]


def _swa_probs(qg, kb, bias, sink, valid, scale):
    logits = _dot(qg, kb, NT) * scale + bias
    logits = jnp.where(valid, logits, -jnp.inf)
    m = jnp.maximum(jnp.max(logits, axis=-1, keepdims=True), sink)
    e = jnp.exp(logits - m)
    es = jnp.exp(sink - m)
    den = jnp.sum(e, axis=-1, keepdims=True) + es
    return e / den, es / den


def _swa_valid(n):
    w = WINDOW
    row = lax.broadcasted_iota(jnp.int32, (w, 2 * w), 0)
    col = lax.broadcasted_iota(jnp.int32, (w, 2 * w), 1)
    dist = row + w - col
    return (dist >= 0) & (dist < w) & ((col >= w) | (n > 0))


def _swa_specs(b, g, s, d):
    w = WINDOW
    q_spec = pl.BlockSpec((1, 1, g, w, d), lambda h, bi, n: (bi, h, 0, n, 0))
    kp_spec = pl.BlockSpec((1, 1, w, d), lambda h, bi, n: (bi, h, jnp.maximum(n - 1, 0), 0))
    kc_spec = pl.BlockSpec((1, 1, w, d), lambda h, bi, n: (bi, h, n, 0))
    bias_spec = pl.BlockSpec((1, g, w, 2 * w), lambda h, bi, n: (h, 0, 0, 0))
    sink_spec = pl.BlockSpec(memory_space=pltpu.SMEM)
    return q_spec, kp_spec, kc_spec, bias_spec, sink_spec


def _swa_fwd(q, k, v, sinks, bias, name):
    b, kvh, g, s, d = q.shape
    w = WINDOW
    scale = d ** -0.5
    q_spec, kp_spec, kc_spec, bias_spec, sink_spec = _swa_specs(b, g, s, d)

    def body(q_ref, kp_ref, kc_ref, vp_ref, vc_ref, bias_ref, sink_ref, o_ref):
        h, n = pl.program_id(0), pl.program_id(2)
        valid = _swa_valid(n)
        kb = jnp.concatenate([kp_ref[0, 0], kc_ref[0, 0]], axis=0)
        vb = jnp.concatenate([vp_ref[0, 0], vc_ref[0, 0]], axis=0)
        for gi in range(g):
            p, _ = _swa_probs(q_ref[0, 0, gi], kb, bias_ref[0, gi], sink_ref[h * g + gi], valid, scale)
            o_ref[0, 0, gi] = _dot(p, vb)

    return pl.pallas_call(
        body, name=name, grid=(kvh, b, s // w),
        in_specs=[q_spec, kp_spec, kc_spec, kp_spec, kc_spec, bias_spec, sink_spec], out_specs=q_spec,
        out_shape=jax.ShapeDtypeStruct(q.shape, F32), compiler_params=_params(("parallel", "parallel", "arbitrary")),
    )(q, k, k, v, v, bias, sinks)


def _swa_bwd(q, k, v, sinks, bias, do, name):
    b, kvh, g, s, d = q.shape
    w = WINDOW
    scale = d ** -0.5
    q_spec, kp_spec, kc_spec, bias_spec, sink_spec = _swa_specs(b, g, s, d)
    kv_acc = pl.BlockSpec((1, 1, s, d), lambda h, bi, n: (bi, h, 0, 0))
    dsink_spec = pl.BlockSpec((1, g, LANES), lambda h, bi, n: (h, 0, 0))

    def body(q_ref, kp_ref, kc_ref, vp_ref, vc_ref, bias_ref, sink_ref, do_ref, dq_ref, dk_ref, dv_ref, dbias_ref, dsink_ref):
        h, bi, n = pl.program_id(0), pl.program_id(1), pl.program_id(2)
        valid = _swa_valid(n)
        kb = jnp.concatenate([kp_ref[0, 0], kc_ref[0, 0]], axis=0)
        vb = jnp.concatenate([vp_ref[0, 0], vc_ref[0, 0]], axis=0)

        @pl.when(n == 0)
        def _():
            dk_ref[...] = jnp.zeros_like(dk_ref)
            dv_ref[...] = jnp.zeros_like(dv_ref)

        @pl.when((n == 0) & (bi == 0))
        def _():
            dbias_ref[...] = jnp.zeros_like(dbias_ref)
            dsink_ref[...] = jnp.zeros_like(dsink_ref)

        dkb = jnp.zeros((2 * w, d), F32)
        dvb = jnp.zeros((2 * w, d), F32)
        for gi in range(g):
            qg, dog = q_ref[0, 0, gi], do_ref[0, 0, gi]
            p, ps = _swa_probs(qg, kb, bias_ref[0, gi], sink_ref[h * g + gi], valid, scale)
            dp = _dot(dog, vb, NT)
            delta = jnp.sum(p * dp, axis=-1, keepdims=True)
            dl = p * (dp - delta)
            dq_ref[0, 0, gi] = _dot(dl, kb) * scale
            dkb = dkb + _dot(dl, qg, TN) * scale
            dvb = dvb + _dot(p, dog, TN)
            dbias_ref[0, gi] += dl
            dsink_ref[0, gi:gi + 1, :] += jnp.broadcast_to(jnp.sum(-ps * delta, axis=0, keepdims=True), (1, LANES))

        c0 = pl.multiple_of(n * w, w)
        dk_ref[0, 0, pl.ds(c0, w), :] += dkb[w:]
        dv_ref[0, 0, pl.ds(c0, w), :] += dvb[w:]

        @pl.when(n > 0)
        def _():
            p0 = pl.multiple_of((n - 1) * w, w)
            dk_ref[0, 0, pl.ds(p0, w), :] += dkb[:w]
            dv_ref[0, 0, pl.ds(p0, w), :] += dvb[:w]

    return pl.pallas_call(
        body, name=name, grid=(kvh, b, s // w),
        in_specs=[q_spec, kp_spec, kc_spec, kp_spec, kc_spec, bias_spec, sink_spec, q_spec],
        out_specs=[q_spec, kv_acc, kv_acc, bias_spec, dsink_spec],
        out_shape=[jax.ShapeDtypeStruct(q.shape, F32), jax.ShapeDtypeStruct(k.shape, F32), jax.ShapeDtypeStruct(k.shape, F32),
                   jax.ShapeDtypeStruct(bias.shape, F32), jax.ShapeDtypeStruct((kvh, g, LANES), F32)],
        compiler_params=_params(("arbitrary", "arbitrary", "arbitrary")),
    )(q, k, k, v, v, bias, sinks, do)


def _t5_bias(rel_bias):
    t = np.arange(WINDOW)[:, None]
    s = np.arange(2 * WINDOW)[None, :]
    dist = t + WINDOW - s
    max_exact = N_BUCKETS // 2
    large = max_exact + (np.log(np.maximum(dist, max_exact) / max_exact) / math.log(MAX_DISTANCE / max_exact)
                         * (N_BUCKETS - max_exact)).astype(np.int32)
    large = np.minimum(large, N_BUCKETS - 1)
    bucket = np.where(dist < max_exact, np.maximum(dist, 0), large).astype(np.int32)
    onehot = jnp.asarray(np.eye(N_BUCKETS, dtype=np.float32)[bucket])
    bias = jnp.einsum("tsb,bh->hts", onehot, rel_bias.astype(F32), precision=lax.Precision.HIGHEST)
    return bias.reshape(SW_KV_HEADS, SW_GROUP, WINDOW, 2 * WINDOW)


CONV_W = 3


def _shift_down(x, k):
    row = lax.broadcasted_iota(jnp.int32, x.shape, 0)
    return jnp.where(row >= k, pltpu.roll(x, k, axis=0), 0.0)


def _shift_up(x, k):
    n = x.shape[0]
    row = lax.broadcasted_iota(jnp.int32, x.shape, 0)
    return jnp.where(row < n - k, pltpu.roll(x, n - k, axis=0), 0.0)


def _conv3(u, w, bvec):
    return w[0:1] * _shift_down(u, 2) + w[1:2] * _shift_down(u, 1) + w[2:3] * u + bvec


def _convglu_fwd(u, w, bvec, name, out_dtype=F32):
    b, s, f2 = u.shape
    f = f2 // 2
    tc = _pick(f, 256)
    nt = f // tc

    def body(ug_ref, uu_ref, wg_ref, wu_ref, bg_ref, bu_ref, o_ref):
        cg = _conv3(ug_ref[0], wg_ref[...], bg_ref[...])
        cu = _conv3(uu_ref[0], wu_ref[...], bu_ref[...])
        o_ref[0] = (_silu(cg) * cu).astype(out_dtype)

    ug = pl.BlockSpec((1, s, tc), lambda j, bi: (bi, 0, j))
    uu = pl.BlockSpec((1, s, tc), lambda j, bi: (bi, 0, j + nt))
    wg = pl.BlockSpec((CONV_W, tc), lambda j, bi: (0, j))
    wu = pl.BlockSpec((CONV_W, tc), lambda j, bi: (0, j + nt))
    bg = pl.BlockSpec((1, tc), lambda j, bi: (0, j))
    bu = pl.BlockSpec((1, tc), lambda j, bi: (0, j + nt))
    bv = bvec.reshape(1, f2)
    return pl.pallas_call(body, name=name, grid=(nt, b), in_specs=[ug, uu, wg, wu, bg, bu], out_specs=ug,
                          out_shape=jax.ShapeDtypeStruct((b, s, f), out_dtype),
                          compiler_params=_params(("parallel", "parallel")))(u, u, w, w, bv, bv)


def _convglu_bwd(u, w, bvec, dact, name):
    b, s, f2 = u.shape
    f = f2 // 2
    tc = LANES
    nt = f // tc

    def taps(dc, uv):
        rows = [jnp.sum(dc * _shift_down(uv, 2), axis=0, keepdims=True), jnp.sum(dc * _shift_down(uv, 1), axis=0, keepdims=True),
                jnp.sum(dc * uv, axis=0, keepdims=True), jnp.sum(dc, axis=0, keepdims=True)]
        return jnp.concatenate(rows + [jnp.zeros((4, tc), F32)], axis=0)

    def back(dc, wv):
        return wv[2:3] * dc + wv[1:2] * _shift_up(dc, 1) + wv[0:1] * _shift_up(dc, 2)

    def body(ug_ref, uu_ref, wg_ref, wu_ref, bg_ref, bu_ref, da_ref, dug_ref, duu_ref, dwg_ref, dwu_ref):
        ugv, uuv, da = ug_ref[0], uu_ref[0], da_ref[0]
        cg = _conv3(ugv, wg_ref[...], bg_ref[...])
        cu = _conv3(uuv, wu_ref[...], bu_ref[...])
        sg = jax.nn.sigmoid(cg)
        dcu = da * (cg * sg)
        dcg = da * cu * (sg * (1.0 + cg * (1.0 - sg)))
        dug_ref[0] = back(dcg, wg_ref[...])
        duu_ref[0] = back(dcu, wu_ref[...])

        @pl.when(pl.program_id(1) == 0)
        def _():
            dwg_ref[...] = jnp.zeros_like(dwg_ref)
            dwu_ref[...] = jnp.zeros_like(dwu_ref)

        dwg_ref[...] += taps(dcg, ugv)
        dwu_ref[...] += taps(dcu, uuv)

    ug = pl.BlockSpec((1, s, tc), lambda j, bi: (bi, 0, j))
    uu = pl.BlockSpec((1, s, tc), lambda j, bi: (bi, 0, j + nt))
    wg = pl.BlockSpec((CONV_W, tc), lambda j, bi: (0, j))
    wu = pl.BlockSpec((CONV_W, tc), lambda j, bi: (0, j + nt))
    bg = pl.BlockSpec((1, tc), lambda j, bi: (0, j))
    bu = pl.BlockSpec((1, tc), lambda j, bi: (0, j + nt))
    acc = pl.BlockSpec((8, tc), lambda j, bi: (0, j))
    bv = bvec.reshape(1, f2)
    return pl.pallas_call(
        body, name=name, grid=(nt, b), in_specs=[ug, uu, wg, wu, bg, bu, ug], out_specs=[ug, ug, acc, acc],
        out_shape=[jax.ShapeDtypeStruct((b, s, f), F32)] * 2 + [jax.ShapeDtypeStruct((8, f), F32)] * 2,
        compiler_params=_params(("parallel", "arbitrary")),
    )(u, u, w, w, bv, bv, dact)


def _adamw(w, g, m, v, name):
    r, c = w.shape
    tr = _row_tile(r, c)
    c1 = 1.0 - ADAM_B1 ** ADAM_STEP
    c2 = 1.0 - ADAM_B2 ** ADAM_STEP

    def body(w_ref, g_ref, m_ref, v_ref, d_ref, mo_ref, vo_ref):
        gv = g_ref[...]
        mn = ADAM_B1 * m_ref[...] + (1.0 - ADAM_B1) * gv
        vn = ADAM_B2 * v_ref[...] + (1.0 - ADAM_B2) * (gv * gv)
        d_ref[...] = -ADAM_LR * ((mn / c1) / (jnp.sqrt(vn / c2) + ADAM_EPS) + ADAM_WD * w_ref[...])
        mo_ref[...] = mn
        vo_ref[...] = vn

    blk = pl.BlockSpec((tr, c), lambda i: (i, 0))
    return pl.pallas_call(body, name=name, grid=(r // tr,), in_specs=[blk] * 4, out_specs=[blk] * 3,
                          out_shape=[jax.ShapeDtypeStruct((r, c), F32)] * 3, compiler_params=_params(("parallel",)))(w, g, m, v)


MESH = pl.DeviceIdType.MESH
ANY = pl.BlockSpec(memory_space=pl.ANY)


def _position():
    return lax.axis_index("x"), lax.axis_index("y"), lax.axis_index("c")


def _all_gather(x, name):
    r, c = x.shape

    def body(x_ref, out_ref, send_sems, recv_sems, local_sem):
        mx, my, mc = _position()
        me, sibling = (mx, my, mc), (mx, my, 1 - mc)
        chips = [(1 - mx, my), (mx, 1 - my), (1 - mx, 1 - my)]

        def slot(px, py, pc):
            return out_ref.at[4 * px + 2 * py + pc]

        def copy(k, block, to, src=None):
            return pltpu.make_async_remote_copy(
                src_ref=slot(*block) if src is None else src, dst_ref=slot(*block),
                send_sem=send_sems.at[k], recv_sem=recv_sems.at[k], device_id=to, device_id_type=MESH)

        mine = pltpu.make_async_copy(x_ref, slot(*me), local_sem.at[0])
        mine.start()
        first = [copy(0, me, sibling, src=x_ref)]
        first += [copy(1 + j, me, (*chip, mc), src=x_ref) for j, chip in enumerate(chips)]
        for cp in first:
            cp.start()
        passed = [copy(4 + j, (*chip, mc), sibling) for j, chip in enumerate(chips)]
        for j, chip in enumerate(chips):
            copy(1 + j, (*chip, mc), me).wait_recv()
            passed[j].start()
        copy(0, sibling, me).wait_recv()
        for j, chip in enumerate(chips):
            copy(4 + j, (*chip, 1 - mc), me).wait_recv()
        for cp in first + passed:
            cp.wait_send()
        mine.wait()

    return pl.pallas_call(
        body, name=name, out_shape=jax.ShapeDtypeStruct((N_DEV, r, c), x.dtype), in_specs=[ANY], out_specs=ANY,
        scratch_shapes=[pltpu.SemaphoreType.DMA((7,)), pltpu.SemaphoreType.DMA((7,)), pltpu.SemaphoreType.DMA((1,))],
    )(x)


def _dev_rows(ref, dev, a):
    return ref.at[:, pl.ds(pl.multiple_of(dev * a, 16), a), :]


def _all_gather_rows(shards, name):
    nt = len(shards)

    def body(*refs):
        x_refs, out_refs = refs[:nt], refs[nt:2 * nt]
        send_sems, recv_sems, local_sems = refs[2 * nt:]
        mx, my, mc = _position()
        me, sibling = (mx, my, mc), (mx, my, 1 - mc)
        chips = [(1 - mx, my), (mx, 1 - my), (1 - mx, 1 - my)]

        def slot(t, px, py, pc):
            return _dev_rows(out_refs[t], 4 * px + 2 * py + pc, shards[t].shape[1])

        def copy(t, k, block, to, src=None):
            return pltpu.make_async_remote_copy(
                src_ref=slot(t, *block) if src is None else src, dst_ref=slot(t, *block),
                send_sem=send_sems.at[7 * t + k], recv_sem=recv_sems.at[7 * t + k], device_id=to, device_id_type=MESH)

        mine = [pltpu.make_async_copy(x_refs[t], slot(t, *me), local_sems.at[t]) for t in range(nt)]
        first = [copy(t, 0, me, sibling, src=x_refs[t]) for t in range(nt)]
        first += [copy(t, 1 + j, me, (*chip, mc), src=x_refs[t]) for j, chip in enumerate(chips) for t in range(nt)]
        for cp in mine + first:
            cp.start()
        passed = []
        for j, chip in enumerate(chips):
            for t in range(nt):
                copy(t, 1 + j, (*chip, mc), me).wait_recv()
                fwd = copy(t, 4 + j, (*chip, mc), sibling)
                fwd.start()
                passed.append(fwd)
        for t in range(nt):
            copy(t, 0, sibling, me).wait_recv()
        for j, chip in enumerate(chips):
            for t in range(nt):
                copy(t, 4 + j, (*chip, 1 - mc), me).wait_recv()
        for cp in first + passed:
            cp.wait_send()
        for cp in mine:
            cp.wait()

    out_shape = [jax.ShapeDtypeStruct((x.shape[0], N_DEV * x.shape[1], x.shape[2]), x.dtype) for x in shards]
    return pl.pallas_call(
        body, name=name, out_shape=out_shape, in_specs=[ANY] * nt, out_specs=[ANY] * nt,
        scratch_shapes=[pltpu.SemaphoreType.DMA((7 * nt,)), pltpu.SemaphoreType.DMA((7 * nt,)), pltpu.SemaphoreType.DMA((nt,))],
    )(*shards)


def _rs_pair(gs, name):
    nt = len(gs)

    def body(*refs):
        g_refs, a_refs = refs[:nt], refs[nt:2 * nt]
        send_sems, recv_sems = refs[2 * nt:]
        mx, my, mc = _position()
        copies = [pltpu.make_async_remote_copy(
            src_ref=_dev_rows(g_refs[t], 2 * j + 1 - mc, gs[t].shape[1] // N_DEV), dst_ref=a_refs[t].at[j],
            send_sem=send_sems.at[4 * t + j], recv_sem=recv_sems.at[4 * t + j],
            device_id=(mx, my, 1 - mc), device_id_type=MESH) for t in range(nt) for j in range(4)]
        for cp in copies:
            cp.start()
        for cp in copies:
            cp.wait()

    out_shape = [jax.ShapeDtypeStruct((4, g.shape[0], g.shape[1] // N_DEV, g.shape[2]), g.dtype) for g in gs]
    return pl.pallas_call(
        body, name=name, out_shape=out_shape, in_specs=[ANY] * nt, out_specs=[ANY] * nt,
        scratch_shapes=[pltpu.SemaphoreType.DMA((4 * nt,)), pltpu.SemaphoreType.DMA((4 * nt,))],
    )(*gs)


def _rs_chips(ps, name):
    nt = len(ps)

    def body(*refs):
        p_refs, b_refs = refs[:nt], refs[nt:2 * nt]
        send_sems, recv_sems = refs[2 * nt:]
        mx, my, mc = _position()
        chips = [(1 - mx, my), (mx, 1 - my), (1 - mx, 1 - my)]
        copies = [pltpu.make_async_remote_copy(
            src_ref=p_refs[t].at[2 * cx + cy], dst_ref=b_refs[t].at[k],
            send_sem=send_sems.at[3 * t + k], recv_sem=recv_sems.at[3 * t + k],
            device_id=(cx, cy, mc), device_id_type=MESH) for t in range(nt) for k, (cx, cy) in enumerate(chips)]
        for cp in copies:
            cp.start()
        for cp in copies:
            cp.wait()

    out_shape = [jax.ShapeDtypeStruct((3,) + p.shape[1:], p.dtype) for p in ps]
    return pl.pallas_call(
        body, name=name, out_shape=out_shape, in_specs=[ANY] * nt, out_specs=[ANY] * nt,
        scratch_shapes=[pltpu.SemaphoreType.DMA((3 * nt,)), pltpu.SemaphoreType.DMA((3 * nt,))],
    )(*ps)


def _div_tile(a, b):
    best = 16
    for t in range(16, a + 1, 16):
        if a % t == 0 and t * b * 4 <= 2 * 1024 * 1024:
            best = t
    return best


def _rs_add_pair(g, a, core, name):
    l, a8, b = g.shape
    rows = a8 // N_DEV
    ta = _div_tile(rows, b)

    def body(core_ref, g_ref, a_ref, p_ref):
        p_ref[...] = (g_ref[...] + a_ref[...]).astype(BF16)

    grid_spec = pltpu.PrefetchScalarGridSpec(
        num_scalar_prefetch=1, grid=(4, l, rows // ta),
        in_specs=[pl.BlockSpec((1, 1, ta, b), lambda j, li, i, core_ref: (li, 2 * j + core_ref[0], i, 0)),
                  pl.BlockSpec((1, 1, ta, b), lambda j, li, i, core_ref: (j, li, i, 0))],
        out_specs=pl.BlockSpec((1, 1, ta, b), lambda j, li, i, core_ref: (j, li, i, 0)))
    return pl.pallas_call(body, name=name, grid_spec=grid_spec, out_shape=jax.ShapeDtypeStruct((4, l, rows, b), BF16),
                          compiler_params=_params(("parallel", "parallel", "parallel")))(core, g.reshape(l, N_DEV, rows, b), a)


def _rs_final(g, a, bsum, where, name):
    l, a8, b = g.shape
    rows = a8 // N_DEV
    ta = _div_tile(rows, b)

    def body(where_ref, g_ref, a_ref, b_ref, o_ref):
        own = g_ref[0, 0] + a_ref[0, 0]
        o_ref[0] = ((own + b_ref[0, 0].astype(F32)) + b_ref[1, 0].astype(F32)) + b_ref[2, 0].astype(F32)

    grid_spec = pltpu.PrefetchScalarGridSpec(
        num_scalar_prefetch=1, grid=(l, rows // ta),
        in_specs=[pl.BlockSpec((1, 1, ta, b), lambda li, i, w_ref: (li, w_ref[0], i, 0)),
                  pl.BlockSpec((1, 1, ta, b), lambda li, i, w_ref: (w_ref[1], li, i, 0)),
                  pl.BlockSpec((3, 1, ta, b), lambda li, i, w_ref: (0, li, i, 0))],
        out_specs=pl.BlockSpec((1, ta, b), lambda li, i, w_ref: (li, i, 0)))
    return pl.pallas_call(body, name=name, grid_spec=grid_spec, out_shape=jax.ShapeDtypeStruct((l, rows, b), F32),
                          compiler_params=_params(("parallel", "parallel")))(where, g.reshape(l, N_DEV, rows, b), a, bsum)


def _sum_devices(x, name):
    _, r, c = x.shape

    def body(x_ref, o_ref):
        acc = x_ref[0]
        for d in range(1, N_DEV):
            acc = acc + x_ref[d]
        o_ref[...] = acc

    return pl.pallas_call(body, name=name, out_shape=jax.ShapeDtypeStruct((r, c), F32))(x)


BIG = (("ab_w_in", "col"), ("ab_w_out", "row"), ("c_w_in", "col"), ("c_w_out", "row"),
       ("ffn_up", "col"), ("ffn_down", "row"), ("ple_gate", "row"), ("ple_proj", "col"))
KIND = dict(BIG)


def _row_block(shard, kind):
    return shard.transpose(0, 2, 1) if kind == "col" else shard


def _pad_rows(flat):
    pad = -flat.shape[0] % (8 * LANES)
    return jnp.pad(flat, (0, pad)).reshape(-1, LANES)


def _heads_out(x, b, s, nh, d):
    return x.reshape(b, s, nh, d).transpose(0, 2, 1, 3).reshape(b * nh, s, d)


def _heads_in(x, b, s, nh, d):
    return x.reshape(b, nh, s, d).transpose(0, 2, 1, 3).reshape(b * s, nh * d)


def _lower_bounds(logits):
    c = jnp.cumsum(jax.nn.softmax(logits.astype(F32), axis=0), axis=0)
    return c - c[0]


SB_COLS = tuple(k * SB_WIDTH // LANES for k in range(3))
HG_COLS = tuple((3 * SB_WIDTH + k * HG_W) // LANES for k in range(3))
HG_GATE_COL = (3 * SB_WIDTH + 3 * HG_W) // LANES
HG_OUT_COL = SB_WIDTH // LANES


def _forward_backward(x, p, target, W, P):
    b, s, dm = x.shape
    n = b * s
    h = x.reshape(n, dm)
    lbs, lb_vjp = jax.vjp(_lower_bounds, P["hg_lb_logits"])
    bias, bias_vjp = jax.vjp(_t5_bias, P["rel_bias"])
    saved = []
    gw = {name: lax.empty(W[name].shape, F32) for name, _ in BIG}

    def times_w(a, name, l, tag, res=None):
        return _mm(a, W[name], "nt" if KIND[name] == "col" else "nn", tag, res=res, layer=l)

    def times_wt(dy, name, l, tag):
        return _mm(dy, W[name], "nn" if KIND[name] == "col" else "nt", tag, layer=l)

    def grad_w(a, dy, name, l, tag):
        lhs, rhs = (dy, a) if KIND[name] == "col" else (a, dy)
        gw[name] = _mm(lhs, rhs, "tn", tag, into=(gw[name], l))

    for i in range(DEPTH):
        j = i // 2
        r = {"h0": h}
        hn = _rms_fwd(h, P["mix_norm"][i], f"mix_norm_f{i}", BF16)
        r["hn"] = hn
        if i % 2 == 0:
            proj = times_w(hn, "ab_w_in", j, f"ab_in_f{i}")
            oa, lta = _sb_fwd(proj, SB_COLS, b, s, f"sb_f{i}")
            ob, st = _hg_fwd(proj, HG_COLS, lbs[j].reshape(HG_HEADS, HG_DK), b, s, f"hg_f{i}")
            obg = _gnorm_fwd(ob, proj, HG_GATE_COL, P["hg_out_norm"][j], f"hg_norm_f{i}")
            cat = jnp.concatenate([oa, obg], axis=1).astype(BF16)
            h = times_w(cat, "ab_w_out", j, f"ab_out_f{i}", res=h)
            r.update(lta=lta, proj=proj, ob=ob, st=st, cat=cat)
        else:
            proj = times_w(hn, "c_w_in", j, f"c_in_f{i}")
            nq = SW_HEADS * SW_DIM
            nkv = SW_KV_HEADS * SW_DIM
            q = _heads_out(proj[:, :nq], b, s, SW_HEADS, SW_DIM).reshape(-1, SW_DIM)
            k = _heads_out(proj[:, nq:nq + nkv], b, s, SW_KV_HEADS, SW_DIM).reshape(-1, SW_DIM)
            v = _heads_out(proj[:, nq + nkv:], b, s, SW_KV_HEADS, SW_DIM).reshape(b, SW_KV_HEADS, s, SW_DIM)
            qn = _rms_fwd(q, P["q_norm"][j], f"q_norm_f{i}").reshape(b, SW_KV_HEADS, SW_GROUP, s, SW_DIM)
            kn = _rms_fwd(k, P["k_norm"][j], f"k_norm_f{i}").reshape(b, SW_KV_HEADS, s, SW_DIM)
            o = _swa_fwd(qn, kn, v, P["sinks"][j], bias, f"swa_f{i}")
            o2 = _heads_in(o.reshape(b * SW_HEADS, s, SW_DIM), b, s, SW_HEADS, SW_DIM).astype(BF16)
            h = times_w(o2, "c_w_out", j, f"c_out_f{i}", res=h)
            r.update(q=q, k=k, v=v, qn=qn, kn=kn, o2=o2)
        r["h1"] = h
        hn2 = _rms_fwd(h, P["ffn_norm"][i], f"ffn_norm_f{i}", BF16)
        u = times_w(hn2, "ffn_up", i, f"ffn_up_f{i}").reshape(b, s, 2 * D_FF)
        act = _convglu_fwd(u, W["ffn_conv"][i], P["ffn_conv_b"][i], f"conv_f{i}", BF16).reshape(n, D_FF)
        h = times_w(act, "ffn_down", i, f"ffn_down_f{i}", res=h)
        r.update(hn2=hn2, u=u, act=act, h2=h)
        hn3 = _rms_fwd(h, P["ple_norm"][i], f"ple_norm_f{i}", BF16)
        z = times_w(hn3, "ple_gate", i, f"ple_gate_f{i}")
        pi = p[i].reshape(n, PLE_DIM)
        e = times_w(pi, "ple_proj", i, f"ple_proj_f{i}")
        h = _sigmul_fwd(z, e, h, f"ple_f{i}")
        r.update(hn3=hn3, z=z, e=e, pi=pi)
        saved.append(r)

    loss, dh = _loss_fwd(h, target.reshape(n, dm), "loss")

    gconv = [None] * DEPTH
    gp = {name: [None] * P[name].shape[0] for name in ("mix_norm", "hg_out_norm", "q_norm", "k_norm", "sinks",
                                                        "ffn_norm", "ffn_conv_b", "ple_norm")}
    dlbs = [None] * (DEPTH // 2)
    dbias = jnp.zeros_like(bias)

    for i in reversed(range(DEPTH)):
        j = i // 2
        r = saved[i]
        dz, de = _sigmul_bwd(r["z"], r["e"], dh, f"ple_b{i}")
        grad_w(r["pi"], de, "ple_proj", i, f"ple_proj_g{i}")
        grad_w(r["hn3"], dz, "ple_gate", i, f"ple_gate_g{i}")
        dhn3 = times_wt(dz, "ple_gate", i, f"ple_gate_b{i}")
        dh, gp["ple_norm"][i] = _rms_bwd(r["h2"], P["ple_norm"][i], dhn3, f"ple_norm_b{i}", res=dh)

        dact = times_wt(dh, "ffn_down", i, f"ffn_down_b{i}").reshape(b, s, D_FF)
        grad_w(r["act"], dh, "ffn_down", i, f"ffn_down_g{i}")
        dug, duu, ag, au = _convglu_bwd(r["u"], W["ffn_conv"][i], P["ffn_conv_b"][i], dact, f"conv_b{i}")
        du = jnp.concatenate([dug, duu], axis=-1).reshape(n, 2 * D_FF)
        gconv[i] = jnp.concatenate([ag[:CONV_W], au[:CONV_W]], axis=-1)
        gp["ffn_conv_b"][i] = jnp.concatenate([ag[CONV_W], au[CONV_W]], axis=-1)
        grad_w(r["hn2"], du, "ffn_up", i, f"ffn_up_g{i}")
        dhn2 = times_wt(du, "ffn_up", i, f"ffn_up_b{i}")
        dh, gp["ffn_norm"][i] = _rms_bwd(r["h1"], P["ffn_norm"][i], dhn2, f"ffn_norm_b{i}", res=dh)

        if i % 2 == 0:
            dcat = times_wt(dh, "ab_w_out", j, f"ab_out_b{i}")
            grad_w(r["cat"], dh, "ab_w_out", j, f"ab_out_g{i}")
            dob, dgb, gp["hg_out_norm"][j] = _gnorm_bwd(r["ob"], r["proj"], HG_GATE_COL, P["hg_out_norm"][j], dcat, HG_OUT_COL,
                                                        f"hg_norm_b{i}")
            dqb, dfb, dib, dlb = _hg_bwd(r["proj"], HG_COLS, lbs[j].reshape(HG_HEADS, HG_DK), r["st"], dob, b, s, f"hg_b{i}")
            dlbs[j] = dlb.reshape(b, HG_W).sum(axis=0)
            dqa, dka, dva = _sb_bwd(r["proj"], SB_COLS, r["lta"], dcat, 0, b, s, f"sb_b{i}")
            dproj = jnp.concatenate([dqa, dka, dva, dqb, dfb, dib, dgb], axis=1)
            grad_w(r["hn"], dproj, "ab_w_in", j, f"ab_in_g{i}")
            dhn = times_wt(dproj, "ab_w_in", j, f"ab_in_b{i}")
        else:
            do2 = times_wt(dh, "c_w_out", j, f"c_out_b{i}")
            grad_w(r["o2"], dh, "c_w_out", j, f"c_out_g{i}")
            do = _heads_out(do2, b, s, SW_HEADS, SW_DIM).reshape(b, SW_KV_HEADS, SW_GROUP, s, SW_DIM)
            dqn, dkn, dv, dbias_i, dsink = _swa_bwd(r["qn"], r["kn"], r["v"], P["sinks"][j], bias, do, f"swa_b{i}")
            dbias = dbias + dbias_i
            gp["sinks"][j] = dsink[:, :, 0].reshape(SW_HEADS)
            dq, gp["q_norm"][j] = _rms_bwd(r["q"], P["q_norm"][j], dqn.reshape(-1, SW_DIM), f"q_norm_b{i}")
            dk, gp["k_norm"][j] = _rms_bwd(r["k"], P["k_norm"][j], dkn.reshape(-1, SW_DIM), f"k_norm_b{i}")
            dproj = jnp.concatenate([_heads_in(dq.reshape(b * SW_HEADS, s, SW_DIM), b, s, SW_HEADS, SW_DIM),
                                     _heads_in(dk.reshape(b * SW_KV_HEADS, s, SW_DIM), b, s, SW_KV_HEADS, SW_DIM),
                                     _heads_in(dv.reshape(b * SW_KV_HEADS, s, SW_DIM), b, s, SW_KV_HEADS, SW_DIM)], axis=1)
            grad_w(r["hn"], dproj, "c_w_in", j, f"c_in_g{i}")
            dhn = times_wt(dproj, "c_w_in", j, f"c_in_b{i}")
        dh, gp["mix_norm"][i] = _rms_bwd(r["h0"], P["mix_norm"][i], dhn, f"mix_norm_b{i}", res=dh)

    gp = {name: jnp.stack(v) for name, v in gp.items()}
    gp["hg_lb_logits"] = lb_vjp(jnp.stack(dlbs))[0]
    gp["rel_bias"] = bias_vjp(dbias)[0]
    return loss[0, 0], dh.reshape(b, s, dm), gw, jnp.stack(gconv), gp


WEIGHTS = ("mix_norm", "ab_w_in", "hg_lb_logits", "hg_out_norm", "ab_w_out", "c_w_in", "q_norm", "k_norm", "sinks", "rel_bias",
           "c_w_out", "ffn_norm", "ffn_up", "ffn_conv", "ffn_conv_b", "ffn_down", "ple_norm", "ple_gate", "ple_proj")
SMALL = ("mix_norm", "hg_lb_logits", "hg_out_norm", "q_norm", "k_norm", "sinks", "rel_bias", "ffn_norm", "ffn_conv_b", "ple_norm")


def _step(x, p, target, w, m, v):
    names = [name for name, _ in BIG]
    mx, my, mc = _position()
    dev = 4 * mx + 2 * my + mc

    blocks = [_row_block(w[name], KIND[name]).astype(BF16) for name in names]
    full = dict(zip(names, _all_gather_rows(blocks, "gather_weights")))
    nl, taps, cs = w["ffn_conv"].shape
    conv_all = _all_gather(_pad_rows(w["ffn_conv"].reshape(-1)), "gather_conv").reshape(N_DEV, -1)[:, :nl * taps * cs]
    full["ffn_conv"] = conv_all.reshape(N_DEV, nl, taps, cs).transpose(1, 2, 0, 3).reshape(nl, taps, N_DEV * cs)

    small = {name: w[name] for name in SMALL}
    loss, grad_x, gw, gconv, gp = _forward_backward(x, p, target, full, small)

    core = jnp.reshape(mc, (1,)).astype(jnp.int32)
    where = jnp.stack([dev, 2 * mx + my]).astype(jnp.int32)
    parts = [gw[name] for name in names]
    from_sibling = _rs_pair(parts, "reduce_pair")
    chip_sums = [_rs_add_pair(g, a, core, f"reduce_pair_add_{name}") for name, g, a in zip(names, parts, from_sibling)]
    from_chips = _rs_chips(chip_sums, "reduce_chips")
    grads = {name: _row_block(_rs_final(g, a, bs, where, f"reduce_final_{name}"), KIND[name])
             for name, g, a, bs in zip(names, parts, from_sibling, from_chips)}

    flat_small = jnp.concatenate([gp[name].reshape(-1) for name in SMALL] + [gconv.reshape(-1), loss.reshape(1)])
    small_sum = _sum_devices(_all_gather(_pad_rows(flat_small), "gather_small"), "sum_small").reshape(-1)
    e0 = 0
    for name in SMALL:
        cnt = math.prod(w[name].shape)
        grads[name] = small_sum[e0:e0 + cnt].reshape(w[name].shape)
        e0 += cnt
    gconv_sum = small_sum[e0:e0 + gconv.size].reshape(gconv.shape)
    grads["ffn_conv"] = lax.dynamic_slice_in_dim(gconv_sum, dev * cs, cs, axis=2)
    loss = small_sum[e0 + gconv.size]

    deltas, new_m, new_v = {}, {}, {}
    for name in WEIGHTS:
        shape = w[name].shape
        view = (-1, shape[-1]) if len(shape) > 1 else (1, -1)
        d_, m_, v_ = _adamw(w[name].reshape(view), grads[name].reshape(view), m[name].reshape(view), v[name].reshape(view), f"adamw_{name}")
        deltas[name], new_m[name], new_v[name] = d_.reshape(shape), m_.reshape(shape), v_.reshape(shape)
    return (loss, grad_x, *[grads[k] for k in WEIGHTS], *[deltas[k] for k in WEIGHTS],
            *[new_m[k] for k in WEIGHTS], *[new_v[k] for k in WEIGHTS])


def kernel(x, p, mix_norm, ab_w_in, hg_lb_logits, hg_out_norm, ab_w_out, c_w_in, q_norm, k_norm, sinks, rel_bias, c_w_out, ffn_norm, ffn_up, ffn_conv, ffn_conv_b, ffn_down, ple_norm, ple_gate, ple_proj, loss_target, m_mix_norm, m_ab_w_in, m_hg_lb_logits, m_hg_out_norm, m_ab_w_out, m_c_w_in, m_q_norm, m_k_norm, m_sinks, m_rel_bias, m_c_w_out, m_ffn_norm, m_ffn_up, m_ffn_conv, m_ffn_conv_b, m_ffn_down, m_ple_norm, m_ple_gate, m_ple_proj, v_mix_norm, v_ab_w_in, v_hg_lb_logits, v_hg_out_norm, v_ab_w_out, v_c_w_in, v_q_norm, v_k_norm, v_sinks, v_rel_bias, v_c_w_out, v_ffn_norm, v_ffn_up, v_ffn_conv, v_ffn_conv_b, v_ffn_down, v_ple_norm, v_ple_gate, v_ple_proj):
    w = dict(zip(WEIGHTS, (mix_norm, ab_w_in, hg_lb_logits, hg_out_norm, ab_w_out, c_w_in, q_norm, k_norm, sinks, rel_bias, c_w_out,
                           ffn_norm, ffn_up, ffn_conv, ffn_conv_b, ffn_down, ple_norm, ple_gate, ple_proj)))
    m = dict(zip(WEIGHTS, (m_mix_norm, m_ab_w_in, m_hg_lb_logits, m_hg_out_norm, m_ab_w_out, m_c_w_in, m_q_norm, m_k_norm, m_sinks,
                           m_rel_bias, m_c_w_out, m_ffn_norm, m_ffn_up, m_ffn_conv, m_ffn_conv_b, m_ffn_down, m_ple_norm, m_ple_gate,
                           m_ple_proj)))
    v = dict(zip(WEIGHTS, (v_mix_norm, v_ab_w_in, v_hg_lb_logits, v_hg_out_norm, v_ab_w_out, v_c_w_in, v_q_norm, v_k_norm, v_sinks,
                           v_rel_bias, v_c_w_out, v_ffn_norm, v_ffn_up, v_ffn_conv, v_ffn_conv_b, v_ffn_down, v_ple_norm, v_ple_gate,
                           v_ple_proj)))
    return _step(x, p, loss_target, w, m, v)
```

```python
import functools
import math

import numpy as np
import jax
import jax.numpy as jnp
from jax import lax
from jax.experimental import pallas as pl
from jax.experimental.pallas import tpu as pltpu

F32 = jnp.float32
BF16 = jnp.bfloat16

D_MODEL = 1024
DEPTH = 4
PLE_DIM = 256
EPS = 1e-6
SB_HEADS, SB_DIM = 8, 64
SB_WIDTH = SB_HEADS * SB_DIM
HG_HEADS, HG_DK, HG_DV = 4, 128, 128
HG_W = HG_HEADS * HG_DK
AB_IN = 3 * SB_WIDTH + 4 * HG_W
SW_HEADS, SW_KV_HEADS, SW_DIM = 16, 4, 64
SW_GROUP = SW_HEADS // SW_KV_HEADS
WINDOW = 128
C_IN = (SW_HEADS + 2 * SW_KV_HEADS) * SW_DIM
N_BUCKETS, MAX_DISTANCE = 32, 128
D_FF = 2816
N_DEV = 8

ADAM_LR, ADAM_B1, ADAM_B2, ADAM_EPS, ADAM_WD, ADAM_STEP = 0.001, 0.9, 0.999, 1e-08, 0.01, 10

LANES = 128
VMEM_LIMIT = 48 * 1024 * 1024

NN = (((1,), (0,)), ((), ()))
NT = (((1,), (1,)), ((), ()))
TN = (((0,), (0,)), ((), ()))


MXU_DTYPE = BF16


def _bf(x):
    return x.astype(MXU_DTYPE)


def _dot(a, b, dims=NN):
    return lax.dot_general(_bf(a), _bf(b), dims, preferred_element_type=F32)


def _split3(x):
    x1 = _bf(x)
    r = x - x1.astype(F32)
    x2 = _bf(r)
    x3 = _bf(r - x2.astype(F32))
    return x1, x2, x3


def _dot_exact_lhs01(m, x, terms=3):
    parts = _split3(x)[:terms]
    out = lax.dot_general(m, parts[0], NN, preferred_element_type=F32)
    for p_ in parts[1:]:
        out = out + lax.dot_general(m, p_, NN, preferred_element_type=F32)
    return out


def _dot_exact_rhs01(x, m, terms=2):
    parts = _split3(x)[:terms]
    out = lax.dot_general(parts[0], m, NN, preferred_element_type=F32)
    for p_ in parts[1:]:
        out = out + lax.dot_general(p_, m, NN, preferred_element_type=F32)
    return out


def _pick(n, target):
    best = None
    for t in range(LANES, target + 1, LANES):
        if n % t == 0:
            best = t
    return best or n


def _params(sem=None):
    return pltpu.CompilerParams(dimension_semantics=sem, vmem_limit_bytes=VMEM_LIMIT)


def _mm(a, b, mode, name, res=None, out_dtype=F32, layer=None, into=None):
    bshape = b.shape if layer is None else b.shape[1:]
    if mode == "nn":
        (M, K), (K2, N) = a.shape, bshape
    elif mode == "nt":
        (M, K), (N, K2) = a.shape, bshape
    else:
        (K, M), (K2, N) = a.shape, bshape
    assert K == K2, (a.shape, b.shape, mode)
    tm, tn, tk = _pick(M, 1024), _pick(N, 1024), _pick(K, 1536)
    nk = K // tk
    dims = {"nn": NN, "nt": NT, "tn": TN}[mode]
    a_spec = pl.BlockSpec((tk, tm), lambda i, j, k: (k, i)) if mode == "tn" else pl.BlockSpec((tm, tk), lambda i, j, k: (i, k))
    if layer is None:
        b_spec = pl.BlockSpec((tn, tk), lambda i, j, k: (j, k)) if mode == "nt" else pl.BlockSpec((tk, tn), lambda i, j, k: (k, j))
    elif mode == "nt":
        b_spec = pl.BlockSpec((None, tn, tk), lambda i, j, k: (layer, j, k))
    else:
        b_spec = pl.BlockSpec((None, tk, tn), lambda i, j, k: (layer, k, j))
    o_spec = pl.BlockSpec((tm, tn), lambda i, j, k: (i, j))
    has_res = res is not None

    def body(*refs):
        a_ref, b_ref = refs[0], refs[1]
        r_ref = refs[2] if has_res else None
        o_ref, acc_ref = refs[-2], refs[-1]
        k = pl.program_id(2)

        @pl.when(k == 0)
        def _():
            acc_ref[...] = jnp.zeros_like(acc_ref)

        acc_ref[...] += _dot(a_ref[...], b_ref[...], dims)

        @pl.when(k == nk - 1)
        def _():
            out = acc_ref[...]
            if has_res:
                out = out + r_ref[...]
            o_ref[...] = out.astype(out_dtype)

    in_specs = [a_spec, b_spec] + ([o_spec] if has_res else [])
    args = (a, b) + ((res,) if has_res else ())
    out_shape, aliases = jax.ShapeDtypeStruct((M, N), out_dtype), {}
    if into is not None:
        stack, slot = into
        assert stack.shape[1:] == (M, N) and stack.dtype == out_dtype and not has_res
        in_specs = in_specs + [pl.BlockSpec(memory_space=pl.ANY)]
        args = args + (stack,)
        o_spec = pl.BlockSpec((None, tm, tn), lambda i, j, k: (slot, i, j))
        out_shape, aliases = jax.ShapeDtypeStruct(stack.shape, out_dtype), {2: 0}

    def body_into(a_ref, b_ref, stack_ref, o_ref, acc_ref):
        body(a_ref, b_ref, o_ref, acc_ref)

    return pl.pallas_call(
        body if into is None else body_into, name=name, grid=(M // tm, N // tn, nk), in_specs=in_specs, out_specs=o_spec,
        out_shape=out_shape, scratch_shapes=[pltpu.VMEM((tm, tn), F32)], input_output_aliases=aliases,
        compiler_params=_params(("parallel", "parallel", "arbitrary")),
    )(*args)


def _row_tile(n, d):
    if n % 8:
        return n
    t = 8
    while t * 2 <= min(n, (256 * 1024) // d) and n % (t * 2) == 0:
        t *= 2
    return t


def _rms_fwd(x, g, name, out_dtype=F32):
    n, d = x.shape
    tm = _row_tile(n, d)

    def body(x_ref, g_ref, o_ref):
        xf = x_ref[...]
        r = lax.rsqrt(jnp.mean(xf * xf, axis=-1, keepdims=True) + EPS)
        o_ref[...] = (xf * r * g_ref[...]).astype(out_dtype)

    return pl.pallas_call(
        body, name=name, grid=(n // tm,),
        in_specs=[pl.BlockSpec((tm, d), lambda i: (i, 0)), pl.BlockSpec((1, d), lambda i: (0, 0))],
        out_specs=pl.BlockSpec((tm, d), lambda i: (i, 0)),
        out_shape=jax.ShapeDtypeStruct((n, d), out_dtype), compiler_params=_params(("parallel",)),
    )(x, g.reshape(1, d))


def _rms_bwd(x, g, dy, name, res=None):
    n, d = x.shape
    tm = _row_tile(n, d)
    has_res = res is not None

    def body(*refs):
        x_ref, g_ref, dy_ref = refs[:3]
        r_ref = refs[3] if has_res else None
        dx_ref, dg_ref = refs[-2:]
        xf = x_ref[...]
        r = lax.rsqrt(jnp.mean(xf * xf, axis=-1, keepdims=True) + EPS)
        xh = xf * r
        dyf = dy_ref[...].astype(F32)
        dxh = dyf * g_ref[...]
        dx = r * (dxh - xh * jnp.mean(dxh * xh, axis=-1, keepdims=True))
        if has_res:
            dx = dx + r_ref[...]
        dx_ref[...] = dx

        @pl.when(pl.program_id(0) == 0)
        def _():
            dg_ref[...] = jnp.zeros_like(dg_ref)

        dg_ref[...] += jnp.sum(dyf * xh, axis=0, keepdims=True)

    row = pl.BlockSpec((tm, d), lambda i: (i, 0))
    vec = pl.BlockSpec((1, d), lambda i: (0, 0))
    dx, dg = pl.pallas_call(
        body, name=name, grid=(n // tm,),
        in_specs=[row, vec, row] + ([row] if has_res else []),
        out_specs=[row, vec],
        out_shape=[jax.ShapeDtypeStruct((n, d), F32), jax.ShapeDtypeStruct((1, d), F32)],
        compiler_params=_params(("arbitrary",)),
    )(x, g.reshape(1, d), dy, *((res,) if has_res else ()))
    return dx, dg.reshape(d)


def _silu(x):
    return x * jax.nn.sigmoid(x)


def _gnorm_fwd(o, gate, gate_col, w, name):
    n, width = o.shape
    d = w.shape[0]
    tm = _row_tile(n, d)

    def body(o_ref, g_ref, w_ref, y_ref):
        of = o_ref[...]
        r = lax.rsqrt(jnp.mean(of * of, axis=-1, keepdims=True) + EPS)
        y_ref[...] = of * r * w_ref[...] * _silu(g_ref[...])

    row = pl.BlockSpec((tm, d), lambda i, h: (i, h))
    vec = pl.BlockSpec((1, d), lambda i, h: (0, 0))
    return pl.pallas_call(body, name=name, grid=(n // tm, width // d),
                          in_specs=[row, pl.BlockSpec((tm, d), lambda i, h: (i, gate_col + h)), vec], out_specs=row,
                          out_shape=jax.ShapeDtypeStruct((n, width), F32), compiler_params=_params(("parallel", "parallel")))(o, gate, w.reshape(1, d))


def _gnorm_bwd(o, gate, gate_col, w, dy, dy_col, name):
    n, width = o.shape
    d = w.shape[0]
    tm = _row_tile(n, d)

    def body(o_ref, g_ref, w_ref, dy_ref, do_ref, dgate_ref, dw_ref):
        of, gf, dyf = o_ref[...], g_ref[...], dy_ref[...]
        r = lax.rsqrt(jnp.mean(of * of, axis=-1, keepdims=True) + EPS)
        xh = of * r
        sg = jax.nn.sigmoid(gf)
        sil = gf * sg
        dnorm = dyf * sil
        dgate_ref[...] = dyf * xh * w_ref[...] * (sg * (1.0 + gf * (1.0 - sg)))
        dxh = dnorm * w_ref[...]
        do_ref[...] = r * (dxh - xh * jnp.mean(dxh * xh, axis=-1, keepdims=True))

        @pl.when((pl.program_id(0) == 0) & (pl.program_id(1) == 0))
        def _():
            dw_ref[...] = jnp.zeros_like(dw_ref)

        dw_ref[...] += jnp.sum(dnorm * xh, axis=0, keepdims=True)

    row = pl.BlockSpec((tm, d), lambda i, h: (i, h))
    vec = pl.BlockSpec((1, d), lambda i, h: (0, 0))
    do, dgate, dw = pl.pallas_call(
        body, name=name, grid=(n // tm, width // d),
        in_specs=[row, pl.BlockSpec((tm, d), lambda i, h: (i, gate_col + h)), vec, pl.BlockSpec((tm, d), lambda i, h: (i, dy_col + h))],
        out_specs=[row, row, vec],
        out_shape=[jax.ShapeDtypeStruct((n, width), F32)] * 2 + [jax.ShapeDtypeStruct((1, d), F32)],
        compiler_params=_params(("arbitrary", "arbitrary")),
    )(o, gate, w.reshape(1, d), dy)
    return do, dgate, dw.reshape(d)


def _sigmul_fwd(z, e, res, name):
    n, d = z.shape
    tm = _row_tile(n, d)

    def body(z_ref, e_ref, r_ref, o_ref):
        o_ref[...] = r_ref[...] + jax.nn.sigmoid(z_ref[...]) * e_ref[...]

    row = pl.BlockSpec((tm, d), lambda i: (i, 0))
    return pl.pallas_call(body, name=name, grid=(n // tm,), in_specs=[row] * 3, out_specs=row,
                          out_shape=jax.ShapeDtypeStruct((n, d), F32), compiler_params=_params(("parallel",)))(z, e, res)


def _sigmul_bwd(z, e, dy, name):
    n, d = z.shape
    tm = _row_tile(n, d)

    def body(z_ref, e_ref, dy_ref, dz_ref, de_ref):
        s = jax.nn.sigmoid(z_ref[...])
        dyf = dy_ref[...]
        dz_ref[...] = dyf * e_ref[...] * s * (1.0 - s)
        de_ref[...] = dyf * s

    row = pl.BlockSpec((tm, d), lambda i: (i, 0))
    return pl.pallas_call(body, name=name, grid=(n // tm,), in_specs=[row] * 3, out_specs=[row] * 2,
                          out_shape=[jax.ShapeDtypeStruct((n, d), F32)] * 2, compiler_params=_params(("parallel",)))(z, e, dy)


def _loss_fwd(y, target, name):
    n, d = y.shape
    tm = _row_tile(n, d)

    def body(y_ref, t_ref, l_ref, dy_ref):
        diff = y_ref[...] - t_ref[...]
        dy_ref[...] = diff * (1.0 / d)

        @pl.when(pl.program_id(0) == 0)
        def _():
            l_ref[...] = jnp.zeros_like(l_ref)

        part = jnp.sum(jnp.mean(diff * diff, axis=-1, keepdims=True), axis=0, keepdims=True)
        l_ref[...] += 0.5 * jnp.broadcast_to(part, l_ref.shape)

    row = pl.BlockSpec((tm, d), lambda i: (i, 0))
    vec = pl.BlockSpec((1, LANES), lambda i: (0, 0))
    return pl.pallas_call(body, name=name, grid=(n // tm,), in_specs=[row, row], out_specs=[vec, row],
                          out_shape=[jax.ShapeDtypeStruct((1, LANES), F32), jax.ShapeDtypeStruct((n, d), F32)],
                          compiler_params=_params(("arbitrary",)))(y, target)


SB_BLK = 128
SB_QBLK = 256


def _sb_logits(z, qi, kj, row, col):
    mask = (kj * SB_BLK + col) < (qi * SB_QBLK + row)
    sp = jnp.maximum(z, 0.0) + jnp.log1p(jnp.exp(-jnp.abs(z)))
    lk = jnp.where(mask, -sp, 0.0)
    return mask, lk, z - sp


SB_PAIRS = SB_WIDTH // LANES


def _sb_iotas():
    row = lax.broadcasted_iota(jnp.int32, (2 * SB_QBLK, SB_BLK), 0)
    row = jnp.where(row >= SB_QBLK, row - SB_QBLK, row)
    col = lax.broadcasted_iota(jnp.int32, (2 * SB_QBLK, SB_BLK), 1)
    return row, col, col[:SB_QBLK] < SB_DIM


def _sb_stack(x, first):
    return jnp.concatenate([jnp.where(first, x, 0.0), jnp.where(first, 0.0, x)], axis=0)


def _sb_unstack(y, first):
    return jnp.where(first, y[:SB_QBLK], y[SB_QBLK:])


def _sb_running(x, u):
    m = x.shape[0]
    hi = _bf(x)
    lo = _bf(x - hi.astype(F32))
    c = lax.dot_general(jnp.concatenate([hi, lo], axis=0), u, NN, preferred_element_type=F32)
    return c[:m] + c[m:]


def _sb_spec(s, col):
    return pl.BlockSpec((s, LANES), lambda e, pr: (e, col + pr))


def _sb_fwd(proj, cols, b, s, name):
    nq = s // SB_QBLK
    scale = SB_DIM ** -0.5

    def body(q_ref, k_ref, v_ref, o_ref, lt_ref):
        row, col, first = _sb_iotas()
        u_after = _bf(row[:SB_BLK] > col[:SB_BLK])

        def qloop(qi, _):
            q0 = pl.multiple_of(qi * SB_QBLK, SB_QBLK)
            q2 = _sb_stack(q_ref[pl.ds(q0, SB_QBLK), :], first)
            nkeys = (qi + 1) * (SB_QBLK // SB_BLK)

            def logits(kj):
                k0 = pl.multiple_of(kj * SB_BLK, SB_BLK)
                return _dot(q2, k_ref[pl.ds(k0, SB_BLK), :], NT) * scale

            def kloop(j, st):
                acc, carry, z = st
                kj = nkeys - 1 - j
                k0 = pl.multiple_of(kj * SB_BLK, SB_BLK)
                z_next = logits(jnp.maximum(kj - 1, 0))
                mask, lk, ls = _sb_logits(z, qi, kj, row, col)
                later = carry + _sb_running(lk, u_after)
                w = jnp.where(mask, jnp.exp(ls + later), 0.0)
                acc = acc + _sb_unstack(_dot(w, v_ref[pl.ds(k0, SB_BLK), :]), first)
                return acc, carry + jnp.sum(lk, axis=1, keepdims=True), z_next

            acc, carry, _ = lax.fori_loop(0, nkeys, kloop, (jnp.zeros((SB_QBLK, LANES), F32), jnp.zeros((2 * SB_QBLK, 1), F32), logits(nkeys - 1)))
            o_ref[pl.ds(q0, SB_QBLK), :] = acc
            lt_ref[pl.ds(q0, SB_QBLK), :] = _sb_unstack(jnp.broadcast_to(carry, (2 * SB_QBLK, LANES)), first)
            return 0

        lax.fori_loop(0, nq, qloop, 0)

    out = _sb_spec(s, 0)
    return pl.pallas_call(body, name=name, grid=(b, SB_PAIRS), in_specs=[_sb_spec(s, c) for c in cols], out_specs=[out, out],
                          out_shape=[jax.ShapeDtypeStruct((b * s, SB_WIDTH), F32)] * 2,
                          compiler_params=_params(("parallel", "parallel")))(proj, proj, proj)


def _sb_bwd(proj, cols, ltot, do, do_col, b, s, name):
    nq = s // SB_QBLK
    scale = SB_DIM ** -0.5

    def body(q_ref, k_ref, v_ref, lt_ref, do_ref, dq_ref, dk_ref, dv_ref):
        row, col, first = _sb_iotas()
        u_upto = _bf(row[:SB_BLK] <= col[:SB_BLK])
        u_before = _bf(row[:SB_BLK] < col[:SB_BLK])
        dk_ref[...] = jnp.zeros_like(dk_ref)
        dv_ref[...] = jnp.zeros_like(dv_ref)

        def qloop(qi, _):
            q0 = pl.multiple_of(qi * SB_QBLK, SB_QBLK)
            q2 = _sb_stack(q_ref[pl.ds(q0, SB_QBLK), :], first)
            nkeys = (qi + 1) * (SB_QBLK // SB_BLK)
            do2 = _sb_stack(do_ref[pl.ds(q0, SB_QBLK), :], first)
            lt2 = jnp.min(_sb_stack(lt_ref[pl.ds(q0, SB_QBLK), :], first), axis=1, keepdims=True)

            def logits(kj):
                k0 = pl.multiple_of(kj * SB_BLK, SB_BLK)
                return _dot(q2, k_ref[pl.ds(k0, SB_BLK), :], NT) * scale

            def kloop(kj, st):
                dq, cl, cg, z = st
                k0 = pl.multiple_of(kj * SB_BLK, SB_BLK)
                kb = k_ref[pl.ds(k0, SB_BLK), :]
                vb = v_ref[pl.ds(k0, SB_BLK), :]
                z_next = logits(jnp.minimum(kj + 1, nkeys - 1))
                mask, lk, ls = _sb_logits(z, qi, kj, row, col)
                later = lt2 - (cl + _sb_running(lk, u_upto))
                w = jnp.where(mask, jnp.exp(ls + later), 0.0)
                g = _dot(do2, vb, NT) * w
                dv_ref[pl.ds(k0, SB_BLK), :] += _dot(w, do2, TN)
                g_before = cg + _sb_running(g, u_before)
                sig = jnp.exp(ls)
                dz = jnp.where(mask, g * (1.0 - sig) - sig * g_before, 0.0) * scale
                dq = dq + _sb_unstack(_dot(dz, kb), first)
                dk_ref[pl.ds(k0, SB_BLK), :] += _dot(dz, q2, TN)
                return dq, cl + jnp.sum(lk, axis=1, keepdims=True), cg + jnp.sum(g, axis=1, keepdims=True), z_next

            z1 = jnp.zeros((2 * SB_QBLK, 1), F32)
            dq = lax.fori_loop(0, nkeys, kloop, (jnp.zeros((SB_QBLK, LANES), F32), z1, z1, logits(0)))[0]
            dq_ref[pl.ds(q0, SB_QBLK), :] = dq
            return 0

        lax.fori_loop(0, nq, qloop, 0)

    out = _sb_spec(s, 0)
    return pl.pallas_call(body, name=name, grid=(b, SB_PAIRS),
                          in_specs=[_sb_spec(s, c) for c in cols] + [out, _sb_spec(s, do_col)], out_specs=[out] * 3,
                          out_shape=[jax.ShapeDtypeStruct((b * s, SB_WIDTH), F32)] * 3,
                          compiler_params=_params(("parallel", "parallel")))(proj, proj, proj, ltot, do)


HG_CHUNK = 64


def _hg_consts(c):
    levels = int(math.log2(c))
    t = np.arange(c)
    tri = (t[:, None] >= t[None, :]).astype(np.float32)
    psel = np.zeros((levels, c, c), np.float32)
    masks = np.zeros((levels + 1, c, c), np.float32)
    for l in range(levels):
        n = c >> (l + 1)
        blk = t // (2 * n)
        psel[l, t, blk * 2 * n + n - 1] = 1.0
        upper = (t % (2 * n)) >= n
        masks[l] = (blk[:, None] == blk[None, :]) & upper[:, None] & (~upper)[None, :]
    masks[levels] = np.eye(c)
    psel = psel.reshape(levels * c, c)
    return levels, jnp.asarray(tri), jnp.asarray(psel), jnp.asarray(masks), jnp.asarray(tri.T.copy()), jnp.asarray(psel.T.copy())


def _hg_elem(qv, fv, lbv):
    sig = jax.nn.sigmoid(fv)
    lf = jnp.log(lbv + (1.0 - lbv) * sig)
    kk = (1.0 - lbv) * jax.nn.sigmoid(-fv)
    qf = qv * jax.nn.sigmoid(qv)
    return qf, kk, lf


def _col_bcast(rowvec):
    n = rowvec.shape[1]
    return jnp.transpose(jnp.broadcast_to(rowvec, (n, n)))


def _hg_chunk(qf, kk, lf, iv, state, tri, psel, m_ref, c, levels):
    b = _dot_exact_lhs01(tri, lf)
    bl = b[c - 1:c, :]
    eb = jnp.exp(b)
    qi = qf * eb
    bsel = _dot_exact_lhs01(psel, b)
    scores = jnp.where(m_ref[levels] > 0, _dot(qf, kk, NT), 0.0)
    lev = []
    for l in range(levels):
        bs = bsel[l * c:(l + 1) * c]
        eq = jnp.exp(jnp.minimum(b - bs, 0.0))
        ek = jnp.exp(jnp.minimum(bs - b, 0.0))
        ql, kl = qf * eq, kk * ek
        scores = scores + jnp.where(m_ref[l] > 0, _dot(ql, kl, NT), 0.0)
        lev.append((eq, ek, ql, kl))
    o = _dot(qi, state) + _dot(scores, iv)
    ebl = jnp.exp(bl - b)
    kd = kk * ebl
    decay = _col_bcast(jnp.exp(bl))
    new_state = decay * state + _dot(kd, iv, TN)
    return o, new_state, (eb, qi, scores, lev, ebl, kd, decay)


def _hg_fwd(proj, cols, lb, b, s, name):
    nh, d = lb.shape
    bh = b * nh
    c = HG_CHUNK
    nc = s // c
    levels, tri, psel, masks, _, _ = _hg_consts(c)

    def body(q_ref, f_ref, i_ref, lb_ref, tri_ref, psel_ref, m_ref, o_ref, st_ref):
        lbv = jnp.broadcast_to(lb_ref[0], (c, d))
        tri_v, psel_v = _bf(tri_ref[...]), _bf(psel_ref[...])

        def chunk(ci, state):
            r0 = pl.multiple_of(ci * c, c)
            qf, kk, lf = _hg_elem(q_ref[pl.ds(r0, c), :], f_ref[pl.ds(r0, c), :], lbv)
            st_ref[0, ci] = state
            o, state, _ = _hg_chunk(qf, kk, lf, i_ref[pl.ds(r0, c), :], state, tri_v, psel_v, m_ref, c, levels)
            o_ref[pl.ds(r0, c), :] = o
            return state

        lax.fori_loop(0, nc, chunk, jnp.zeros((d, d), F32))

    full = lambda a: pl.BlockSpec(a.shape, lambda e, hd: (0,) * a.ndim)
    return pl.pallas_call(
        body, name=name, grid=(b, nh),
        in_specs=[_hg_seq(s, d, cols[0]), _hg_seq(s, d, cols[1]), _hg_seq(s, d, cols[2]),
                  pl.BlockSpec((1, 1, d), lambda e, hd: (hd, 0, 0)), full(tri), full(psel), full(masks)],
        out_specs=[_hg_seq(s, d, 0), pl.BlockSpec((1, nc, d, d), lambda e, hd: (e * nh + hd, 0, 0, 0))],
        out_shape=[jax.ShapeDtypeStruct((b * s, nh * d), F32), jax.ShapeDtypeStruct((bh, nc, d, d), F32)],
        compiler_params=_params(("parallel", "parallel")),
    )(proj, proj, proj, lb.reshape(nh, 1, d), tri, psel, masks)


def _hg_seq(s, d, col):
    return pl.BlockSpec((s, d), lambda e, hd: (e, col + hd))


def _hg_bwd(proj, cols, lb, states, do, b, s, name):
    nh, d = lb.shape
    bh = b * nh
    c = HG_CHUNK
    nc = s // c
    levels, tri, psel, masks, tri_t, psel_t = _hg_consts(c)

    def body(q_ref, f_ref, i_ref, lb_ref, st_ref, do_ref, tri_ref, psel_ref, m_ref, trit_ref, pselt_ref,
             dq_ref, df_ref, di_ref, dlb_ref):
        lbv = jnp.broadcast_to(lb_ref[0], (c, d))
        tri_v, psel_v = _bf(tri_ref[...]), _bf(psel_ref[...])
        trit_v, pselt_v = _bf(trit_ref[...]), _bf(pselt_ref[...])
        last_row = lax.broadcasted_iota(jnp.int32, (c, d), 0) == c - 1

        def chunk(step, carry):
            ds_out, dlb = carry
            ci = nc - 1 - step
            r0 = pl.multiple_of(ci * c, c)
            qv, fv, iv = q_ref[pl.ds(r0, c), :], f_ref[pl.ds(r0, c), :], i_ref[pl.ds(r0, c), :]
            dov = do_ref[pl.ds(r0, c), :]
            state = st_ref[0, ci]
            (qf, kk, lf), elem_vjp = jax.vjp(_hg_elem, qv, fv, lbv)
            _, _, (eb, qi, scores, lev, ebl, kd, decay) = _hg_chunk(qf, kk, lf, iv, state, tri_v, psel_v, m_ref, c, levels)

            dscores = _dot(dov, iv, NT)
            di_ref[pl.ds(r0, c), :] = _dot(scores, dov, TN) + _dot(kd, ds_out)
            dqi = _dot(dov, state, NT)
            ds_in = decay * ds_out + _dot(qi, dov, TN)
            dkd = _dot(iv, ds_out, NT)
            dqf = dqi * eb
            dkk = dkd * ebl
            tkd = dkd * kd
            db = dqi * qi - tkd
            dbl = jnp.sum(tkd, axis=0, keepdims=True) + _col_bcast_t(jnp.sum(ds_out * decay * state, axis=1, keepdims=True))
            dsd = jnp.where(m_ref[levels] > 0, dscores, 0.0)
            dqf = dqf + _dot(dsd, kk)
            dkk = dkk + _dot(dsd, qf, TN)
            dbsel = []
            for l in range(levels):
                eq, ek, ql, kl = lev[l]
                dsl = jnp.where(m_ref[l] > 0, dscores, 0.0)
                dql = _dot(dsl, kl)
                dkl = _dot(dsl, ql, TN)
                dqf = dqf + dql * eq
                dkk = dkk + dkl * ek
                diff = dql * ql - dkl * kl
                db = db + diff
                dbsel.append(-diff)
            db = db + _dot_exact_lhs01(pselt_v, jnp.concatenate(dbsel, axis=0))
            db = db + jnp.where(last_row, dbl, 0.0)
            dlf = _dot_exact_lhs01(trit_v, db)
            dq, df, dlb_c = elem_vjp((dqf, dkk, dlf))
            dq_ref[pl.ds(r0, c), :] = dq
            df_ref[pl.ds(r0, c), :] = df
            return ds_in, dlb + jnp.sum(dlb_c, axis=0, keepdims=True)

        _, dlb = lax.fori_loop(0, nc, chunk, (jnp.zeros((d, d), F32), jnp.zeros((1, d), F32)))
        dlb_ref[0] = dlb

    seq = _hg_seq(s, d, 0)
    full = lambda a: pl.BlockSpec(a.shape, lambda e, hd: (0,) * a.ndim)
    return pl.pallas_call(
        body, name=name, grid=(b, nh),
        in_specs=[_hg_seq(s, d, cols[0]), _hg_seq(s, d, cols[1]), _hg_seq(s, d, cols[2]),
                  pl.BlockSpec((1, 1, d), lambda e, hd: (hd, 0, 0)),
                  pl.BlockSpec((1, nc, d, d), lambda e, hd: (e * nh + hd, 0, 0, 0)), seq,
                  full(tri), full(psel), full(masks), full(tri_t), full(psel_t)],
        out_specs=[seq, seq, seq, pl.BlockSpec((1, 1, d), lambda e, hd: (e * nh + hd, 0, 0))],
        out_shape=[jax.ShapeDtypeStruct((b * s, nh * d), F32)] * 3 + [jax.ShapeDtypeStruct((bh, 1, d), F32)],
        compiler_params=_params(("parallel", "parallel")),
    )(proj, proj, proj, lb.reshape(nh, 1, d), states, do, tri, psel, masks, tri_t, psel_t)


def _col_bcast_t(colvec):
    n = colvec.shape[0]
    return jnp.transpose(jnp.broadcast_to(colvec, (n, n)))[0:1, :]


def _swa_probs(qg, kb, bias, sink, valid, scale):
    logits = _dot(qg, kb, NT) * scale + bias
    logits = jnp.where(valid, logits, -jnp.inf)
    m = jnp.maximum(jnp.max(logits, axis=-1, keepdims=True), sink)
    e = jnp.exp(logits - m)
    es = jnp.exp(sink - m)
    den = jnp.sum(e, axis=-1, keepdims=True) + es
    return e / den, es / den


def _swa_valid(n):
    w = WINDOW
    row = lax.broadcasted_iota(jnp.int32, (w, 2 * w), 0)
    col = lax.broadcasted_iota(jnp.int32, (w, 2 * w), 1)
    dist = row + w - col
    return (dist >= 0) & (dist < w) & ((col >= w) | (n > 0))


def _swa_specs(b, g, s, d):
    w = WINDOW
    q_spec = pl.BlockSpec((1, 1, g, w, d), lambda h, bi, n: (bi, h, 0, n, 0))
    kp_spec = pl.BlockSpec((1, 1, w, d), lambda h, bi, n: (bi, h, jnp.maximum(n - 1, 0), 0))
    kc_spec = pl.BlockSpec((1, 1, w, d), lambda h, bi, n: (bi, h, n, 0))
    bias_spec = pl.BlockSpec((1, g, w, 2 * w), lambda h, bi, n: (h, 0, 0, 0))
    sink_spec = pl.BlockSpec(memory_space=pltpu.SMEM)
    return q_spec, kp_spec, kc_spec, bias_spec, sink_spec


def _swa_fwd(q, k, v, sinks, bias, name):
    b, kvh, g, s, d = q.shape
    w = WINDOW
    scale = d ** -0.5
    q_spec, kp_spec, kc_spec, bias_spec, sink_spec = _swa_specs(b, g, s, d)

    def body(q_ref, kp_ref, kc_ref, vp_ref, vc_ref, bias_ref, sink_ref, o_ref):
        h, n = pl.program_id(0), pl.program_id(2)
        valid = _swa_valid(n)
        kb = jnp.concatenate([kp_ref[0, 0], kc_ref[0, 0]], axis=0)
        vb = jnp.concatenate([vp_ref[0, 0], vc_ref[0, 0]], axis=0)
        for gi in range(g):
            p, _ = _swa_probs(q_ref[0, 0, gi], kb, bias_ref[0, gi], sink_ref[h * g + gi], valid, scale)
            o_ref[0, 0, gi] = _dot(p, vb)

    return pl.pallas_call(
        body, name=name, grid=(kvh, b, s // w),
        in_specs=[q_spec, kp_spec, kc_spec, kp_spec, kc_spec, bias_spec, sink_spec], out_specs=q_spec,
        out_shape=jax.ShapeDtypeStruct(q.shape, F32), compiler_params=_params(("parallel", "parallel", "arbitrary")),
    )(q, k, k, v, v, bias, sinks)


def _swa_bwd(q, k, v, sinks, bias, do, name):
    b, kvh, g, s, d = q.shape
    w = WINDOW
    scale = d ** -0.5
    q_spec, kp_spec, kc_spec, bias_spec, sink_spec = _swa_specs(b, g, s, d)
    kv_acc = pl.BlockSpec((1, 1, s, d), lambda h, bi, n: (bi, h, 0, 0))
    dsink_spec = pl.BlockSpec((1, g, LANES), lambda h, bi, n: (h, 0, 0))

    def body(q_ref, kp_ref, kc_ref, vp_ref, vc_ref, bias_ref, sink_ref, do_ref, dq_ref, dk_ref, dv_ref, dbias_ref, dsink_ref):
        h, bi, n = pl.program_id(0), pl.program_id(1), pl.program_id(2)
        valid = _swa_valid(n)
        kb = jnp.concatenate([kp_ref[0, 0], kc_ref[0, 0]], axis=0)
        vb = jnp.concatenate([vp_ref[0, 0], vc_ref[0, 0]], axis=0)

        @pl.when(n == 0)
        def _():
            dk_ref[...] = jnp.zeros_like(dk_ref)
            dv_ref[...] = jnp.zeros_like(dv_ref)

        @pl.when((n == 0) & (bi == 0))
        def _():
            dbias_ref[...] = jnp.zeros_like(dbias_ref)
            dsink_ref[...] = jnp.zeros_like(dsink_ref)

        dkb = jnp.zeros((2 * w, d), F32)
        dvb = jnp.zeros((2 * w, d), F32)
        for gi in range(g):
            qg, dog = q_ref[0, 0, gi], do_ref[0, 0, gi]
            p, ps = _swa_probs(qg, kb, bias_ref[0, gi], sink_ref[h * g + gi], valid, scale)
            dp = _dot(dog, vb, NT)
            delta = jnp.sum(p * dp, axis=-1, keepdims=True)
            dl = p * (dp - delta)
            dq_ref[0, 0, gi] = _dot(dl, kb) * scale
            dkb = dkb + _dot(dl, qg, TN) * scale
            dvb = dvb + _dot(p, dog, TN)
            dbias_ref[0, gi] += dl
            dsink_ref[0, gi:gi + 1, :] += jnp.broadcast_to(jnp.sum(-ps * delta, axis=0, keepdims=True), (1, LANES))

        c0 = pl.multiple_of(n * w, w)
        dk_ref[0, 0, pl.ds(c0, w), :] += dkb[w:]
        dv_ref[0, 0, pl.ds(c0, w), :] += dvb[w:]

        @pl.when(n > 0)
        def _():
            p0 = pl.multiple_of((n - 1) * w, w)
            dk_ref[0, 0, pl.ds(p0, w), :] += dkb[:w]
            dv_ref[0, 0, pl.ds(p0, w), :] += dvb[:w]

    return pl.pallas_call(
        body, name=name, grid=(kvh, b, s // w),
        in_specs=[q_spec, kp_spec, kc_spec, kp_spec, kc_spec, bias_spec, sink_spec, q_spec],
        out_specs=[q_spec, kv_acc, kv_acc, bias_spec, dsink_spec],
        out_shape=[jax.ShapeDtypeStruct(q.shape, F32), jax.ShapeDtypeStruct(k.shape, F32), jax.ShapeDtypeStruct(k.shape, F32),
                   jax.ShapeDtypeStruct(bias.shape, F32), jax.ShapeDtypeStruct((kvh, g, LANES), F32)],
        compiler_params=_params(("arbitrary", "arbitrary", "arbitrary")),
    )(q, k, k, v, v, bias, sinks, do)


def _t5_bias(rel_bias):
    t = np.arange(WINDOW)[:, None]
    s = np.arange(2 * WINDOW)[None, :]
    dist = t + WINDOW - s
    max_exact = N_BUCKETS // 2
    large = max_exact + (np.log(np.maximum(dist, max_exact) / max_exact) / math.log(MAX_DISTANCE / max_exact)
                         * (N_BUCKETS - max_exact)).astype(np.int32)
    large = np.minimum(large, N_BUCKETS - 1)
    bucket = np.where(dist < max_exact, np.maximum(dist, 0), large).astype(np.int32)
    onehot = jnp.asarray(np.eye(N_BUCKETS, dtype=np.float32)[bucket])
    bias = jnp.einsum("tsb,bh->hts", onehot, rel_bias.astype(F32), precision=lax.Precision.HIGHEST)
    return bias.reshape(SW_KV_HEADS, SW_GROUP, WINDOW, 2 * WINDOW)


CONV_W = 3


def _shift_down(x, k):
    row = lax.broadcasted_iota(jnp.int32, x.shape, 0)
    return jnp.where(row >= k, pltpu.roll(x, k, axis=0), 0.0)


def _shift_up(x, k):
    n = x.shape[0]
    row = lax.broadcasted_iota(jnp.int32, x.shape, 0)
    return jnp.where(row < n - k, pltpu.roll(x, n - k, axis=0), 0.0)


def _conv3(u, w, bvec):
    return w[0:1] * _shift_down(u, 2) + w[1:2] * _shift_down(u, 1) + w[2:3] * u + bvec


def _convglu_fwd(u, w, bvec, name, out_dtype=F32):
    b, s, f2 = u.shape
    f = f2 // 2
    tc = _pick(f, 256)
    nt = f // tc

    def body(ug_ref, uu_ref, wg_ref, wu_ref, bg_ref, bu_ref, o_ref):
        cg = _conv3(ug_ref[0], wg_ref[...], bg_ref[...])
        cu = _conv3(uu_ref[0], wu_ref[...], bu_ref[...])
        o_ref[0] = (_silu(cg) * cu).astype(out_dtype)

    ug = pl.BlockSpec((1, s, tc), lambda j, bi: (bi, 0, j))
    uu = pl.BlockSpec((1, s, tc), lambda j, bi: (bi, 0, j + nt))
    wg = pl.BlockSpec((CONV_W, tc), lambda j, bi: (0, j))
    wu = pl.BlockSpec((CONV_W, tc), lambda j, bi: (0, j + nt))
    bg = pl.BlockSpec((1, tc), lambda j, bi: (0, j))
    bu = pl.BlockSpec((1, tc), lambda j, bi: (0, j + nt))
    bv = bvec.reshape(1, f2)
    return pl.pallas_call(body, name=name, grid=(nt, b), in_specs=[ug, uu, wg, wu, bg, bu], out_specs=ug,
                          out_shape=jax.ShapeDtypeStruct((b, s, f), out_dtype),
                          compiler_params=_params(("parallel", "parallel")))(u, u, w, w, bv, bv)


def _convglu_bwd(u, w, bvec, dact, name):
    b, s, f2 = u.shape
    f = f2 // 2
    tc = LANES
    nt = f // tc

    def taps(dc, uv):
        rows = [jnp.sum(dc * _shift_down(uv, 2), axis=0, keepdims=True), jnp.sum(dc * _shift_down(uv, 1), axis=0, keepdims=True),
                jnp.sum(dc * uv, axis=0, keepdims=True), jnp.sum(dc, axis=0, keepdims=True)]
        return jnp.concatenate(rows + [jnp.zeros((4, tc), F32)], axis=0)

    def back(dc, wv):
        return wv[2:3] * dc + wv[1:2] * _shift_up(dc, 1) + wv[0:1] * _shift_up(dc, 2)

    def body(ug_ref, uu_ref, wg_ref, wu_ref, bg_ref, bu_ref, da_ref, dug_ref, duu_ref, dwg_ref, dwu_ref):
        ugv, uuv, da = ug_ref[0], uu_ref[0], da_ref[0]
        cg = _conv3(ugv, wg_ref[...], bg_ref[...])
        cu = _conv3(uuv, wu_ref[...], bu_ref[...])
        sg = jax.nn.sigmoid(cg)
        dcu = da * (cg * sg)
        dcg = da * cu * (sg * (1.0 + cg * (1.0 - sg)))
        dug_ref[0] = back(dcg, wg_ref[...])
        duu_ref[0] = back(dcu, wu_ref[...])

        @pl.when(pl.program_id(1) == 0)
        def _():
            dwg_ref[...] = jnp.zeros_like(dwg_ref)
            dwu_ref[...] = jnp.zeros_like(dwu_ref)

        dwg_ref[...] += taps(dcg, ugv)
        dwu_ref[...] += taps(dcu, uuv)

    ug = pl.BlockSpec((1, s, tc), lambda j, bi: (bi, 0, j))
    uu = pl.BlockSpec((1, s, tc), lambda j, bi: (bi, 0, j + nt))
    wg = pl.BlockSpec((CONV_W, tc), lambda j, bi: (0, j))
    wu = pl.BlockSpec((CONV_W, tc), lambda j, bi: (0, j + nt))
    bg = pl.BlockSpec((1, tc), lambda j, bi: (0, j))
    bu = pl.BlockSpec((1, tc), lambda j, bi: (0, j + nt))
    acc = pl.BlockSpec((8, tc), lambda j, bi: (0, j))
    bv = bvec.reshape(1, f2)
    return pl.pallas_call(
        body, name=name, grid=(nt, b), in_specs=[ug, uu, wg, wu, bg, bu, ug], out_specs=[ug, ug, acc, acc],
        out_shape=[jax.ShapeDtypeStruct((b, s, f), F32)] * 2 + [jax.ShapeDtypeStruct((8, f), F32)] * 2,
        compiler_params=_params(("parallel", "arbitrary")),
    )(u, u, w, w, bv, bv, dact)


def _adamw(w, g, m, v, name):
    r, c = w.shape
    tr = _row_tile(r, c)
    c1 = 1.0 - ADAM_B1 ** ADAM_STEP
    c2 = 1.0 - ADAM_B2 ** ADAM_STEP

    def body(w_ref, g_ref, m_ref, v_ref, d_ref, mo_ref, vo_ref):
        gv = g_ref[...]
        mn = ADAM_B1 * m_ref[...] + (1.0 - ADAM_B1) * gv
        vn = ADAM_B2 * v_ref[...] + (1.0 - ADAM_B2) * (gv * gv)
        d_ref[...] = -ADAM_LR * ((mn / c1) / (jnp.sqrt(vn / c2) + ADAM_EPS) + ADAM_WD * w_ref[...])
        mo_ref[...] = mn
        vo_ref[...] = vn

    blk = pl.BlockSpec((tr, c), lambda i: (i, 0))
    return pl.pallas_call(body, name=name, grid=(r // tr,), in_specs=[blk] * 4, out_specs=[blk] * 3,
                          out_shape=[jax.ShapeDtypeStruct((r, c), F32)] * 3, compiler_params=_params(("parallel",)))(w, g, m, v)


MESH = pl.DeviceIdType.MESH
ANY = pl.BlockSpec(memory_space=pl.ANY)


def _position():
    return lax.axis_index("x"), lax.axis_index("y"), lax.axis_index("c")


def _all_gather(x, name):
    r, c = x.shape

    def body(x_ref, out_ref, send_sems, recv_sems, local_sem):
        mx, my, mc = _position()
        me, sibling = (mx, my, mc), (mx, my, 1 - mc)
        chips = [(1 - mx, my), (mx, 1 - my), (1 - mx, 1 - my)]

        def slot(px, py, pc):
            return out_ref.at[4 * px + 2 * py + pc]

        def copy(k, block, to, src=None):
            return pltpu.make_async_remote_copy(
                src_ref=slot(*block) if src is None else src, dst_ref=slot(*block),
                send_sem=send_sems.at[k], recv_sem=recv_sems.at[k], device_id=to, device_id_type=MESH)

        mine = pltpu.make_async_copy(x_ref, slot(*me), local_sem.at[0])
        mine.start()
        first = [copy(0, me, sibling, src=x_ref)]
        first += [copy(1 + j, me, (*chip, mc), src=x_ref) for j, chip in enumerate(chips)]
        for cp in first:
            cp.start()
        passed = [copy(4 + j, (*chip, mc), sibling) for j, chip in enumerate(chips)]
        for j, chip in enumerate(chips):
            copy(1 + j, (*chip, mc), me).wait_recv()
            passed[j].start()
        copy(0, sibling, me).wait_recv()
        for j, chip in enumerate(chips):
            copy(4 + j, (*chip, 1 - mc), me).wait_recv()
        for cp in first + passed:
            cp.wait_send()
        mine.wait()

    return pl.pallas_call(
        body, name=name, out_shape=jax.ShapeDtypeStruct((N_DEV, r, c), x.dtype), in_specs=[ANY], out_specs=ANY,
        scratch_shapes=[pltpu.SemaphoreType.DMA((7,)), pltpu.SemaphoreType.DMA((7,)), pltpu.SemaphoreType.DMA((1,))],
    )(x)


def _dev_rows(ref, dev, a):
    return ref.at[:, pl.ds(pl.multiple_of(dev * a, 16), a), :]


def _all_gather_rows(shards, name):
    nt = len(shards)

    def body(*refs):
        x_refs, out_refs = refs[:nt], refs[nt:2 * nt]
        send_sems, recv_sems, local_sems = refs[2 * nt:]
        mx, my, mc = _position()
        me, sibling = (mx, my, mc), (mx, my, 1 - mc)
        chips = [(1 - mx, my), (mx, 1 - my), (1 - mx, 1 - my)]

        def slot(t, px, py, pc):
            return _dev_rows(out_refs[t], 4 * px + 2 * py + pc, shards[t].shape[1])

        def copy(t, k, block, to, src=None):
            return pltpu.make_async_remote_copy(
                src_ref=slot(t, *block) if src is None else src, dst_ref=slot(t, *block),
                send_sem=send_sems.at[7 * t + k], recv_sem=recv_sems.at[7 * t + k], device_id=to, device_id_type=MESH)

        mine = [pltpu.make_async_copy(x_refs[t], slot(t, *me), local_sems.at[t]) for t in range(nt)]
        first = [copy(t, 0, me, sibling, src=x_refs[t]) for t in range(nt)]
        first += [copy(t, 1 + j, me, (*chip, mc), src=x_refs[t]) for j, chip in enumerate(chips) for t in range(nt)]
        for cp in mine + first:
            cp.start()
        passed = []
        for j, chip in enumerate(chips):
            for t in range(nt):
                copy(t, 1 + j, (*chip, mc), me).wait_recv()
                fwd = copy(t, 4 + j, (*chip, mc), sibling)
                fwd.start()
                passed.append(fwd)
        for t in range(nt):
            copy(t, 0, sibling, me).wait_recv()
        for j, chip in enumerate(chips):
            for t in range(nt):
                copy(t, 4 + j, (*chip, 1 - mc), me).wait_recv()
        for cp in first + passed:
            cp.wait_send()
        for cp in mine:
            cp.wait()

    out_shape = [jax.ShapeDtypeStruct((x.shape[0], N_DEV * x.shape[1], x.shape[2]), x.dtype) for x in shards]
    return pl.pallas_call(
        body, name=name, out_shape=out_shape, in_specs=[ANY] * nt, out_specs=[ANY] * nt,
        scratch_shapes=[pltpu.SemaphoreType.DMA((7 * nt,)), pltpu.SemaphoreType.DMA((7 * nt,)), pltpu.SemaphoreType.DMA((nt,))],
    )(*shards)


def _rs_pair(gs, name):
    nt = len(gs)

    def body(*refs):
        g_refs, a_refs = refs[:nt], refs[nt:2 * nt]
        send_sems, recv_sems = refs[2 * nt:]
        mx, my, mc = _position()
        copies = [pltpu.make_async_remote_copy(
            src_ref=_dev_rows(g_refs[t], 2 * j + 1 - mc, gs[t].shape[1] // N_DEV), dst_ref=a_refs[t].at[j],
            send_sem=send_sems.at[4 * t + j], recv_sem=recv_sems.at[4 * t + j],
            device_id=(mx, my, 1 - mc), device_id_type=MESH) for t in range(nt) for j in range(4)]
        for cp in copies:
            cp.start()
        for cp in copies:
            cp.wait()

    out_shape = [jax.ShapeDtypeStruct((4, g.shape[0], g.shape[1] // N_DEV, g.shape[2]), g.dtype) for g in gs]
    return pl.pallas_call(
        body, name=name, out_shape=out_shape, in_specs=[ANY] * nt, out_specs=[ANY] * nt,
        scratch_shapes=[pltpu.SemaphoreType.DMA((4 * nt,)), pltpu.SemaphoreType.DMA((4 * nt,))],
    )(*gs)


def _rs_chips(ps, name):
    nt = len(ps)

    def body(*refs):
        p_refs, b_refs = refs[:nt], refs[nt:2 * nt]
        send_sems, recv_sems = refs[2 * nt:]
        mx, my, mc = _position()
        chips = [(1 - mx, my), (mx, 1 - my), (1 - mx, 1 - my)]
        copies = [pltpu.make_async_remote_copy(
            src_ref=p_refs[t].at[2 * cx + cy], dst_ref=b_refs[t].at[k],
            send_sem=send_sems.at[3 * t + k], recv_sem=recv_sems.at[3 * t + k],
            device_id=(cx, cy, mc), device_id_type=MESH) for t in range(nt) for k, (cx, cy) in enumerate(chips)]
        for cp in copies:
            cp.start()
        for cp in copies:
            cp.wait()

    out_shape = [jax.ShapeDtypeStruct((3,) + p.shape[1:], p.dtype) for p in ps]
    return pl.pallas_call(
        body, name=name, out_shape=out_shape, in_specs=[ANY] * nt, out_specs=[ANY] * nt,
        scratch_shapes=[pltpu.SemaphoreType.DMA((3 * nt,)), pltpu.SemaphoreType.DMA((3 * nt,))],
    )(*ps)


def _div_tile(a, b):
    best = 16
    for t in range(16, a + 1, 16):
        if a % t == 0 and t * b * 4 <= 2 * 1024 * 1024:
            best = t
    return best


def _rs_add_pair(g, a, core, name):
    l, a8, b = g.shape
    rows = a8 // N_DEV
    ta = _div_tile(rows, b)

    def body(core_ref, g_ref, a_ref, p_ref):
        p_ref[...] = (g_ref[...] + a_ref[...]).astype(BF16)

    grid_spec = pltpu.PrefetchScalarGridSpec(
        num_scalar_prefetch=1, grid=(4, l, rows // ta),
        in_specs=[pl.BlockSpec((1, 1, ta, b), lambda j, li, i, core_ref: (li, 2 * j + core_ref[0], i, 0)),
                  pl.BlockSpec((1, 1, ta, b), lambda j, li, i, core_ref: (j, li, i, 0))],
        out_specs=pl.BlockSpec((1, 1, ta, b), lambda j, li, i, core_ref: (j, li, i, 0)))
    return pl.pallas_call(body, name=name, grid_spec=grid_spec, out_shape=jax.ShapeDtypeStruct((4, l, rows, b), BF16),
                          compiler_params=_params(("parallel", "parallel", "parallel")))(core, g.reshape(l, N_DEV, rows, b), a)


def _rs_final(g, a, bsum, where, name):
    l, a8, b = g.shape
    rows = a8 // N_DEV
    ta = _div_tile(rows, b)

    def body(where_ref, g_ref, a_ref, b_ref, o_ref):
        own = g_ref[0, 0] + a_ref[0, 0]
        o_ref[0] = ((own + b_ref[0, 0].astype(F32)) + b_ref[1, 0].astype(F32)) + b_ref[2, 0].astype(F32)

    grid_spec = pltpu.PrefetchScalarGridSpec(
        num_scalar_prefetch=1, grid=(l, rows // ta),
        in_specs=[pl.BlockSpec((1, 1, ta, b), lambda li, i, w_ref: (li, w_ref[0], i, 0)),
                  pl.BlockSpec((1, 1, ta, b), lambda li, i, w_ref: (w_ref[1], li, i, 0)),
                  pl.BlockSpec((3, 1, ta, b), lambda li, i, w_ref: (0, li, i, 0))],
        out_specs=pl.BlockSpec((1, ta, b), lambda li, i, w_ref: (li, i, 0)))
    return pl.pallas_call(body, name=name, grid_spec=grid_spec, out_shape=jax.ShapeDtypeStruct((l, rows, b), F32),
                          compiler_params=_params(("parallel", "parallel")))(where, g.reshape(l, N_DEV, rows, b), a, bsum)


def _sum_devices(x, name):
    _, r, c = x.shape

    def body(x_ref, o_ref):
        acc = x_ref[0]
        for d in range(1, N_DEV):
            acc = acc + x_ref[d]
        o_ref[...] = acc

    return pl.pallas_call(body, name=name, out_shape=jax.ShapeDtypeStruct((r, c), F32))(x)


BIG = (("ab_w_in", "col"), ("ab_w_out", "row"), ("c_w_in", "col"), ("c_w_out", "row"),
       ("ffn_up", "col"), ("ffn_down", "row"), ("ple_gate", "row"), ("ple_proj", "col"))
KIND = dict(BIG)


def _row_block(shard, kind):
    return shard.transpose(0, 2, 1) if kind == "col" else shard


def _pad_rows(flat):
    pad = -flat.shape[0] % (8 * LANES)
    return jnp.pad(flat, (0, pad)).reshape(-1, LANES)


def _heads_out(x, b, s, nh, d):
    return x.reshape(b, s, nh, d).transpose(0, 2, 1, 3).reshape(b * nh, s, d)


def _heads_in(x, b, s, nh, d):
    return x.reshape(b, nh, s, d).transpose(0, 2, 1, 3).reshape(b * s, nh * d)


def _lower_bounds(logits):
    c = jnp.cumsum(jax.nn.softmax(logits.astype(F32), axis=0), axis=0)
    return c - c[0]


SB_COLS = tuple(k * SB_WIDTH // LANES for k in range(3))
HG_COLS = tuple((3 * SB_WIDTH + k * HG_W) // LANES for k in range(3))
HG_GATE_COL = (3 * SB_WIDTH + 3 * HG_W) // LANES
HG_OUT_COL = SB_WIDTH // LANES


def _forward_backward(x, p, target, W, P):
    b, s, dm = x.shape
    n = b * s
    h = x.reshape(n, dm)
    lbs, lb_vjp = jax.vjp(_lower_bounds, P["hg_lb_logits"])
    bias, bias_vjp = jax.vjp(_t5_bias, P["rel_bias"])
    saved = []
    gw = {name: lax.empty(W[name].shape, F32) for name, _ in BIG}

    def times_w(a, name, l, tag, res=None):
        return _mm(a, W[name], "nt" if KIND[name] == "col" else "nn", tag, res=res, layer=l)

    def times_wt(dy, name, l, tag):
        return _mm(dy, W[name], "nn" if KIND[name] == "col" else "nt", tag, layer=l)

    def grad_w(a, dy, name, l, tag):
        lhs, rhs = (dy, a) if KIND[name] == "col" else (a, dy)
        gw[name] = _mm(lhs, rhs, "tn", tag, into=(gw[name], l))

    for i in range(DEPTH):
        j = i // 2
        r = {"h0": h}
        hn = _rms_fwd(h, P["mix_norm"][i], f"mix_norm_f{i}", BF16)
        r["hn"] = hn
        if i % 2 == 0:
            proj = times_w(hn, "ab_w_in", j, f"ab_in_f{i}")
            oa, lta = _sb_fwd(proj, SB_COLS, b, s, f"sb_f{i}")
            ob, st = _hg_fwd(proj, HG_COLS, lbs[j].reshape(HG_HEADS, HG_DK), b, s, f"hg_f{i}")
            obg = _gnorm_fwd(ob, proj, HG_GATE_COL, P["hg_out_norm"][j], f"hg_norm_f{i}")
            cat = jnp.concatenate([oa, obg], axis=1).astype(BF16)
            h = times_w(cat, "ab_w_out", j, f"ab_out_f{i}", res=h)
            r.update(lta=lta, proj=proj, ob=ob, st=st, cat=cat)
        else:
            proj = times_w(hn, "c_w_in", j, f"c_in_f{i}")
            nq = SW_HEADS * SW_DIM
            nkv = SW_KV_HEADS * SW_DIM
            q = _heads_out(proj[:, :nq], b, s, SW_HEADS, SW_DIM).reshape(-1, SW_DIM)
            k = _heads_out(proj[:, nq:nq + nkv], b, s, SW_KV_HEADS, SW_DIM).reshape(-1, SW_DIM)
            v = _heads_out(proj[:, nq + nkv:], b, s, SW_KV_HEADS, SW_DIM).reshape(b, SW_KV_HEADS, s, SW_DIM)
            qn = _rms_fwd(q, P["q_norm"][j], f"q_norm_f{i}").reshape(b, SW_KV_HEADS, SW_GROUP, s, SW_DIM)
            kn = _rms_fwd(k, P["k_norm"][j], f"k_norm_f{i}").reshape(b, SW_KV_HEADS, s, SW_DIM)
            o = _swa_fwd(qn, kn, v, P["sinks"][j], bias, f"swa_f{i}")
            o2 = _heads_in(o.reshape(b * SW_HEADS, s, SW_DIM), b, s, SW_HEADS, SW_DIM).astype(BF16)
            h = times_w(o2, "c_w_out", j, f"c_out_f{i}", res=h)
            r.update(q=q, k=k, v=v, qn=qn, kn=kn, o2=o2)
        r["h1"] = h
        hn2 = _rms_fwd(h, P["ffn_norm"][i], f"ffn_norm_f{i}", BF16)
        u = times_w(hn2, "ffn_up", i, f"ffn_up_f{i}").reshape(b, s, 2 * D_FF)
        act = _convglu_fwd(u, W["ffn_conv"][i], P["ffn_conv_b"][i], f"conv_f{i}", BF16).reshape(n, D_FF)
        h = times_w(act, "ffn_down", i, f"ffn_down_f{i}", res=h)
        r.update(hn2=hn2, u=u, act=act, h2=h)
        hn3 = _rms_fwd(h, P["ple_norm"][i], f"ple_norm_f{i}", BF16)
        z = times_w(hn3, "ple_gate", i, f"ple_gate_f{i}")
        pi = p[i].reshape(n, PLE_DIM)
        e = times_w(pi, "ple_proj", i, f"ple_proj_f{i}")
        h = _sigmul_fwd(z, e, h, f"ple_f{i}")
        r.update(hn3=hn3, z=z, e=e, pi=pi)
        saved.append(r)

    loss, dh = _loss_fwd(h, target.reshape(n, dm), "loss")

    gconv = [None] * DEPTH
    gp = {name: [None] * P[name].shape[0] for name in ("mix_norm", "hg_out_norm", "q_norm", "k_norm", "sinks",
                                                        "ffn_norm", "ffn_conv_b", "ple_norm")}
    dlbs = [None] * (DEPTH // 2)
    dbias = jnp.zeros_like(bias)

    for i in reversed(range(DEPTH)):
        j = i // 2
        r = saved[i]
        dz, de = _sigmul_bwd(r["z"], r["e"], dh, f"ple_b{i}")
        grad_w(r["pi"], de, "ple_proj", i, f"ple_proj_g{i}")
        grad_w(r["hn3"], dz, "ple_gate", i, f"ple_gate_g{i}")
        dhn3 = times_wt(dz, "ple_gate", i, f"ple_gate_b{i}")
        dh, gp["ple_norm"][i] = _rms_bwd(r["h2"], P["ple_norm"][i], dhn3, f"ple_norm_b{i}", res=dh)

        dact = times_wt(dh, "ffn_down", i, f"ffn_down_b{i}").reshape(b, s, D_FF)
        grad_w(r["act"], dh, "ffn_down", i, f"ffn_down_g{i}")
        dug, duu, ag, au = _convglu_bwd(r["u"], W["ffn_conv"][i], P["ffn_conv_b"][i], dact, f"conv_b{i}")
        du = jnp.concatenate([dug, duu], axis=-1).reshape(n, 2 * D_FF)
        gconv[i] = jnp.concatenate([ag[:CONV_W], au[:CONV_W]], axis=-1)
        gp["ffn_conv_b"][i] = jnp.concatenate([ag[CONV_W], au[CONV_W]], axis=-1)
        grad_w(r["hn2"], du, "ffn_up", i, f"ffn_up_g{i}")
        dhn2 = times_wt(du, "ffn_up", i, f"ffn_up_b{i}")
        dh, gp["ffn_norm"][i] = _rms_bwd(r["h1"], P["ffn_norm"][i], dhn2, f"ffn_norm_b{i}", res=dh)

        if i % 2 == 0:
            dcat = times_wt(dh, "ab_w_out", j, f"ab_out_b{i}")
            grad_w(r["cat"], dh, "ab_w_out", j, f"ab_out_g{i}")
            dob, dgb, gp["hg_out_norm"][j] = _gnorm_bwd(r["ob"], r["proj"], HG_GATE_COL, P["hg_out_norm"][j], dcat, HG_OUT_COL,
                                                        f"hg_norm_b{i}")
            dqb, dfb, dib, dlb = _hg_bwd(r["proj"], HG_COLS, lbs[j].reshape(HG_HEADS, HG_DK), r["st"], dob, b, s, f"hg_b{i}")
            dlbs[j] = dlb.reshape(b, HG_W).sum(axis=0)
            dqa, dka, dva = _sb_bwd(r["proj"], SB_COLS, r["lta"], dcat, 0, b, s, f"sb_b{i}")
            dproj = jnp.concatenate([dqa, dka, dva, dqb, dfb, dib, dgb], axis=1)
            grad_w(r["hn"], dproj, "ab_w_in", j, f"ab_in_g{i}")
            dhn = times_wt(dproj, "ab_w_in", j, f"ab_in_b{i}")
        else:
            do2 = times_wt(dh, "c_w_out", j, f"c_out_b{i}")
            grad_w(r["o2"], dh, "c_w_out", j, f"c_out_g{i}")
            do = _heads_out(do2, b, s, SW_HEADS, SW_DIM).reshape(b, SW_KV_HEADS, SW_GROUP, s, SW_DIM)
            dqn, dkn, dv, dbias_i, dsink = _swa_bwd(r["qn"], r["kn"], r["v"], P["sinks"][j], bias, do, f"swa_b{i}")
            dbias = dbias + dbias_i
            gp["sinks"][j] = dsink[:, :, 0].reshape(SW_HEADS)
            dq, gp["q_norm"][j] = _rms_bwd(r["q"], P["q_norm"][j], dqn.reshape(-1, SW_DIM), f"q_norm_b{i}")
            dk, gp["k_norm"][j] = _rms_bwd(r["k"], P["k_norm"][j], dkn.reshape(-1, SW_DIM), f"k_norm_b{i}")
            dproj = jnp.concatenate([_heads_in(dq.reshape(b * SW_HEADS, s, SW_DIM), b, s, SW_HEADS, SW_DIM),
                                     _heads_in(dk.reshape(b * SW_KV_HEADS, s, SW_DIM), b, s, SW_KV_HEADS, SW_DIM),
                                     _heads_in(dv.reshape(b * SW_KV_HEADS, s, SW_DIM), b, s, SW_KV_HEADS, SW_DIM)], axis=1)
            grad_w(r["hn"], dproj, "c_w_in", j, f"c_in_g{i}")
            dhn = times_wt(dproj, "c_w_in", j, f"c_in_b{i}")
        dh, gp["mix_norm"][i] = _rms_bwd(r["h0"], P["mix_norm"][i], dhn, f"mix_norm_b{i}", res=dh)

    gp = {name: jnp.stack(v) for name, v in gp.items()}
    gp["hg_lb_logits"] = lb_vjp(jnp.stack(dlbs))[0]
    gp["rel_bias"] = bias_vjp(dbias)[0]
    return loss[0, 0], dh.reshape(b, s, dm), gw, jnp.stack(gconv), gp


WEIGHTS = ("mix_norm", "ab_w_in", "hg_lb_logits", "hg_out_norm", "ab_w_out", "c_w_in", "q_norm", "k_norm", "sinks", "rel_bias",
           "c_w_out", "ffn_norm", "ffn_up", "ffn_conv", "ffn_conv_b", "ffn_down", "ple_norm", "ple_gate", "ple_proj")
SMALL = ("mix_norm", "hg_lb_logits", "hg_out_norm", "q_norm", "k_norm", "sinks", "rel_bias", "ffn_norm", "ffn_conv_b", "ple_norm")


def _step(x, p, target, w, m, v):
    names = [name for name, _ in BIG]
    mx, my, mc = _position()
    dev = 4 * mx + 2 * my + mc

    blocks = [_row_block(w[name], KIND[name]).astype(BF16) for name in names]
    full = dict(zip(names, _all_gather_rows(blocks, "gather_weights")))
    nl, taps, cs = w["ffn_conv"].shape
    conv_all = _all_gather(_pad_rows(w["ffn_conv"].reshape(-1)), "gather_conv").reshape(N_DEV, -1)[:, :nl * taps * cs]
    full["ffn_conv"] = conv_all.reshape(N_DEV, nl, taps, cs).transpose(1, 2, 0, 3).reshape(nl, taps, N_DEV * cs)

    small = {name: w[name] for name in SMALL}
    loss, grad_x, gw, gconv, gp = _forward_backward(x, p, target, full, small)

    core = jnp.reshape(mc, (1,)).astype(jnp.int32)
    where = jnp.stack([dev, 2 * mx + my]).astype(jnp.int32)
    parts = [gw[name] for name in names]
    from_sibling = _rs_pair(parts, "reduce_pair")
    chip_sums = [_rs_add_pair(g, a, core, f"reduce_pair_add_{name}") for name, g, a in zip(names, parts, from_sibling)]
    from_chips = _rs_chips(chip_sums, "reduce_chips")
    grads = {name: _row_block(_rs_final(g, a, bs, where, f"reduce_final_{name}"), KIND[name])
             for name, g, a, bs in zip(names, parts, from_sibling, from_chips)}

    flat_small = jnp.concatenate([gp[name].reshape(-1) for name in SMALL] + [gconv.reshape(-1), loss.reshape(1)])
    small_sum = _sum_devices(_all_gather(_pad_rows(flat_small), "gather_small"), "sum_small").reshape(-1)
    e0 = 0
    for name in SMALL:
        cnt = math.prod(w[name].shape)
        grads[name] = small_sum[e0:e0 + cnt].reshape(w[name].shape)
        e0 += cnt
    gconv_sum = small_sum[e0:e0 + gconv.size].reshape(gconv.shape)
    grads["ffn_conv"] = lax.dynamic_slice_in_dim(gconv_sum, dev * cs, cs, axis=2)
    loss = small_sum[e0 + gconv.size]

    deltas, new_m, new_v = {}, {}, {}
    for name in WEIGHTS:
        shape = w[name].shape
        view = (-1, shape[-1]) if len(shape) > 1 else (1, -1)
        d_, m_, v_ = _adamw(w[name].reshape(view), grads[name].reshape(view), m[name].reshape(view), v[name].reshape(view), f"adamw_{name}")
        deltas[name], new_m[name], new_v[name] = d_.reshape(shape), m_.reshape(shape), v_.reshape(shape)
    return (loss, grad_x, *[grads[k] for k in WEIGHTS], *[deltas[k] for k in WEIGHTS],
            *[new_m[k] for k in WEIGHTS], *[new_v[k] for k in WEIGHTS])


def kernel(x, p, mix_norm, ab_w_in, hg_lb_logits, hg_out_norm, ab_w_out, c_w_in, q_norm, k_norm, sinks, rel_bias, c_w_out, ffn_norm, ffn_up, ffn_conv, ffn_conv_b, ffn_down, ple_norm, ple_gate, ple_proj, loss_target, m_mix_norm, m_ab_w_in, m_hg_lb_logits, m_hg_out_norm, m_ab_w_out, m_c_w_in, m_q_norm, m_k_norm, m_sinks, m_rel_bias, m_c_w_out, m_ffn_norm, m_ffn_up, m_ffn_conv, m_ffn_conv_b, m_ffn_down, m_ple_norm, m_ple_gate, m_ple_proj, v_mix_norm, v_ab_w_in, v_hg_lb_logits, v_hg_out_norm, v_ab_w_out, v_c_w_in, v_q_norm, v_k_norm, v_sinks, v_rel_bias, v_c_w_out, v_ffn_norm, v_ffn_up, v_ffn_conv, v_ffn_conv_b, v_ffn_down, v_ple_norm, v_ple_gate, v_ple_proj):
    w = dict(zip(WEIGHTS, (mix_norm, ab_w_in, hg_lb_logits, hg_out_norm, ab_w_out, c_w_in, q_norm, k_norm, sinks, rel_bias, c_w_out,
                           ffn_norm, ffn_up, ffn_conv, ffn_conv_b, ffn_down, ple_norm, ple_gate, ple_proj)))
    m = dict(zip(WEIGHTS, (m_mix_norm, m_ab_w_in, m_hg_lb_logits, m_hg_out_norm, m_ab_w_out, m_c_w_in, m_q_norm, m_k_norm, m_sinks,
                           m_rel_bias, m_c_w_out, m_ffn_norm, m_ffn_up, m_ffn_conv, m_ffn_conv_b, m_ffn_down, m_ple_norm, m_ple_gate,
                           m_ple_proj)))
    v = dict(zip(WEIGHTS, (v_mix_norm, v_ab_w_in, v_hg_lb_logits, v_hg_out_norm, v_ab_w_out, v_c_w_in, v_q_norm, v_k_norm, v_sinks,
                           v_rel_bias, v_c_w_out, v_ffn_norm, v_ffn_up, v_ffn_conv, v_ffn_conv_b, v_ffn_down, v_ple_norm, v_ple_gate,
                           v_ple_proj)))
    return _step(x, p, loss_target, w, m, v)
```

```python
import functools
import math

import numpy as np
import jax
import jax.numpy as jnp
from jax import lax
from jax.experimental import pallas as pl
from jax.experimental.pallas import tpu as pltpu

F32 = jnp.float32
BF16 = jnp.bfloat16

D_MODEL = 1024
DEPTH = 4
PLE_DIM = 256
EPS = 1e-6
SB_HEADS, SB_DIM = 8, 64
SB_WIDTH = SB_HEADS * SB_DIM
HG_HEADS, HG_DK, HG_DV = 4, 128, 128
HG_W = HG_HEADS * HG_DK
AB_IN = 3 * SB_WIDTH + 4 * HG_W
SW_HEADS, SW_KV_HEADS, SW_DIM = 16, 4, 64
SW_GROUP = SW_HEADS // SW_KV_HEADS
WINDOW = 128
C_IN = (SW_HEADS + 2 * SW_KV_HEADS) * SW_DIM
N_BUCKETS, MAX_DISTANCE = 32, 128
D_FF = 2816
N_DEV = 8

ADAM_LR, ADAM_B1, ADAM_B2, ADAM_EPS, ADAM_WD, ADAM_STEP = 0.001, 0.9, 0.999, 1e-08, 0.01, 10

LANES = 128
VMEM_LIMIT = 48 * 1024 * 1024

NN = (((1,), (0,)), ((), ()))
NT = (((1,), (1,)), ((), ()))
TN = (((0,), (0,)), ((), ()))


MXU_DTYPE = BF16


def _bf(x):
    return x.astype(MXU_DTYPE)


def _dot(a, b, dims=NN):
    return lax.dot_general(_bf(a), _bf(b), dims, preferred_element_type=F32)


def _split3(x):
    x1 = _bf(x)
    r = x - x1.astype(F32)
    x2 = _bf(r)
    x3 = _bf(r - x2.astype(F32))
    return x1, x2, x3


def _dot_exact_lhs01(m, x, terms=3):
    parts = _split3(x)[:terms]
    out = lax.dot_general(m, parts[0], NN, preferred_element_type=F32)
    for p_ in parts[1:]:
        out = out + lax.dot_general(m, p_, NN, preferred_element_type=F32)
    return out


def _dot_exact_rhs01(x, m, terms=2):
    parts = _split3(x)[:terms]
    out = lax.dot_general(parts[0], m, NN, preferred_element_type=F32)
    for p_ in parts[1:]:
        out = out + lax.dot_general(p_, m, NN, preferred_element_type=F32)
    return out


def _pick(n, target):
    best = None
    for t in range(LANES, target + 1, LANES):
        if n % t == 0:
            best = t
    return best or n


def _params(sem=None):
    return pltpu.CompilerParams(dimension_semantics=sem, vmem_limit_bytes=VMEM_LIMIT)


def _mm(a, b, mode, name, res=None, out_dtype=F32, layer=None, b_rows=None, into=None):
    bshape = b.shape if layer is None else b.shape[1:]
    if b_rows is not None:
        assert mode == "nn"
        bshape = (b_rows[1], bshape[1])
    if mode == "nn":
        (M, K), (K2, N) = a.shape, bshape
    elif mode == "nt":
        (M, K), (N, K2) = a.shape, bshape
    else:
        (K, M), (K2, N) = a.shape, bshape
    assert K == K2, (a.shape, b.shape, mode)
    tm, tn, tk = _pick(M, 1024), _pick(N, 1024), _pick(K, 1536)
    nk = K // tk
    k_off = 0 if b_rows is None else b_rows[0] // tk
    assert b_rows is None or b_rows[0] % tk == 0
    dims = {"nn": NN, "nt": NT, "tn": TN}[mode]
    a_spec = pl.BlockSpec((tk, tm), lambda i, j, k: (k, i)) if mode == "tn" else pl.BlockSpec((tm, tk), lambda i, j, k: (i, k))
    if layer is None:
        b_spec = pl.BlockSpec((tn, tk), lambda i, j, k: (j, k)) if mode == "nt" else pl.BlockSpec((tk, tn), lambda i, j, k: (k, j))
    elif mode == "nt":
        b_spec = pl.BlockSpec((None, tn, tk), lambda i, j, k: (layer, j, k))
    else:
        b_spec = pl.BlockSpec((None, tk, tn), lambda i, j, k: (layer, k + k_off, j))
    o_spec = pl.BlockSpec((tm, tn), lambda i, j, k: (i, j))
    has_res = res is not None

    def finish(acc, r_ref, o_ref):
        if has_res:
            acc = acc + r_ref[...]
        o_ref[...] = acc.astype(out_dtype)

    def body(*refs):
        a_ref, b_ref = refs[0], refs[1]
        r_ref = refs[2] if has_res else None
        o_ref = refs[-1] if nk == 1 else refs[-2]
        part = _dot(a_ref[...], b_ref[...], dims)
        if nk == 1:
            finish(part, r_ref, o_ref)
            return
        acc_ref = refs[-1]
        k = pl.program_id(2)

        @pl.when(k == 0)
        def _():
            acc_ref[...] = part

        @pl.when((k > 0) & (k < nk - 1))
        def _():
            acc_ref[...] += part

        @pl.when(k == nk - 1)
        def _():
            finish(acc_ref[...] + part, r_ref, o_ref)

    in_specs = [a_spec, b_spec] + ([o_spec] if has_res else [])
    args = (a, b) + ((res,) if has_res else ())
    out_shape, aliases = jax.ShapeDtypeStruct((M, N), out_dtype), {}
    if into is not None:
        stack, slot, row = into
        assert stack.shape[2] == N and row % tm == 0 and row + M <= stack.shape[1] and stack.dtype == out_dtype and not has_res
        in_specs = in_specs + [pl.BlockSpec(memory_space=pl.ANY)]
        args = args + (stack,)
        o_spec = pl.BlockSpec((None, tm, tn), lambda i, j, k: (slot, i + row // tm, j))
        out_shape, aliases = jax.ShapeDtypeStruct(stack.shape, out_dtype), {2: 0}

    def body_into(a_ref, b_ref, stack_ref, *rest):
        body(a_ref, b_ref, *rest)

    return pl.pallas_call(
        body if into is None else body_into, name=name, grid=(M // tm, N // tn, nk), in_specs=in_specs, out_specs=o_spec,
        out_shape=out_shape, scratch_shapes=[] if nk == 1 else [pltpu.VMEM((tm, tn), F32)], input_output_aliases=aliases,
        compiler_params=_params(("parallel", "parallel", "arbitrary")),
    )(*args)


def _row_tile(n, d):
    if n % 8:
        return n
    t = 8
    while t * 2 <= min(n, (256 * 1024) // d) and n % (t * 2) == 0:
        t *= 2
    return t


def _rms_fwd(x, g, name, out_dtype=F32):
    n, d = x.shape
    tm = _row_tile(n, d)

    def body(x_ref, g_ref, o_ref):
        xf = x_ref[...]
        r = lax.rsqrt(jnp.mean(xf * xf, axis=-1, keepdims=True) + EPS)
        o_ref[...] = (xf * r * g_ref[...]).astype(out_dtype)

    return pl.pallas_call(
        body, name=name, grid=(n // tm,),
        in_specs=[pl.BlockSpec((tm, d), lambda i: (i, 0)), pl.BlockSpec((1, d), lambda i: (0, 0))],
        out_specs=pl.BlockSpec((tm, d), lambda i: (i, 0)),
        out_shape=jax.ShapeDtypeStruct((n, d), out_dtype), compiler_params=_params(("parallel",)),
    )(x, g.reshape(1, d))


def _rms_bwd(x, g, dy, name, res=None):
    n, d = x.shape
    tm = _row_tile(n, d)
    has_res = res is not None

    def body(*refs):
        x_ref, g_ref, dy_ref = refs[:3]
        r_ref = refs[3] if has_res else None
        dx_ref, dg_ref = refs[-2:]
        xf = x_ref[...]
        r = lax.rsqrt(jnp.mean(xf * xf, axis=-1, keepdims=True) + EPS)
        xh = xf * r
        dyf = dy_ref[...].astype(F32)
        dxh = dyf * g_ref[...]
        dx = r * (dxh - xh * jnp.mean(dxh * xh, axis=-1, keepdims=True))
        if has_res:
            dx = dx + r_ref[...]
        dx_ref[...] = dx

        @pl.when(pl.program_id(0) == 0)
        def _():
            dg_ref[...] = jnp.zeros_like(dg_ref)

        dg_ref[...] += jnp.sum(dyf * xh, axis=0, keepdims=True)

    row = pl.BlockSpec((tm, d), lambda i: (i, 0))
    vec = pl.BlockSpec((1, d), lambda i: (0, 0))
    dx, dg = pl.pallas_call(
        body, name=name, grid=(n // tm,),
        in_specs=[row, vec, row] + ([row] if has_res else []),
        out_specs=[row, vec],
        out_shape=[jax.ShapeDtypeStruct((n, d), F32), jax.ShapeDtypeStruct((1, d), F32)],
        compiler_params=_params(("arbitrary",)),
    )(x, g.reshape(1, d), dy, *((res,) if has_res else ()))
    return dx, dg.reshape(d)


def _silu(x):
    return x * jax.nn.sigmoid(x)


def _gnorm_fwd(o, gate, gate_col, w, name):
    n, width = o.shape
    d = w.shape[0]
    tm = _row_tile(n, d)

    def body(o_ref, g_ref, w_ref, y_ref):
        of = o_ref[...]
        r = lax.rsqrt(jnp.mean(of * of, axis=-1, keepdims=True) + EPS)
        y_ref[...] = of * r * w_ref[...] * _silu(g_ref[...])

    row = pl.BlockSpec((tm, d), lambda i, h: (i, h))
    vec = pl.BlockSpec((1, d), lambda i, h: (0, 0))
    return pl.pallas_call(body, name=name, grid=(n // tm, width // d),
                          in_specs=[row, pl.BlockSpec((tm, d), lambda i, h: (i, gate_col + h)), vec], out_specs=row,
                          out_shape=jax.ShapeDtypeStruct((n, width), F32), compiler_params=_params(("parallel", "parallel")))(o, gate, w.reshape(1, d))


def _gnorm_bwd(o, gate, gate_col, w, dy, dy_col, name):
    n, width = o.shape
    d = w.shape[0]
    tm = _row_tile(n, d)

    def body(o_ref, g_ref, w_ref, dy_ref, do_ref, dgate_ref, dw_ref):
        of, gf, dyf = o_ref[...], g_ref[...], dy_ref[...]
        r = lax.rsqrt(jnp.mean(of * of, axis=-1, keepdims=True) + EPS)
        xh = of * r
        sg = jax.nn.sigmoid(gf)
        sil = gf * sg
        dnorm = dyf * sil
        dgate_ref[...] = dyf * xh * w_ref[...] * (sg * (1.0 + gf * (1.0 - sg)))
        dxh = dnorm * w_ref[...]
        do_ref[...] = r * (dxh - xh * jnp.mean(dxh * xh, axis=-1, keepdims=True))

        @pl.when((pl.program_id(0) == 0) & (pl.program_id(1) == 0))
        def _():
            dw_ref[...] = jnp.zeros_like(dw_ref)

        dw_ref[...] += jnp.sum(dnorm * xh, axis=0, keepdims=True)

    row = pl.BlockSpec((tm, d), lambda i, h: (i, h))
    vec = pl.BlockSpec((1, d), lambda i, h: (0, 0))
    do, dgate, dw = pl.pallas_call(
        body, name=name, grid=(n // tm, width // d),
        in_specs=[row, pl.BlockSpec((tm, d), lambda i, h: (i, gate_col + h)), vec, pl.BlockSpec((tm, d), lambda i, h: (i, dy_col + h))],
        out_specs=[row, row, vec],
        out_shape=[jax.ShapeDtypeStruct((n, width), F32)] * 2 + [jax.ShapeDtypeStruct((1, d), F32)],
        compiler_params=_params(("arbitrary", "arbitrary")),
    )(o, gate, w.reshape(1, d), dy)
    return do, dgate, dw.reshape(d)


def _sigmul_fwd(z, e, res, name):
    n, d = z.shape
    tm = _row_tile(n, d)

    def body(z_ref, e_ref, r_ref, o_ref):
        o_ref[...] = r_ref[...] + jax.nn.sigmoid(z_ref[...]) * e_ref[...]

    row = pl.BlockSpec((tm, d), lambda i: (i, 0))
    return pl.pallas_call(body, name=name, grid=(n // tm,), in_specs=[row] * 3, out_specs=row,
                          out_shape=jax.ShapeDtypeStruct((n, d), F32), compiler_params=_params(("parallel",)))(z, e, res)


def _sigmul_bwd(z, e, dy, name):
    n, d = z.shape
    tm = _row_tile(n, d)

    def body(z_ref, e_ref, dy_ref, dz_ref, de_ref):
        s = jax.nn.sigmoid(z_ref[...])
        dyf = dy_ref[...]
        dz_ref[...] = dyf * e_ref[...] * s * (1.0 - s)
        de_ref[...] = dyf * s

    row = pl.BlockSpec((tm, d), lambda i: (i, 0))
    return pl.pallas_call(body, name=name, grid=(n // tm,), in_specs=[row] * 3, out_specs=[row] * 2,
                          out_shape=[jax.ShapeDtypeStruct((n, d), F32)] * 2, compiler_params=_params(("parallel",)))(z, e, dy)


def _loss_fwd(y, target, name):
    n, d = y.shape
    tm = _row_tile(n, d)

    def body(y_ref, t_ref, l_ref, dy_ref):
        diff = y_ref[...] - t_ref[...]
        dy_ref[...] = diff * (1.0 / d)

        @pl.when(pl.program_id(0) == 0)
        def _():
            l_ref[...] = jnp.zeros_like(l_ref)

        part = jnp.sum(jnp.mean(diff * diff, axis=-1, keepdims=True), axis=0, keepdims=True)
        l_ref[...] += 0.5 * jnp.broadcast_to(part, l_ref.shape)

    row = pl.BlockSpec((tm, d), lambda i: (i, 0))
    vec = pl.BlockSpec((1, LANES), lambda i: (0, 0))
    return pl.pallas_call(body, name=name, grid=(n // tm,), in_specs=[row, row], out_specs=[vec, row],
                          out_shape=[jax.ShapeDtypeStruct((1, LANES), F32), jax.ShapeDtypeStruct((n, d), F32)],
                          compiler_params=_params(("arbitrary",)))(y, target)


SB_BLK = 128
SB_QBLK = 256


def _sb_logits(z, qi, kj, row, col):
    mask = (kj * SB_BLK + col) < (qi * SB_QBLK + row)
    sp = jnp.maximum(z, 0.0) + jnp.log1p(jnp.exp(-jnp.abs(z)))
    lk = jnp.where(mask, -sp, 0.0)
    return mask, lk, z - sp


SB_PAIRS = SB_WIDTH // LANES


def _sb_iotas():
    row = lax.broadcasted_iota(jnp.int32, (2 * SB_QBLK, SB_BLK), 0)
    row = jnp.where(row >= SB_QBLK, row - SB_QBLK, row)
    col = lax.broadcasted_iota(jnp.int32, (2 * SB_QBLK, SB_BLK), 1)
    return row, col, col[:SB_QBLK] < SB_DIM


def _sb_stack(x, first):
    return jnp.concatenate([jnp.where(first, x, 0.0), jnp.where(first, 0.0, x)], axis=0)


def _sb_unstack(y, first):
    return jnp.where(first, y[:SB_QBLK], y[SB_QBLK:])


def _sb_running(x, u):
    m = x.shape[0]
    hi = _bf(x)
    lo = _bf(x - hi.astype(F32))
    c = lax.dot_general(jnp.concatenate([hi, lo], axis=0), u, NN, preferred_element_type=F32)
    return c[:m] + c[m:]


def _sb_spec(s, col):
    return pl.BlockSpec((s, LANES), lambda e, pr: (e, col + pr))


def _sb_fwd(proj, cols, b, s, name):
    nq = s // SB_QBLK
    scale = SB_DIM ** -0.5

    def body(q_ref, k_ref, v_ref, o_ref, lt_ref):
        row, col, first = _sb_iotas()
        u_after = _bf(row[:SB_BLK] > col[:SB_BLK])

        def qloop(qi, _):
            q0 = pl.multiple_of(qi * SB_QBLK, SB_QBLK)
            q2 = _sb_stack(q_ref[pl.ds(q0, SB_QBLK), :], first)
            nkeys = (qi + 1) * (SB_QBLK // SB_BLK)

            def logits(kj):
                k0 = pl.multiple_of(kj * SB_BLK, SB_BLK)
                return _dot(q2, k_ref[pl.ds(k0, SB_BLK), :], NT) * scale

            def kloop(j, st):
                acc, carry, z = st
                kj = nkeys - 1 - j
                k0 = pl.multiple_of(kj * SB_BLK, SB_BLK)
                z_next = logits(jnp.maximum(kj - 1, 0))
                mask, lk, ls = _sb_logits(z, qi, kj, row, col)
                later = carry + _sb_running(lk, u_after)
                w = jnp.where(mask, jnp.exp(ls + later), 0.0)
                acc = acc + _sb_unstack(_dot(w, v_ref[pl.ds(k0, SB_BLK), :]), first)
                return acc, carry + jnp.sum(lk, axis=1, keepdims=True), z_next

            acc, carry, _ = lax.fori_loop(0, nkeys, kloop, (jnp.zeros((SB_QBLK, LANES), F32), jnp.zeros((2 * SB_QBLK, 1), F32), logits(nkeys - 1)))
            o_ref[pl.ds(q0, SB_QBLK), :] = acc
            lt_ref[pl.ds(q0, SB_QBLK), :] = _sb_unstack(jnp.broadcast_to(carry, (2 * SB_QBLK, LANES)), first)
            return 0

        lax.fori_loop(0, nq, qloop, 0)

    out = _sb_spec(s, 0)
    return pl.pallas_call(body, name=name, grid=(b, SB_PAIRS), in_specs=[_sb_spec(s, c) for c in cols], out_specs=[out, out],
                          out_shape=[jax.ShapeDtypeStruct((b * s, SB_WIDTH), F32)] * 2,
                          compiler_params=_params(("parallel", "parallel")))(proj, proj, proj)


def _sb_bwd(proj, cols, ltot, do, do_col, b, s, name):
    nq = s // SB_QBLK
    scale = SB_DIM ** -0.5

    def body(q_ref, k_ref, v_ref, lt_ref, do_ref, dq_ref, dk_ref, dv_ref):
        row, col, first = _sb_iotas()
        u_upto = _bf(row[:SB_BLK] <= col[:SB_BLK])
        u_before = _bf(row[:SB_BLK] < col[:SB_BLK])
        dk_ref[...] = jnp.zeros_like(dk_ref)
        dv_ref[...] = jnp.zeros_like(dv_ref)

        def qloop(qi, _):
            q0 = pl.multiple_of(qi * SB_QBLK, SB_QBLK)
            q2 = _sb_stack(q_ref[pl.ds(q0, SB_QBLK), :], first)
            nkeys = (qi + 1) * (SB_QBLK // SB_BLK)
            do2 = _sb_stack(do_ref[pl.ds(q0, SB_QBLK), :], first)
            lt2 = jnp.min(_sb_stack(lt_ref[pl.ds(q0, SB_QBLK), :], first), axis=1, keepdims=True)

            def logits(kj):
                k0 = pl.multiple_of(kj * SB_BLK, SB_BLK)
                return _dot(q2, k_ref[pl.ds(k0, SB_BLK), :], NT) * scale

            def kloop(kj, st):
                dq, cl, cg, z = st
                k0 = pl.multiple_of(kj * SB_BLK, SB_BLK)
                kb = k_ref[pl.ds(k0, SB_BLK), :]
                vb = v_ref[pl.ds(k0, SB_BLK), :]
                z_next = logits(jnp.minimum(kj + 1, nkeys - 1))
                mask, lk, ls = _sb_logits(z, qi, kj, row, col)
                later = lt2 - (cl + _sb_running(lk, u_upto))
                w = jnp.where(mask, jnp.exp(ls + later), 0.0)
                g = _dot(do2, vb, NT) * w
                dv_ref[pl.ds(k0, SB_BLK), :] += _dot(w, do2, TN)
                g_before = cg + _sb_running(g, u_before)
                sig = jnp.exp(ls)
                dz = jnp.where(mask, g * (1.0 - sig) - sig * g_before, 0.0) * scale
                dq = dq + _sb_unstack(_dot(dz, kb), first)
                dk_ref[pl.ds(k0, SB_BLK), :] += _dot(dz, q2, TN)
                return dq, cl + jnp.sum(lk, axis=1, keepdims=True), cg + jnp.sum(g, axis=1, keepdims=True), z_next

            z1 = jnp.zeros((2 * SB_QBLK, 1), F32)
            dq = lax.fori_loop(0, nkeys, kloop, (jnp.zeros((SB_QBLK, LANES), F32), z1, z1, logits(0)))[0]
            dq_ref[pl.ds(q0, SB_QBLK), :] = dq
            return 0

        lax.fori_loop(0, nq, qloop, 0)

    out = _sb_spec(s, 0)
    return pl.pallas_call(body, name=name, grid=(b, SB_PAIRS),
                          in_specs=[_sb_spec(s, c) for c in cols] + [out, _sb_spec(s, do_col)], out_specs=[out] * 3,
                          out_shape=[jax.ShapeDtypeStruct((b * s, SB_WIDTH), F32)] * 3,
                          compiler_params=_params(("parallel", "parallel")))(proj, proj, proj, ltot, do)


HG_CHUNK = 64
HG_GROUP = 2


def _hg_consts(c, r):
    levels = int(math.log2(c))
    t = np.arange(r)
    same = (t[:, None] // c) == (t[None, :] // c)
    tri = ((t[:, None] >= t[None, :]) & same).astype(np.float32)
    psel = np.zeros((levels, r, r), np.float32)
    masks = np.zeros((levels + 1, r, r), np.float32)
    for l in range(levels):
        n = c >> (l + 1)
        blk = t // (2 * n)
        psel[l, t, blk * 2 * n + n - 1] = 1.0
        upper = (t % (2 * n)) >= n
        masks[l] = (blk[:, None] == blk[None, :]) & upper[:, None] & (~upper)[None, :]
    masks[levels] = np.eye(r)
    psel = psel.reshape(levels * r, r)
    return levels, jnp.asarray(tri), jnp.asarray(psel), jnp.asarray(masks), jnp.asarray(tri.T.copy()), jnp.asarray(psel.T.copy())


def _hg_elem(qv, fv, lbv):
    sig = jax.nn.sigmoid(fv)
    lf = jnp.log(lbv + (1.0 - lbv) * sig)
    kk = (1.0 - lbv) * jax.nn.sigmoid(-fv)
    qf = qv * jax.nn.sigmoid(qv)
    return qf, kk, lf


def _col_bcast(rowvec):
    n = rowvec.shape[1]
    return jnp.transpose(jnp.broadcast_to(rowvec, (n, n)))


def _hg_within(qf, kk, lf, tri, psel, m_ref, c, levels):
    r = qf.shape[0]
    b = _dot_exact_lhs01(tri, lf)
    bls = [b[(g + 1) * c - 1:(g + 1) * c, :] for g in range(r // c)]
    blb = jnp.concatenate([jnp.broadcast_to(bl, (c, bl.shape[1])) for bl in bls], axis=0)
    eb = jnp.exp(b)
    qi = qf * eb
    bsel = _dot_exact_lhs01(psel, b)
    scores = jnp.where(m_ref[levels] > 0, _dot(qf, kk, NT), 0.0)
    lev = []
    for l in range(levels):
        bs = bsel[l * r:(l + 1) * r]
        eq = jnp.exp(jnp.minimum(b - bs, 0.0))
        ek = jnp.exp(jnp.minimum(bs - b, 0.0))
        ql, kl = qf * eq, kk * ek
        scores = scores + jnp.where(m_ref[l] > 0, _dot(ql, kl, NT), 0.0)
        lev.append((eq, ek, ql, kl))
    ebl = jnp.exp(blb - b)
    kd = kk * ebl
    decays = [_col_bcast(jnp.exp(bl)) for bl in bls]
    return eb, qi, scores, lev, ebl, kd, decays


def _hg_fwd(proj, cols, lb, b, s, name):
    nh, d = lb.shape
    bh = b * nh
    c = HG_CHUNK
    nc = s // c
    grp = math.gcd(HG_GROUP, nc)
    r = grp * c
    levels, tri, psel, masks, _, _ = _hg_consts(c, r)

    def body(q_ref, f_ref, i_ref, lb_ref, tri_ref, psel_ref, m_ref, o_ref, st_ref):
        lbv = jnp.broadcast_to(lb_ref[0], (r, d))
        tri_v, psel_v = _bf(tri_ref[...]), _bf(psel_ref[...])

        def group(gi, state):
            r0 = pl.multiple_of(gi * r, r)
            qf, kk, lf = _hg_elem(q_ref[pl.ds(r0, r), :], f_ref[pl.ds(r0, r), :], lbv)
            iv = i_ref[pl.ds(r0, r), :]
            _, qi, scores, _, _, kd, decays = _hg_within(qf, kk, lf, tri_v, psel_v, m_ref, c, levels)
            within = _dot(scores, iv)
            for g in range(grp):
                rows = slice(g * c, (g + 1) * c)
                st_ref[0, gi * grp + g] = state
                o_ref[pl.ds(r0 + g * c, c), :] = _dot(qi[rows], state) + within[rows]
                state = decays[g] * state + _dot(kd[rows], iv[rows], TN)
            return state

        lax.fori_loop(0, nc // grp, group, jnp.zeros((d, d), F32))

    full = lambda a: pl.BlockSpec(a.shape, lambda e, hd: (0,) * a.ndim)
    return pl.pallas_call(
        body, name=name, grid=(b, nh),
        in_specs=[_hg_seq(s, d, cols[0]), _hg_seq(s, d, cols[1]), _hg_seq(s, d, cols[2]),
                  pl.BlockSpec((1, 1, d), lambda e, hd: (hd, 0, 0)), full(tri), full(psel), full(masks)],
        out_specs=[_hg_seq(s, d, 0), pl.BlockSpec((1, nc, d, d), lambda e, hd: (e * nh + hd, 0, 0, 0))],
        out_shape=[jax.ShapeDtypeStruct((b * s, nh * d), F32), jax.ShapeDtypeStruct((bh, nc, d, d), F32)],
        compiler_params=_params(("parallel", "parallel")),
    )(proj, proj, proj, lb.reshape(nh, 1, d), tri, psel, masks)


def _hg_seq(s, d, col):
    return pl.BlockSpec((s, d), lambda e, hd: (e, col + hd))


def _hg_bwd(proj, cols, lb, states, do, b, s, name):
    nh, d = lb.shape
    bh = b * nh
    c = HG_CHUNK
    nc = s // c
    grp = math.gcd(HG_GROUP, nc)
    r = grp * c
    levels, tri, psel, masks, tri_t, psel_t = _hg_consts(c, r)

    def body(q_ref, f_ref, i_ref, lb_ref, st_ref, do_ref, tri_ref, psel_ref, m_ref, trit_ref, pselt_ref,
             dq_ref, df_ref, di_ref, dlb_ref):
        lbv = jnp.broadcast_to(lb_ref[0], (r, d))
        tri_v, psel_v = _bf(tri_ref[...]), _bf(psel_ref[...])
        trit_v, pselt_v = _bf(trit_ref[...]), _bf(pselt_ref[...])
        row_in_chunk = lax.broadcasted_iota(jnp.int32, (c, d), 0)

        def chunk(step, carry):
            ds_out, dlb = carry
            gi = nc // grp - 1 - step
            r0 = pl.multiple_of(gi * r, r)
            qv, fv, iv = q_ref[pl.ds(r0, r), :], f_ref[pl.ds(r0, r), :], i_ref[pl.ds(r0, r), :]
            dov = do_ref[pl.ds(r0, r), :]
            (qf, kk, lf), elem_vjp = jax.vjp(_hg_elem, qv, fv, lbv)
            eb, qi, scores, lev, ebl, kd, decays = _hg_within(qf, kk, lf, tri_v, psel_v, m_ref, c, levels)

            dscores = _dot(dov, iv, NT)
            di_within = _dot(scores, dov, TN)
            dqi_parts, dkd_parts, dbl_parts = [None] * grp, [None] * grp, [None] * grp
            for g in reversed(range(grp)):
                rows = slice(g * c, (g + 1) * c)
                state = st_ref[0, gi * grp + g]
                di_ref[pl.ds(r0 + g * c, c), :] = di_within[rows] + _dot(kd[rows], ds_out)
                dqi_parts[g] = _dot(dov[rows], state, NT)
                dkd_parts[g] = _dot(iv[rows], ds_out, NT)
                dbl = (jnp.sum(dkd_parts[g] * kd[rows], axis=0, keepdims=True)
                       + _col_bcast_t(jnp.sum(ds_out * decays[g] * state, axis=1, keepdims=True)))
                dbl_parts[g] = jnp.where(row_in_chunk == c - 1, dbl, 0.0)
                ds_out = decays[g] * ds_out + _dot(qi[rows], dov[rows], TN)
            ds_in = ds_out
            dqi = jnp.concatenate(dqi_parts, axis=0)
            dkd = jnp.concatenate(dkd_parts, axis=0)
            dqf = dqi * eb
            dkk = dkd * ebl
            db = dqi * qi - dkd * kd + jnp.concatenate(dbl_parts, axis=0)
            dsd = jnp.where(m_ref[levels] > 0, dscores, 0.0)
            dqf = dqf + _dot(dsd, kk)
            dkk = dkk + _dot(dsd, qf, TN)
            dbsel = []
            for l in range(levels):
                eq, ek, ql, kl = lev[l]
                dsl = jnp.where(m_ref[l] > 0, dscores, 0.0)
                dql = _dot(dsl, kl)
                dkl = _dot(dsl, ql, TN)
                dqf = dqf + dql * eq
                dkk = dkk + dkl * ek
                diff = dql * ql - dkl * kl
                db = db + diff
                dbsel.append(-diff)
            db = db + _dot_exact_lhs01(pselt_v, jnp.concatenate(dbsel, axis=0))
            dlf = _dot_exact_lhs01(trit_v, db)
            dq, df, dlb_c = elem_vjp((dqf, dkk, dlf))
            dq_ref[pl.ds(r0, r), :] = dq
            df_ref[pl.ds(r0, r), :] = df
            return ds_in, dlb + jnp.sum(dlb_c, axis=0, keepdims=True)

        _, dlb = lax.fori_loop(0, nc // grp, chunk, (jnp.zeros((d, d), F32), jnp.zeros((1, d), F32)))
        dlb_ref[0] = dlb

    seq = _hg_seq(s, d, 0)
    full = lambda a: pl.BlockSpec(a.shape, lambda e, hd: (0,) * a.ndim)
    return pl.pallas_call(
        body, name=name, grid=(b, nh),
        in_specs=[_hg_seq(s, d, cols[0]), _hg_seq(s, d, cols[1]), _hg_seq(s, d, cols[2]),
                  pl.BlockSpec((1, 1, d), lambda e, hd: (hd, 0, 0)),
                  pl.BlockSpec((1, nc, d, d), lambda e, hd: (e * nh + hd, 0, 0, 0)), seq,
                  full(tri), full(psel), full(masks), full(tri_t), full(psel_t)],
        out_specs=[seq, seq, seq, pl.BlockSpec((1, 1, d), lambda e, hd: (e * nh + hd, 0, 0))],
        out_shape=[jax.ShapeDtypeStruct((b * s, nh * d), F32)] * 3 + [jax.ShapeDtypeStruct((bh, 1, d), F32)],
        compiler_params=_params(("parallel", "parallel")),
    )(proj, proj, proj, lb.reshape(nh, 1, d), states, do, tri, psel, masks, tri_t, psel_t)


def _col_bcast_t(colvec):
    n = colvec.shape[0]
    return jnp.transpose(jnp.broadcast_to(colvec, (n, n)))[0:1, :]


def _swa_probs(qg, kb, bias, sink, valid, scale):
    logits = _dot(qg, kb, NT) * scale + bias
    logits = jnp.where(valid, logits, -jnp.inf)
    m = jnp.maximum(jnp.max(logits, axis=-1, keepdims=True), sink)
    e = jnp.exp(logits - m)
    es = jnp.exp(sink - m)
    den = jnp.sum(e, axis=-1, keepdims=True) + es
    return e / den, es / den


def _swa_valid(n):
    w = WINDOW
    row = lax.broadcasted_iota(jnp.int32, (w, 2 * w), 0)
    col = lax.broadcasted_iota(jnp.int32, (w, 2 * w), 1)
    dist = row + w - col
    return (dist >= 0) & (dist < w) & ((col >= w) | (n > 0))


def _swa_specs(b, g, s, d):
    w = WINDOW
    q_spec = pl.BlockSpec((1, 1, g, w, d), lambda h, bi, n: (bi, h, 0, n, 0))
    kp_spec = pl.BlockSpec((1, 1, w, d), lambda h, bi, n: (bi, h, jnp.maximum(n - 1, 0), 0))
    kc_spec = pl.BlockSpec((1, 1, w, d), lambda h, bi, n: (bi, h, n, 0))
    bias_spec = pl.BlockSpec((1, g, w, 2 * w), lambda h, bi, n: (h, 0, 0, 0))
    sink_spec = pl.BlockSpec(memory_space=pltpu.SMEM)
    return q_spec, kp_spec, kc_spec, bias_spec, sink_spec


def _swa_fwd(q, k, v, sinks, bias, name):
    b, kvh, g, s, d = q.shape
    w = WINDOW
    scale = d ** -0.5
    q_spec, kp_spec, kc_spec, bias_spec, sink_spec = _swa_specs(b, g, s, d)

    def body(q_ref, kp_ref, kc_ref, vp_ref, vc_ref, bias_ref, sink_ref, o_ref):
        h, n = pl.program_id(0), pl.program_id(2)
        valid = _swa_valid(n)
        kb = jnp.concatenate([kp_ref[0, 0], kc_ref[0, 0]], axis=0)
        vb = jnp.concatenate([vp_ref[0, 0], vc_ref[0, 0]], axis=0)
        for gi in range(g):
            p, _ = _swa_probs(q_ref[0, 0, gi], kb, bias_ref[0, gi], sink_ref[h * g + gi], valid, scale)
            o_ref[0, 0, gi] = _dot(p, vb)

    return pl.pallas_call(
        body, name=name, grid=(kvh, b, s // w),
        in_specs=[q_spec, kp_spec, kc_spec, kp_spec, kc_spec, bias_spec, sink_spec], out_specs=q_spec,
        out_shape=jax.ShapeDtypeStruct(q.shape, F32), compiler_params=_params(("parallel", "parallel", "arbitrary")),
    )(q, k, k, v, v, bias, sinks)


def _swa_bwd(q, k, v, sinks, bias, do, name):
    b, kvh, g, s, d = q.shape
    w = WINDOW
    scale = d ** -0.5
    q_spec, kp_spec, kc_spec, bias_spec, sink_spec = _swa_specs(b, g, s, d)
    kv_acc = pl.BlockSpec((1, 1, s, d), lambda h, bi, n: (bi, h, 0, 0))
    dsink_spec = pl.BlockSpec((1, g, LANES), lambda h, bi, n: (h, 0, 0))

    def body(q_ref, kp_ref, kc_ref, vp_ref, vc_ref, bias_ref, sink_ref, do_ref, dq_ref, dk_ref, dv_ref, dbias_ref, dsink_ref):
        h, bi, n = pl.program_id(0), pl.program_id(1), pl.program_id(2)
        valid = _swa_valid(n)
        kb = jnp.concatenate([kp_ref[0, 0], kc_ref[0, 0]], axis=0)
        vb = jnp.concatenate([vp_ref[0, 0], vc_ref[0, 0]], axis=0)

        @pl.when(n == 0)
        def _():
            dk_ref[...] = jnp.zeros_like(dk_ref)
            dv_ref[...] = jnp.zeros_like(dv_ref)

        @pl.when((n == 0) & (bi == 0))
        def _():
            dbias_ref[...] = jnp.zeros_like(dbias_ref)
            dsink_ref[...] = jnp.zeros_like(dsink_ref)

        dkb = jnp.zeros((2 * w, d), F32)
        dvb = jnp.zeros((2 * w, d), F32)
        for gi in range(g):
            qg, dog = q_ref[0, 0, gi], do_ref[0, 0, gi]
            p, ps = _swa_probs(qg, kb, bias_ref[0, gi], sink_ref[h * g + gi], valid, scale)
            dp = _dot(dog, vb, NT)
            delta = jnp.sum(p * dp, axis=-1, keepdims=True)
            dl = p * (dp - delta)
            dq_ref[0, 0, gi] = _dot(dl, kb) * scale
            dkb = dkb + _dot(dl, qg, TN) * scale
            dvb = dvb + _dot(p, dog, TN)
            dbias_ref[0, gi] += dl
            dsink_ref[0, gi:gi + 1, :] += jnp.broadcast_to(jnp.sum(-ps * delta, axis=0, keepdims=True), (1, LANES))

        c0 = pl.multiple_of(n * w, w)
        dk_ref[0, 0, pl.ds(c0, w), :] += dkb[w:]
        dv_ref[0, 0, pl.ds(c0, w), :] += dvb[w:]

        @pl.when(n > 0)
        def _():
            p0 = pl.multiple_of((n - 1) * w, w)
            dk_ref[0, 0, pl.ds(p0, w), :] += dkb[:w]
            dv_ref[0, 0, pl.ds(p0, w), :] += dvb[:w]

    return pl.pallas_call(
        body, name=name, grid=(kvh, b, s // w),
        in_specs=[q_spec, kp_spec, kc_spec, kp_spec, kc_spec, bias_spec, sink_spec, q_spec],
        out_specs=[q_spec, kv_acc, kv_acc, bias_spec, dsink_spec],
        out_shape=[jax.ShapeDtypeStruct(q.shape, F32), jax.ShapeDtypeStruct(k.shape, F32), jax.ShapeDtypeStruct(k.shape, F32),
                   jax.ShapeDtypeStruct(bias.shape, F32), jax.ShapeDtypeStruct((kvh, g, LANES), F32)],
        compiler_params=_params(("arbitrary", "arbitrary", "arbitrary")),
    )(q, k, k, v, v, bias, sinks, do)


def _t5_bias(rel_bias):
    t = np.arange(WINDOW)[:, None]
    s = np.arange(2 * WINDOW)[None, :]
    dist = t + WINDOW - s
    max_exact = N_BUCKETS // 2
    large = max_exact + (np.log(np.maximum(dist, max_exact) / max_exact) / math.log(MAX_DISTANCE / max_exact)
                         * (N_BUCKETS - max_exact)).astype(np.int32)
    large = np.minimum(large, N_BUCKETS - 1)
    bucket = np.where(dist < max_exact, np.maximum(dist, 0), large).astype(np.int32)
    onehot = jnp.asarray(np.eye(N_BUCKETS, dtype=np.float32)[bucket])
    bias = jnp.einsum("tsb,bh->hts", onehot, rel_bias.astype(F32), precision=lax.Precision.HIGHEST)
    return bias.reshape(SW_KV_HEADS, SW_GROUP, WINDOW, 2 * WINDOW)


CONV_W = 3


def _shift_down(x, k):
    row = lax.broadcasted_iota(jnp.int32, x.shape, 0)
    return jnp.where(row >= k, pltpu.roll(x, k, axis=0), 0.0)


def _shift_up(x, k):
    n = x.shape[0]
    row = lax.broadcasted_iota(jnp.int32, x.shape, 0)
    return jnp.where(row < n - k, pltpu.roll(x, n - k, axis=0), 0.0)


def _conv3(u, w, bvec):
    return w[0:1] * _shift_down(u, 2) + w[1:2] * _shift_down(u, 1) + w[2:3] * u + bvec


def _convglu_fwd(u, w, bvec, name, out_dtype=F32):
    b, s, f2 = u.shape
    f = f2 // 2
    tc = _pick(f, 256)
    nt = f // tc

    def body(ug_ref, uu_ref, wg_ref, wu_ref, bg_ref, bu_ref, o_ref):
        cg = _conv3(ug_ref[0], wg_ref[...], bg_ref[...])
        cu = _conv3(uu_ref[0], wu_ref[...], bu_ref[...])
        o_ref[0] = (_silu(cg) * cu).astype(out_dtype)

    ug = pl.BlockSpec((1, s, tc), lambda j, bi: (bi, 0, j))
    uu = pl.BlockSpec((1, s, tc), lambda j, bi: (bi, 0, j + nt))
    wg = pl.BlockSpec((CONV_W, tc), lambda j, bi: (0, j))
    wu = pl.BlockSpec((CONV_W, tc), lambda j, bi: (0, j + nt))
    bg = pl.BlockSpec((1, tc), lambda j, bi: (0, j))
    bu = pl.BlockSpec((1, tc), lambda j, bi: (0, j + nt))
    bv = bvec.reshape(1, f2)
    return pl.pallas_call(body, name=name, grid=(nt, b), in_specs=[ug, uu, wg, wu, bg, bu], out_specs=ug,
                          out_shape=jax.ShapeDtypeStruct((b, s, f), out_dtype),
                          compiler_params=_params(("parallel", "parallel")))(u, u, w, w, bv, bv)


def _convglu_bwd(u, w, bvec, dact, name):
    b, s, f2 = u.shape
    f = f2 // 2
    tc = LANES
    nt = f // tc

    def taps(dc, uv):
        rows = [jnp.sum(dc * _shift_down(uv, 2), axis=0, keepdims=True), jnp.sum(dc * _shift_down(uv, 1), axis=0, keepdims=True),
                jnp.sum(dc * uv, axis=0, keepdims=True), jnp.sum(dc, axis=0, keepdims=True)]
        return jnp.concatenate(rows + [jnp.zeros((4, tc), F32)], axis=0)

    def back(dc, wv):
        return wv[2:3] * dc + wv[1:2] * _shift_up(dc, 1) + wv[0:1] * _shift_up(dc, 2)

    def body(ug_ref, uu_ref, wg_ref, wu_ref, bg_ref, bu_ref, da_ref, dug_ref, duu_ref, dwg_ref, dwu_ref):
        ugv, uuv, da = ug_ref[0], uu_ref[0], da_ref[0]
        cg = _conv3(ugv, wg_ref[...], bg_ref[...])
        cu = _conv3(uuv, wu_ref[...], bu_ref[...])
        sg = jax.nn.sigmoid(cg)
        dcu = da * (cg * sg)
        dcg = da * cu * (sg * (1.0 + cg * (1.0 - sg)))
        dug_ref[0] = back(dcg, wg_ref[...])
        duu_ref[0] = back(dcu, wu_ref[...])

        @pl.when(pl.program_id(1) == 0)
        def _():
            dwg_ref[...] = jnp.zeros_like(dwg_ref)
            dwu_ref[...] = jnp.zeros_like(dwu_ref)

        dwg_ref[...] += taps(dcg, ugv)
        dwu_ref[...] += taps(dcu, uuv)

    ug = pl.BlockSpec((1, s, tc), lambda j, bi: (bi, 0, j))
    uu = pl.BlockSpec((1, s, tc), lambda j, bi: (bi, 0, j + nt))
    wg = pl.BlockSpec((CONV_W, tc), lambda j, bi: (0, j))
    wu = pl.BlockSpec((CONV_W, tc), lambda j, bi: (0, j + nt))
    bg = pl.BlockSpec((1, tc), lambda j, bi: (0, j))
    bu = pl.BlockSpec((1, tc), lambda j, bi: (0, j + nt))
    acc = pl.BlockSpec((8, tc), lambda j, bi: (0, j))
    bv = bvec.reshape(1, f2)
    return pl.pallas_call(
        body, name=name, grid=(nt, b), in_specs=[ug, uu, wg, wu, bg, bu, ug], out_specs=[ug, ug, acc, acc],
        out_shape=[jax.ShapeDtypeStruct((b, s, f), F32)] * 2 + [jax.ShapeDtypeStruct((8, f), F32)] * 2,
        compiler_params=_params(("parallel", "arbitrary")),
    )(u, u, w, w, bv, bv, dact)


def _adamw(w, g, m, v, name):
    r, c = w.shape
    tr = _row_tile(r, c)
    c1 = 1.0 - ADAM_B1 ** ADAM_STEP
    c2 = 1.0 - ADAM_B2 ** ADAM_STEP

    def body(w_ref, g_ref, m_ref, v_ref, d_ref, mo_ref, vo_ref):
        gv = g_ref[...]
        mn = ADAM_B1 * m_ref[...] + (1.0 - ADAM_B1) * gv
        vn = ADAM_B2 * v_ref[...] + (1.0 - ADAM_B2) * (gv * gv)
        d_ref[...] = -ADAM_LR * ((mn / c1) / (jnp.sqrt(vn / c2) + ADAM_EPS) + ADAM_WD * w_ref[...])
        mo_ref[...] = mn
        vo_ref[...] = vn

    blk = pl.BlockSpec((tr, c), lambda i: (i, 0))
    return pl.pallas_call(body, name=name, grid=(r // tr,), in_specs=[blk] * 4, out_specs=[blk] * 3,
                          out_shape=[jax.ShapeDtypeStruct((r, c), F32)] * 3, compiler_params=_params(("parallel",)))(w, g, m, v)


MESH = pl.DeviceIdType.MESH
ANY = pl.BlockSpec(memory_space=pl.ANY)


def _position():
    return lax.axis_index("x"), lax.axis_index("y"), lax.axis_index("c")


def _all_gather(x, name):
    r, c = x.shape

    def body(x_ref, out_ref, send_sems, recv_sems, local_sem):
        mx, my, mc = _position()
        me, sibling = (mx, my, mc), (mx, my, 1 - mc)
        chips = [(1 - mx, my), (mx, 1 - my), (1 - mx, 1 - my)]

        def slot(px, py, pc):
            return out_ref.at[4 * px + 2 * py + pc]

        def copy(k, block, to, src=None):
            return pltpu.make_async_remote_copy(
                src_ref=slot(*block) if src is None else src, dst_ref=slot(*block),
                send_sem=send_sems.at[k], recv_sem=recv_sems.at[k], device_id=to, device_id_type=MESH)

        mine = pltpu.make_async_copy(x_ref, slot(*me), local_sem.at[0])
        mine.start()
        first = [copy(0, me, sibling, src=x_ref)]
        first += [copy(1 + j, me, (*chip, mc), src=x_ref) for j, chip in enumerate(chips)]
        for cp in first:
            cp.start()
        passed = [copy(4 + j, (*chip, mc), sibling) for j, chip in enumerate(chips)]
        for j, chip in enumerate(chips):
            copy(1 + j, (*chip, mc), me).wait_recv()
            passed[j].start()
        copy(0, sibling, me).wait_recv()
        for j, chip in enumerate(chips):
            copy(4 + j, (*chip, 1 - mc), me).wait_recv()
        for cp in first + passed:
            cp.wait_send()
        mine.wait()

    return pl.pallas_call(
        body, name=name, out_shape=jax.ShapeDtypeStruct((N_DEV, r, c), x.dtype), in_specs=[ANY], out_specs=ANY,
        scratch_shapes=[pltpu.SemaphoreType.DMA((7,)), pltpu.SemaphoreType.DMA((7,)), pltpu.SemaphoreType.DMA((1,))],
    )(x)


def _dev_rows(ref, dev, a):
    return ref.at[:, pl.ds(pl.multiple_of(dev * a, 16), a), :]


def _all_gather_rows(shards, name):
    nt = len(shards)

    def body(*refs):
        x_refs, out_refs = refs[:nt], refs[nt:2 * nt]
        send_sems, recv_sems, local_sems = refs[2 * nt:]
        mx, my, mc = _position()
        me, sibling = (mx, my, mc), (mx, my, 1 - mc)
        chips = [(1 - mx, my), (mx, 1 - my), (1 - mx, 1 - my)]

        def slot(t, px, py, pc):
            return _dev_rows(out_refs[t], 4 * px + 2 * py + pc, shards[t].shape[1])

        def copy(t, k, block, to, src=None):
            return pltpu.make_async_remote_copy(
                src_ref=slot(t, *block) if src is None else src, dst_ref=slot(t, *block),
                send_sem=send_sems.at[7 * t + k], recv_sem=recv_sems.at[7 * t + k], device_id=to, device_id_type=MESH)

        mine = [pltpu.make_async_copy(x_refs[t], slot(t, *me), local_sems.at[t]) for t in range(nt)]
        first = [copy(t, 0, me, sibling, src=x_refs[t]) for t in range(nt)]
        first += [copy(t, 1 + j, me, (*chip, mc), src=x_refs[t]) for j, chip in enumerate(chips) for t in range(nt)]
        for cp in mine + first:
            cp.start()
        passed = []
        for j, chip in enumerate(chips):
            for t in range(nt):
                copy(t, 1 + j, (*chip, mc), me).wait_recv()
                fwd = copy(t, 4 + j, (*chip, mc), sibling)
                fwd.start()
                passed.append(fwd)
        for t in range(nt):
            copy(t, 0, sibling, me).wait_recv()
        for j, chip in enumerate(chips):
            for t in range(nt):
                copy(t, 4 + j, (*chip, 1 - mc), me).wait_recv()
        for cp in first + passed:
            cp.wait_send()
        for cp in mine:
            cp.wait()

    out_shape = [jax.ShapeDtypeStruct((x.shape[0], N_DEV * x.shape[1], x.shape[2]), x.dtype) for x in shards]
    return pl.pallas_call(
        body, name=name, out_shape=out_shape, in_specs=[ANY] * nt, out_specs=[ANY] * nt,
        scratch_shapes=[pltpu.SemaphoreType.DMA((7 * nt,)), pltpu.SemaphoreType.DMA((7 * nt,)), pltpu.SemaphoreType.DMA((nt,))],
    )(*shards)


def _rs_pair(gs, name):
    nt = len(gs)

    def body(*refs):
        g_refs, a_refs = refs[:nt], refs[nt:2 * nt]
        send_sems, recv_sems = refs[2 * nt:]
        mx, my, mc = _position()
        copies = [pltpu.make_async_remote_copy(
            src_ref=_dev_rows(g_refs[t], 2 * j + 1 - mc, gs[t].shape[1] // N_DEV), dst_ref=a_refs[t].at[j],
            send_sem=send_sems.at[4 * t + j], recv_sem=recv_sems.at[4 * t + j],
            device_id=(mx, my, 1 - mc), device_id_type=MESH) for t in range(nt) for j in range(4)]
        for cp in copies:
            cp.start()
        for cp in copies:
            cp.wait()

    out_shape = [jax.ShapeDtypeStruct((4, g.shape[0], g.shape[1] // N_DEV, g.shape[2]), g.dtype) for g in gs]
    return pl.pallas_call(
        body, name=name, out_shape=out_shape, in_specs=[ANY] * nt, out_specs=[ANY] * nt,
        scratch_shapes=[pltpu.SemaphoreType.DMA((4 * nt,)), pltpu.SemaphoreType.DMA((4 * nt,))],
    )(*gs)


def _rs_chips(ps, name):
    nt = len(ps)

    def body(*refs):
        p_refs, b_refs = refs[:nt], refs[nt:2 * nt]
        send_sems, recv_sems = refs[2 * nt:]
        mx, my, mc = _position()
        chips = [(1 - mx, my), (mx, 1 - my), (1 - mx, 1 - my)]
        copies = [pltpu.make_async_remote_copy(
            src_ref=p_refs[t].at[2 * cx + cy], dst_ref=b_refs[t].at[k],
            send_sem=send_sems.at[3 * t + k], recv_sem=recv_sems.at[3 * t + k],
            device_id=(cx, cy, mc), device_id_type=MESH) for t in range(nt) for k, (cx, cy) in enumerate(chips)]
        for cp in copies:
            cp.start()
        for cp in copies:
            cp.wait()

    out_shape = [jax.ShapeDtypeStruct((3,) + p.shape[1:], p.dtype) for p in ps]
    return pl.pallas_call(
        body, name=name, out_shape=out_shape, in_specs=[ANY] * nt, out_specs=[ANY] * nt,
        scratch_shapes=[pltpu.SemaphoreType.DMA((3 * nt,)), pltpu.SemaphoreType.DMA((3 * nt,))],
    )(*ps)


def _div_tile(a, b):
    best = 16
    for t in range(16, a + 1, 16):
        if a % t == 0 and t * b * 4 <= 2 * 1024 * 1024:
            best = t
    return best


def _rs_add_pair(g, a, core, name):
    l, a8, b = g.shape
    rows = a8 // N_DEV
    ta = _div_tile(rows, b)

    def body(core_ref, g_ref, a_ref, p_ref):
        p_ref[...] = (g_ref[...] + a_ref[...]).astype(BF16)

    grid_spec = pltpu.PrefetchScalarGridSpec(
        num_scalar_prefetch=1, grid=(4, l, rows // ta),
        in_specs=[pl.BlockSpec((1, 1, ta, b), lambda j, li, i, core_ref: (li, 2 * j + core_ref[0], i, 0)),
                  pl.BlockSpec((1, 1, ta, b), lambda j, li, i, core_ref: (j, li, i, 0))],
        out_specs=pl.BlockSpec((1, 1, ta, b), lambda j, li, i, core_ref: (j, li, i, 0)))
    return pl.pallas_call(body, name=name, grid_spec=grid_spec, out_shape=jax.ShapeDtypeStruct((4, l, rows, b), BF16),
                          compiler_params=_params(("parallel", "parallel", "parallel")))(core, g.reshape(l, N_DEV, rows, b), a)


def _rs_final(g, a, bsum, where, name):
    l, a8, b = g.shape
    rows = a8 // N_DEV
    ta = _div_tile(rows, b)

    def body(where_ref, g_ref, a_ref, b_ref, o_ref):
        own = g_ref[0, 0] + a_ref[0, 0]
        o_ref[0] = ((own + b_ref[0, 0].astype(F32)) + b_ref[1, 0].astype(F32)) + b_ref[2, 0].astype(F32)

    grid_spec = pltpu.PrefetchScalarGridSpec(
        num_scalar_prefetch=1, grid=(l, rows // ta),
        in_specs=[pl.BlockSpec((1, 1, ta, b), lambda li, i, w_ref: (li, w_ref[0], i, 0)),
                  pl.BlockSpec((1, 1, ta, b), lambda li, i, w_ref: (w_ref[1], li, i, 0)),
                  pl.BlockSpec((3, 1, ta, b), lambda li, i, w_ref: (0, li, i, 0))],
        out_specs=pl.BlockSpec((1, ta, b), lambda li, i, w_ref: (li, i, 0)))
    return pl.pallas_call(body, name=name, grid_spec=grid_spec, out_shape=jax.ShapeDtypeStruct((l, rows, b), F32),
                          compiler_params=_params(("parallel", "parallel")))(where, g.reshape(l, N_DEV, rows, b), a, bsum)


def _sum_devices(x, name):
    _, r, c = x.shape

    def body(x_ref, o_ref):
        acc = x_ref[0]
        for d in range(1, N_DEV):
            acc = acc + x_ref[d]
        o_ref[...] = acc

    return pl.pallas_call(body, name=name, out_shape=jax.ShapeDtypeStruct((r, c), F32))(x)


BIG = (("ab_w_in", "col"), ("ab_w_out", "row"), ("c_w_in", "col"), ("c_w_out", "row"),
       ("ffn_up", "col"), ("ffn_down", "row"), ("ple_gate", "row"), ("ple_proj", "col"))
KIND = dict(BIG)


def _row_block(shard, kind):
    return shard.transpose(0, 2, 1) if kind == "col" else shard


def _pad_rows(flat):
    pad = -flat.shape[0] % (8 * LANES)
    return jnp.pad(flat, (0, pad)).reshape(-1, LANES)


def _heads_out(x, b, s, nh, d):
    return x.reshape(b, s, nh, d).transpose(0, 2, 1, 3).reshape(b * nh, s, d)


def _heads_in(x, b, s, nh, d):
    return x.reshape(b, nh, s, d).transpose(0, 2, 1, 3).reshape(b * s, nh * d)


def _lower_bounds(logits):
    c = jnp.cumsum(jax.nn.softmax(logits.astype(F32), axis=0), axis=0)
    return c - c[0]


SB_COLS = tuple(k * SB_WIDTH // LANES for k in range(3))
HG_COLS = tuple((3 * SB_WIDTH + k * HG_W) // LANES for k in range(3))
HG_GATE_COL = (3 * SB_WIDTH + 3 * HG_W) // LANES
HG_OUT_COL = SB_WIDTH // LANES


def _forward_backward(x, p, target, W, P):
    b, s, dm = x.shape
    n = b * s
    h = x.reshape(n, dm)
    lbs, lb_vjp = jax.vjp(_lower_bounds, P["hg_lb_logits"])
    bias, bias_vjp = jax.vjp(_t5_bias, P["rel_bias"])
    saved = []
    gw = {name: lax.empty(W[name].shape, F32) for name, _ in BIG}

    def times_w(a, name, l, tag, res=None):
        return _mm(a, W[name], "nt" if KIND[name] == "col" else "nn", tag, res=res, layer=l)

    def times_wt(dy, name, l, tag):
        return _mm(dy, W[name], "nn" if KIND[name] == "col" else "nt", tag, layer=l)

    def grad_w(a, dy, name, l, tag):
        lhs, rhs = (dy, a) if KIND[name] == "col" else (a, dy)
        gw[name] = _mm(lhs, rhs, "tn", tag, into=(gw[name], l, 0))

    for i in range(DEPTH):
        j = i // 2
        r = {"h0": h}
        hn = _rms_fwd(h, P["mix_norm"][i], f"mix_norm_f{i}", BF16)
        r["hn"] = hn
        if i % 2 == 0:
            proj = times_w(hn, "ab_w_in", j, f"ab_in_f{i}")
            oa, lta = _sb_fwd(proj, SB_COLS, b, s, f"sb_f{i}")
            ob, st = _hg_fwd(proj, HG_COLS, lbs[j].reshape(HG_HEADS, HG_DK), b, s, f"hg_f{i}")
            obg = _gnorm_fwd(ob, proj, HG_GATE_COL, P["hg_out_norm"][j], f"hg_norm_f{i}")
            cat = jnp.concatenate([oa, obg], axis=1).astype(BF16)
            h = times_w(cat, "ab_w_out", j, f"ab_out_f{i}", res=h)
            r.update(lta=lta, proj=proj, ob=ob, st=st, cat=cat)
        else:
            proj = times_w(hn, "c_w_in", j, f"c_in_f{i}")
            nq = SW_HEADS * SW_DIM
            nkv = SW_KV_HEADS * SW_DIM
            q = _heads_out(proj[:, :nq], b, s, SW_HEADS, SW_DIM).reshape(-1, SW_DIM)
            k = _heads_out(proj[:, nq:nq + nkv], b, s, SW_KV_HEADS, SW_DIM).reshape(-1, SW_DIM)
            v = _heads_out(proj[:, nq + nkv:], b, s, SW_KV_HEADS, SW_DIM).reshape(b, SW_KV_HEADS, s, SW_DIM)
            qn = _rms_fwd(q, P["q_norm"][j], f"q_norm_f{i}").reshape(b, SW_KV_HEADS, SW_GROUP, s, SW_DIM)
            kn = _rms_fwd(k, P["k_norm"][j], f"k_norm_f{i}").reshape(b, SW_KV_HEADS, s, SW_DIM)
            o = _swa_fwd(qn, kn, v, P["sinks"][j], bias, f"swa_f{i}")
            o2 = _heads_in(o.reshape(b * SW_HEADS, s, SW_DIM), b, s, SW_HEADS, SW_DIM).astype(BF16)
            h = times_w(o2, "c_w_out", j, f"c_out_f{i}", res=h)
            r.update(q=q, k=k, v=v, qn=qn, kn=kn, o2=o2)
        r["h1"] = h
        hn2 = _rms_fwd(h, P["ffn_norm"][i], f"ffn_norm_f{i}", BF16)
        u = times_w(hn2, "ffn_up", i, f"ffn_up_f{i}").reshape(b, s, 2 * D_FF)
        act = _convglu_fwd(u, W["ffn_conv"][i], P["ffn_conv_b"][i], f"conv_f{i}", BF16).reshape(n, D_FF)
        h = times_w(act, "ffn_down", i, f"ffn_down_f{i}", res=h)
        r.update(hn2=hn2, u=u, act=act, h2=h)
        hn3 = _rms_fwd(h, P["ple_norm"][i], f"ple_norm_f{i}", BF16)
        z = times_w(hn3, "ple_gate", i, f"ple_gate_f{i}")
        pi = p[i].reshape(n, PLE_DIM)
        e = times_w(pi, "ple_proj", i, f"ple_proj_f{i}")
        h = _sigmul_fwd(z, e, h, f"ple_f{i}")
        r.update(hn3=hn3, z=z, e=e, pi=pi)
        saved.append(r)

    loss, dh = _loss_fwd(h, target.reshape(n, dm), "loss")

    gconv = [None] * DEPTH
    gp = {name: [None] * P[name].shape[0] for name in ("mix_norm", "hg_out_norm", "q_norm", "k_norm", "sinks",
                                                        "ffn_norm", "ffn_conv_b", "ple_norm")}
    dlbs = [None] * (DEPTH // 2)
    dbias = jnp.zeros_like(bias)

    for i in reversed(range(DEPTH)):
        j = i // 2
        r = saved[i]
        dz, de = _sigmul_bwd(r["z"], r["e"], dh, f"ple_b{i}")
        grad_w(r["pi"], de, "ple_proj", i, f"ple_proj_g{i}")
        grad_w(r["hn3"], dz, "ple_gate", i, f"ple_gate_g{i}")
        dhn3 = times_wt(dz, "ple_gate", i, f"ple_gate_b{i}")
        dh, gp["ple_norm"][i] = _rms_bwd(r["h2"], P["ple_norm"][i], dhn3, f"ple_norm_b{i}", res=dh)

        dact = times_wt(dh, "ffn_down", i, f"ffn_down_b{i}").reshape(b, s, D_FF)
        grad_w(r["act"], dh, "ffn_down", i, f"ffn_down_g{i}")
        dug, duu, ag, au = _convglu_bwd(r["u"], W["ffn_conv"][i], P["ffn_conv_b"][i], dact, f"conv_b{i}")
        gconv[i] = jnp.concatenate([ag[:CONV_W], au[:CONV_W]], axis=-1)
        gp["ffn_conv_b"][i] = jnp.concatenate([ag[CONV_W], au[CONV_W]], axis=-1)
        dhn2 = None
        for half, dpart in enumerate((dug.reshape(n, D_FF), duu.reshape(n, D_FF))):
            gw["ffn_up"] = _mm(dpart, r["hn2"], "tn", f"ffn_up_g{i}_{half}", into=(gw["ffn_up"], i, half * D_FF))
            dhn2 = _mm(dpart, W["ffn_up"], "nn", f"ffn_up_b{i}_{half}", layer=i, b_rows=(half * D_FF, D_FF), res=dhn2)
        dh, gp["ffn_norm"][i] = _rms_bwd(r["h1"], P["ffn_norm"][i], dhn2, f"ffn_norm_b{i}", res=dh)

        if i % 2 == 0:
            dcat = times_wt(dh, "ab_w_out", j, f"ab_out_b{i}")
            grad_w(r["cat"], dh, "ab_w_out", j, f"ab_out_g{i}")
            dob, dgb, gp["hg_out_norm"][j] = _gnorm_bwd(r["ob"], r["proj"], HG_GATE_COL, P["hg_out_norm"][j], dcat, HG_OUT_COL,
                                                        f"hg_norm_b{i}")
            dqb, dfb, dib, dlb = _hg_bwd(r["proj"], HG_COLS, lbs[j].reshape(HG_HEADS, HG_DK), r["st"], dob, b, s, f"hg_b{i}")
            dlbs[j] = dlb.reshape(b, HG_W).sum(axis=0)
            dqa, dka, dva = _sb_bwd(r["proj"], SB_COLS, r["lta"], dcat, 0, b, s, f"sb_b{i}")
            dproj = jnp.concatenate([dqa, dka, dva, dqb, dfb, dib, dgb], axis=1)
            grad_w(r["hn"], dproj, "ab_w_in", j, f"ab_in_g{i}")
            dhn = times_wt(dproj, "ab_w_in", j, f"ab_in_b{i}")
        else:
            do2 = times_wt(dh, "c_w_out", j, f"c_out_b{i}")
            grad_w(r["o2"], dh, "c_w_out", j, f"c_out_g{i}")
            do = _heads_out(do2, b, s, SW_HEADS, SW_DIM).reshape(b, SW_KV_HEADS, SW_GROUP, s, SW_DIM)
            dqn, dkn, dv, dbias_i, dsink = _swa_bwd(r["qn"], r["kn"], r["v"], P["sinks"][j], bias, do, f"swa_b{i}")
            dbias = dbias + dbias_i
            gp["sinks"][j] = dsink[:, :, 0].reshape(SW_HEADS)
            dq, gp["q_norm"][j] = _rms_bwd(r["q"], P["q_norm"][j], dqn.reshape(-1, SW_DIM), f"q_norm_b{i}")
            dk, gp["k_norm"][j] = _rms_bwd(r["k"], P["k_norm"][j], dkn.reshape(-1, SW_DIM), f"k_norm_b{i}")
            dproj = jnp.concatenate([_heads_in(dq.reshape(b * SW_HEADS, s, SW_DIM), b, s, SW_HEADS, SW_DIM),
                                     _heads_in(dk.reshape(b * SW_KV_HEADS, s, SW_DIM), b, s, SW_KV_HEADS, SW_DIM),
                                     _heads_in(dv.reshape(b * SW_KV_HEADS, s, SW_DIM), b, s, SW_KV_HEADS, SW_DIM)], axis=1)
            grad_w(r["hn"], dproj, "c_w_in", j, f"c_in_g{i}")
            dhn = times_wt(dproj, "c_w_in", j, f"c_in_b{i}")
        dh, gp["mix_norm"][i] = _rms_bwd(r["h0"], P["mix_norm"][i], dhn, f"mix_norm_b{i}", res=dh)

    gp = {name: jnp.stack(v) for name, v in gp.items()}
    gp["hg_lb_logits"] = lb_vjp(jnp.stack(dlbs))[0]
    gp["rel_bias"] = bias_vjp(dbias)[0]
    return loss[0, 0], dh.reshape(b, s, dm), gw, jnp.stack(gconv), gp


WEIGHTS = ("mix_norm", "ab_w_in", "hg_lb_logits", "hg_out_norm", "ab_w_out", "c_w_in", "q_norm", "k_norm", "sinks", "rel_bias",
           "c_w_out", "ffn_norm", "ffn_up", "ffn_conv", "ffn_conv_b", "ffn_down", "ple_norm", "ple_gate", "ple_proj")
SMALL = ("mix_norm", "hg_lb_logits", "hg_out_norm", "q_norm", "k_norm", "sinks", "rel_bias", "ffn_norm", "ffn_conv_b", "ple_norm")


def _step(x, p, target, w, m, v):
    names = [name for name, _ in BIG]
    mx, my, mc = _position()
    dev = 4 * mx + 2 * my + mc

    blocks = [_row_block(w[name], KIND[name]).astype(BF16) for name in names]
    full = dict(zip(names, _all_gather_rows(blocks, "gather_weights")))
    nl, taps, cs = w["ffn_conv"].shape
    conv_all = _all_gather(_pad_rows(w["ffn_conv"].reshape(-1)), "gather_conv").reshape(N_DEV, -1)[:, :nl * taps * cs]
    full["ffn_conv"] = conv_all.reshape(N_DEV, nl, taps, cs).transpose(1, 2, 0, 3).reshape(nl, taps, N_DEV * cs)

    small = {name: w[name] for name in SMALL}
    loss, grad_x, gw, gconv, gp = _forward_backward(x, p, target, full, small)

    core = jnp.reshape(mc, (1,)).astype(jnp.int32)
    where = jnp.stack([dev, 2 * mx + my]).astype(jnp.int32)
    parts = [gw[name] for name in names]
    from_sibling = _rs_pair(parts, "reduce_pair")
    chip_sums = [_rs_add_pair(g, a, core, f"reduce_pair_add_{name}") for name, g, a in zip(names, parts, from_sibling)]
    from_chips = _rs_chips(chip_sums, "reduce_chips")
    grads = {name: _row_block(_rs_final(g, a, bs, where, f"reduce_final_{name}"), KIND[name])
             for name, g, a, bs in zip(names, parts, from_sibling, from_chips)}

    flat_small = jnp.concatenate([gp[name].reshape(-1) for name in SMALL] + [gconv.reshape(-1), loss.reshape(1)])
    small_sum = _sum_devices(_all_gather(_pad_rows(flat_small), "gather_small"), "sum_small").reshape(-1)
    e0 = 0
    for name in SMALL:
        cnt = math.prod(w[name].shape)
        grads[name] = small_sum[e0:e0 + cnt].reshape(w[name].shape)
        e0 += cnt
    gconv_sum = small_sum[e0:e0 + gconv.size].reshape(gconv.shape)
    grads["ffn_conv"] = lax.dynamic_slice_in_dim(gconv_sum, dev * cs, cs, axis=2)
    loss = small_sum[e0 + gconv.size]

    deltas, new_m, new_v = {}, {}, {}
    for name in WEIGHTS:
        shape = w[name].shape
        view = (-1, shape[-1]) if len(shape) > 1 else (1, -1)
        d_, m_, v_ = _adamw(w[name].reshape(view), grads[name].reshape(view), m[name].reshape(view), v[name].reshape(view), f"adamw_{name}")
        deltas[name], new_m[name], new_v[name] = d_.reshape(shape), m_.reshape(shape), v_.reshape(shape)
    return (loss, grad_x, *[grads[k] for k in WEIGHTS], *[deltas[k] for k in WEIGHTS],
            *[new_m[k] for k in WEIGHTS], *[new_v[k] for k in WEIGHTS])


def kernel(x, p, mix_norm, ab_w_in, hg_lb_logits, hg_out_norm, ab_w_out, c_w_in, q_norm, k_norm, sinks, rel_bias, c_w_out, ffn_norm, ffn_up, ffn_conv, ffn_conv_b, ffn_down, ple_norm, ple_gate, ple_proj, loss_target, m_mix_norm, m_ab_w_in, m_hg_lb_logits, m_hg_out_norm, m_ab_w_out, m_c_w_in, m_q_norm, m_k_norm, m_sinks, m_rel_bias, m_c_w_out, m_ffn_norm, m_ffn_up, m_ffn_conv, m_ffn_conv_b, m_ffn_down, m_ple_norm, m_ple_gate, m_ple_proj, v_mix_norm, v_ab_w_in, v_hg_lb_logits, v_hg_out_norm, v_ab_w_out, v_c_w_in, v_q_norm, v_k_norm, v_sinks, v_rel_bias, v_c_w_out, v_ffn_norm, v_ffn_up, v_ffn_conv, v_ffn_conv_b, v_ffn_down, v_ple_norm, v_ple_gate, v_ple_proj):
    w = dict(zip(WEIGHTS, (mix_norm, ab_w_in, hg_lb_logits, hg_out_norm, ab_w_out, c_w_in, q_norm, k_norm, sinks, rel_bias, c_w_out,
                           ffn_norm, ffn_up, ffn_conv, ffn_conv_b, ffn_down, ple_norm, ple_gate, ple_proj)))
    m = dict(zip(WEIGHTS, (m_mix_norm, m_ab_w_in, m_hg_lb_logits, m_hg_out_norm, m_ab_w_out, m_c_w_in, m_q_norm, m_k_norm, m_sinks,
                           m_rel_bias, m_c_w_out, m_ffn_norm, m_ffn_up, m_ffn_conv, m_ffn_conv_b, m_ffn_down, m_ple_norm, m_ple_gate,
                           m_ple_proj)))
    v = dict(zip(WEIGHTS, (v_mix_norm, v_ab_w_in, v_hg_lb_logits, v_hg_out_norm, v_ab_w_out, v_c_w_in, v_q_norm, v_k_norm, v_sinks,
                           v_rel_bias, v_c_w_out, v_ffn_norm, v_ffn_up, v_ffn_conv, v_ffn_conv_b, v_ffn_down, v_ple_norm, v_ple_gate,
                           v_ple_proj)))
    return _step(x, p, loss_target, w, m, v)
```

```python
import functools
import math

import numpy as np
import jax
import jax.numpy as jnp
from jax import lax
from jax.experimental import pallas as pl
from jax.experimental.pallas import tpu as pltpu

F32 = jnp.float32
BF16 = jnp.bfloat16

D_MODEL = 1024
DEPTH = 4
PLE_DIM = 256
EPS = 1e-6
SB_HEADS, SB_DIM = 8, 64
SB_WIDTH = SB_HEADS * SB_DIM
HG_HEADS, HG_DK, HG_DV = 4, 128, 128
HG_W = HG_HEADS * HG_DK
AB_IN = 3 * SB_WIDTH + 4 * HG_W
SW_HEADS, SW_KV_HEADS, SW_DIM = 16, 4, 64
SW_GROUP = SW_HEADS // SW_KV_HEADS
WINDOW = 128
C_IN = (SW_HEADS + 2 * SW_KV_HEADS) * SW_DIM
N_BUCKETS, MAX_DISTANCE = 32, 128
D_FF = 2816
N_DEV = 8

ADAM_LR, ADAM_B1, ADAM_B2, ADAM_EPS, ADAM_WD, ADAM_STEP = 0.001, 0.9, 0.999, 1e-08, 0.01, 10

LANES = 128
VMEM_LIMIT = 48 * 1024 * 1024

NN = (((1,), (0,)), ((), ()))
NT = (((1,), (1,)), ((), ()))
TN = (((0,), (0,)), ((), ()))


MXU_DTYPE = BF16


def _bf(x):
    return x.astype(MXU_DTYPE)


def _dot(a, b, dims=NN):
    return lax.dot_general(_bf(a), _bf(b), dims, preferred_element_type=F32)


def _split3(x):
    x1 = _bf(x)
    r = x - x1.astype(F32)
    x2 = _bf(r)
    x3 = _bf(r - x2.astype(F32))
    return x1, x2, x3


def _dot_exact_lhs01(m, x, terms=3):
    parts = _split3(x)[:terms]
    out = lax.dot_general(m, parts[0], NN, preferred_element_type=F32)
    for p_ in parts[1:]:
        out = out + lax.dot_general(m, p_, NN, preferred_element_type=F32)
    return out


def _dot_exact_rhs01(x, m, terms=2):
    parts = _split3(x)[:terms]
    out = lax.dot_general(parts[0], m, NN, preferred_element_type=F32)
    for p_ in parts[1:]:
        out = out + lax.dot_general(p_, m, NN, preferred_element_type=F32)
    return out


def _pick(n, target):
    best = None
    for t in range(LANES, target + 1, LANES):
        if n % t == 0:
            best = t
    return best or n


def _params(sem=None):
    return pltpu.CompilerParams(dimension_semantics=sem, vmem_limit_bytes=VMEM_LIMIT)


def _mm(a, b, mode, name, res=None, out_dtype=F32, layer=None, b_rows=None, into=None):
    bshape = b.shape if layer is None else b.shape[1:]
    if b_rows is not None:
        assert mode == "nn"
        bshape = (b_rows[1], bshape[1])
    if mode == "nn":
        (M, K), (K2, N) = a.shape, bshape
    elif mode == "nt":
        (M, K), (N, K2) = a.shape, bshape
    else:
        (K, M), (K2, N) = a.shape, bshape
    assert K == K2, (a.shape, b.shape, mode)
    tm, tn, tk = _pick(M, 1024), _pick(N, 1024), _pick(K, 1536)
    nk = K // tk
    k_off = 0 if b_rows is None else b_rows[0] // tk
    assert b_rows is None or b_rows[0] % tk == 0
    dims = {"nn": NN, "nt": NT, "tn": TN}[mode]
    a_spec = pl.BlockSpec((tk, tm), lambda i, j, k: (k, i)) if mode == "tn" else pl.BlockSpec((tm, tk), lambda i, j, k: (i, k))
    if layer is None:
        b_spec = pl.BlockSpec((tn, tk), lambda i, j, k: (j, k)) if mode == "nt" else pl.BlockSpec((tk, tn), lambda i, j, k: (k, j))
    elif mode == "nt":
        b_spec = pl.BlockSpec((None, tn, tk), lambda i, j, k: (layer, j, k))
    else:
        b_spec = pl.BlockSpec((None, tk, tn), lambda i, j, k: (layer, k + k_off, j))
    o_spec = pl.BlockSpec((tm, tn), lambda i, j, k: (i, j))
    has_res = res is not None

    def finish(acc, r_ref, o_ref):
        if has_res:
            acc = acc + r_ref[...]
        o_ref[...] = acc.astype(out_dtype)

    def body(*refs):
        a_ref, b_ref = refs[0], refs[1]
        r_ref = refs[2] if has_res else None
        o_ref = refs[-1] if nk == 1 else refs[-2]
        part = _dot(a_ref[...], b_ref[...], dims)
        if nk == 1:
            finish(part, r_ref, o_ref)
            return
        acc_ref = refs[-1]
        k = pl.program_id(2)

        @pl.when(k == 0)
        def _():
            acc_ref[...] = part

        @pl.when((k > 0) & (k < nk - 1))
        def _():
            acc_ref[...] += part

        @pl.when(k == nk - 1)
        def _():
            finish(acc_ref[...] + part, r_ref, o_ref)

    in_specs = [a_spec, b_spec] + ([o_spec] if has_res else [])
    args = (a, b) + ((res,) if has_res else ())
    out_shape, aliases = jax.ShapeDtypeStruct((M, N), out_dtype), {}
    if into is not None:
        stack, slot, row = into
        assert stack.shape[2] == N and row % tm == 0 and row + M <= stack.shape[1] and stack.dtype == out_dtype and not has_res
        in_specs = in_specs + [pl.BlockSpec(memory_space=pl.ANY)]
        args = args + (stack,)
        o_spec = pl.BlockSpec((None, tm, tn), lambda i, j, k: (slot, i + row // tm, j))
        out_shape, aliases = jax.ShapeDtypeStruct(stack.shape, out_dtype), {2: 0}

    def body_into(a_ref, b_ref, stack_ref, *rest):
        body(a_ref, b_ref, *rest)

    return pl.pallas_call(
        body if into is None else body_into, name=name, grid=(M // tm, N // tn, nk), in_specs=in_specs, out_specs=o_spec,
        out_shape=out_shape, scratch_shapes=[] if nk == 1 else [pltpu.VMEM((tm, tn), F32)], input_output_aliases=aliases,
        compiler_params=_params(("parallel", "parallel", "arbitrary")),
    )(*args)


def _row_tile(n, d):
    if n % 8:
        return n
    t = 8
    while t * 2 <= min(n, (256 * 1024) // d) and n % (t * 2) == 0:
        t *= 2
    return t


def _rms_fwd(x, g, name, out_dtype=F32):
    n, d = x.shape
    tm = _row_tile(n, d)

    def body(x_ref, g_ref, o_ref):
        xf = x_ref[...]
        r = lax.rsqrt(jnp.mean(xf * xf, axis=-1, keepdims=True) + EPS)
        o_ref[...] = (xf * r * g_ref[...]).astype(out_dtype)

    return pl.pallas_call(
        body, name=name, grid=(n // tm,),
        in_specs=[pl.BlockSpec((tm, d), lambda i: (i, 0)), pl.BlockSpec((1, d), lambda i: (0, 0))],
        out_specs=pl.BlockSpec((tm, d), lambda i: (i, 0)),
        out_shape=jax.ShapeDtypeStruct((n, d), out_dtype), compiler_params=_params(("parallel",)),
    )(x, g.reshape(1, d))


def _rms_bwd(x, g, dy, name, res=None):
    n, d = x.shape
    tm = _row_tile(n, d)
    has_res = res is not None

    def body(*refs):
        x_ref, g_ref, dy_ref = refs[:3]
        r_ref = refs[3] if has_res else None
        dx_ref, dg_ref = refs[-2:]
        xf = x_ref[...]
        r = lax.rsqrt(jnp.mean(xf * xf, axis=-1, keepdims=True) + EPS)
        xh = xf * r
        dyf = dy_ref[...].astype(F32)
        dxh = dyf * g_ref[...]
        dx = r * (dxh - xh * jnp.mean(dxh * xh, axis=-1, keepdims=True))
        if has_res:
            dx = dx + r_ref[...]
        dx_ref[...] = dx

        @pl.when(pl.program_id(0) == 0)
        def _():
            dg_ref[...] = jnp.zeros_like(dg_ref)

        dg_ref[...] += jnp.sum(dyf * xh, axis=0, keepdims=True)

    row = pl.BlockSpec((tm, d), lambda i: (i, 0))
    vec = pl.BlockSpec((1, d), lambda i: (0, 0))
    dx, dg = pl.pallas_call(
        body, name=name, grid=(n // tm,),
        in_specs=[row, vec, row] + ([row] if has_res else []),
        out_specs=[row, vec],
        out_shape=[jax.ShapeDtypeStruct((n, d), F32), jax.ShapeDtypeStruct((1, d), F32)],
        compiler_params=_params(("arbitrary",)),
    )(x, g.reshape(1, d), dy, *((res,) if has_res else ()))
    return dx, dg.reshape(d)


def _silu(x):
    return x * jax.nn.sigmoid(x)


def _gnorm_fwd(o, gate, gate_col, w, name):
    n, width = o.shape
    d = w.shape[0]
    tm = _row_tile(n, d)

    def body(o_ref, g_ref, w_ref, y_ref):
        of = o_ref[...]
        r = lax.rsqrt(jnp.mean(of * of, axis=-1, keepdims=True) + EPS)
        y_ref[...] = of * r * w_ref[...] * _silu(g_ref[...])

    row = pl.BlockSpec((tm, d), lambda i, h: (i, h))
    vec = pl.BlockSpec((1, d), lambda i, h: (0, 0))
    return pl.pallas_call(body, name=name, grid=(n // tm, width // d),
                          in_specs=[row, pl.BlockSpec((tm, d), lambda i, h: (i, gate_col + h)), vec], out_specs=row,
                          out_shape=jax.ShapeDtypeStruct((n, width), F32), compiler_params=_params(("parallel", "parallel")))(o, gate, w.reshape(1, d))


def _gnorm_bwd(o, gate, gate_col, w, dy, dy_col, name):
    n, width = o.shape
    d = w.shape[0]
    tm = _row_tile(n, d)

    def body(o_ref, g_ref, w_ref, dy_ref, do_ref, dgate_ref, dw_ref):
        of, gf, dyf = o_ref[...], g_ref[...], dy_ref[...]
        r = lax.rsqrt(jnp.mean(of * of, axis=-1, keepdims=True) + EPS)
        xh = of * r
        sg = jax.nn.sigmoid(gf)
        sil = gf * sg
        dnorm = dyf * sil
        dgate_ref[...] = dyf * xh * w_ref[...] * (sg * (1.0 + gf * (1.0 - sg)))
        dxh = dnorm * w_ref[...]
        do_ref[...] = r * (dxh - xh * jnp.mean(dxh * xh, axis=-1, keepdims=True))

        @pl.when((pl.program_id(0) == 0) & (pl.program_id(1) == 0))
        def _():
            dw_ref[...] = jnp.zeros_like(dw_ref)

        dw_ref[...] += jnp.sum(dnorm * xh, axis=0, keepdims=True)

    row = pl.BlockSpec((tm, d), lambda i, h: (i, h))
    vec = pl.BlockSpec((1, d), lambda i, h: (0, 0))
    do, dgate, dw = pl.pallas_call(
        body, name=name, grid=(n // tm, width // d),
        in_specs=[row, pl.BlockSpec((tm, d), lambda i, h: (i, gate_col + h)), vec, pl.BlockSpec((tm, d), lambda i, h: (i, dy_col + h))],
        out_specs=[row, row, vec],
        out_shape=[jax.ShapeDtypeStruct((n, width), F32)] * 2 + [jax.ShapeDtypeStruct((1, d), F32)],
        compiler_params=_params(("arbitrary", "arbitrary")),
    )(o, gate, w.reshape(1, d), dy)
    return do, dgate, dw.reshape(d)


def _sigmul_fwd(z, e, res, name):
    n, d = z.shape
    tm = _row_tile(n, d)

    def body(z_ref, e_ref, r_ref, o_ref):
        o_ref[...] = r_ref[...] + jax.nn.sigmoid(z_ref[...]) * e_ref[...]

    row = pl.BlockSpec((tm, d), lambda i: (i, 0))
    return pl.pallas_call(body, name=name, grid=(n // tm,), in_specs=[row] * 3, out_specs=row,
                          out_shape=jax.ShapeDtypeStruct((n, d), F32), compiler_params=_params(("parallel",)))(z, e, res)


def _sigmul_bwd(z, e, dy, name):
    n, d = z.shape
    tm = _row_tile(n, d)

    def body(z_ref, e_ref, dy_ref, dz_ref, de_ref):
        s = jax.nn.sigmoid(z_ref[...])
        dyf = dy_ref[...]
        dz_ref[...] = dyf * e_ref[...] * s * (1.0 - s)
        de_ref[...] = dyf * s

    row = pl.BlockSpec((tm, d), lambda i: (i, 0))
    return pl.pallas_call(body, name=name, grid=(n // tm,), in_specs=[row] * 3, out_specs=[row] * 2,
                          out_shape=[jax.ShapeDtypeStruct((n, d), F32)] * 2, compiler_params=_params(("parallel",)))(z, e, dy)


def _loss_fwd(y, target, name):
    n, d = y.shape
    tm = _row_tile(n, d)

    def body(y_ref, t_ref, l_ref, dy_ref):
        diff = y_ref[...] - t_ref[...]
        dy_ref[...] = diff * (1.0 / d)

        @pl.when(pl.program_id(0) == 0)
        def _():
            l_ref[...] = jnp.zeros_like(l_ref)

        part = jnp.sum(jnp.mean(diff * diff, axis=-1, keepdims=True), axis=0, keepdims=True)
        l_ref[...] += 0.5 * jnp.broadcast_to(part, l_ref.shape)

    row = pl.BlockSpec((tm, d), lambda i: (i, 0))
    vec = pl.BlockSpec((1, LANES), lambda i: (0, 0))
    return pl.pallas_call(body, name=name, grid=(n // tm,), in_specs=[row, row], out_specs=[vec, row],
                          out_shape=[jax.ShapeDtypeStruct((1, LANES), F32), jax.ShapeDtypeStruct((n, d), F32)],
                          compiler_params=_params(("arbitrary",)))(y, target)


SB_BLK = 128
SB_QBLK = 256


def _sb_logits(z, qi, kj, row, col):
    mask = (kj * SB_BLK + col) < (qi * SB_QBLK + row)
    sp = jnp.maximum(z, 0.0) + jnp.log1p(jnp.exp(-jnp.abs(z)))
    lk = jnp.where(mask, -sp, 0.0)
    return mask, lk, z - sp


SB_PAIRS = SB_WIDTH // LANES


def _sb_iotas():
    row = lax.broadcasted_iota(jnp.int32, (2 * SB_QBLK, SB_BLK), 0)
    row = jnp.where(row >= SB_QBLK, row - SB_QBLK, row)
    col = lax.broadcasted_iota(jnp.int32, (2 * SB_QBLK, SB_BLK), 1)
    return row, col, col[:SB_QBLK] < SB_DIM


def _sb_stack(x, first):
    return jnp.concatenate([jnp.where(first, x, 0.0), jnp.where(first, 0.0, x)], axis=0)


def _sb_unstack(y, first):
    return jnp.where(first, y[:SB_QBLK], y[SB_QBLK:])


def _sb_running(x, u):
    m = x.shape[0]
    hi = _bf(x)
    lo = _bf(x - hi.astype(F32))
    c = lax.dot_general(jnp.concatenate([hi, lo], axis=0), u, NN, preferred_element_type=F32)
    return c[:m] + c[m:]


def _sb_spec(s, col):
    return pl.BlockSpec((s, LANES), lambda e, pr: (e, col + pr))


def _sb_fwd(proj, cols, b, s, name):
    nq = s // SB_QBLK
    scale = SB_DIM ** -0.5

    def body(q_ref, k_ref, v_ref, o_ref, lt_ref):
        row, col, first = _sb_iotas()
        u_after = _bf(row[:SB_BLK] > col[:SB_BLK])

        def qloop(qi, _):
            q0 = pl.multiple_of(qi * SB_QBLK, SB_QBLK)
            q2 = _sb_stack(q_ref[pl.ds(q0, SB_QBLK), :], first)
            nkeys = (qi + 1) * (SB_QBLK // SB_BLK)

            def logits(kj):
                k0 = pl.multiple_of(kj * SB_BLK, SB_BLK)
                return _dot(q2, k_ref[pl.ds(k0, SB_BLK), :], NT) * scale

            def kloop(j, st):
                acc, carry, z = st
                kj = nkeys - 1 - j
                k0 = pl.multiple_of(kj * SB_BLK, SB_BLK)
                z_next = logits(jnp.maximum(kj - 1, 0))
                mask, lk, ls = _sb_logits(z, qi, kj, row, col)
                later = carry + _sb_running(lk, u_after)
                w = jnp.where(mask, jnp.exp(ls + later), 0.0)
                acc = acc + _sb_unstack(_dot(w, v_ref[pl.ds(k0, SB_BLK), :]), first)
                return acc, carry + jnp.sum(lk, axis=1, keepdims=True), z_next

            acc, carry, _ = lax.fori_loop(0, nkeys, kloop, (jnp.zeros((SB_QBLK, LANES), F32), jnp.zeros((2 * SB_QBLK, 1), F32), logits(nkeys - 1)))
            o_ref[pl.ds(q0, SB_QBLK), :] = acc
            lt_ref[pl.ds(q0, SB_QBLK), :] = _sb_unstack(jnp.broadcast_to(carry, (2 * SB_QBLK, LANES)), first)
            return 0

        lax.fori_loop(0, nq, qloop, 0)

    out = _sb_spec(s, 0)
    return pl.pallas_call(body, name=name, grid=(b, SB_PAIRS), in_specs=[_sb_spec(s, c) for c in cols], out_specs=[out, out],
                          out_shape=[jax.ShapeDtypeStruct((b * s, SB_WIDTH), F32)] * 2,
                          compiler_params=_params(("parallel", "parallel")))(proj, proj, proj)


def _sb_bwd(proj, cols, ltot, do, do_col, b, s, name):
    nq = s // SB_QBLK
    scale = SB_DIM ** -0.5

    def body(q_ref, k_ref, v_ref, lt_ref, do_ref, dq_ref, dk_ref, dv_ref):
        row, col, first = _sb_iotas()
        u_upto = _bf(row[:SB_BLK] <= col[:SB_BLK])
        u_before = _bf(row[:SB_BLK] < col[:SB_BLK])
        dk_ref[...] = jnp.zeros_like(dk_ref)
        dv_ref[...] = jnp.zeros_like(dv_ref)

        def qloop(qi, _):
            q0 = pl.multiple_of(qi * SB_QBLK, SB_QBLK)
            q2 = _sb_stack(q_ref[pl.ds(q0, SB_QBLK), :], first)
            nkeys = (qi + 1) * (SB_QBLK // SB_BLK)
            do2 = _sb_stack(do_ref[pl.ds(q0, SB_QBLK), :], first)
            lt2 = jnp.min(_sb_stack(lt_ref[pl.ds(q0, SB_QBLK), :], first), axis=1, keepdims=True)

            def logits(kj):
                k0 = pl.multiple_of(kj * SB_BLK, SB_BLK)
                return _dot(q2, k_ref[pl.ds(k0, SB_BLK), :], NT) * scale

            def kloop(kj, st):
                dq, cl, cg, z = st
                k0 = pl.multiple_of(kj * SB_BLK, SB_BLK)
                kb = k_ref[pl.ds(k0, SB_BLK), :]
                vb = v_ref[pl.ds(k0, SB_BLK), :]
                z_next = logits(jnp.minimum(kj + 1, nkeys - 1))
                mask, lk, ls = _sb_logits(z, qi, kj, row, col)
                later = lt2 - (cl + _sb_running(lk, u_upto))
                w = jnp.where(mask, jnp.exp(ls + later), 0.0)
                g = _dot(do2, vb, NT) * w
                dv_ref[pl.ds(k0, SB_BLK), :] += _dot(w, do2, TN)
                g_before = cg + _sb_running(g, u_before)
                sig = jnp.exp(ls)
                dz = jnp.where(mask, g * (1.0 - sig) - sig * g_before, 0.0) * scale
                dq = dq + _sb_unstack(_dot(dz, kb), first)
                dk_ref[pl.ds(k0, SB_BLK), :] += _dot(dz, q2, TN)
                return dq, cl + jnp.sum(lk, axis=1, keepdims=True), cg + jnp.sum(g, axis=1, keepdims=True), z_next

            z1 = jnp.zeros((2 * SB_QBLK, 1), F32)
            dq = lax.fori_loop(0, nkeys, kloop, (jnp.zeros((SB_QBLK, LANES), F32), z1, z1, logits(0)))[0]
            dq_ref[pl.ds(q0, SB_QBLK), :] = dq
            return 0

        lax.fori_loop(0, nq, qloop, 0)

    out = _sb_spec(s, 0)
    return pl.pallas_call(body, name=name, grid=(b, SB_PAIRS),
                          in_specs=[_sb_spec(s, c) for c in cols] + [out, _sb_spec(s, do_col)], out_specs=[out] * 3,
                          out_shape=[jax.ShapeDtypeStruct((b * s, SB_WIDTH), F32)] * 3,
                          compiler_params=_params(("parallel", "parallel")))(proj, proj, proj, ltot, do)


HG_CHUNK = 64
HG_GROUP = 2


def _hg_consts(c, r):
    levels = int(math.log2(c))
    t = np.arange(r)
    same = (t[:, None] // c) == (t[None, :] // c)
    tri = ((t[:, None] >= t[None, :]) & same).astype(np.float32)
    psel = np.zeros((levels, r, r), np.float32)
    masks = np.zeros((levels + 1, r, r), np.float32)
    for l in range(levels):
        n = c >> (l + 1)
        blk = t // (2 * n)
        psel[l, t, blk * 2 * n + n - 1] = 1.0
        upper = (t % (2 * n)) >= n
        masks[l] = (blk[:, None] == blk[None, :]) & upper[:, None] & (~upper)[None, :]
    masks[levels] = np.eye(r)
    psel = psel.reshape(levels * r, r)
    return levels, jnp.asarray(tri), jnp.asarray(psel), jnp.asarray(masks), jnp.asarray(tri.T.copy()), jnp.asarray(psel.T.copy())


def _hg_elem(qv, fv, lbv):
    sig = jax.nn.sigmoid(fv)
    lf = jnp.log(lbv + (1.0 - lbv) * sig)
    kk = (1.0 - lbv) * jax.nn.sigmoid(-fv)
    qf = qv * jax.nn.sigmoid(qv)
    return qf, kk, lf


def _col_bcast(rowvec):
    n = rowvec.shape[1]
    return jnp.transpose(jnp.broadcast_to(rowvec, (n, n)))


def _hg_within(qf, kk, lf, tri, psel, m_ref, c, levels):
    r = qf.shape[0]
    b = _dot_exact_lhs01(tri, lf)
    bls = [b[(g + 1) * c - 1:(g + 1) * c, :] for g in range(r // c)]
    blb = jnp.concatenate([jnp.broadcast_to(bl, (c, bl.shape[1])) for bl in bls], axis=0)
    eb = jnp.exp(b)
    qi = qf * eb
    bsel = _dot_exact_lhs01(psel, b)
    scores = jnp.where(m_ref[levels] > 0, _dot(qf, kk, NT), 0.0)
    lev = []
    for l in range(levels):
        bs = bsel[l * r:(l + 1) * r]
        eq = jnp.exp(jnp.minimum(b - bs, 0.0))
        ek = jnp.exp(jnp.minimum(bs - b, 0.0))
        ql, kl = qf * eq, kk * ek
        scores = scores + jnp.where(m_ref[l] > 0, _dot(ql, kl, NT), 0.0)
        lev.append((eq, ek, ql, kl))
    ebl = jnp.exp(blb - b)
    kd = kk * ebl
    decays = [_col_bcast(jnp.exp(bl)) for bl in bls]
    return eb, qi, scores, lev, ebl, kd, decays


def _hg_fwd(proj, cols, lb, b, s, name):
    nh, d = lb.shape
    bh = b * nh
    c = HG_CHUNK
    nc = s // c
    grp = math.gcd(HG_GROUP, nc)
    r = grp * c
    levels, tri, psel, masks, _, _ = _hg_consts(c, r)

    def body(q_ref, f_ref, i_ref, lb_ref, tri_ref, psel_ref, m_ref, o_ref, st_ref):
        lbv = jnp.broadcast_to(lb_ref[0], (r, d))
        tri_v, psel_v = _bf(tri_ref[...]), _bf(psel_ref[...])

        def group(gi, state):
            r0 = pl.multiple_of(gi * r, r)
            qf, kk, lf = _hg_elem(q_ref[pl.ds(r0, r), :], f_ref[pl.ds(r0, r), :], lbv)
            iv = i_ref[pl.ds(r0, r), :]
            _, qi, scores, _, _, kd, decays = _hg_within(qf, kk, lf, tri_v, psel_v, m_ref, c, levels)
            within = _dot(scores, iv)
            for g in range(grp):
                rows = slice(g * c, (g + 1) * c)
                st_ref[0, gi * grp + g] = state
                o_ref[pl.ds(r0 + g * c, c), :] = _dot(qi[rows], state) + within[rows]
                state = decays[g] * state + _dot(kd[rows], iv[rows], TN)
            return state

        lax.fori_loop(0, nc // grp, group, jnp.zeros((d, d), F32))

    full = lambda a: pl.BlockSpec(a.shape, lambda e, hd: (0,) * a.ndim)
    return pl.pallas_call(
        body, name=name, grid=(b, nh),
        in_specs=[_hg_seq(s, d, cols[0]), _hg_seq(s, d, cols[1]), _hg_seq(s, d, cols[2]),
                  pl.BlockSpec((1, 1, d), lambda e, hd: (hd, 0, 0)), full(tri), full(psel), full(masks)],
        out_specs=[_hg_seq(s, d, 0), pl.BlockSpec((1, nc, d, d), lambda e, hd: (e * nh + hd, 0, 0, 0))],
        out_shape=[jax.ShapeDtypeStruct((b * s, nh * d), F32), jax.ShapeDtypeStruct((bh, nc, d, d), F32)],
        compiler_params=_params(("parallel", "parallel")),
    )(proj, proj, proj, lb.reshape(nh, 1, d), tri, psel, masks)


def _hg_seq(s, d, col):
    return pl.BlockSpec((s, d), lambda e, hd: (e, col + hd))


def _hg_bwd(proj, cols, lb, states, do, b, s, name):
    nh, d = lb.shape
    bh = b * nh
    c = HG_CHUNK
    nc = s // c
    grp = math.gcd(HG_GROUP, nc)
    r = grp * c
    levels, tri, psel, masks, tri_t, psel_t = _hg_consts(c, r)

    def body(q_ref, f_ref, i_ref, lb_ref, st_ref, do_ref, tri_ref, psel_ref, m_ref, trit_ref, pselt_ref,
             dq_ref, df_ref, di_ref, dlb_ref):
        lbv = jnp.broadcast_to(lb_ref[0], (r, d))
        tri_v, psel_v = _bf(tri_ref[...]), _bf(psel_ref[...])
        trit_v, pselt_v = _bf(trit_ref[...]), _bf(pselt_ref[...])
        row_in_chunk = lax.broadcasted_iota(jnp.int32, (c, d), 0)

        def chunk(step, carry):
            ds_out, dlb = carry
            gi = nc // grp - 1 - step
            r0 = pl.multiple_of(gi * r, r)
            qv, fv, iv = q_ref[pl.ds(r0, r), :], f_ref[pl.ds(r0, r), :], i_ref[pl.ds(r0, r), :]
            dov = do_ref[pl.ds(r0, r), :]
            (qf, kk, lf), elem_vjp = jax.vjp(_hg_elem, qv, fv, lbv)
            eb, qi, scores, lev, ebl, kd, decays = _hg_within(qf, kk, lf, tri_v, psel_v, m_ref, c, levels)

            dscores = _dot(dov, iv, NT)
            di_within = _dot(scores, dov, TN)
            dqi_parts, dkd_parts, dbl_parts = [None] * grp, [None] * grp, [None] * grp
            for g in reversed(range(grp)):
                rows = slice(g * c, (g + 1) * c)
                state = st_ref[0, gi * grp + g]
                di_ref[pl.ds(r0 + g * c, c), :] = di_within[rows] + _dot(kd[rows], ds_out)
                dqi_parts[g] = _dot(dov[rows], state, NT)
                dkd_parts[g] = _dot(iv[rows], ds_out, NT)
                dbl = (jnp.sum(dkd_parts[g] * kd[rows], axis=0, keepdims=True)
                       + _col_bcast_t(jnp.sum(ds_out * decays[g] * state, axis=1, keepdims=True)))
                dbl_parts[g] = jnp.where(row_in_chunk == c - 1, dbl, 0.0)
                ds_out = decays[g] * ds_out + _dot(qi[rows], dov[rows], TN)
            ds_in = ds_out
            dqi = jnp.concatenate(dqi_parts, axis=0)
            dkd = jnp.concatenate(dkd_parts, axis=0)
            dqf = dqi * eb
            dkk = dkd * ebl
            db = dqi * qi - dkd * kd + jnp.concatenate(dbl_parts, axis=0)
            dsd = jnp.where(m_ref[levels] > 0, dscores, 0.0)
            dqf = dqf + _dot(dsd, kk)
            dkk = dkk + _dot(dsd, qf, TN)
            dbsel = []
            for l in range(levels):
                eq, ek, ql, kl = lev[l]
                dsl = jnp.where(m_ref[l] > 0, dscores, 0.0)
                dql = _dot(dsl, kl)
                dkl = _dot(dsl, ql, TN)
                dqf = dqf + dql * eq
                dkk = dkk + dkl * ek
                diff = dql * ql - dkl * kl
                db = db + diff
                dbsel.append(-diff)
            db = db + _dot_exact_lhs01(pselt_v, jnp.concatenate(dbsel, axis=0))
            dlf = _dot_exact_lhs01(trit_v, db)
            dq, df, dlb_c = elem_vjp((dqf, dkk, dlf))
            dq_ref[pl.ds(r0, r), :] = dq
            df_ref[pl.ds(r0, r), :] = df
            return ds_in, dlb + jnp.sum(dlb_c, axis=0, keepdims=True)

        _, dlb = lax.fori_loop(0, nc // grp, chunk, (jnp.zeros((d, d), F32), jnp.zeros((1, d), F32)))
        dlb_ref[0] = dlb

    seq = _hg_seq(s, d, 0)
    full = lambda a: pl.BlockSpec(a.shape, lambda e, hd: (0,) * a.ndim)
    return pl.pallas_call(
        body, name=name, grid=(b, nh),
        in_specs=[_hg_seq(s, d, cols[0]), _hg_seq(s, d, cols[1]), _hg_seq(s, d, cols[2]),
                  pl.BlockSpec((1, 1, d), lambda e, hd: (hd, 0, 0)),
                  pl.BlockSpec((1, nc, d, d), lambda e, hd: (e * nh + hd, 0, 0, 0)), seq,
                  full(tri), full(psel), full(masks), full(tri_t), full(psel_t)],
        out_specs=[seq, seq, seq, pl.BlockSpec((1, 1, d), lambda e, hd: (e * nh + hd, 0, 0))],
        out_shape=[jax.ShapeDtypeStruct((b * s, nh * d), F32)] * 3 + [jax.ShapeDtypeStruct((bh, 1, d), F32)],
        compiler_params=_params(("parallel", "parallel")),
    )(proj, proj, proj, lb.reshape(nh, 1, d), states, do, tri, psel, masks, tri_t, psel_t)


def _col_bcast_t(colvec):
    n = colvec.shape[0]
    return jnp.transpose(jnp.broadcast_to(colvec, (n, n)))[0:1, :]


def _swa_probs(qg, kb, bias, sink, valid, scale):
    logits = _dot(qg, kb, NT) * scale + bias
    logits = jnp.where(valid, logits, -jnp.inf)
    m = jnp.maximum(jnp.max(logits, axis=-1, keepdims=True), sink)
    e = jnp.exp(logits - m)
    es = jnp.exp(sink - m)
    den = jnp.sum(e, axis=-1, keepdims=True) + es
    return e / den, es / den


def _swa_valid(n):
    w = WINDOW
    row = lax.broadcasted_iota(jnp.int32, (w, 2 * w), 0)
    col = lax.broadcasted_iota(jnp.int32, (w, 2 * w), 1)
    dist = row + w - col
    return (dist >= 0) & (dist < w) & ((col >= w) | (n > 0))


def _swa_specs(b, g, s, d):
    w = WINDOW
    q_spec = pl.BlockSpec((1, 1, g, w, d), lambda h, bi, n: (bi, h, 0, n, 0))
    kp_spec = pl.BlockSpec((1, 1, w, d), lambda h, bi, n: (bi, h, jnp.maximum(n - 1, 0), 0))
    kc_spec = pl.BlockSpec((1, 1, w, d), lambda h, bi, n: (bi, h, n, 0))
    bias_spec = pl.BlockSpec((1, g, w, 2 * w), lambda h, bi, n: (h, 0, 0, 0))
    sink_spec = pl.BlockSpec(memory_space=pltpu.SMEM)
    return q_spec, kp_spec, kc_spec, bias_spec, sink_spec


def _swa_fwd(q, k, v, sinks, bias, name):
    b, kvh, g, s, d = q.shape
    w = WINDOW
    scale = d ** -0.5
    q_spec, kp_spec, kc_spec, bias_spec, sink_spec = _swa_specs(b, g, s, d)

    def body(q_ref, kp_ref, kc_ref, vp_ref, vc_ref, bias_ref, sink_ref, o_ref):
        h, n = pl.program_id(0), pl.program_id(2)
        valid = _swa_valid(n)
        kb = jnp.concatenate([kp_ref[0, 0], kc_ref[0, 0]], axis=0)
        vb = jnp.concatenate([vp_ref[0, 0], vc_ref[0, 0]], axis=0)
        for gi in range(g):
            p, _ = _swa_probs(q_ref[0, 0, gi], kb, bias_ref[0, gi], sink_ref[h * g + gi], valid, scale)
            o_ref[0, 0, gi] = _dot(p, vb)

    return pl.pallas_call(
        body, name=name, grid=(kvh, b, s // w),
        in_specs=[q_spec, kp_spec, kc_spec, kp_spec, kc_spec, bias_spec, sink_spec], out_specs=q_spec,
        out_shape=jax.ShapeDtypeStruct(q.shape, F32), compiler_params=_params(("parallel", "parallel", "arbitrary")),
    )(q, k, k, v, v, bias, sinks)


def _swa_bwd(q, k, v, sinks, bias, do, name):
    b, kvh, g, s, d = q.shape
    w = WINDOW
    scale = d ** -0.5
    q_spec, kp_spec, kc_spec, bias_spec, sink_spec = _swa_specs(b, g, s, d)
    kv_acc = pl.BlockSpec((1, 1, s, d), lambda h, bi, n: (bi, h, 0, 0))
    dsink_spec = pl.BlockSpec((1, g, LANES), lambda h, bi, n: (h, 0, 0))

    def body(q_ref, kp_ref, kc_ref, vp_ref, vc_ref, bias_ref, sink_ref, do_ref, dq_ref, dk_ref, dv_ref, dbias_ref, dsink_ref):
        h, bi, n = pl.program_id(0), pl.program_id(1), pl.program_id(2)
        valid = _swa_valid(n)
        kb = jnp.concatenate([kp_ref[0, 0], kc_ref[0, 0]], axis=0)
        vb = jnp.concatenate([vp_ref[0, 0], vc_ref[0, 0]], axis=0)

        @pl.when(n == 0)
        def _():
            dk_ref[...] = jnp.zeros_like(dk_ref)
            dv_ref[...] = jnp.zeros_like(dv_ref)

        @pl.when((n == 0) & (bi == 0))
        def _():
            dbias_ref[...] = jnp.zeros_like(dbias_ref)
            dsink_ref[...] = jnp.zeros_like(dsink_ref)

        dkb = jnp.zeros((2 * w, d), F32)
        dvb = jnp.zeros((2 * w, d), F32)
        for gi in range(g):
            qg, dog = q_ref[0, 0, gi], do_ref[0, 0, gi]
            p, ps = _swa_probs(qg, kb, bias_ref[0, gi], sink_ref[h * g + gi], valid, scale)
            dp = _dot(dog, vb, NT)
            delta = jnp.sum(p * dp, axis=-1, keepdims=True)
            dl = p * (dp - delta)
            dq_ref[0, 0, gi] = _dot(dl, kb) * scale
            dkb = dkb + _dot(dl, qg, TN) * scale
            dvb = dvb + _dot(p, dog, TN)
            dbias_ref[0, gi] += dl
            dsink_ref[0, gi:gi + 1, :] += jnp.broadcast_to(jnp.sum(-ps * delta, axis=0, keepdims=True), (1, LANES))

        c0 = pl.multiple_of(n * w, w)
        dk_ref[0, 0, pl.ds(c0, w), :] += dkb[w:]
        dv_ref[0, 0, pl.ds(c0, w), :] += dvb[w:]

        @pl.when(n > 0)
        def _():
            p0 = pl.multiple_of((n - 1) * w, w)
            dk_ref[0, 0, pl.ds(p0, w), :] += dkb[:w]
            dv_ref[0, 0, pl.ds(p0, w), :] += dvb[:w]

    return pl.pallas_call(
        body, name=name, grid=(kvh, b, s // w),
        in_specs=[q_spec, kp_spec, kc_spec, kp_spec, kc_spec, bias_spec, sink_spec, q_spec],
        out_specs=[q_spec, kv_acc, kv_acc, bias_spec, dsink_spec],
        out_shape=[jax.ShapeDtypeStruct(q.shape, F32), jax.ShapeDtypeStruct(k.shape, F32), jax.ShapeDtypeStruct(k.shape, F32),
                   jax.ShapeDtypeStruct(bias.shape, F32), jax.ShapeDtypeStruct((kvh, g, LANES), F32)],
        compiler_params=_params(("arbitrary", "arbitrary", "arbitrary")),
    )(q, k, k, v, v, bias, sinks, do)


def _t5_bias(rel_bias):
    t = np.arange(WINDOW)[:, None]
    s = np.arange(2 * WINDOW)[None, :]
    dist = t + WINDOW - s
    max_exact = N_BUCKETS // 2
    large = max_exact + (np.log(np.maximum(dist, max_exact) / max_exact) / math.log(MAX_DISTANCE / max_exact)
                         * (N_BUCKETS - max_exact)).astype(np.int32)
    large = np.minimum(large, N_BUCKETS - 1)
    bucket = np.where(dist < max_exact, np.maximum(dist, 0), large).astype(np.int32)
    onehot = jnp.asarray(np.eye(N_BUCKETS, dtype=np.float32)[bucket])
    bias = jnp.einsum("tsb,bh->hts", onehot, rel_bias.astype(F32), precision=lax.Precision.HIGHEST)
    return bias.reshape(SW_KV_HEADS, SW_GROUP, WINDOW, 2 * WINDOW)


CONV_W = 3


def _shift_down(x, k):
    row = lax.broadcasted_iota(jnp.int32, x.shape, 0)
    return jnp.where(row >= k, pltpu.roll(x, k, axis=0), 0.0)


def _shift_up(x, k):
    n = x.shape[0]
    row = lax.broadcasted_iota(jnp.int32, x.shape, 0)
    return jnp.where(row < n - k, pltpu.roll(x, n - k, axis=0), 0.0)


def _conv3(u, w, bvec):
    return w[0:1] * _shift_down(u, 2) + w[1:2] * _shift_down(u, 1) + w[2:3] * u + bvec


def _convglu_fwd(u, w, bvec, name, out_dtype=F32):
    b, s, f2 = u.shape
    f = f2 // 2
    tc = _pick(f, 256)
    nt = f // tc

    def body(ug_ref, uu_ref, wg_ref, wu_ref, bg_ref, bu_ref, o_ref, ot_ref):
        cg = _conv3(ug_ref[0], wg_ref[...], bg_ref[...])
        cu = _conv3(uu_ref[0], wu_ref[...], bu_ref[...])
        act = _silu(cg) * cu
        o_ref[0] = act.astype(out_dtype)
        ot_ref[...] = jnp.transpose(act).astype(out_dtype)

    ug = pl.BlockSpec((1, s, tc), lambda j, bi: (bi, 0, j))
    uu = pl.BlockSpec((1, s, tc), lambda j, bi: (bi, 0, j + nt))
    wg = pl.BlockSpec((CONV_W, tc), lambda j, bi: (0, j))
    wu = pl.BlockSpec((CONV_W, tc), lambda j, bi: (0, j + nt))
    bg = pl.BlockSpec((1, tc), lambda j, bi: (0, j))
    bu = pl.BlockSpec((1, tc), lambda j, bi: (0, j + nt))
    bv = bvec.reshape(1, f2)
    return pl.pallas_call(body, name=name, grid=(nt, b), in_specs=[ug, uu, wg, wu, bg, bu],
                          out_specs=[ug, pl.BlockSpec((tc, s), lambda j, bi: (j, bi))],
                          out_shape=[jax.ShapeDtypeStruct((b, s, f), out_dtype), jax.ShapeDtypeStruct((f, b * s), out_dtype)],
                          compiler_params=_params(("parallel", "parallel")))(u, u, w, w, bv, bv)


def _convglu_bwd(u, w, bvec, dact, name):
    b, s, f2 = u.shape
    f = f2 // 2
    tc = LANES
    nt = f // tc

    def taps(dc, uv):
        rows = [jnp.sum(dc * _shift_down(uv, 2), axis=0, keepdims=True), jnp.sum(dc * _shift_down(uv, 1), axis=0, keepdims=True),
                jnp.sum(dc * uv, axis=0, keepdims=True), jnp.sum(dc, axis=0, keepdims=True)]
        return jnp.concatenate(rows + [jnp.zeros((4, tc), F32)], axis=0)

    def back(dc, wv):
        return wv[2:3] * dc + wv[1:2] * _shift_up(dc, 1) + wv[0:1] * _shift_up(dc, 2)

    def body(ug_ref, uu_ref, wg_ref, wu_ref, bg_ref, bu_ref, da_ref, dug_ref, duu_ref, dwg_ref, dwu_ref, dugt_ref, duut_ref):
        ugv, uuv, da = ug_ref[0], uu_ref[0], da_ref[0]
        cg = _conv3(ugv, wg_ref[...], bg_ref[...])
        cu = _conv3(uuv, wu_ref[...], bu_ref[...])
        sg = jax.nn.sigmoid(cg)
        dcu = da * (cg * sg)
        dcg = da * cu * (sg * (1.0 + cg * (1.0 - sg)))
        dug = back(dcg, wg_ref[...])
        duu = back(dcu, wu_ref[...])
        dug_ref[0] = dug
        duu_ref[0] = duu
        dugt_ref[...] = jnp.transpose(dug).astype(BF16)
        duut_ref[...] = jnp.transpose(duu).astype(BF16)

        @pl.when(pl.program_id(1) == 0)
        def _():
            dwg_ref[...] = jnp.zeros_like(dwg_ref)
            dwu_ref[...] = jnp.zeros_like(dwu_ref)

        dwg_ref[...] += taps(dcg, ugv)
        dwu_ref[...] += taps(dcu, uuv)

    ug = pl.BlockSpec((1, s, tc), lambda j, bi: (bi, 0, j))
    uu = pl.BlockSpec((1, s, tc), lambda j, bi: (bi, 0, j + nt))
    wg = pl.BlockSpec((CONV_W, tc), lambda j, bi: (0, j))
    wu = pl.BlockSpec((CONV_W, tc), lambda j, bi: (0, j + nt))
    bg = pl.BlockSpec((1, tc), lambda j, bi: (0, j))
    bu = pl.BlockSpec((1, tc), lambda j, bi: (0, j + nt))
    acc = pl.BlockSpec((8, tc), lambda j, bi: (0, j))
    bv = bvec.reshape(1, f2)
    tr = pl.BlockSpec((tc, s), lambda j, bi: (j, bi))
    return pl.pallas_call(
        body, name=name, grid=(nt, b), in_specs=[ug, uu, wg, wu, bg, bu, ug], out_specs=[ug, ug, acc, acc, tr, tr],
        out_shape=[jax.ShapeDtypeStruct((b, s, f), F32)] * 2 + [jax.ShapeDtypeStruct((8, f), F32)] * 2
        + [jax.ShapeDtypeStruct((f, b * s), BF16)] * 2,
        compiler_params=_params(("parallel", "arbitrary")),
    )(u, u, w, w, bv, bv, dact)


def _adamw(w, g, m, v, name):
    r, c = w.shape
    tr = _row_tile(r, c)
    c1 = 1.0 - ADAM_B1 ** ADAM_STEP
    c2 = 1.0 - ADAM_B2 ** ADAM_STEP

    def body(w_ref, g_ref, m_ref, v_ref, d_ref, mo_ref, vo_ref):
        gv = g_ref[...]
        mn = ADAM_B1 * m_ref[...] + (1.0 - ADAM_B1) * gv
        vn = ADAM_B2 * v_ref[...] + (1.0 - ADAM_B2) * (gv * gv)
        d_ref[...] = -ADAM_LR * ((mn / c1) / (jnp.sqrt(vn / c2) + ADAM_EPS) + ADAM_WD * w_ref[...])
        mo_ref[...] = mn
        vo_ref[...] = vn

    blk = pl.BlockSpec((tr, c), lambda i: (i, 0))
    return pl.pallas_call(body, name=name, grid=(r // tr,), in_specs=[blk] * 4, out_specs=[blk] * 3,
                          out_shape=[jax.ShapeDtypeStruct((r, c), F32)] * 3, compiler_params=_params(("parallel",)))(w, g, m, v)


MESH = pl.DeviceIdType.MESH
ANY = pl.BlockSpec(memory_space=pl.ANY)


def _position():
    return lax.axis_index("x"), lax.axis_index("y"), lax.axis_index("c")


def _all_gather(x, name):
    r, c = x.shape

    def body(x_ref, out_ref, send_sems, recv_sems, local_sem):
        mx, my, mc = _position()
        me, sibling = (mx, my, mc), (mx, my, 1 - mc)
        chips = [(1 - mx, my), (mx, 1 - my), (1 - mx, 1 - my)]

        def slot(px, py, pc):
            return out_ref.at[4 * px + 2 * py + pc]

        def copy(k, block, to, src=None):
            return pltpu.make_async_remote_copy(
                src_ref=slot(*block) if src is None else src, dst_ref=slot(*block),
                send_sem=send_sems.at[k], recv_sem=recv_sems.at[k], device_id=to, device_id_type=MESH)

        mine = pltpu.make_async_copy(x_ref, slot(*me), local_sem.at[0])
        mine.start()
        first = [copy(0, me, sibling, src=x_ref)]
        first += [copy(1 + j, me, (*chip, mc), src=x_ref) for j, chip in enumerate(chips)]
        for cp in first:
            cp.start()
        passed = [copy(4 + j, (*chip, mc), sibling) for j, chip in enumerate(chips)]
        for j, chip in enumerate(chips):
            copy(1 + j, (*chip, mc), me).wait_recv()
            passed[j].start()
        copy(0, sibling, me).wait_recv()
        for j, chip in enumerate(chips):
            copy(4 + j, (*chip, 1 - mc), me).wait_recv()
        for cp in first + passed:
            cp.wait_send()
        mine.wait()

    return pl.pallas_call(
        body, name=name, out_shape=jax.ShapeDtypeStruct((N_DEV, r, c), x.dtype), in_specs=[ANY], out_specs=ANY,
        scratch_shapes=[pltpu.SemaphoreType.DMA((7,)), pltpu.SemaphoreType.DMA((7,)), pltpu.SemaphoreType.DMA((1,))],
    )(x)


def _dev_rows(ref, dev, a):
    return ref.at[:, pl.ds(pl.multiple_of(dev * a, 16), a), :]


def _all_gather_rows(shards, name):
    nt = len(shards)

    def body(*refs):
        x_refs, out_refs = refs[:nt], refs[nt:2 * nt]
        send_sems, recv_sems, local_sems = refs[2 * nt:]
        mx, my, mc = _position()
        me, sibling = (mx, my, mc), (mx, my, 1 - mc)
        chips = [(1 - mx, my), (mx, 1 - my), (1 - mx, 1 - my)]

        def slot(t, px, py, pc):
            return _dev_rows(out_refs[t], 4 * px + 2 * py + pc, shards[t].shape[1])

        def copy(t, k, block, to, src=None):
            return pltpu.make_async_remote_copy(
                src_ref=slot(t, *block) if src is None else src, dst_ref=slot(t, *block),
                send_sem=send_sems.at[7 * t + k], recv_sem=recv_sems.at[7 * t + k], device_id=to, device_id_type=MESH)

        mine = [pltpu.make_async_copy(x_refs[t], slot(t, *me), local_sems.at[t]) for t in range(nt)]
        first = [copy(t, 0, me, sibling, src=x_refs[t]) for t in range(nt)]
        first += [copy(t, 1 + j, me, (*chip, mc), src=x_refs[t]) for j, chip in enumerate(chips) for t in range(nt)]
        for cp in mine + first:
            cp.start()
        passed = []
        for j, chip in enumerate(chips):
            for t in range(nt):
                copy(t, 1 + j, (*chip, mc), me).wait_recv()
                fwd = copy(t, 4 + j, (*chip, mc), sibling)
                fwd.start()
                passed.append(fwd)
        for t in range(nt):
            copy(t, 0, sibling, me).wait_recv()
        for j, chip in enumerate(chips):
            for t in range(nt):
                copy(t, 4 + j, (*chip, 1 - mc), me).wait_recv()
        for cp in first + passed:
            cp.wait_send()
        for cp in mine:
            cp.wait()

    out_shape = [jax.ShapeDtypeStruct((x.shape[0], N_DEV * x.shape[1], x.shape[2]), x.dtype) for x in shards]
    return pl.pallas_call(
        body, name=name, out_shape=out_shape, in_specs=[ANY] * nt, out_specs=[ANY] * nt,
        scratch_shapes=[pltpu.SemaphoreType.DMA((7 * nt,)), pltpu.SemaphoreType.DMA((7 * nt,)), pltpu.SemaphoreType.DMA((nt,))],
    )(*shards)


def _rs_pair(gs, name):
    nt = len(gs)

    def body(*refs):
        g_refs, a_refs = refs[:nt], refs[nt:2 * nt]
        send_sems, recv_sems = refs[2 * nt:]
        mx, my, mc = _position()
        copies = [pltpu.make_async_remote_copy(
            src_ref=_dev_rows(g_refs[t], 2 * j + 1 - mc, gs[t].shape[1] // N_DEV), dst_ref=a_refs[t].at[j],
            send_sem=send_sems.at[4 * t + j], recv_sem=recv_sems.at[4 * t + j],
            device_id=(mx, my, 1 - mc), device_id_type=MESH) for t in range(nt) for j in range(4)]
        for cp in copies:
            cp.start()
        for cp in copies:
            cp.wait()

    out_shape = [jax.ShapeDtypeStruct((4, g.shape[0], g.shape[1] // N_DEV, g.shape[2]), g.dtype) for g in gs]
    return pl.pallas_call(
        body, name=name, out_shape=out_shape, in_specs=[ANY] * nt, out_specs=[ANY] * nt,
        scratch_shapes=[pltpu.SemaphoreType.DMA((4 * nt,)), pltpu.SemaphoreType.DMA((4 * nt,))],
    )(*gs)


def _rs_chips(ps, name):
    nt = len(ps)

    def body(*refs):
        p_refs, b_refs = refs[:nt], refs[nt:2 * nt]
        send_sems, recv_sems = refs[2 * nt:]
        mx, my, mc = _position()
        chips = [(1 - mx, my), (mx, 1 - my), (1 - mx, 1 - my)]
        copies = [pltpu.make_async_remote_copy(
            src_ref=p_refs[t].at[2 * cx + cy], dst_ref=b_refs[t].at[k],
            send_sem=send_sems.at[3 * t + k], recv_sem=recv_sems.at[3 * t + k],
            device_id=(cx, cy, mc), device_id_type=MESH) for t in range(nt) for k, (cx, cy) in enumerate(chips)]
        for cp in copies:
            cp.start()
        for cp in copies:
            cp.wait()

    out_shape = [jax.ShapeDtypeStruct((3,) + p.shape[1:], p.dtype) for p in ps]
    return pl.pallas_call(
        body, name=name, out_shape=out_shape, in_specs=[ANY] * nt, out_specs=[ANY] * nt,
        scratch_shapes=[pltpu.SemaphoreType.DMA((3 * nt,)), pltpu.SemaphoreType.DMA((3 * nt,))],
    )(*ps)


def _div_tile(a, b):
    best = 16
    for t in range(16, a + 1, 16):
        if a % t == 0 and t * b * 4 <= 2 * 1024 * 1024:
            best = t
    return best


def _rs_add_pair(g, a, core, name):
    l, a8, b = g.shape
    rows = a8 // N_DEV
    ta = _div_tile(rows, b)

    def body(core_ref, g_ref, a_ref, p_ref):
        p_ref[...] = (g_ref[...] + a_ref[...]).astype(BF16)

    grid_spec = pltpu.PrefetchScalarGridSpec(
        num_scalar_prefetch=1, grid=(4, l, rows // ta),
        in_specs=[pl.BlockSpec((1, 1, ta, b), lambda j, li, i, core_ref: (li, 2 * j + core_ref[0], i, 0)),
                  pl.BlockSpec((1, 1, ta, b), lambda j, li, i, core_ref: (j, li, i, 0))],
        out_specs=pl.BlockSpec((1, 1, ta, b), lambda j, li, i, core_ref: (j, li, i, 0)))
    return pl.pallas_call(body, name=name, grid_spec=grid_spec, out_shape=jax.ShapeDtypeStruct((4, l, rows, b), BF16),
                          compiler_params=_params(("parallel", "parallel", "parallel")))(core, g.reshape(l, N_DEV, rows, b), a)


def _rs_final(g, a, bsum, where, name):
    l, a8, b = g.shape
    rows = a8 // N_DEV
    ta = _div_tile(rows, b)

    def body(where_ref, g_ref, a_ref, b_ref, o_ref):
        own = g_ref[0, 0] + a_ref[0, 0]
        o_ref[0] = ((own + b_ref[0, 0].astype(F32)) + b_ref[1, 0].astype(F32)) + b_ref[2, 0].astype(F32)

    grid_spec = pltpu.PrefetchScalarGridSpec(
        num_scalar_prefetch=1, grid=(l, rows // ta),
        in_specs=[pl.BlockSpec((1, 1, ta, b), lambda li, i, w_ref: (li, w_ref[0], i, 0)),
                  pl.BlockSpec((1, 1, ta, b), lambda li, i, w_ref: (w_ref[1], li, i, 0)),
                  pl.BlockSpec((3, 1, ta, b), lambda li, i, w_ref: (0, li, i, 0))],
        out_specs=pl.BlockSpec((1, ta, b), lambda li, i, w_ref: (li, i, 0)))
    return pl.pallas_call(body, name=name, grid_spec=grid_spec, out_shape=jax.ShapeDtypeStruct((l, rows, b), F32),
                          compiler_params=_params(("parallel", "parallel")))(where, g.reshape(l, N_DEV, rows, b), a, bsum)


def _sum_devices(x, name):
    _, r, c = x.shape

    def body(x_ref, o_ref):
        acc = x_ref[0]
        for d in range(1, N_DEV):
            acc = acc + x_ref[d]
        o_ref[...] = acc

    return pl.pallas_call(body, name=name, out_shape=jax.ShapeDtypeStruct((r, c), F32))(x)


BIG = (("ab_w_in", "col"), ("ab_w_out", "row"), ("c_w_in", "col"), ("c_w_out", "row"),
       ("ffn_up", "col"), ("ffn_down", "row"), ("ple_gate", "row"), ("ple_proj", "col"))
KIND = dict(BIG)


def _row_block(shard, kind):
    return shard.transpose(0, 2, 1) if kind == "col" else shard


def _pad_rows(flat):
    pad = -flat.shape[0] % (8 * LANES)
    return jnp.pad(flat, (0, pad)).reshape(-1, LANES)


def _heads_out(x, b, s, nh, d):
    return x.reshape(b, s, nh, d).transpose(0, 2, 1, 3).reshape(b * nh, s, d)


def _heads_in(x, b, s, nh, d):
    return x.reshape(b, nh, s, d).transpose(0, 2, 1, 3).reshape(b * s, nh * d)


def _lower_bounds(logits):
    c = jnp.cumsum(jax.nn.softmax(logits.astype(F32), axis=0), axis=0)
    return c - c[0]


SB_COLS = tuple(k * SB_WIDTH // LANES for k in range(3))
HG_COLS = tuple((3 * SB_WIDTH + k * HG_W) // LANES for k in range(3))
HG_GATE_COL = (3 * SB_WIDTH + 3 * HG_W) // LANES
HG_OUT_COL = SB_WIDTH // LANES


def _forward_backward(x, p, target, W, P):
    b, s, dm = x.shape
    n = b * s
    h = x.reshape(n, dm)
    lbs, lb_vjp = jax.vjp(_lower_bounds, P["hg_lb_logits"])
    bias, bias_vjp = jax.vjp(_t5_bias, P["rel_bias"])
    saved = []
    gw = {name: lax.empty(W[name].shape, F32) for name, _ in BIG}

    def times_w(a, name, l, tag, res=None):
        return _mm(a, W[name], "nt" if KIND[name] == "col" else "nn", tag, res=res, layer=l)

    def times_wt(dy, name, l, tag):
        return _mm(dy, W[name], "nn" if KIND[name] == "col" else "nt", tag, layer=l)

    def grad_w(a, dy, name, l, tag):
        lhs, rhs = (dy, a) if KIND[name] == "col" else (a, dy)
        gw[name] = _mm(lhs, rhs, "tn", tag, into=(gw[name], l, 0))

    for i in range(DEPTH):
        j = i // 2
        r = {"h0": h}
        hn = _rms_fwd(h, P["mix_norm"][i], f"mix_norm_f{i}", BF16)
        r["hn"] = hn
        if i % 2 == 0:
            proj = times_w(hn, "ab_w_in", j, f"ab_in_f{i}")
            oa, lta = _sb_fwd(proj, SB_COLS, b, s, f"sb_f{i}")
            ob, st = _hg_fwd(proj, HG_COLS, lbs[j].reshape(HG_HEADS, HG_DK), b, s, f"hg_f{i}")
            obg = _gnorm_fwd(ob, proj, HG_GATE_COL, P["hg_out_norm"][j], f"hg_norm_f{i}")
            cat = jnp.concatenate([oa, obg], axis=1).astype(BF16)
            h = times_w(cat, "ab_w_out", j, f"ab_out_f{i}", res=h)
            r.update(lta=lta, proj=proj, ob=ob, st=st, cat=cat)
        else:
            proj = times_w(hn, "c_w_in", j, f"c_in_f{i}")
            nq = SW_HEADS * SW_DIM
            nkv = SW_KV_HEADS * SW_DIM
            q = _heads_out(proj[:, :nq], b, s, SW_HEADS, SW_DIM).reshape(-1, SW_DIM)
            k = _heads_out(proj[:, nq:nq + nkv], b, s, SW_KV_HEADS, SW_DIM).reshape(-1, SW_DIM)
            v = _heads_out(proj[:, nq + nkv:], b, s, SW_KV_HEADS, SW_DIM).reshape(b, SW_KV_HEADS, s, SW_DIM)
            qn = _rms_fwd(q, P["q_norm"][j], f"q_norm_f{i}").reshape(b, SW_KV_HEADS, SW_GROUP, s, SW_DIM)
            kn = _rms_fwd(k, P["k_norm"][j], f"k_norm_f{i}").reshape(b, SW_KV_HEADS, s, SW_DIM)
            o = _swa_fwd(qn, kn, v, P["sinks"][j], bias, f"swa_f{i}")
            o2 = _heads_in(o.reshape(b * SW_HEADS, s, SW_DIM), b, s, SW_HEADS, SW_DIM).astype(BF16)
            h = times_w(o2, "c_w_out", j, f"c_out_f{i}", res=h)
            r.update(q=q, k=k, v=v, qn=qn, kn=kn, o2=o2)
        r["h1"] = h
        hn2 = _rms_fwd(h, P["ffn_norm"][i], f"ffn_norm_f{i}", BF16)
        u = times_w(hn2, "ffn_up", i, f"ffn_up_f{i}").reshape(b, s, 2 * D_FF)
        act, act_t = _convglu_fwd(u, W["ffn_conv"][i], P["ffn_conv_b"][i], f"conv_f{i}", BF16)
        h = times_w(act.reshape(n, D_FF), "ffn_down", i, f"ffn_down_f{i}", res=h)
        r.update(hn2=hn2, u=u, act_t=act_t, h2=h)
        hn3 = _rms_fwd(h, P["ple_norm"][i], f"ple_norm_f{i}", BF16)
        z = times_w(hn3, "ple_gate", i, f"ple_gate_f{i}")
        pi = p[i].reshape(n, PLE_DIM)
        e = times_w(pi, "ple_proj", i, f"ple_proj_f{i}")
        h = _sigmul_fwd(z, e, h, f"ple_f{i}")
        r.update(hn3=hn3, z=z, e=e, pi=pi)
        saved.append(r)

    loss, dh = _loss_fwd(h, target.reshape(n, dm), "loss")

    gconv = [None] * DEPTH
    gp = {name: [None] * P[name].shape[0] for name in ("mix_norm", "hg_out_norm", "q_norm", "k_norm", "sinks",
                                                        "ffn_norm", "ffn_conv_b", "ple_norm")}
    dlbs = [None] * (DEPTH // 2)
    dbias = jnp.zeros_like(bias)

    for i in reversed(range(DEPTH)):
        j = i // 2
        r = saved[i]
        dz, de = _sigmul_bwd(r["z"], r["e"], dh, f"ple_b{i}")
        grad_w(r["pi"], de, "ple_proj", i, f"ple_proj_g{i}")
        grad_w(r["hn3"], dz, "ple_gate", i, f"ple_gate_g{i}")
        dhn3 = times_wt(dz, "ple_gate", i, f"ple_gate_b{i}")
        dh, gp["ple_norm"][i] = _rms_bwd(r["h2"], P["ple_norm"][i], dhn3, f"ple_norm_b{i}", res=dh)

        dact = times_wt(dh, "ffn_down", i, f"ffn_down_b{i}").reshape(b, s, D_FF)
        gw["ffn_down"] = _mm(r["act_t"], dh, "nn", f"ffn_down_g{i}", into=(gw["ffn_down"], i, 0))
        dug, duu, ag, au, dug_t, duu_t = _convglu_bwd(r["u"], W["ffn_conv"][i], P["ffn_conv_b"][i], dact, f"conv_b{i}")
        gconv[i] = jnp.concatenate([ag[:CONV_W], au[:CONV_W]], axis=-1)
        gp["ffn_conv_b"][i] = jnp.concatenate([ag[CONV_W], au[CONV_W]], axis=-1)
        dhn2 = None
        for half, (dpart, dpart_t) in enumerate(((dug.reshape(n, D_FF), dug_t), (duu.reshape(n, D_FF), duu_t))):
            gw["ffn_up"] = _mm(dpart_t, r["hn2"], "nn", f"ffn_up_g{i}_{half}", into=(gw["ffn_up"], i, half * D_FF))
            dhn2 = _mm(dpart, W["ffn_up"], "nn", f"ffn_up_b{i}_{half}", layer=i, b_rows=(half * D_FF, D_FF), res=dhn2)
        dh, gp["ffn_norm"][i] = _rms_bwd(r["h1"], P["ffn_norm"][i], dhn2, f"ffn_norm_b{i}", res=dh)

        if i % 2 == 0:
            dcat = times_wt(dh, "ab_w_out", j, f"ab_out_b{i}")
            grad_w(r["cat"], dh, "ab_w_out", j, f"ab_out_g{i}")
            dob, dgb, gp["hg_out_norm"][j] = _gnorm_bwd(r["ob"], r["proj"], HG_GATE_COL, P["hg_out_norm"][j], dcat, HG_OUT_COL,
                                                        f"hg_norm_b{i}")
            dqb, dfb, dib, dlb = _hg_bwd(r["proj"], HG_COLS, lbs[j].reshape(HG_HEADS, HG_DK), r["st"], dob, b, s, f"hg_b{i}")
            dlbs[j] = dlb.reshape(b, HG_W).sum(axis=0)
            dqa, dka, dva = _sb_bwd(r["proj"], SB_COLS, r["lta"], dcat, 0, b, s, f"sb_b{i}")
            dproj = jnp.concatenate([dqa, dka, dva, dqb, dfb, dib, dgb], axis=1)
            grad_w(r["hn"], dproj, "ab_w_in", j, f"ab_in_g{i}")
            dhn = times_wt(dproj, "ab_w_in", j, f"ab_in_b{i}")
        else:
            do2 = times_wt(dh, "c_w_out", j, f"c_out_b{i}")
            grad_w(r["o2"], dh, "c_w_out", j, f"c_out_g{i}")
            do = _heads_out(do2, b, s, SW_HEADS, SW_DIM).reshape(b, SW_KV_HEADS, SW_GROUP, s, SW_DIM)
            dqn, dkn, dv, dbias_i, dsink = _swa_bwd(r["qn"], r["kn"], r["v"], P["sinks"][j], bias, do, f"swa_b{i}")
            dbias = dbias + dbias_i
            gp["sinks"][j] = dsink[:, :, 0].reshape(SW_HEADS)
            dq, gp["q_norm"][j] = _rms_bwd(r["q"], P["q_norm"][j], dqn.reshape(-1, SW_DIM), f"q_norm_b{i}")
            dk, gp["k_norm"][j] = _rms_bwd(r["k"], P["k_norm"][j], dkn.reshape(-1, SW_DIM), f"k_norm_b{i}")
            dproj = jnp.concatenate([_heads_in(dq.reshape(b * SW_HEADS, s, SW_DIM), b, s, SW_HEADS, SW_DIM),
                                     _heads_in(dk.reshape(b * SW_KV_HEADS, s, SW_DIM), b, s, SW_KV_HEADS, SW_DIM),
                                     _heads_in(dv.reshape(b * SW_KV_HEADS, s, SW_DIM), b, s, SW_KV_HEADS, SW_DIM)], axis=1)
            grad_w(r["hn"], dproj, "c_w_in", j, f"c_in_g{i}")
            dhn = times_wt(dproj, "c_w_in", j, f"c_in_b{i}")
        dh, gp["mix_norm"][i] = _rms_bwd(r["h0"], P["mix_norm"][i], dhn, f"mix_norm_b{i}", res=dh)

    gp = {name: jnp.stack(v) for name, v in gp.items()}
    gp["hg_lb_logits"] = lb_vjp(jnp.stack(dlbs))[0]
    gp["rel_bias"] = bias_vjp(dbias)[0]
    return loss[0, 0], dh.reshape(b, s, dm), gw, jnp.stack(gconv), gp


WEIGHTS = ("mix_norm", "ab_w_in", "hg_lb_logits", "hg_out_norm", "ab_w_out", "c_w_in", "q_norm", "k_norm", "sinks", "rel_bias",
           "c_w_out", "ffn_norm", "ffn_up", "ffn_conv", "ffn_conv_b", "ffn_down", "ple_norm", "ple_gate", "ple_proj")
SMALL = ("mix_norm", "hg_lb_logits", "hg_out_norm", "q_norm", "k_norm", "sinks", "rel_bias", "ffn_norm", "ffn_conv_b", "ple_norm")


def _step(x, p, target, w, m, v):
    names = [name for name, _ in BIG]
    mx, my, mc = _position()
    dev = 4 * mx + 2 * my + mc

    blocks = [_row_block(w[name], KIND[name]).astype(BF16) for name in names]
    full = dict(zip(names, _all_gather_rows(blocks, "gather_weights")))
    nl, taps, cs = w["ffn_conv"].shape
    conv_all = _all_gather(_pad_rows(w["ffn_conv"].reshape(-1)), "gather_conv").reshape(N_DEV, -1)[:, :nl * taps * cs]
    full["ffn_conv"] = conv_all.reshape(N_DEV, nl, taps, cs).transpose(1, 2, 0, 3).reshape(nl, taps, N_DEV * cs)

    small = {name: w[name] for name in SMALL}
    loss, grad_x, gw, gconv, gp = _forward_backward(x, p, target, full, small)

    core = jnp.reshape(mc, (1,)).astype(jnp.int32)
    where = jnp.stack([dev, 2 * mx + my]).astype(jnp.int32)
    parts = [gw[name] for name in names]
    from_sibling = _rs_pair(parts, "reduce_pair")
    chip_sums = [_rs_add_pair(g, a, core, f"reduce_pair_add_{name}") for name, g, a in zip(names, parts, from_sibling)]
    from_chips = _rs_chips(chip_sums, "reduce_chips")
    grads = {name: _row_block(_rs_final(g, a, bs, where, f"reduce_final_{name}"), KIND[name])
             for name, g, a, bs in zip(names, parts, from_sibling, from_chips)}

    flat_small = jnp.concatenate([gp[name].reshape(-1) for name in SMALL] + [gconv.reshape(-1), loss.reshape(1)])
    small_sum = _sum_devices(_all_gather(_pad_rows(flat_small), "gather_small"), "sum_small").reshape(-1)
    e0 = 0
    for name in SMALL:
        cnt = math.prod(w[name].shape)
        grads[name] = small_sum[e0:e0 + cnt].reshape(w[name].shape)
        e0 += cnt
    gconv_sum = small_sum[e0:e0 + gconv.size].reshape(gconv.shape)
    grads["ffn_conv"] = lax.dynamic_slice_in_dim(gconv_sum, dev * cs, cs, axis=2)
    loss = small_sum[e0 + gconv.size]

    deltas, new_m, new_v = {}, {}, {}
    for name in WEIGHTS:
        shape = w[name].shape
        view = (-1, shape[-1]) if len(shape) > 1 else (1, -1)
        d_, m_, v_ = _adamw(w[name].reshape(view), grads[name].reshape(view), m[name].reshape(view), v[name].reshape(view), f"adamw_{name}")
        deltas[name], new_m[name], new_v[name] = d_.reshape(shape), m_.reshape(shape), v_.reshape(shape)
    return (loss, grad_x, *[grads[k] for k in WEIGHTS], *[deltas[k] for k in WEIGHTS],
            *[new_m[k] for k in WEIGHTS], *[new_v[k] for k in WEIGHTS])


def kernel(x, p, mix_norm, ab_w_in, hg_lb_logits, hg_out_norm, ab_w_out, c_w_in, q_norm, k_norm, sinks, rel_bias, c_w_out, ffn_norm, ffn_up, ffn_conv, ffn_conv_b, ffn_down, ple_norm, ple_gate, ple_proj, loss_target, m_mix_norm, m_ab_w_in, m_hg_lb_logits, m_hg_out_norm, m_ab_w_out, m_c_w_in, m_q_norm, m_k_norm, m_sinks, m_rel_bias, m_c_w_out, m_ffn_norm, m_ffn_up, m_ffn_conv, m_ffn_conv_b, m_ffn_down, m_ple_norm, m_ple_gate, m_ple_proj, v_mix_norm, v_ab_w_in, v_hg_lb_logits, v_hg_out_norm, v_ab_w_out, v_c_w_in, v_q_norm, v_k_norm, v_sinks, v_rel_bias, v_c_w_out, v_ffn_norm, v_ffn_up, v_ffn_conv, v_ffn_conv_b, v_ffn_down, v_ple_norm, v_ple_gate, v_ple_proj):
    w = dict(zip(WEIGHTS, (mix_norm, ab_w_in, hg_lb_logits, hg_out_norm, ab_w_out, c_w_in, q_norm, k_norm, sinks, rel_bias, c_w_out,
                           ffn_norm, ffn_up, ffn_conv, ffn_conv_b, ffn_down, ple_norm, ple_gate, ple_proj)))
    m = dict(zip(WEIGHTS, (m_mix_norm, m_ab_w_in, m_hg_lb_logits, m_hg_out_norm, m_ab_w_out, m_c_w_in, m_q_norm, m_k_norm, m_sinks,
                           m_rel_bias, m_c_w_out, m_ffn_norm, m_ffn_up, m_ffn_conv, m_ffn_conv_b, m_ffn_down, m_ple_norm, m_ple_gate,
                           m_ple_proj)))
    v = dict(zip(WEIGHTS, (v_mix_norm, v_ab_w_in, v_hg_lb_logits, v_hg_out_norm, v_ab_w_out, v_c_w_in, v_q_norm, v_k_norm, v_sinks,
                           v_rel_bias, v_c_w_out, v_ffn_norm, v_ffn_up, v_ffn_conv, v_ffn_conv_b, v_ffn_down, v_ple_norm, v_ple_gate,
                           v_ple_proj)))
    return _step(x, p, loss_target, w, m, v)
```

```python
import functools
import math

import numpy as np
import jax
import jax.numpy as jnp
from jax import lax
from jax.experimental import pallas as pl
from jax.experimental.pallas import tpu as pltpu

F32 = jnp.float32
BF16 = jnp.bfloat16

D_MODEL = 1024
DEPTH = 4
PLE_DIM = 256
EPS = 1e-6
SB_HEADS, SB_DIM = 8, 64
SB_WIDTH = SB_HEADS * SB_DIM
HG_HEADS, HG_DK, HG_DV = 4, 128, 128
HG_W = HG_HEADS * HG_DK
AB_IN = 3 * SB_WIDTH + 4 * HG_W
SW_HEADS, SW_KV_HEADS, SW_DIM = 16, 4, 64
SW_GROUP = SW_HEADS // SW_KV_HEADS
WINDOW = 128
C_IN = (SW_HEADS + 2 * SW_KV_HEADS) * SW_DIM
N_BUCKETS, MAX_DISTANCE = 32, 128
D_FF = 2816
N_DEV = 8

ADAM_LR, ADAM_B1, ADAM_B2, ADAM_EPS, ADAM_WD, ADAM_STEP = 0.001, 0.9, 0.999, 1e-08, 0.01, 10

LANES = 128
VMEM_LIMIT = 48 * 1024 * 1024

NN = (((1,), (0,)), ((), ()))
NT = (((1,), (1,)), ((), ()))
TN = (((0,), (0,)), ((), ()))


MXU_DTYPE = BF16


def _bf(x):
    return x.astype(MXU_DTYPE)


def _dot(a, b, dims=NN):
    return lax.dot_general(_bf(a), _bf(b), dims, preferred_element_type=F32)


def _split3(x):
    x1 = _bf(x)
    r = x - x1.astype(F32)
    x2 = _bf(r)
    x3 = _bf(r - x2.astype(F32))
    return x1, x2, x3


def _dot_exact_lhs01(m, x, terms=3):
    parts = _split3(x)[:terms]
    out = lax.dot_general(m, parts[0], NN, preferred_element_type=F32)
    for p_ in parts[1:]:
        out = out + lax.dot_general(m, p_, NN, preferred_element_type=F32)
    return out


def _dot_exact_rhs01(x, m, terms=2):
    parts = _split3(x)[:terms]
    out = lax.dot_general(parts[0], m, NN, preferred_element_type=F32)
    for p_ in parts[1:]:
        out = out + lax.dot_general(p_, m, NN, preferred_element_type=F32)
    return out


def _pick(n, target):
    best = None
    for t in range(LANES, target + 1, LANES):
        if n % t == 0:
            best = t
    return best or n


def _params(sem=None):
    return pltpu.CompilerParams(dimension_semantics=sem, vmem_limit_bytes=VMEM_LIMIT)


def _mm(a, b, mode, name, res=None, out_dtype=F32, layer=None, b_rows=None, into=None):
    bshape = b.shape if layer is None else b.shape[1:]
    if b_rows is not None:
        assert mode == "nn"
        bshape = (b_rows[1], bshape[1])
    if mode == "nn":
        (M, K), (K2, N) = a.shape, bshape
    elif mode == "nt":
        (M, K), (N, K2) = a.shape, bshape
    else:
        (K, M), (K2, N) = a.shape, bshape
    assert K == K2, (a.shape, b.shape, mode)
    tm, tn, tk = _pick(M, 1408 if mode == "tn" else 1024), _pick(N, 1408), _pick(K, 1536)
    nk = K // tk
    k_off = 0 if b_rows is None else b_rows[0] // tk
    assert b_rows is None or b_rows[0] % tk == 0
    dims = {"nn": NN, "nt": NT, "tn": TN}[mode]
    a_spec = pl.BlockSpec((tk, tm), lambda i, j, k: (k, i)) if mode == "tn" else pl.BlockSpec((tm, tk), lambda i, j, k: (i, k))
    if layer is None:
        b_spec = pl.BlockSpec((tn, tk), lambda i, j, k: (j, k)) if mode == "nt" else pl.BlockSpec((tk, tn), lambda i, j, k: (k, j))
    elif mode == "nt":
        b_spec = pl.BlockSpec((None, tn, tk), lambda i, j, k: (layer, j, k))
    else:
        b_spec = pl.BlockSpec((None, tk, tn), lambda i, j, k: (layer, k + k_off, j))
    o_spec = pl.BlockSpec((tm, tn), lambda i, j, k: (i, j))
    has_res = res is not None

    def finish(acc, r_ref, o_ref):
        if has_res:
            acc = acc + r_ref[...]
        o_ref[...] = acc.astype(out_dtype)

    def body(*refs):
        a_ref, b_ref = refs[0], refs[1]
        r_ref = refs[2] if has_res else None
        o_ref = refs[-1] if nk == 1 else refs[-2]
        part = _dot(a_ref[...], b_ref[...], dims)
        if nk == 1:
            finish(part, r_ref, o_ref)
            return
        acc_ref = refs[-1]
        k = pl.program_id(2)

        @pl.when(k == 0)
        def _():
            acc_ref[...] = part

        @pl.when((k > 0) & (k < nk - 1))
        def _():
            acc_ref[...] += part

        @pl.when(k == nk - 1)
        def _():
            finish(acc_ref[...] + part, r_ref, o_ref)

    in_specs = [a_spec, b_spec] + ([o_spec] if has_res else [])
    args = (a, b) + ((res,) if has_res else ())
    out_shape, aliases = jax.ShapeDtypeStruct((M, N), out_dtype), {}
    if into is not None:
        stack, slot, row = into
        assert stack.shape[2] == N and row % tm == 0 and row + M <= stack.shape[1] and stack.dtype == out_dtype and not has_res
        in_specs = in_specs + [pl.BlockSpec(memory_space=pl.ANY)]
        args = args + (stack,)
        o_spec = pl.BlockSpec((None, tm, tn), lambda i, j, k: (slot, i + row // tm, j))
        out_shape, aliases = jax.ShapeDtypeStruct(stack.shape, out_dtype), {2: 0}

    def body_into(a_ref, b_ref, stack_ref, *rest):
        body(a_ref, b_ref, *rest)

    return pl.pallas_call(
        body if into is None else body_into, name=name, grid=(M // tm, N // tn, nk), in_specs=in_specs, out_specs=o_spec,
        out_shape=out_shape, scratch_shapes=[] if nk == 1 else [pltpu.VMEM((tm, tn), F32)], input_output_aliases=aliases,
        compiler_params=_params(("parallel", "parallel", "arbitrary")),
    )(*args)


def _row_tile(n, d):
    if n % 8:
        return n
    t = 8
    while t * 2 <= min(n, (256 * 1024) // d) and n % (t * 2) == 0:
        t *= 2
    return t


def _rms_fwd(x, g, name, out_dtype=F32):
    n, d = x.shape
    tm = _row_tile(n, d)

    def body(x_ref, g_ref, o_ref):
        xf = x_ref[...]
        r = lax.rsqrt(jnp.mean(xf * xf, axis=-1, keepdims=True) + EPS)
        o_ref[...] = (xf * r * g_ref[...]).astype(out_dtype)

    return pl.pallas_call(
        body, name=name, grid=(n // tm,),
        in_specs=[pl.BlockSpec((tm, d), lambda i: (i, 0)), pl.BlockSpec((1, d), lambda i: (0, 0))],
        out_specs=pl.BlockSpec((tm, d), lambda i: (i, 0)),
        out_shape=jax.ShapeDtypeStruct((n, d), out_dtype), compiler_params=_params(("parallel",)),
    )(x, g.reshape(1, d))


def _rms_bwd(x, g, dy, name, res=None):
    n, d = x.shape
    tm = _row_tile(n, d)
    has_res = res is not None

    def body(*refs):
        x_ref, g_ref, dy_ref = refs[:3]
        r_ref = refs[3] if has_res else None
        dx_ref, dg_ref = refs[-2:]
        xf = x_ref[...]
        r = lax.rsqrt(jnp.mean(xf * xf, axis=-1, keepdims=True) + EPS)
        xh = xf * r
        dyf = dy_ref[...].astype(F32)
        dxh = dyf * g_ref[...]
        dx = r * (dxh - xh * jnp.mean(dxh * xh, axis=-1, keepdims=True))
        if has_res:
            dx = dx + r_ref[...]
        dx_ref[...] = dx

        @pl.when(pl.program_id(0) == 0)
        def _():
            dg_ref[...] = jnp.zeros_like(dg_ref)

        dg_ref[...] += jnp.sum(dyf * xh, axis=0, keepdims=True)

    row = pl.BlockSpec((tm, d), lambda i: (i, 0))
    vec = pl.BlockSpec((1, d), lambda i: (0, 0))
    dx, dg = pl.pallas_call(
        body, name=name, grid=(n // tm,),
        in_specs=[row, vec, row] + ([row] if has_res else []),
        out_specs=[row, vec],
        out_shape=[jax.ShapeDtypeStruct((n, d), F32), jax.ShapeDtypeStruct((1, d), F32)],
        compiler_params=_params(("arbitrary",)),
    )(x, g.reshape(1, d), dy, *((res,) if has_res else ()))
    return dx, dg.reshape(d)


def _silu(x):
    return x * jax.nn.sigmoid(x)


def _gnorm_fwd(o, gate, gate_col, w, name):
    n, width = o.shape
    d = w.shape[0]
    tm = _row_tile(n, d)

    def body(o_ref, g_ref, w_ref, y_ref):
        of = o_ref[...]
        r = lax.rsqrt(jnp.mean(of * of, axis=-1, keepdims=True) + EPS)
        y_ref[...] = of * r * w_ref[...] * _silu(g_ref[...])

    row = pl.BlockSpec((tm, d), lambda i, h: (i, h))
    vec = pl.BlockSpec((1, d), lambda i, h: (0, 0))
    return pl.pallas_call(body, name=name, grid=(n // tm, width // d),
                          in_specs=[row, pl.BlockSpec((tm, d), lambda i, h: (i, gate_col + h)), vec], out_specs=row,
                          out_shape=jax.ShapeDtypeStruct((n, width), F32), compiler_params=_params(("parallel", "parallel")))(o, gate, w.reshape(1, d))


def _gnorm_bwd(o, gate, gate_col, w, dy, dy_col, name):
    n, width = o.shape
    d = w.shape[0]
    tm = _row_tile(n, d)

    def body(o_ref, g_ref, w_ref, dy_ref, do_ref, dgate_ref, dw_ref):
        of, gf, dyf = o_ref[...], g_ref[...], dy_ref[...]
        r = lax.rsqrt(jnp.mean(of * of, axis=-1, keepdims=True) + EPS)
        xh = of * r
        sg = jax.nn.sigmoid(gf)
        sil = gf * sg
        dnorm = dyf * sil
        dgate_ref[...] = dyf * xh * w_ref[...] * (sg * (1.0 + gf * (1.0 - sg)))
        dxh = dnorm * w_ref[...]
        do_ref[...] = r * (dxh - xh * jnp.mean(dxh * xh, axis=-1, keepdims=True))

        @pl.when((pl.program_id(0) == 0) & (pl.program_id(1) == 0))
        def _():
            dw_ref[...] = jnp.zeros_like(dw_ref)

        dw_ref[...] += jnp.sum(dnorm * xh, axis=0, keepdims=True)

    row = pl.BlockSpec((tm, d), lambda i, h: (i, h))
    vec = pl.BlockSpec((1, d), lambda i, h: (0, 0))
    do, dgate, dw = pl.pallas_call(
        body, name=name, grid=(n // tm, width // d),
        in_specs=[row, pl.BlockSpec((tm, d), lambda i, h: (i, gate_col + h)), vec, pl.BlockSpec((tm, d), lambda i, h: (i, dy_col + h))],
        out_specs=[row, row, vec],
        out_shape=[jax.ShapeDtypeStruct((n, width), F32)] * 2 + [jax.ShapeDtypeStruct((1, d), F32)],
        compiler_params=_params(("arbitrary", "arbitrary")),
    )(o, gate, w.reshape(1, d), dy)
    return do, dgate, dw.reshape(d)


def _sigmul_fwd(z, e, res, name):
    n, d = z.shape
    tm = _row_tile(n, d)

    def body(z_ref, e_ref, r_ref, o_ref):
        o_ref[...] = r_ref[...] + jax.nn.sigmoid(z_ref[...]) * e_ref[...]

    row = pl.BlockSpec((tm, d), lambda i: (i, 0))
    return pl.pallas_call(body, name=name, grid=(n // tm,), in_specs=[row] * 3, out_specs=row,
                          out_shape=jax.ShapeDtypeStruct((n, d), F32), compiler_params=_params(("parallel",)))(z, e, res)


def _sigmul_bwd(z, e, dy, name):
    n, d = z.shape
    tm = _row_tile(n, d)

    def body(z_ref, e_ref, dy_ref, dz_ref, de_ref):
        s = jax.nn.sigmoid(z_ref[...])
        dyf = dy_ref[...]
        dz_ref[...] = dyf * e_ref[...] * s * (1.0 - s)
        de_ref[...] = dyf * s

    row = pl.BlockSpec((tm, d), lambda i: (i, 0))
    return pl.pallas_call(body, name=name, grid=(n // tm,), in_specs=[row] * 3, out_specs=[row] * 2,
                          out_shape=[jax.ShapeDtypeStruct((n, d), F32)] * 2, compiler_params=_params(("parallel",)))(z, e, dy)


def _loss_fwd(y, target, name):
    n, d = y.shape
    tm = _row_tile(n, d)

    def body(y_ref, t_ref, l_ref, dy_ref):
        diff = y_ref[...] - t_ref[...]
        dy_ref[...] = diff * (1.0 / d)

        @pl.when(pl.program_id(0) == 0)
        def _():
            l_ref[...] = jnp.zeros_like(l_ref)

        part = jnp.sum(jnp.mean(diff * diff, axis=-1, keepdims=True), axis=0, keepdims=True)
        l_ref[...] += 0.5 * jnp.broadcast_to(part, l_ref.shape)

    row = pl.BlockSpec((tm, d), lambda i: (i, 0))
    vec = pl.BlockSpec((1, LANES), lambda i: (0, 0))
    return pl.pallas_call(body, name=name, grid=(n // tm,), in_specs=[row, row], out_specs=[vec, row],
                          out_shape=[jax.ShapeDtypeStruct((1, LANES), F32), jax.ShapeDtypeStruct((n, d), F32)],
                          compiler_params=_params(("arbitrary",)))(y, target)


SB_BLK = 128
SB_QBLK = 256


def _sb_logits(z, qi, kj, row, col):
    mask = (kj * SB_BLK + col) < (qi * SB_QBLK + row)
    sp = jnp.maximum(z, 0.0) + jnp.log1p(jnp.exp(-jnp.abs(z)))
    lk = jnp.where(mask, -sp, 0.0)
    return mask, lk, z - sp


SB_PAIRS = SB_WIDTH // LANES


def _sb_iotas():
    row = lax.broadcasted_iota(jnp.int32, (2 * SB_QBLK, SB_BLK), 0)
    row = jnp.where(row >= SB_QBLK, row - SB_QBLK, row)
    col = lax.broadcasted_iota(jnp.int32, (2 * SB_QBLK, SB_BLK), 1)
    return row, col, col[:SB_QBLK] < SB_DIM


def _sb_stack(x, first):
    return jnp.concatenate([jnp.where(first, x, 0.0), jnp.where(first, 0.0, x)], axis=0)


def _sb_unstack(y, first):
    return jnp.where(first, y[:SB_QBLK], y[SB_QBLK:])


def _sb_running(x, u):
    m = x.shape[0]
    hi = _bf(x)
    lo = _bf(x - hi.astype(F32))
    c = lax.dot_general(jnp.concatenate([hi, lo], axis=0), u, NN, preferred_element_type=F32)
    return c[:m] + c[m:]


def _sb_spec(s, col):
    return pl.BlockSpec((s, LANES), lambda e, pr: (e, col + pr))


def _sb_fwd(proj, cols, b, s, name):
    nq = s // SB_QBLK
    scale = SB_DIM ** -0.5

    def body(q_ref, k_ref, v_ref, o_ref, lt_ref):
        row, col, first = _sb_iotas()
        u_after = _bf(row[:SB_BLK] > col[:SB_BLK])

        def qloop(qi, _):
            q0 = pl.multiple_of(qi * SB_QBLK, SB_QBLK)
            q2 = _sb_stack(q_ref[pl.ds(q0, SB_QBLK), :], first)
            nkeys = (qi + 1) * (SB_QBLK // SB_BLK)

            def logits(kj):
                k0 = pl.multiple_of(kj * SB_BLK, SB_BLK)
                return _dot(q2, k_ref[pl.ds(k0, SB_BLK), :], NT) * scale

            def kloop(j, st):
                acc, carry, z = st
                kj = nkeys - 1 - j
                k0 = pl.multiple_of(kj * SB_BLK, SB_BLK)
                z_next = logits(jnp.maximum(kj - 1, 0))
                mask, lk, ls = _sb_logits(z, qi, kj, row, col)
                later = carry + _sb_running(lk, u_after)
                w = jnp.where(mask, jnp.exp(ls + later), 0.0)
                acc = acc + _sb_unstack(_dot(w, v_ref[pl.ds(k0, SB_BLK), :]), first)
                return acc, carry + jnp.sum(lk, axis=1, keepdims=True), z_next

            acc, carry, _ = lax.fori_loop(0, nkeys, kloop, (jnp.zeros((SB_QBLK, LANES), F32), jnp.zeros((2 * SB_QBLK, 1), F32), logits(nkeys - 1)))
            o_ref[pl.ds(q0, SB_QBLK), :] = acc
            lt_ref[pl.ds(q0, SB_QBLK), :] = _sb_unstack(jnp.broadcast_to(carry, (2 * SB_QBLK, LANES)), first)
            return 0

        lax.fori_loop(0, nq, qloop, 0)

    out = _sb_spec(s, 0)
    return pl.pallas_call(body, name=name, grid=(b, SB_PAIRS), in_specs=[_sb_spec(s, c) for c in cols], out_specs=[out, out],
                          out_shape=[jax.ShapeDtypeStruct((b * s, SB_WIDTH), F32)] * 2,
                          compiler_params=_params(("parallel", "parallel")))(proj, proj, proj)


def _sb_bwd(proj, cols, ltot, do, do_col, b, s, name):
    nq = s // SB_QBLK
    scale = SB_DIM ** -0.5

    def body(q_ref, k_ref, v_ref, lt_ref, do_ref, dq_ref, dk_ref, dv_ref):
        row, col, first = _sb_iotas()
        u_upto = _bf(row[:SB_BLK] <= col[:SB_BLK])
        u_before = _bf(row[:SB_BLK] < col[:SB_BLK])
        dk_ref[...] = jnp.zeros_like(dk_ref)
        dv_ref[...] = jnp.zeros_like(dv_ref)

        def qloop(qi, _):
            q0 = pl.multiple_of(qi * SB_QBLK, SB_QBLK)
            q2 = _sb_stack(q_ref[pl.ds(q0, SB_QBLK), :], first)
            nkeys = (qi + 1) * (SB_QBLK // SB_BLK)
            do2 = _sb_stack(do_ref[pl.ds(q0, SB_QBLK), :], first)
            lt2 = jnp.min(_sb_stack(lt_ref[pl.ds(q0, SB_QBLK), :], first), axis=1, keepdims=True)

            def logits(kj):
                k0 = pl.multiple_of(kj * SB_BLK, SB_BLK)
                return _dot(q2, k_ref[pl.ds(k0, SB_BLK), :], NT) * scale

            def kloop(kj, st):
                dq, cl, cg, z = st
                k0 = pl.multiple_of(kj * SB_BLK, SB_BLK)
                kb = k_ref[pl.ds(k0, SB_BLK), :]
                vb = v_ref[pl.ds(k0, SB_BLK), :]
                z_next = logits(jnp.minimum(kj + 1, nkeys - 1))
                mask, lk, ls = _sb_logits(z, qi, kj, row, col)
                later = lt2 - (cl + _sb_running(lk, u_upto))
                w = jnp.where(mask, jnp.exp(ls + later), 0.0)
                g = _dot(do2, vb, NT) * w
                dv_ref[pl.ds(k0, SB_BLK), :] += _dot(w, do2, TN)
                g_before = cg + _sb_running(g, u_before)
                sig = jnp.exp(ls)
                dz = jnp.where(mask, g * (1.0 - sig) - sig * g_before, 0.0) * scale
                dq = dq + _sb_unstack(_dot(dz, kb), first)
                dk_ref[pl.ds(k0, SB_BLK), :] += _dot(dz, q2, TN)
                return dq, cl + jnp.sum(lk, axis=1, keepdims=True), cg + jnp.sum(g, axis=1, keepdims=True), z_next

            z1 = jnp.zeros((2 * SB_QBLK, 1), F32)
            dq = lax.fori_loop(0, nkeys, kloop, (jnp.zeros((SB_QBLK, LANES), F32), z1, z1, logits(0)))[0]
            dq_ref[pl.ds(q0, SB_QBLK), :] = dq
            return 0

        lax.fori_loop(0, nq, qloop, 0)

    out = _sb_spec(s, 0)
    return pl.pallas_call(body, name=name, grid=(b, SB_PAIRS),
                          in_specs=[_sb_spec(s, c) for c in cols] + [out, _sb_spec(s, do_col)], out_specs=[out] * 3,
                          out_shape=[jax.ShapeDtypeStruct((b * s, SB_WIDTH), F32)] * 3,
                          compiler_params=_params(("parallel", "parallel")))(proj, proj, proj, ltot, do)


HG_CHUNK = 64
HG_GROUP = 2


def _hg_consts(c, r):
    levels = int(math.log2(c))
    t = np.arange(r)
    same = (t[:, None] // c) == (t[None, :] // c)
    tri = ((t[:, None] >= t[None, :]) & same).astype(np.float32)
    psel = np.zeros((levels, r, r), np.float32)
    masks = np.zeros((levels + 1, r, r), np.float32)
    for l in range(levels):
        n = c >> (l + 1)
        blk = t // (2 * n)
        psel[l, t, blk * 2 * n + n - 1] = 1.0
        upper = (t % (2 * n)) >= n
        masks[l] = (blk[:, None] == blk[None, :]) & upper[:, None] & (~upper)[None, :]
    masks[levels] = np.eye(r)
    psel = psel.reshape(levels * r, r)
    return levels, jnp.asarray(tri), jnp.asarray(psel), jnp.asarray(masks), jnp.asarray(tri.T.copy()), jnp.asarray(psel.T.copy())


def _hg_elem(qv, fv, lbv):
    sig = jax.nn.sigmoid(fv)
    lf = jnp.log(lbv + (1.0 - lbv) * sig)
    kk = (1.0 - lbv) * jax.nn.sigmoid(-fv)
    qf = qv * jax.nn.sigmoid(qv)
    return qf, kk, lf


def _col_bcast(rowvec):
    n = rowvec.shape[1]
    return jnp.transpose(jnp.broadcast_to(rowvec, (n, n)))


def _hg_within(qf, kk, lf, tri, psel, m_ref, c, levels):
    r = qf.shape[0]
    b = _dot_exact_lhs01(tri, lf)
    bls = [b[(g + 1) * c - 1:(g + 1) * c, :] for g in range(r // c)]
    blb = jnp.concatenate([jnp.broadcast_to(bl, (c, bl.shape[1])) for bl in bls], axis=0)
    eb = jnp.exp(b)
    qi = qf * eb
    bsel = _dot_exact_lhs01(psel, b)
    scores = jnp.where(m_ref[levels] > 0, _dot(qf, kk, NT), 0.0)
    lev = []
    for l in range(levels):
        bs = bsel[l * r:(l + 1) * r]
        eq = jnp.exp(jnp.minimum(b - bs, 0.0))
        ek = jnp.exp(jnp.minimum(bs - b, 0.0))
        ql, kl = qf * eq, kk * ek
        scores = scores + jnp.where(m_ref[l] > 0, _dot(ql, kl, NT), 0.0)
        lev.append((eq, ek, ql, kl))
    ebl = jnp.exp(blb - b)
    kd = kk * ebl
    decays = [_col_bcast(jnp.exp(bl)) for bl in bls]
    return eb, qi, scores, lev, ebl, kd, decays


def _hg_fwd(proj, cols, lb, b, s, name):
    nh, d = lb.shape
    bh = b * nh
    c = HG_CHUNK
    nc = s // c
    grp = math.gcd(HG_GROUP, nc)
    r = grp * c
    levels, tri, psel, masks, _, _ = _hg_consts(c, r)

    def body(q_ref, f_ref, i_ref, lb_ref, tri_ref, psel_ref, m_ref, o_ref, st_ref):
        lbv = jnp.broadcast_to(lb_ref[0], (r, d))
        tri_v, psel_v = _bf(tri_ref[...]), _bf(psel_ref[...])

        def group(gi, state):
            r0 = pl.multiple_of(gi * r, r)
            qf, kk, lf = _hg_elem(q_ref[pl.ds(r0, r), :], f_ref[pl.ds(r0, r), :], lbv)
            iv = i_ref[pl.ds(r0, r), :]
            _, qi, scores, _, _, kd, decays = _hg_within(qf, kk, lf, tri_v, psel_v, m_ref, c, levels)
            within = _dot(scores, iv)
            for g in range(grp):
                rows = slice(g * c, (g + 1) * c)
                st_ref[0, gi * grp + g] = state
                o_ref[pl.ds(r0 + g * c, c), :] = _dot(qi[rows], state) + within[rows]
                state = decays[g] * state + _dot(kd[rows], iv[rows], TN)
            return state

        lax.fori_loop(0, nc // grp, group, jnp.zeros((d, d), F32))

    full = lambda a: pl.BlockSpec(a.shape, lambda e, hd: (0,) * a.ndim)
    return pl.pallas_call(
        body, name=name, grid=(b, nh),
        in_specs=[_hg_seq(s, d, cols[0]), _hg_seq(s, d, cols[1]), _hg_seq(s, d, cols[2]),
                  pl.BlockSpec((1, 1, d), lambda e, hd: (hd, 0, 0)), full(tri), full(psel), full(masks)],
        out_specs=[_hg_seq(s, d, 0), pl.BlockSpec((1, nc, d, d), lambda e, hd: (e * nh + hd, 0, 0, 0))],
        out_shape=[jax.ShapeDtypeStruct((b * s, nh * d), F32), jax.ShapeDtypeStruct((bh, nc, d, d), F32)],
        compiler_params=_params(("parallel", "parallel")),
    )(proj, proj, proj, lb.reshape(nh, 1, d), tri, psel, masks)


def _hg_seq(s, d, col):
    return pl.BlockSpec((s, d), lambda e, hd: (e, col + hd))


def _hg_bwd(proj, cols, lb, states, do, b, s, name):
    nh, d = lb.shape
    bh = b * nh
    c = HG_CHUNK
    nc = s // c
    grp = math.gcd(HG_GROUP, nc)
    r = grp * c
    levels, tri, psel, masks, tri_t, psel_t = _hg_consts(c, r)

    def body(q_ref, f_ref, i_ref, lb_ref, st_ref, do_ref, tri_ref, psel_ref, m_ref, trit_ref, pselt_ref,
             dq_ref, df_ref, di_ref, dlb_ref):
        lbv = jnp.broadcast_to(lb_ref[0], (r, d))
        tri_v, psel_v = _bf(tri_ref[...]), _bf(psel_ref[...])
        trit_v, pselt_v = _bf(trit_ref[...]), _bf(pselt_ref[...])
        row_in_chunk = lax.broadcasted_iota(jnp.int32, (c, d), 0)

        def chunk(step, carry):
            ds_out, dlb = carry
            gi = nc // grp - 1 - step
            r0 = pl.multiple_of(gi * r, r)
            qv, fv, iv = q_ref[pl.ds(r0, r), :], f_ref[pl.ds(r0, r), :], i_ref[pl.ds(r0, r), :]
            dov = do_ref[pl.ds(r0, r), :]
            (qf, kk, lf), elem_vjp = jax.vjp(_hg_elem, qv, fv, lbv)
            eb, qi, scores, lev, ebl, kd, decays = _hg_within(qf, kk, lf, tri_v, psel_v, m_ref, c, levels)

            dscores = _dot(dov, iv, NT)
            di_within = _dot(scores, dov, TN)
            dqi_parts, dkd_parts, dbl_parts = [None] * grp, [None] * grp, [None] * grp
            for g in reversed(range(grp)):
                rows = slice(g * c, (g + 1) * c)
                state = st_ref[0, gi * grp + g]
                di_ref[pl.ds(r0 + g * c, c), :] = di_within[rows] + _dot(kd[rows], ds_out)
                dqi_parts[g] = _dot(dov[rows], state, NT)
                dkd_parts[g] = _dot(iv[rows], ds_out, NT)
                dbl = (jnp.sum(dkd_parts[g] * kd[rows], axis=0, keepdims=True)
                       + _col_bcast_t(jnp.sum(ds_out * decays[g] * state, axis=1, keepdims=True)))
                dbl_parts[g] = jnp.where(row_in_chunk == c - 1, dbl, 0.0)
                ds_out = decays[g] * ds_out + _dot(qi[rows], dov[rows], TN)
            ds_in = ds_out
            dqi = jnp.concatenate(dqi_parts, axis=0)
            dkd = jnp.concatenate(dkd_parts, axis=0)
            dqf = dqi * eb
            dkk = dkd * ebl
            db = dqi * qi - dkd * kd + jnp.concatenate(dbl_parts, axis=0)
            dsd = jnp.where(m_ref[levels] > 0, dscores, 0.0)
            dqf = dqf + _dot(dsd, kk)
            dkk = dkk + _dot(dsd, qf, TN)
            dbsel = []
            for l in range(levels):
                eq, ek, ql, kl = lev[l]
                dsl = jnp.where(m_ref[l] > 0, dscores, 0.0)
                dql = _dot(dsl, kl)
                dkl = _dot(dsl, ql, TN)
                dqf = dqf + dql * eq
                dkk = dkk + dkl * ek
                diff = dql * ql - dkl * kl
                db = db + diff
                dbsel.append(-diff)
            db = db + _dot_exact_lhs01(pselt_v, jnp.concatenate(dbsel, axis=0))
            dlf = _dot_exact_lhs01(trit_v, db)
            dq, df, dlb_c = elem_vjp((dqf, dkk, dlf))
            dq_ref[pl.ds(r0, r), :] = dq
            df_ref[pl.ds(r0, r), :] = df
            return ds_in, dlb + jnp.sum(dlb_c, axis=0, keepdims=True)

        _, dlb = lax.fori_loop(0, nc // grp, chunk, (jnp.zeros((d, d), F32), jnp.zeros((1, d), F32)))
        dlb_ref[0] = dlb

    seq = _hg_seq(s, d, 0)
    full = lambda a: pl.BlockSpec(a.shape, lambda e, hd: (0,) * a.ndim)
    return pl.pallas_call(
        body, name=name, grid=(b, nh),
        in_specs=[_hg_seq(s, d, cols[0]), _hg_seq(s, d, cols[1]), _hg_seq(s, d, cols[2]),
                  pl.BlockSpec((1, 1, d), lambda e, hd: (hd, 0, 0)),
                  pl.BlockSpec((1, nc, d, d), lambda e, hd: (e * nh + hd, 0, 0, 0)), seq,
                  full(tri), full(psel), full(masks), full(tri_t), full(psel_t)],
        out_specs=[seq, seq, seq, pl.BlockSpec((1, 1, d), lambda e, hd: (e * nh + hd, 0, 0))],
        out_shape=[jax.ShapeDtypeStruct((b * s, nh * d), F32)] * 3 + [jax.ShapeDtypeStruct((bh, 1, d), F32)],
        compiler_params=_params(("parallel", "parallel")),
    )(proj, proj, proj, lb.reshape(nh, 1, d), states, do, tri, psel, masks, tri_t, psel_t)


def _col_bcast_t(colvec):
    n = colvec.shape[0]
    return jnp.transpose(jnp.broadcast_to(colvec, (n, n)))[0:1, :]


def _swa_probs(qg, kb, bias, sink, valid, scale):
    logits = _dot(qg, kb, NT) * scale + bias
    logits = jnp.where(valid, logits, -jnp.inf)
    m = jnp.maximum(jnp.max(logits, axis=-1, keepdims=True), sink)
    e = jnp.exp(logits - m)
    es = jnp.exp(sink - m)
    den = jnp.sum(e, axis=-1, keepdims=True) + es
    return e / den, es / den


def _swa_valid(n):
    w = WINDOW
    row = lax.broadcasted_iota(jnp.int32, (w, 2 * w), 0)
    col = lax.broadcasted_iota(jnp.int32, (w, 2 * w), 1)
    dist = row + w - col
    return (dist >= 0) & (dist < w) & ((col >= w) | (n > 0))


def _swa_specs(b, g, s, d):
    w = WINDOW
    q_spec = pl.BlockSpec((1, 1, g, w, d), lambda h, bi, n: (bi, h, 0, n, 0))
    kp_spec = pl.BlockSpec((1, 1, w, d), lambda h, bi, n: (bi, h, jnp.maximum(n - 1, 0), 0))
    kc_spec = pl.BlockSpec((1, 1, w, d), lambda h, bi, n: (bi, h, n, 0))
    bias_spec = pl.BlockSpec((1, g, w, 2 * w), lambda h, bi, n: (h, 0, 0, 0))
    sink_spec = pl.BlockSpec(memory_space=pltpu.SMEM)
    return q_spec, kp_spec, kc_spec, bias_spec, sink_spec


def _swa_fwd(q, k, v, sinks, bias, name):
    b, kvh, g, s, d = q.shape
    w = WINDOW
    scale = d ** -0.5
    q_spec, kp_spec, kc_spec, bias_spec, sink_spec = _swa_specs(b, g, s, d)

    def body(q_ref, kp_ref, kc_ref, vp_ref, vc_ref, bias_ref, sink_ref, o_ref):
        h, n = pl.program_id(0), pl.program_id(2)
        valid = _swa_valid(n)
        kb = jnp.concatenate([kp_ref[0, 0], kc_ref[0, 0]], axis=0)
        vb = jnp.concatenate([vp_ref[0, 0], vc_ref[0, 0]], axis=0)
        for gi in range(g):
            p, _ = _swa_probs(q_ref[0, 0, gi], kb, bias_ref[0, gi], sink_ref[h * g + gi], valid, scale)
            o_ref[0, 0, gi] = _dot(p, vb)

    return pl.pallas_call(
        body, name=name, grid=(kvh, b, s // w),
        in_specs=[q_spec, kp_spec, kc_spec, kp_spec, kc_spec, bias_spec, sink_spec], out_specs=q_spec,
        out_shape=jax.ShapeDtypeStruct(q.shape, F32), compiler_params=_params(("parallel", "parallel", "arbitrary")),
    )(q, k, k, v, v, bias, sinks)


def _swa_bwd(q, k, v, sinks, bias, do, name):
    b, kvh, g, s, d = q.shape
    w = WINDOW
    scale = d ** -0.5
    q_spec, kp_spec, kc_spec, bias_spec, sink_spec = _swa_specs(b, g, s, d)
    kv_acc = pl.BlockSpec((1, 1, s, d), lambda h, bi, n: (bi, h, 0, 0))
    dsink_spec = pl.BlockSpec((1, g, LANES), lambda h, bi, n: (h, 0, 0))

    def body(q_ref, kp_ref, kc_ref, vp_ref, vc_ref, bias_ref, sink_ref, do_ref, dq_ref, dk_ref, dv_ref, dbias_ref, dsink_ref):
        h, bi, n = pl.program_id(0), pl.program_id(1), pl.program_id(2)
        valid = _swa_valid(n)
        kb = jnp.concatenate([kp_ref[0, 0], kc_ref[0, 0]], axis=0)
        vb = jnp.concatenate([vp_ref[0, 0], vc_ref[0, 0]], axis=0)

        @pl.when(n == 0)
        def _():
            dk_ref[...] = jnp.zeros_like(dk_ref)
            dv_ref[...] = jnp.zeros_like(dv_ref)

        @pl.when((n == 0) & (bi == 0))
        def _():
            dbias_ref[...] = jnp.zeros_like(dbias_ref)
            dsink_ref[...] = jnp.zeros_like(dsink_ref)

        dkb = jnp.zeros((2 * w, d), F32)
        dvb = jnp.zeros((2 * w, d), F32)
        for gi in range(g):
            qg, dog = q_ref[0, 0, gi], do_ref[0, 0, gi]
            p, ps = _swa_probs(qg, kb, bias_ref[0, gi], sink_ref[h * g + gi], valid, scale)
            dp = _dot(dog, vb, NT)
            delta = jnp.sum(p * dp, axis=-1, keepdims=True)
            dl = p * (dp - delta)
            dq_ref[0, 0, gi] = _dot(dl, kb) * scale
            dkb = dkb + _dot(dl, qg, TN) * scale
            dvb = dvb + _dot(p, dog, TN)
            dbias_ref[0, gi] += dl
            dsink_ref[0, gi:gi + 1, :] += jnp.broadcast_to(jnp.sum(-ps * delta, axis=0, keepdims=True), (1, LANES))

        c0 = pl.multiple_of(n * w, w)
        dk_ref[0, 0, pl.ds(c0, w), :] += dkb[w:]
        dv_ref[0, 0, pl.ds(c0, w), :] += dvb[w:]

        @pl.when(n > 0)
        def _():
            p0 = pl.multiple_of((n - 1) * w, w)
            dk_ref[0, 0, pl.ds(p0, w), :] += dkb[:w]
            dv_ref[0, 0, pl.ds(p0, w), :] += dvb[:w]

    return pl.pallas_call(
        body, name=name, grid=(kvh, b, s // w),
        in_specs=[q_spec, kp_spec, kc_spec, kp_spec, kc_spec, bias_spec, sink_spec, q_spec],
        out_specs=[q_spec, kv_acc, kv_acc, bias_spec, dsink_spec],
        out_shape=[jax.ShapeDtypeStruct(q.shape, F32), jax.ShapeDtypeStruct(k.shape, F32), jax.ShapeDtypeStruct(k.shape, F32),
                   jax.ShapeDtypeStruct(bias.shape, F32), jax.ShapeDtypeStruct((kvh, g, LANES), F32)],
        compiler_params=_params(("arbitrary", "arbitrary", "arbitrary")),
    )(q, k, k, v, v, bias, sinks, do)


def _t5_bias(rel_bias):
    t = np.arange(WINDOW)[:, None]
    s = np.arange(2 * WINDOW)[None, :]
    dist = t + WINDOW - s
    max_exact = N_BUCKETS // 2
    large = max_exact + (np.log(np.maximum(dist, max_exact) / max_exact) / math.log(MAX_DISTANCE / max_exact)
                         * (N_BUCKETS - max_exact)).astype(np.int32)
    large = np.minimum(large, N_BUCKETS - 1)
    bucket = np.where(dist < max_exact, np.maximum(dist, 0), large).astype(np.int32)
    onehot = jnp.asarray(np.eye(N_BUCKETS, dtype=np.float32)[bucket])
    bias = jnp.einsum("tsb,bh->hts", onehot, rel_bias.astype(F32), precision=lax.Precision.HIGHEST)
    return bias.reshape(SW_KV_HEADS, SW_GROUP, WINDOW, 2 * WINDOW)


CONV_W = 3


def _shift_down(x, k):
    row = lax.broadcasted_iota(jnp.int32, x.shape, 0)
    return jnp.where(row >= k, pltpu.roll(x, k, axis=0), 0.0)


def _shift_up(x, k):
    n = x.shape[0]
    row = lax.broadcasted_iota(jnp.int32, x.shape, 0)
    return jnp.where(row < n - k, pltpu.roll(x, n - k, axis=0), 0.0)


def _conv3(u, w, bvec):
    return w[0:1] * _shift_down(u, 2) + w[1:2] * _shift_down(u, 1) + w[2:3] * u + bvec


def _convglu_fwd(u, w, bvec, name, out_dtype=F32):
    b, s, f2 = u.shape
    f = f2 // 2
    tc = _pick(f, 256)
    nt = f // tc

    def body(ug_ref, uu_ref, wg_ref, wu_ref, bg_ref, bu_ref, o_ref):
        cg = _conv3(ug_ref[0], wg_ref[...], bg_ref[...])
        cu = _conv3(uu_ref[0], wu_ref[...], bu_ref[...])
        o_ref[0] = (_silu(cg) * cu).astype(out_dtype)

    ug = pl.BlockSpec((1, s, tc), lambda j, bi: (bi, 0, j))
    uu = pl.BlockSpec((1, s, tc), lambda j, bi: (bi, 0, j + nt))
    wg = pl.BlockSpec((CONV_W, tc), lambda j, bi: (0, j))
    wu = pl.BlockSpec((CONV_W, tc), lambda j, bi: (0, j + nt))
    bg = pl.BlockSpec((1, tc), lambda j, bi: (0, j))
    bu = pl.BlockSpec((1, tc), lambda j, bi: (0, j + nt))
    bv = bvec.reshape(1, f2)
    return pl.pallas_call(body, name=name, grid=(nt, b), in_specs=[ug, uu, wg, wu, bg, bu], out_specs=ug,
                          out_shape=jax.ShapeDtypeStruct((b, s, f), out_dtype),
                          compiler_params=_params(("parallel", "parallel")))(u, u, w, w, bv, bv)


def _convglu_bwd(u, w, bvec, dact, name):
    b, s, f2 = u.shape
    f = f2 // 2
    tc = LANES
    nt = f // tc

    def taps(dc, uv):
        rows = [jnp.sum(dc * _shift_down(uv, 2), axis=0, keepdims=True), jnp.sum(dc * _shift_down(uv, 1), axis=0, keepdims=True),
                jnp.sum(dc * uv, axis=0, keepdims=True), jnp.sum(dc, axis=0, keepdims=True)]
        return jnp.concatenate(rows + [jnp.zeros((4, tc), F32)], axis=0)

    def back(dc, wv):
        return wv[2:3] * dc + wv[1:2] * _shift_up(dc, 1) + wv[0:1] * _shift_up(dc, 2)

    def body(ug_ref, uu_ref, wg_ref, wu_ref, bg_ref, bu_ref, da_ref, dug_ref, duu_ref, dwg_ref, dwu_ref):
        ugv, uuv, da = ug_ref[0], uu_ref[0], da_ref[0]
        cg = _conv3(ugv, wg_ref[...], bg_ref[...])
        cu = _conv3(uuv, wu_ref[...], bu_ref[...])
        sg = jax.nn.sigmoid(cg)
        dcu = da * (cg * sg)
        dcg = da * cu * (sg * (1.0 + cg * (1.0 - sg)))
        dug_ref[0] = back(dcg, wg_ref[...])
        duu_ref[0] = back(dcu, wu_ref[...])

        @pl.when(pl.program_id(1) == 0)
        def _():
            dwg_ref[...] = jnp.zeros_like(dwg_ref)
            dwu_ref[...] = jnp.zeros_like(dwu_ref)

        dwg_ref[...] += taps(dcg, ugv)
        dwu_ref[...] += taps(dcu, uuv)

    ug = pl.BlockSpec((1, s, tc), lambda j, bi: (bi, 0, j))
    uu = pl.BlockSpec((1, s, tc), lambda j, bi: (bi, 0, j + nt))
    wg = pl.BlockSpec((CONV_W, tc), lambda j, bi: (0, j))
    wu = pl.BlockSpec((CONV_W, tc), lambda j, bi: (0, j + nt))
    bg = pl.BlockSpec((1, tc), lambda j, bi: (0, j))
    bu = pl.BlockSpec((1, tc), lambda j, bi: (0, j + nt))
    acc = pl.BlockSpec((8, tc), lambda j, bi: (0, j))
    bv = bvec.reshape(1, f2)
    return pl.pallas_call(
        body, name=name, grid=(nt, b), in_specs=[ug, uu, wg, wu, bg, bu, ug], out_specs=[ug, ug, acc, acc],
        out_shape=[jax.ShapeDtypeStruct((b, s, f), F32)] * 2 + [jax.ShapeDtypeStruct((8, f), F32)] * 2,
        compiler_params=_params(("parallel", "arbitrary")),
    )(u, u, w, w, bv, bv, dact)


def _adamw(w, g, m, v, name):
    r, c = w.shape
    tr = _row_tile(r, c)
    c1 = 1.0 - ADAM_B1 ** ADAM_STEP
    c2 = 1.0 - ADAM_B2 ** ADAM_STEP

    def body(w_ref, g_ref, m_ref, v_ref, d_ref, mo_ref, vo_ref):
        gv = g_ref[...]
        mn = ADAM_B1 * m_ref[...] + (1.0 - ADAM_B1) * gv
        vn = ADAM_B2 * v_ref[...] + (1.0 - ADAM_B2) * (gv * gv)
        d_ref[...] = -ADAM_LR * ((mn / c1) / (jnp.sqrt(vn / c2) + ADAM_EPS) + ADAM_WD * w_ref[...])
        mo_ref[...] = mn
        vo_ref[...] = vn

    blk = pl.BlockSpec((tr, c), lambda i: (i, 0))
    return pl.pallas_call(body, name=name, grid=(r // tr,), in_specs=[blk] * 4, out_specs=[blk] * 3,
                          out_shape=[jax.ShapeDtypeStruct((r, c), F32)] * 3, compiler_params=_params(("parallel",)))(w, g, m, v)


MESH = pl.DeviceIdType.MESH
ANY = pl.BlockSpec(memory_space=pl.ANY)


def _position():
    return lax.axis_index("x"), lax.axis_index("y"), lax.axis_index("c")


def _all_gather(x, name):
    r, c = x.shape

    def body(x_ref, out_ref, send_sems, recv_sems, local_sem):
        mx, my, mc = _position()
        me, sibling = (mx, my, mc), (mx, my, 1 - mc)
        chips = [(1 - mx, my), (mx, 1 - my), (1 - mx, 1 - my)]

        def slot(px, py, pc):
            return out_ref.at[4 * px + 2 * py + pc]

        def copy(k, block, to, src=None):
            return pltpu.make_async_remote_copy(
                src_ref=slot(*block) if src is None else src, dst_ref=slot(*block),
                send_sem=send_sems.at[k], recv_sem=recv_sems.at[k], device_id=to, device_id_type=MESH)

        mine = pltpu.make_async_copy(x_ref, slot(*me), local_sem.at[0])
        mine.start()
        first = [copy(0, me, sibling, src=x_ref)]
        first += [copy(1 + j, me, (*chip, mc), src=x_ref) for j, chip in enumerate(chips)]
        for cp in first:
            cp.start()
        passed = [copy(4 + j, (*chip, mc), sibling) for j, chip in enumerate(chips)]
        for j, chip in enumerate(chips):
            copy(1 + j, (*chip, mc), me).wait_recv()
            passed[j].start()
        copy(0, sibling, me).wait_recv()
        for j, chip in enumerate(chips):
            copy(4 + j, (*chip, 1 - mc), me).wait_recv()
        for cp in first + passed:
            cp.wait_send()
        mine.wait()

    return pl.pallas_call(
        body, name=name, out_shape=jax.ShapeDtypeStruct((N_DEV, r, c), x.dtype), in_specs=[ANY], out_specs=ANY,
        scratch_shapes=[pltpu.SemaphoreType.DMA((7,)), pltpu.SemaphoreType.DMA((7,)), pltpu.SemaphoreType.DMA((1,))],
    )(x)


def _dev_rows(ref, dev, a):
    return ref.at[:, pl.ds(pl.multiple_of(dev * a, 16), a), :]


def _all_gather_rows(shards, name):
    nt = len(shards)

    def body(*refs):
        x_refs, out_refs = refs[:nt], refs[nt:2 * nt]
        send_sems, recv_sems, local_sems = refs[2 * nt:]
        mx, my, mc = _position()
        me, sibling = (mx, my, mc), (mx, my, 1 - mc)
        chips = [(1 - mx, my), (mx, 1 - my), (1 - mx, 1 - my)]

        def slot(t, px, py, pc):
            return _dev_rows(out_refs[t], 4 * px + 2 * py + pc, shards[t].shape[1])

        def copy(t, k, block, to, src=None):
            return pltpu.make_async_remote_copy(
                src_ref=slot(t, *block) if src is None else src, dst_ref=slot(t, *block),
                send_sem=send_sems.at[7 * t + k], recv_sem=recv_sems.at[7 * t + k], device_id=to, device_id_type=MESH)

        mine = [pltpu.make_async_copy(x_refs[t], slot(t, *me), local_sems.at[t]) for t in range(nt)]
        first = [copy(t, 0, me, sibling, src=x_refs[t]) for t in range(nt)]
        first += [copy(t, 1 + j, me, (*chip, mc), src=x_refs[t]) for j, chip in enumerate(chips) for t in range(nt)]
        for cp in mine + first:
            cp.start()
        passed = []
        for j, chip in enumerate(chips):
            for t in range(nt):
                copy(t, 1 + j, (*chip, mc), me).wait_recv()
                fwd = copy(t, 4 + j, (*chip, mc), sibling)
                fwd.start()
                passed.append(fwd)
        for t in range(nt):
            copy(t, 0, sibling, me).wait_recv()
        for j, chip in enumerate(chips):
            for t in range(nt):
                copy(t, 4 + j, (*chip, 1 - mc), me).wait_recv()
        for cp in first + passed:
            cp.wait_send()
        for cp in mine:
            cp.wait()

    out_shape = [jax.ShapeDtypeStruct((x.shape[0], N_DEV * x.shape[1], x.shape[2]), x.dtype) for x in shards]
    return pl.pallas_call(
        body, name=name, out_shape=out_shape, in_specs=[ANY] * nt, out_specs=[ANY] * nt,
        scratch_shapes=[pltpu.SemaphoreType.DMA((7 * nt,)), pltpu.SemaphoreType.DMA((7 * nt,)), pltpu.SemaphoreType.DMA((nt,))],
    )(*shards)


def _rs_pair(gs, name):
    nt = len(gs)

    def body(*refs):
        g_refs, a_refs = refs[:nt], refs[nt:2 * nt]
        send_sems, recv_sems = refs[2 * nt:]
        mx, my, mc = _position()
        copies = [pltpu.make_async_remote_copy(
            src_ref=_dev_rows(g_refs[t], 2 * j + 1 - mc, gs[t].shape[1] // N_DEV), dst_ref=a_refs[t].at[j],
            send_sem=send_sems.at[4 * t + j], recv_sem=recv_sems.at[4 * t + j],
            device_id=(mx, my, 1 - mc), device_id_type=MESH) for t in range(nt) for j in range(4)]
        for cp in copies:
            cp.start()
        for cp in copies:
            cp.wait()

    out_shape = [jax.ShapeDtypeStruct((4, g.shape[0], g.shape[1] // N_DEV, g.shape[2]), g.dtype) for g in gs]
    return pl.pallas_call(
        body, name=name, out_shape=out_shape, in_specs=[ANY] * nt, out_specs=[ANY] * nt,
        scratch_shapes=[pltpu.SemaphoreType.DMA((4 * nt,)), pltpu.SemaphoreType.DMA((4 * nt,))],
    )(*gs)


def _rs_chips(ps, name):
    nt = len(ps)

    def body(*refs):
        p_refs, b_refs = refs[:nt], refs[nt:2 * nt]
        send_sems, recv_sems = refs[2 * nt:]
        mx, my, mc = _position()
        chips = [(1 - mx, my), (mx, 1 - my), (1 - mx, 1 - my)]
        copies = [pltpu.make_async_remote_copy(
            src_ref=p_refs[t].at[2 * cx + cy], dst_ref=b_refs[t].at[k],
            send_sem=send_sems.at[3 * t + k], recv_sem=recv_sems.at[3 * t + k],
            device_id=(cx, cy, mc), device_id_type=MESH) for t in range(nt) for k, (cx, cy) in enumerate(chips)]
        for cp in copies:
            cp.start()
        for cp in copies:
            cp.wait()

    out_shape = [jax.ShapeDtypeStruct((3,) + p.shape[1:], p.dtype) for p in ps]
    return pl.pallas_call(
        body, name=name, out_shape=out_shape, in_specs=[ANY] * nt, out_specs=[ANY] * nt,
        scratch_shapes=[pltpu.SemaphoreType.DMA((3 * nt,)), pltpu.SemaphoreType.DMA((3 * nt,))],
    )(*ps)


def _div_tile(a, b):
    best = 16
    for t in range(16, a + 1, 16):
        if a % t == 0 and t * b * 4 <= 2 * 1024 * 1024:
            best = t
    return best


def _rs_add_pair(g, a, core, name):
    l, a8, b = g.shape
    rows = a8 // N_DEV
    ta = _div_tile(rows, b)

    def body(core_ref, g_ref, a_ref, p_ref):
        p_ref[...] = (g_ref[...] + a_ref[...]).astype(BF16)

    grid_spec = pltpu.PrefetchScalarGridSpec(
        num_scalar_prefetch=1, grid=(4, l, rows // ta),
        in_specs=[pl.BlockSpec((1, 1, ta, b), lambda j, li, i, core_ref: (li, 2 * j + core_ref[0], i, 0)),
                  pl.BlockSpec((1, 1, ta, b), lambda j, li, i, core_ref: (j, li, i, 0))],
        out_specs=pl.BlockSpec((1, 1, ta, b), lambda j, li, i, core_ref: (j, li, i, 0)))
    return pl.pallas_call(body, name=name, grid_spec=grid_spec, out_shape=jax.ShapeDtypeStruct((4, l, rows, b), BF16),
                          compiler_params=_params(("parallel", "parallel", "parallel")))(core, g.reshape(l, N_DEV, rows, b), a)


def _rs_final(g, a, bsum, where, name):
    l, a8, b = g.shape
    rows = a8 // N_DEV
    ta = _div_tile(rows, b)

    def body(where_ref, g_ref, a_ref, b_ref, o_ref):
        own = g_ref[0, 0] + a_ref[0, 0]
        o_ref[0] = ((own + b_ref[0, 0].astype(F32)) + b_ref[1, 0].astype(F32)) + b_ref[2, 0].astype(F32)

    grid_spec = pltpu.PrefetchScalarGridSpec(
        num_scalar_prefetch=1, grid=(l, rows // ta),
        in_specs=[pl.BlockSpec((1, 1, ta, b), lambda li, i, w_ref: (li, w_ref[0], i, 0)),
                  pl.BlockSpec((1, 1, ta, b), lambda li, i, w_ref: (w_ref[1], li, i, 0)),
                  pl.BlockSpec((3, 1, ta, b), lambda li, i, w_ref: (0, li, i, 0))],
        out_specs=pl.BlockSpec((1, ta, b), lambda li, i, w_ref: (li, i, 0)))
    return pl.pallas_call(body, name=name, grid_spec=grid_spec, out_shape=jax.ShapeDtypeStruct((l, rows, b), F32),
                          compiler_params=_params(("parallel", "parallel")))(where, g.reshape(l, N_DEV, rows, b), a, bsum)


def _sum_devices(x, name):
    _, r, c = x.shape

    def body(x_ref, o_ref):
        acc = x_ref[0]
        for d in range(1, N_DEV):
            acc = acc + x_ref[d]
        o_ref[...] = acc

    return pl.pallas_call(body, name=name, out_shape=jax.ShapeDtypeStruct((r, c), F32))(x)


BIG = (("ab_w_in", "col"), ("ab_w_out", "row"), ("c_w_in", "col"), ("c_w_out", "row"),
       ("ffn_up", "col"), ("ffn_down", "row"), ("ple_gate", "row"), ("ple_proj", "col"))
KIND = dict(BIG)


def _row_block(shard, kind):
    return shard.transpose(0, 2, 1) if kind == "col" else shard


def _pad_rows(flat):
    pad = -flat.shape[0] % (8 * LANES)
    return jnp.pad(flat, (0, pad)).reshape(-1, LANES)


def _heads_out(x, b, s, nh, d):
    return x.reshape(b, s, nh, d).transpose(0, 2, 1, 3).reshape(b * nh, s, d)


def _heads_in(x, b, s, nh, d):
    return x.reshape(b, nh, s, d).transpose(0, 2, 1, 3).reshape(b * s, nh * d)


def _lower_bounds(logits):
    c = jnp.cumsum(jax.nn.softmax(logits.astype(F32), axis=0), axis=0)
    return c - c[0]


SB_COLS = tuple(k * SB_WIDTH // LANES for k in range(3))
HG_COLS = tuple((3 * SB_WIDTH + k * HG_W) // LANES for k in range(3))
HG_GATE_COL = (3 * SB_WIDTH + 3 * HG_W) // LANES
HG_OUT_COL = SB_WIDTH // LANES


def _forward_backward(x, p, target, W, P):
    b, s, dm = x.shape
    n = b * s
    h = x.reshape(n, dm)
    lbs, lb_vjp = jax.vjp(_lower_bounds, P["hg_lb_logits"])
    bias, bias_vjp = jax.vjp(_t5_bias, P["rel_bias"])
    saved = []
    gw = {name: lax.empty(W[name].shape, F32) for name, _ in BIG}

    def times_w(a, name, l, tag, res=None):
        return _mm(a, W[name], "nt" if KIND[name] == "col" else "nn", tag, res=res, layer=l)

    def times_wt(dy, name, l, tag):
        return _mm(dy, W[name], "nn" if KIND[name] == "col" else "nt", tag, layer=l)

    def grad_w(a, dy, name, l, tag):
        lhs, rhs = (dy, a) if KIND[name] == "col" else (a, dy)
        gw[name] = _mm(lhs, rhs, "tn", tag, into=(gw[name], l, 0))

    for i in range(DEPTH):
        j = i // 2
        r = {"h0": h}
        hn = _rms_fwd(h, P["mix_norm"][i], f"mix_norm_f{i}", BF16)
        r["hn"] = hn
        if i % 2 == 0:
            proj = times_w(hn, "ab_w_in", j, f"ab_in_f{i}")
            oa, lta = _sb_fwd(proj, SB_COLS, b, s, f"sb_f{i}")
            ob, st = _hg_fwd(proj, HG_COLS, lbs[j].reshape(HG_HEADS, HG_DK), b, s, f"hg_f{i}")
            obg = _gnorm_fwd(ob, proj, HG_GATE_COL, P["hg_out_norm"][j], f"hg_norm_f{i}")
            cat = jnp.concatenate([oa, obg], axis=1).astype(BF16)
            h = times_w(cat, "ab_w_out", j, f"ab_out_f{i}", res=h)
            r.update(lta=lta, proj=proj, ob=ob, st=st, cat=cat)
        else:
            proj = times_w(hn, "c_w_in", j, f"c_in_f{i}")
            nq = SW_HEADS * SW_DIM
            nkv = SW_KV_HEADS * SW_DIM
            q = _heads_out(proj[:, :nq], b, s, SW_HEADS, SW_DIM).reshape(-1, SW_DIM)
            k = _heads_out(proj[:, nq:nq + nkv], b, s, SW_KV_HEADS, SW_DIM).reshape(-1, SW_DIM)
            v = _heads_out(proj[:, nq + nkv:], b, s, SW_KV_HEADS, SW_DIM).reshape(b, SW_KV_HEADS, s, SW_DIM)
            qn = _rms_fwd(q, P["q_norm"][j], f"q_norm_f{i}").reshape(b, SW_KV_HEADS, SW_GROUP, s, SW_DIM)
            kn = _rms_fwd(k, P["k_norm"][j], f"k_norm_f{i}").reshape(b, SW_KV_HEADS, s, SW_DIM)
            o = _swa_fwd(qn, kn, v, P["sinks"][j], bias, f"swa_f{i}")
            o2 = _heads_in(o.reshape(b * SW_HEADS, s, SW_DIM), b, s, SW_HEADS, SW_DIM).astype(BF16)
            h = times_w(o2, "c_w_out", j, f"c_out_f{i}", res=h)
            r.update(q=q, k=k, v=v, qn=qn, kn=kn, o2=o2)
        r["h1"] = h
        hn2 = _rms_fwd(h, P["ffn_norm"][i], f"ffn_norm_f{i}", BF16)
        u = times_w(hn2, "ffn_up", i, f"ffn_up_f{i}").reshape(b, s, 2 * D_FF)
        act = _convglu_fwd(u, W["ffn_conv"][i], P["ffn_conv_b"][i], f"conv_f{i}", BF16).reshape(n, D_FF)
        h = times_w(act, "ffn_down", i, f"ffn_down_f{i}", res=h)
        r.update(hn2=hn2, u=u, act=act, h2=h)
        hn3 = _rms_fwd(h, P["ple_norm"][i], f"ple_norm_f{i}", BF16)
        z = times_w(hn3, "ple_gate", i, f"ple_gate_f{i}")
        pi = p[i].reshape(n, PLE_DIM)
        e = times_w(pi, "ple_proj", i, f"ple_proj_f{i}")
        h = _sigmul_fwd(z, e, h, f"ple_f{i}")
        r.update(hn3=hn3, z=z, e=e, pi=pi)
        saved.append(r)

    loss, dh = _loss_fwd(h, target.reshape(n, dm), "loss")

    gconv = [None] * DEPTH
    gp = {name: [None] * P[name].shape[0] for name in ("mix_norm", "hg_out_norm", "q_norm", "k_norm", "sinks",
                                                        "ffn_norm", "ffn_conv_b", "ple_norm")}
    dlbs = [None] * (DEPTH // 2)
    dbias = jnp.zeros_like(bias)

    for i in reversed(range(DEPTH)):
        j = i // 2
        r = saved[i]
        dz, de = _sigmul_bwd(r["z"], r["e"], dh, f"ple_b{i}")
        grad_w(r["pi"], de, "ple_proj", i, f"ple_proj_g{i}")
        grad_w(r["hn3"], dz, "ple_gate", i, f"ple_gate_g{i}")
        dhn3 = times_wt(dz, "ple_gate", i, f"ple_gate_b{i}")
        dh, gp["ple_norm"][i] = _rms_bwd(r["h2"], P["ple_norm"][i], dhn3, f"ple_norm_b{i}", res=dh)

        dact = times_wt(dh, "ffn_down", i, f"ffn_down_b{i}").reshape(b, s, D_FF)
        grad_w(r["act"], dh, "ffn_down", i, f"ffn_down_g{i}")
        dug, duu, ag, au = _convglu_bwd(r["u"], W["ffn_conv"][i], P["ffn_conv_b"][i], dact, f"conv_b{i}")
        gconv[i] = jnp.concatenate([ag[:CONV_W], au[:CONV_W]], axis=-1)
        gp["ffn_conv_b"][i] = jnp.concatenate([ag[CONV_W], au[CONV_W]], axis=-1)
        dhn2 = None
        for half, dpart in enumerate((dug.reshape(n, D_FF), duu.reshape(n, D_FF))):
            gw["ffn_up"] = _mm(dpart, r["hn2"], "tn", f"ffn_up_g{i}_{half}", into=(gw["ffn_up"], i, half * D_FF))
            dhn2 = _mm(dpart, W["ffn_up"], "nn", f"ffn_up_b{i}_{half}", layer=i, b_rows=(half * D_FF, D_FF), res=dhn2)
        dh, gp["ffn_norm"][i] = _rms_bwd(r["h1"], P["ffn_norm"][i], dhn2, f"ffn_norm_b{i}", res=dh)

        if i % 2 == 0:
            dcat = times_wt(dh, "ab_w_out", j, f"ab_out_b{i}")
            grad_w(r["cat"], dh, "ab_w_out", j, f"ab_out_g{i}")
            dob, dgb, gp["hg_out_norm"][j] = _gnorm_bwd(r["ob"], r["proj"], HG_GATE_COL, P["hg_out_norm"][j], dcat, HG_OUT_COL,
                                                        f"hg_norm_b{i}")
            dqb, dfb, dib, dlb = _hg_bwd(r["proj"], HG_COLS, lbs[j].reshape(HG_HEADS, HG_DK), r["st"], dob, b, s, f"hg_b{i}")
            dlbs[j] = dlb.reshape(b, HG_W).sum(axis=0)
            dqa, dka, dva = _sb_bwd(r["proj"], SB_COLS, r["lta"], dcat, 0, b, s, f"sb_b{i}")
            dproj = jnp.concatenate([dqa, dka, dva, dqb, dfb, dib, dgb], axis=1)
            grad_w(r["hn"], dproj, "ab_w_in", j, f"ab_in_g{i}")
            dhn = times_wt(dproj, "ab_w_in", j, f"ab_in_b{i}")
        else:
            do2 = times_wt(dh, "c_w_out", j, f"c_out_b{i}")
            grad_w(r["o2"], dh, "c_w_out", j, f"c_out_g{i}")
            do = _heads_out(do2, b, s, SW_HEADS, SW_DIM).reshape(b, SW_KV_HEADS, SW_GROUP, s, SW_DIM)
            dqn, dkn, dv, dbias_i, dsink = _swa_bwd(r["qn"], r["kn"], r["v"], P["sinks"][j], bias, do, f"swa_b{i}")
            dbias = dbias + dbias_i
            gp["sinks"][j] = dsink[:, :, 0].reshape(SW_HEADS)
            dq, gp["q_norm"][j] = _rms_bwd(r["q"], P["q_norm"][j], dqn.reshape(-1, SW_DIM), f"q_norm_b{i}")
            dk, gp["k_norm"][j] = _rms_bwd(r["k"], P["k_norm"][j], dkn.reshape(-1, SW_DIM), f"k_norm_b{i}")
            dproj = jnp.concatenate([_heads_in(dq.reshape(b * SW_HEADS, s, SW_DIM), b, s, SW_HEADS, SW_DIM),
                                     _heads_in(dk.reshape(b * SW_KV_HEADS, s, SW_DIM), b, s, SW_KV_HEADS, SW_DIM),
                                     _heads_in(dv.reshape(b * SW_KV_HEADS, s, SW_DIM), b, s, SW_KV_HEADS, SW_DIM)], axis=1)
            grad_w(r["hn"], dproj, "c_w_in", j, f"c_in_g{i}")
            dhn = times_wt(dproj, "c_w_in", j, f"c_in_b{i}")
        dh, gp["mix_norm"][i] = _rms_bwd(r["h0"], P["mix_norm"][i], dhn, f"mix_norm_b{i}", res=dh)

    gp = {name: jnp.stack(v) for name, v in gp.items()}
    gp["hg_lb_logits"] = lb_vjp(jnp.stack(dlbs))[0]
    gp["rel_bias"] = bias_vjp(dbias)[0]
    return loss[0, 0], dh.reshape(b, s, dm), gw, jnp.stack(gconv), gp


WEIGHTS = ("mix_norm", "ab_w_in", "hg_lb_logits", "hg_out_norm", "ab_w_out", "c_w_in", "q_norm", "k_norm", "sinks", "rel_bias",
           "c_w_out", "ffn_norm", "ffn_up", "ffn_conv", "ffn_conv_b", "ffn_down", "ple_norm", "ple_gate", "ple_proj")
SMALL = ("mix_norm", "hg_lb_logits", "hg_out_norm", "q_norm", "k_norm", "sinks", "rel_bias", "ffn_norm", "ffn_conv_b", "ple_norm")


def _step(x, p, target, w, m, v):
    names = [name for name, _ in BIG]
    mx, my, mc = _position()
    dev = 4 * mx + 2 * my + mc

    blocks = [_row_block(w[name], KIND[name]).astype(BF16) for name in names]
    full = dict(zip(names, _all_gather_rows(blocks, "gather_weights")))
    nl, taps, cs = w["ffn_conv"].shape
    conv_all = _all_gather(_pad_rows(w["ffn_conv"].reshape(-1)), "gather_conv").reshape(N_DEV, -1)[:, :nl * taps * cs]
    full["ffn_conv"] = conv_all.reshape(N_DEV, nl, taps, cs).transpose(1, 2, 0, 3).reshape(nl, taps, N_DEV * cs)

    small = {name: w[name] for name in SMALL}
    loss, grad_x, gw, gconv, gp = _forward_backward(x, p, target, full, small)

    core = jnp.reshape(mc, (1,)).astype(jnp.int32)
    where = jnp.stack([dev, 2 * mx + my]).astype(jnp.int32)
    parts = [gw[name] for name in names]
    from_sibling = _rs_pair(parts, "reduce_pair")
    chip_sums = [_rs_add_pair(g, a, core, f"reduce_pair_add_{name}") for name, g, a in zip(names, parts, from_sibling)]
    from_chips = _rs_chips(chip_sums, "reduce_chips")
    grads = {name: _row_block(_rs_final(g, a, bs, where, f"reduce_final_{name}"), KIND[name])
             for name, g, a, bs in zip(names, parts, from_sibling, from_chips)}

    flat_small = jnp.concatenate([gp[name].reshape(-1) for name in SMALL] + [gconv.reshape(-1), loss.reshape(1)])
    small_sum = _sum_devices(_all_gather(_pad_rows(flat_small), "gather_small"), "sum_small").reshape(-1)
    e0 = 0
    for name in SMALL:
        cnt = math.prod(w[name].shape)
        grads[name] = small_sum[e0:e0 + cnt].reshape(w[name].shape)
        e0 += cnt
    gconv_sum = small_sum[e0:e0 + gconv.size].reshape(gconv.shape)
    grads["ffn_conv"] = lax.dynamic_slice_in_dim(gconv_sum, dev * cs, cs, axis=2)
    loss = small_sum[e0 + gconv.size]

    deltas, new_m, new_v = {}, {}, {}
    for name in WEIGHTS:
        shape = w[name].shape
        view = (-1, shape[-1]) if len(shape) > 1 else (1, -1)
        d_, m_, v_ = _adamw(w[name].reshape(view), grads[name].reshape(view), m[name].reshape(view), v[name].reshape(view), f"adamw_{name}")
        deltas[name], new_m[name], new_v[name] = d_.reshape(shape), m_.reshape(shape), v_.reshape(shape)
    return (loss, grad_x, *[grads[k] for k in WEIGHTS], *[deltas[k] for k in WEIGHTS],
            *[new_m[k] for k in WEIGHTS], *[new_v[k] for k in WEIGHTS])


def kernel(x, p, mix_norm, ab_w_in, hg_lb_logits, hg_out_norm, ab_w_out, c_w_in, q_norm, k_norm, sinks, rel_bias, c_w_out, ffn_norm, ffn_up, ffn_conv, ffn_conv_b, ffn_down, ple_norm, ple_gate, ple_proj, loss_target, m_mix_norm, m_ab_w_in, m_hg_lb_logits, m_hg_out_norm, m_ab_w_out, m_c_w_in, m_q_norm, m_k_norm, m_sinks, m_rel_bias, m_c_w_out, m_ffn_norm, m_ffn_up, m_ffn_conv, m_ffn_conv_b, m_ffn_down, m_ple_norm, m_ple_gate, m_ple_proj, v_mix_norm, v_ab_w_in, v_hg_lb_logits, v_hg_out_norm, v_ab_w_out, v_c_w_in, v_q_norm, v_k_norm, v_sinks, v_rel_bias, v_c_w_out, v_ffn_norm, v_ffn_up, v_ffn_conv, v_ffn_conv_b, v_ffn_down, v_ple_norm, v_ple_gate, v_ple_proj):
    w = dict(zip(WEIGHTS, (mix_norm, ab_w_in, hg_lb_logits, hg_out_norm, ab_w_out, c_w_in, q_norm, k_norm, sinks, rel_bias, c_w_out,
                           ffn_norm, ffn_up, ffn_conv, ffn_conv_b, ffn_down, ple_norm, ple_gate, ple_proj)))
    m = dict(zip(WEIGHTS, (m_mix_norm, m_ab_w_in, m_hg_lb_logits, m_hg_out_norm, m_ab_w_out, m_c_w_in, m_q_norm, m_k_norm, m_sinks,
                           m_rel_bias, m_c_w_out, m_ffn_norm, m_ffn_up, m_ffn_conv, m_ffn_conv_b, m_ffn_down, m_ple_norm, m_ple_gate,
                           m_ple_proj)))
    v = dict(zip(WEIGHTS, (v_mix_norm, v_ab_w_in, v_hg_lb_logits, v_hg_out_norm, v_ab_w_out, v_c_w_in, v_q_norm, v_k_norm, v_sinks,
                           v_rel_bias, v_c_w_out, v_ffn_norm, v_ffn_up, v_ffn_conv, v_ffn_conv_b, v_ffn_down, v_ple_norm, v_ple_gate,
                           v_ple_proj)))
    return _step(x, p, loss_target, w, m, v)
```

```python
import functools
import math

import numpy as np
import jax
import jax.numpy as jnp
from jax import lax
from jax.experimental import pallas as pl
from jax.experimental.pallas import tpu as pltpu

F32 = jnp.float32
BF16 = jnp.bfloat16

D_MODEL = 1024
DEPTH = 4
PLE_DIM = 256
EPS = 1e-6
SB_HEADS, SB_DIM = 8, 64
SB_WIDTH = SB_HEADS * SB_DIM
HG_HEADS, HG_DK, HG_DV = 4, 128, 128
HG_W = HG_HEADS * HG_DK
AB_IN = 3 * SB_WIDTH + 4 * HG_W
SW_HEADS, SW_KV_HEADS, SW_DIM = 16, 4, 64
SW_GROUP = SW_HEADS // SW_KV_HEADS
WINDOW = 128
C_IN = (SW_HEADS + 2 * SW_KV_HEADS) * SW_DIM
N_BUCKETS, MAX_DISTANCE = 32, 128
D_FF = 2816
N_DEV = 8

ADAM_LR, ADAM_B1, ADAM_B2, ADAM_EPS, ADAM_WD, ADAM_STEP = 0.001, 0.9, 0.999, 1e-08, 0.01, 10

LANES = 128
VMEM_LIMIT = 48 * 1024 * 1024

NN = (((1,), (0,)), ((), ()))
NT = (((1,), (1,)), ((), ()))
TN = (((0,), (0,)), ((), ()))


MXU_DTYPE = BF16


def _bf(x):
    return x.astype(MXU_DTYPE)


def _dot(a, b, dims=NN):
    return lax.dot_general(_bf(a), _bf(b), dims, preferred_element_type=F32)


def _split3(x):
    x1 = _bf(x)
    r = x - x1.astype(F32)
    x2 = _bf(r)
    x3 = _bf(r - x2.astype(F32))
    return x1, x2, x3


def _dot_exact_lhs01(m, x, terms=3):
    parts = _split3(x)[:terms]
    out = lax.dot_general(m, parts[0], NN, preferred_element_type=F32)
    for p_ in parts[1:]:
        out = out + lax.dot_general(m, p_, NN, preferred_element_type=F32)
    return out


def _dot_exact_rhs01(x, m, terms=2):
    parts = _split3(x)[:terms]
    out = lax.dot_general(parts[0], m, NN, preferred_element_type=F32)
    for p_ in parts[1:]:
        out = out + lax.dot_general(p_, m, NN, preferred_element_type=F32)
    return out


def _pick(n, target):
    best = None
    for t in range(LANES, target + 1, LANES):
        if n % t == 0:
            best = t
    return best or n


def _params(sem=None):
    return pltpu.CompilerParams(dimension_semantics=sem, vmem_limit_bytes=VMEM_LIMIT)


MM_ROWS, MM_ROWS_TN, MM_COLS, MM_DEPTH, MM_DEPTH_TN = 1024, 1408, 1408, 2048, 1024

def _mm(a, b, mode, name, res=None, out_dtype=F32, layer=None, b_rows=None, into=None):
    bshape = b.shape if layer is None else b.shape[1:]
    if b_rows is not None:
        assert mode == "nn"
        bshape = (b_rows[1], bshape[1])
    if mode == "nn":
        (M, K), (K2, N) = a.shape, bshape
    elif mode == "nt":
        (M, K), (N, K2) = a.shape, bshape
    else:
        (K, M), (K2, N) = a.shape, bshape
    assert K == K2, (a.shape, b.shape, mode)
    tm, tn = _pick(M, MM_ROWS_TN if mode == "tn" else MM_ROWS), _pick(N, MM_COLS)
    tk = _pick(K, MM_DEPTH_TN if mode == "tn" else MM_DEPTH)
    nk = K // tk
    k_off = 0 if b_rows is None else b_rows[0] // tk
    assert b_rows is None or b_rows[0] % tk == 0
    dims = {"nn": NN, "nt": NT, "tn": TN}[mode]
    a_spec = pl.BlockSpec((tk, tm), lambda i, j, k: (k, i)) if mode == "tn" else pl.BlockSpec((tm, tk), lambda i, j, k: (i, k))
    if layer is None:
        b_spec = pl.BlockSpec((tn, tk), lambda i, j, k: (j, k)) if mode == "nt" else pl.BlockSpec((tk, tn), lambda i, j, k: (k, j))
    elif mode == "nt":
        b_spec = pl.BlockSpec((None, tn, tk), lambda i, j, k: (layer, j, k))
    else:
        b_spec = pl.BlockSpec((None, tk, tn), lambda i, j, k: (layer, k + k_off, j))
    o_spec = pl.BlockSpec((tm, tn), lambda i, j, k: (i, j))
    has_res = res is not None

    def finish(acc, r_ref, o_ref):
        if has_res:
            acc = acc + r_ref[...]
        o_ref[...] = acc.astype(out_dtype)

    def body(*refs):
        a_ref, b_ref = refs[0], refs[1]
        r_ref = refs[2] if has_res else None
        o_ref = refs[-1] if nk == 1 else refs[-2]
        part = _dot(a_ref[...], b_ref[...], dims)
        if nk == 1:
            finish(part, r_ref, o_ref)
            return
        acc_ref = refs[-1]
        k = pl.program_id(2)

        @pl.when(k == 0)
        def _():
            acc_ref[...] = part

        @pl.when((k > 0) & (k < nk - 1))
        def _():
            acc_ref[...] += part

        @pl.when(k == nk - 1)
        def _():
            finish(acc_ref[...] + part, r_ref, o_ref)

    in_specs = [a_spec, b_spec] + ([o_spec] if has_res else [])
    args = (a, b) + ((res,) if has_res else ())
    out_shape, aliases = jax.ShapeDtypeStruct((M, N), out_dtype), {}
    if into is not None:
        stack, slot, row = into
        assert stack.shape[2] == N and row % tm == 0 and row + M <= stack.shape[1] and stack.dtype == out_dtype and not has_res
        in_specs = in_specs + [pl.BlockSpec(memory_space=pl.ANY)]
        args = args + (stack,)
        o_spec = pl.BlockSpec((None, tm, tn), lambda i, j, k: (slot, i + row // tm, j))
        out_shape, aliases = jax.ShapeDtypeStruct(stack.shape, out_dtype), {2: 0}

    def body_into(a_ref, b_ref, stack_ref, *rest):
        body(a_ref, b_ref, *rest)

    return pl.pallas_call(
        body if into is None else body_into, name=name, grid=(M // tm, N // tn, nk), in_specs=in_specs, out_specs=o_spec,
        out_shape=out_shape, scratch_shapes=[] if nk == 1 else [pltpu.VMEM((tm, tn), F32)], input_output_aliases=aliases,
        compiler_params=_params(("parallel", "parallel", "arbitrary")),
    )(*args)


def _row_tile(n, d):
    if n % 8:
        return n
    t = 8
    while t * 2 <= min(n, (256 * 1024) // d) and n % (t * 2) == 0:
        t *= 2
    return t


def _rms_fwd(x, g, name, out_dtype=F32):
    n, d = x.shape
    tm = _row_tile(n, d)

    def body(x_ref, g_ref, o_ref):
        xf = x_ref[...]
        r = lax.rsqrt(jnp.mean(xf * xf, axis=-1, keepdims=True) + EPS)
        o_ref[...] = (xf * r * g_ref[...]).astype(out_dtype)

    return pl.pallas_call(
        body, name=name, grid=(n // tm,),
        in_specs=[pl.BlockSpec((tm, d), lambda i: (i, 0)), pl.BlockSpec((1, d), lambda i: (0, 0))],
        out_specs=pl.BlockSpec((tm, d), lambda i: (i, 0)),
        out_shape=jax.ShapeDtypeStruct((n, d), out_dtype), compiler_params=_params(("parallel",)),
    )(x, g.reshape(1, d))


def _rms_bwd(x, g, dy, name, res=None):
    n, d = x.shape
    tm = _row_tile(n, d)
    has_res = res is not None

    def body(*refs):
        x_ref, g_ref, dy_ref = refs[:3]
        r_ref = refs[3] if has_res else None
        dx_ref, dg_ref = refs[-2:]
        xf = x_ref[...]
        r = lax.rsqrt(jnp.mean(xf * xf, axis=-1, keepdims=True) + EPS)
        xh = xf * r
        dyf = dy_ref[...].astype(F32)
        dxh = dyf * g_ref[...]
        dx = r * (dxh - xh * jnp.mean(dxh * xh, axis=-1, keepdims=True))
        if has_res:
            dx = dx + r_ref[...]
        dx_ref[...] = dx

        @pl.when(pl.program_id(0) == 0)
        def _():
            dg_ref[...] = jnp.zeros_like(dg_ref)

        dg_ref[...] += jnp.sum(dyf * xh, axis=0, keepdims=True)

    row = pl.BlockSpec((tm, d), lambda i: (i, 0))
    vec = pl.BlockSpec((1, d), lambda i: (0, 0))
    dx, dg = pl.pallas_call(
        body, name=name, grid=(n // tm,),
        in_specs=[row, vec, row] + ([row] if has_res else []),
        out_specs=[row, vec],
        out_shape=[jax.ShapeDtypeStruct((n, d), F32), jax.ShapeDtypeStruct((1, d), F32)],
        compiler_params=_params(("arbitrary",)),
    )(x, g.reshape(1, d), dy, *((res,) if has_res else ()))
    return dx, dg.reshape(d)


def _silu(x):
    return x * jax.nn.sigmoid(x)


def _gnorm_fwd(o, gate, gate_col, w, name):
    n, width = o.shape
    d = w.shape[0]
    tm = _row_tile(n, d)

    def body(o_ref, g_ref, w_ref, y_ref):
        of = o_ref[...]
        r = lax.rsqrt(jnp.mean(of * of, axis=-1, keepdims=True) + EPS)
        y_ref[...] = of * r * w_ref[...] * _silu(g_ref[...])

    row = pl.BlockSpec((tm, d), lambda i, h: (i, h))
    vec = pl.BlockSpec((1, d), lambda i, h: (0, 0))
    return pl.pallas_call(body, name=name, grid=(n // tm, width // d),
                          in_specs=[row, pl.BlockSpec((tm, d), lambda i, h: (i, gate_col + h)), vec], out_specs=row,
                          out_shape=jax.ShapeDtypeStruct((n, width), F32), compiler_params=_params(("parallel", "parallel")))(o, gate, w.reshape(1, d))


def _gnorm_bwd(o, gate, gate_col, w, dy, dy_col, name):
    n, width = o.shape
    d = w.shape[0]
    tm = _row_tile(n, d)

    def body(o_ref, g_ref, w_ref, dy_ref, do_ref, dgate_ref, dw_ref):
        of, gf, dyf = o_ref[...], g_ref[...], dy_ref[...]
        r = lax.rsqrt(jnp.mean(of * of, axis=-1, keepdims=True) + EPS)
        xh = of * r
        sg = jax.nn.sigmoid(gf)
        sil = gf * sg
        dnorm = dyf * sil
        dgate_ref[...] = dyf * xh * w_ref[...] * (sg * (1.0 + gf * (1.0 - sg)))
        dxh = dnorm * w_ref[...]
        do_ref[...] = r * (dxh - xh * jnp.mean(dxh * xh, axis=-1, keepdims=True))

        @pl.when((pl.program_id(0) == 0) & (pl.program_id(1) == 0))
        def _():
            dw_ref[...] = jnp.zeros_like(dw_ref)

        dw_ref[...] += jnp.sum(dnorm * xh, axis=0, keepdims=True)

    row = pl.BlockSpec((tm, d), lambda i, h: (i, h))
    vec = pl.BlockSpec((1, d), lambda i, h: (0, 0))
    do, dgate, dw = pl.pallas_call(
        body, name=name, grid=(n // tm, width // d),
        in_specs=[row, pl.BlockSpec((tm, d), lambda i, h: (i, gate_col + h)), vec, pl.BlockSpec((tm, d), lambda i, h: (i, dy_col + h))],
        out_specs=[row, row, vec],
        out_shape=[jax.ShapeDtypeStruct((n, width), F32)] * 2 + [jax.ShapeDtypeStruct((1, d), F32)],
        compiler_params=_params(("arbitrary", "arbitrary")),
    )(o, gate, w.reshape(1, d), dy)
    return do, dgate, dw.reshape(d)


def _sigmul_fwd(z, e, res, name):
    n, d = z.shape
    tm = _row_tile(n, d)

    def body(z_ref, e_ref, r_ref, o_ref):
        o_ref[...] = r_ref[...] + jax.nn.sigmoid(z_ref[...]) * e_ref[...]

    row = pl.BlockSpec((tm, d), lambda i: (i, 0))
    return pl.pallas_call(body, name=name, grid=(n // tm,), in_specs=[row] * 3, out_specs=row,
                          out_shape=jax.ShapeDtypeStruct((n, d), F32), compiler_params=_params(("parallel",)))(z, e, res)


def _sigmul_bwd(z, e, dy, name):
    n, d = z.shape
    tm = _row_tile(n, d)

    def body(z_ref, e_ref, dy_ref, dz_ref, de_ref):
        s = jax.nn.sigmoid(z_ref[...])
        dyf = dy_ref[...]
        dz_ref[...] = dyf * e_ref[...] * s * (1.0 - s)
        de_ref[...] = dyf * s

    row = pl.BlockSpec((tm, d), lambda i: (i, 0))
    return pl.pallas_call(body, name=name, grid=(n // tm,), in_specs=[row] * 3, out_specs=[row] * 2,
                          out_shape=[jax.ShapeDtypeStruct((n, d), F32)] * 2, compiler_params=_params(("parallel",)))(z, e, dy)


def _loss_fwd(y, target, name):
    n, d = y.shape
    tm = _row_tile(n, d)

    def body(y_ref, t_ref, l_ref, dy_ref):
        diff = y_ref[...] - t_ref[...]
        dy_ref[...] = diff * (1.0 / d)

        @pl.when(pl.program_id(0) == 0)
        def _():
            l_ref[...] = jnp.zeros_like(l_ref)

        part = jnp.sum(jnp.mean(diff * diff, axis=-1, keepdims=True), axis=0, keepdims=True)
        l_ref[...] += 0.5 * jnp.broadcast_to(part, l_ref.shape)

    row = pl.BlockSpec((tm, d), lambda i: (i, 0))
    vec = pl.BlockSpec((1, LANES), lambda i: (0, 0))
    return pl.pallas_call(body, name=name, grid=(n // tm,), in_specs=[row, row], out_specs=[vec, row],
                          out_shape=[jax.ShapeDtypeStruct((1, LANES), F32), jax.ShapeDtypeStruct((n, d), F32)],
                          compiler_params=_params(("arbitrary",)))(y, target)


SB_BLK = 128
SB_QBLK = 512


def _sb_logits(z, qi, kj, row, col):
    mask = (kj * SB_BLK + col) < (qi * SB_QBLK + row)
    sp = jnp.maximum(z, 0.0) + jnp.log1p(jnp.exp(-jnp.abs(z)))
    lk = jnp.where(mask, -sp, 0.0)
    return mask, lk, z - sp


SB_PAIRS = SB_WIDTH // LANES


def _sb_iotas():
    row = lax.broadcasted_iota(jnp.int32, (2 * SB_QBLK, SB_BLK), 0)
    row = jnp.where(row >= SB_QBLK, row - SB_QBLK, row)
    col = lax.broadcasted_iota(jnp.int32, (2 * SB_QBLK, SB_BLK), 1)
    return row, col, col[:SB_QBLK] < SB_DIM


def _sb_stack(x, first):
    return jnp.concatenate([jnp.where(first, x, 0.0), jnp.where(first, 0.0, x)], axis=0)


def _sb_unstack(y, first):
    return jnp.where(first, y[:SB_QBLK], y[SB_QBLK:])


def _sb_running(x, u):
    m = x.shape[0]
    hi = _bf(x)
    lo = _bf(x - hi.astype(F32))
    c = lax.dot_general(jnp.concatenate([hi, lo], axis=0), u, NN, preferred_element_type=F32)
    return c[:m] + c[m:]


def _sb_spec(s, col):
    return pl.BlockSpec((s, LANES), lambda e, pr: (e, col + pr))


def _sb_fwd(proj, cols, b, s, name):
    nq = s // SB_QBLK
    scale = SB_DIM ** -0.5

    def body(q_ref, k_ref, v_ref, o_ref, lt_ref):
        row, col, first = _sb_iotas()
        u_after = _bf(row[:SB_BLK] > col[:SB_BLK])

        def qloop(qi, _):
            q0 = pl.multiple_of(qi * SB_QBLK, SB_QBLK)
            q2 = _sb_stack(q_ref[pl.ds(q0, SB_QBLK), :], first)
            nkeys = (qi + 1) * (SB_QBLK // SB_BLK)

            def logits(kj):
                k0 = pl.multiple_of(kj * SB_BLK, SB_BLK)
                return _dot(q2, k_ref[pl.ds(k0, SB_BLK), :], NT) * scale

            def kloop(j, st):
                acc, carry, z = st
                kj = nkeys - 1 - j
                k0 = pl.multiple_of(kj * SB_BLK, SB_BLK)
                z_next = logits(jnp.maximum(kj - 1, 0))
                mask, lk, ls = _sb_logits(z, qi, kj, row, col)
                later = carry + _sb_running(lk, u_after)
                w = jnp.where(mask, jnp.exp(ls + later), 0.0)
                acc = acc + _sb_unstack(_dot(w, v_ref[pl.ds(k0, SB_BLK), :]), first)
                return acc, carry + jnp.sum(lk, axis=1, keepdims=True), z_next

            acc, carry, _ = lax.fori_loop(0, nkeys, kloop, (jnp.zeros((SB_QBLK, LANES), F32), jnp.zeros((2 * SB_QBLK, 1), F32), logits(nkeys - 1)))
            o_ref[pl.ds(q0, SB_QBLK), :] = acc
            lt_ref[pl.ds(q0, SB_QBLK), :] = _sb_unstack(jnp.broadcast_to(carry, (2 * SB_QBLK, LANES)), first)
            return 0

        lax.fori_loop(0, nq, qloop, 0)

    out = _sb_spec(s, 0)
    return pl.pallas_call(body, name=name, grid=(b, SB_PAIRS), in_specs=[_sb_spec(s, c) for c in cols], out_specs=[out, out],
                          out_shape=[jax.ShapeDtypeStruct((b * s, SB_WIDTH), F32)] * 2,
                          compiler_params=_params(("parallel", "parallel")))(proj, proj, proj)


def _sb_bwd(proj, cols, ltot, do, do_col, b, s, name):
    nq = s // SB_QBLK
    scale = SB_DIM ** -0.5

    def body(q_ref, k_ref, v_ref, lt_ref, do_ref, dq_ref, dk_ref, dv_ref):
        row, col, first = _sb_iotas()
        u_upto = _bf(row[:SB_BLK] <= col[:SB_BLK])
        u_before = _bf(row[:SB_BLK] < col[:SB_BLK])
        dk_ref[...] = jnp.zeros_like(dk_ref)
        dv_ref[...] = jnp.zeros_like(dv_ref)

        def qloop(qi, _):
            q0 = pl.multiple_of(qi * SB_QBLK, SB_QBLK)
            q2 = _sb_stack(q_ref[pl.ds(q0, SB_QBLK), :], first)
            nkeys = (qi + 1) * (SB_QBLK // SB_BLK)
            do2 = _sb_stack(do_ref[pl.ds(q0, SB_QBLK), :], first)
            lt2 = jnp.min(_sb_stack(lt_ref[pl.ds(q0, SB_QBLK), :], first), axis=1, keepdims=True)

            def logits(kj):
                k0 = pl.multiple_of(kj * SB_BLK, SB_BLK)
                return _dot(q2, k_ref[pl.ds(k0, SB_BLK), :], NT) * scale

            def kloop(kj, st):
                dq, cl, cg, z = st
                k0 = pl.multiple_of(kj * SB_BLK, SB_BLK)
                kb = k_ref[pl.ds(k0, SB_BLK), :]
                vb = v_ref[pl.ds(k0, SB_BLK), :]
                z_next = logits(jnp.minimum(kj + 1, nkeys - 1))
                mask, lk, ls = _sb_logits(z, qi, kj, row, col)
                later = lt2 - (cl + _sb_running(lk, u_upto))
                w = jnp.where(mask, jnp.exp(ls + later), 0.0)
                g = _dot(do2, vb, NT) * w
                dv_ref[pl.ds(k0, SB_BLK), :] += _dot(w, do2, TN)
                g_before = cg + _sb_running(g, u_before)
                sig = jnp.exp(ls)
                dz = jnp.where(mask, g * (1.0 - sig) - sig * g_before, 0.0) * scale
                dq = dq + _sb_unstack(_dot(dz, kb), first)
                dk_ref[pl.ds(k0, SB_BLK), :] += _dot(dz, q2, TN)
                return dq, cl + jnp.sum(lk, axis=1, keepdims=True), cg + jnp.sum(g, axis=1, keepdims=True), z_next

            z1 = jnp.zeros((2 * SB_QBLK, 1), F32)
            dq = lax.fori_loop(0, nkeys, kloop, (jnp.zeros((SB_QBLK, LANES), F32), z1, z1, logits(0)))[0]
            dq_ref[pl.ds(q0, SB_QBLK), :] = dq
            return 0

        lax.fori_loop(0, nq, qloop, 0)

    out = _sb_spec(s, 0)
    return pl.pallas_call(body, name=name, grid=(b, SB_PAIRS),
                          in_specs=[_sb_spec(s, c) for c in cols] + [out, _sb_spec(s, do_col)], out_specs=[out] * 3,
                          out_shape=[jax.ShapeDtypeStruct((b * s, SB_WIDTH), F32)] * 3,
                          compiler_params=_params(("parallel", "parallel")))(proj, proj, proj, ltot, do)


HG_CHUNK = 64
HG_GROUP = 2


def _hg_consts(c, r):
    levels = int(math.log2(c))
    t = np.arange(r)
    same = (t[:, None] // c) == (t[None, :] // c)
    tri = ((t[:, None] >= t[None, :]) & same).astype(np.float32)
    psel = np.zeros((levels, r, r), np.float32)
    masks = np.zeros((levels + 1, r, r), np.float32)
    for l in range(levels):
        n = c >> (l + 1)
        blk = t // (2 * n)
        psel[l, t, blk * 2 * n + n - 1] = 1.0
        upper = (t % (2 * n)) >= n
        masks[l] = (blk[:, None] == blk[None, :]) & upper[:, None] & (~upper)[None, :]
    masks[levels] = np.eye(r)
    psel = psel.reshape(levels * r, r)
    return levels, jnp.asarray(tri), jnp.asarray(psel), jnp.asarray(masks), jnp.asarray(tri.T.copy()), jnp.asarray(psel.T.copy())


def _hg_elem(qv, fv, lbv):
    sig = jax.nn.sigmoid(fv)
    lf = jnp.log(lbv + (1.0 - lbv) * sig)
    kk = (1.0 - lbv) * jax.nn.sigmoid(-fv)
    qf = qv * jax.nn.sigmoid(qv)
    return qf, kk, lf


def _col_bcast(rowvec):
    n = rowvec.shape[1]
    return jnp.transpose(jnp.broadcast_to(rowvec, (n, n)))


def _hg_within(qf, kk, lf, tri, psel, m_ref, c, levels):
    r = qf.shape[0]
    b = _dot_exact_lhs01(tri, lf)
    bls = [b[(g + 1) * c - 1:(g + 1) * c, :] for g in range(r // c)]
    blb = jnp.concatenate([jnp.broadcast_to(bl, (c, bl.shape[1])) for bl in bls], axis=0)
    eb = jnp.exp(b)
    qi = qf * eb
    bsel = _dot_exact_lhs01(psel, b)
    scores = jnp.where(m_ref[levels] > 0, _dot(qf, kk, NT), 0.0)
    lev = []
    for l in range(levels):
        bs = bsel[l * r:(l + 1) * r]
        eq = jnp.exp(jnp.minimum(b - bs, 0.0))
        ek = jnp.exp(jnp.minimum(bs - b, 0.0))
        ql, kl = qf * eq, kk * ek
        scores = scores + jnp.where(m_ref[l] > 0, _dot(ql, kl, NT), 0.0)
        lev.append((eq, ek, ql, kl))
    ebl = jnp.exp(blb - b)
    kd = kk * ebl
    decays = [_col_bcast(jnp.exp(bl)) for bl in bls]
    return eb, qi, scores, lev, ebl, kd, decays


def _hg_fwd(proj, cols, lb, b, s, name):
    nh, d = lb.shape
    bh = b * nh
    c = HG_CHUNK
    nc = s // c
    grp = math.gcd(HG_GROUP, nc)
    r = grp * c
    levels, tri, psel, masks, _, _ = _hg_consts(c, r)

    def body(q_ref, f_ref, i_ref, lb_ref, tri_ref, psel_ref, m_ref, o_ref, st_ref):
        lbv = jnp.broadcast_to(lb_ref[0], (r, d))
        tri_v, psel_v = _bf(tri_ref[...]), _bf(psel_ref[...])

        def group(gi, state):
            r0 = pl.multiple_of(gi * r, r)
            qf, kk, lf = _hg_elem(q_ref[pl.ds(r0, r), :], f_ref[pl.ds(r0, r), :], lbv)
            iv = i_ref[pl.ds(r0, r), :]
            _, qi, scores, _, _, kd, decays = _hg_within(qf, kk, lf, tri_v, psel_v, m_ref, c, levels)
            within = _dot(scores, iv)
            for g in range(grp):
                rows = slice(g * c, (g + 1) * c)
                st_ref[0, gi * grp + g] = state
                o_ref[pl.ds(r0 + g * c, c), :] = _dot(qi[rows], state) + within[rows]
                state = decays[g] * state + _dot(kd[rows], iv[rows], TN)
            return state

        lax.fori_loop(0, nc // grp, group, jnp.zeros((d, d), F32))

    full = lambda a: pl.BlockSpec(a.shape, lambda e, hd: (0,) * a.ndim)
    return pl.pallas_call(
        body, name=name, grid=(b, nh),
        in_specs=[_hg_seq(s, d, cols[0]), _hg_seq(s, d, cols[1]), _hg_seq(s, d, cols[2]),
                  pl.BlockSpec((1, 1, d), lambda e, hd: (hd, 0, 0)), full(tri), full(psel), full(masks)],
        out_specs=[_hg_seq(s, d, 0), pl.BlockSpec((1, nc, d, d), lambda e, hd: (e * nh + hd, 0, 0, 0))],
        out_shape=[jax.ShapeDtypeStruct((b * s, nh * d), F32), jax.ShapeDtypeStruct((bh, nc, d, d), F32)],
        compiler_params=_params(("parallel", "parallel")),
    )(proj, proj, proj, lb.reshape(nh, 1, d), tri, psel, masks)


def _hg_seq(s, d, col):
    return pl.BlockSpec((s, d), lambda e, hd: (e, col + hd))


def _hg_bwd(proj, cols, lb, states, do, b, s, name):
    nh, d = lb.shape
    bh = b * nh
    c = HG_CHUNK
    nc = s // c
    grp = math.gcd(HG_GROUP, nc)
    r = grp * c
    levels, tri, psel, masks, tri_t, psel_t = _hg_consts(c, r)

    def body(q_ref, f_ref, i_ref, lb_ref, st_ref, do_ref, tri_ref, psel_ref, m_ref, trit_ref, pselt_ref,
             dq_ref, df_ref, di_ref, dlb_ref):
        lbv = jnp.broadcast_to(lb_ref[0], (r, d))
        tri_v, psel_v = _bf(tri_ref[...]), _bf(psel_ref[...])
        trit_v, pselt_v = _bf(trit_ref[...]), _bf(pselt_ref[...])
        row_in_chunk = lax.broadcasted_iota(jnp.int32, (c, d), 0)

        def chunk(step, carry):
            ds_out, dlb = carry
            gi = nc // grp - 1 - step
            r0 = pl.multiple_of(gi * r, r)
            qv, fv, iv = q_ref[pl.ds(r0, r), :], f_ref[pl.ds(r0, r), :], i_ref[pl.ds(r0, r), :]
            dov = do_ref[pl.ds(r0, r), :]
            (qf, kk, lf), elem_vjp = jax.vjp(_hg_elem, qv, fv, lbv)
            eb, qi, scores, lev, ebl, kd, decays = _hg_within(qf, kk, lf, tri_v, psel_v, m_ref, c, levels)

            dscores = _dot(dov, iv, NT)
            di_within = _dot(scores, dov, TN)
            dqi_parts, dkd_parts, dbl_parts = [None] * grp, [None] * grp, [None] * grp
            for g in reversed(range(grp)):
                rows = slice(g * c, (g + 1) * c)
                state = st_ref[0, gi * grp + g]
                di_ref[pl.ds(r0 + g * c, c), :] = di_within[rows] + _dot(kd[rows], ds_out)
                dqi_parts[g] = _dot(dov[rows], state, NT)
                dkd_parts[g] = _dot(iv[rows], ds_out, NT)
                dbl = (jnp.sum(dkd_parts[g] * kd[rows], axis=0, keepdims=True)
                       + _col_bcast_t(jnp.sum(ds_out * decays[g] * state, axis=1, keepdims=True)))
                dbl_parts[g] = jnp.where(row_in_chunk == c - 1, dbl, 0.0)
                ds_out = decays[g] * ds_out + _dot(qi[rows], dov[rows], TN)
            ds_in = ds_out
            dqi = jnp.concatenate(dqi_parts, axis=0)
            dkd = jnp.concatenate(dkd_parts, axis=0)
            dqf = dqi * eb
            dkk = dkd * ebl
            db = dqi * qi - dkd * kd + jnp.concatenate(dbl_parts, axis=0)
            dsd = jnp.where(m_ref[levels] > 0, dscores, 0.0)
            dqf = dqf + _dot(dsd, kk)
            dkk = dkk + _dot(dsd, qf, TN)
            dbsel = []
            for l in range(levels):
                eq, ek, ql, kl = lev[l]
                dsl = jnp.where(m_ref[l] > 0, dscores, 0.0)
                dql = _dot(dsl, kl)
                dkl = _dot(dsl, ql, TN)
                dqf = dqf + dql * eq
                dkk = dkk + dkl * ek
                diff = dql * ql - dkl * kl
                db = db + diff
                dbsel.append(-diff)
            db = db + _dot_exact_lhs01(pselt_v, jnp.concatenate(dbsel, axis=0))
            dlf = _dot_exact_lhs01(trit_v, db)
            dq, df, dlb_c = elem_vjp((dqf, dkk, dlf))
            dq_ref[pl.ds(r0, r), :] = dq
            df_ref[pl.ds(r0, r), :] = df
            return ds_in, dlb + jnp.sum(dlb_c, axis=0, keepdims=True)

        _, dlb = lax.fori_loop(0, nc // grp, chunk, (jnp.zeros((d, d), F32), jnp.zeros((1, d), F32)))
        dlb_ref[0] = dlb

    seq = _hg_seq(s, d, 0)
    full = lambda a: pl.BlockSpec(a.shape, lambda e, hd: (0,) * a.ndim)
    return pl.pallas_call(
        body, name=name, grid=(b, nh),
        in_specs=[_hg_seq(s, d, cols[0]), _hg_seq(s, d, cols[1]), _hg_seq(s, d, cols[2]),
                  pl.BlockSpec((1, 1, d), lambda e, hd: (hd, 0, 0)),
                  pl.BlockSpec((1, nc, d, d), lambda e, hd: (e * nh + hd, 0, 0, 0)), seq,
                  full(tri), full(psel), full(masks), full(tri_t), full(psel_t)],
        out_specs=[seq, seq, seq, pl.BlockSpec((1, 1, d), lambda e, hd: (e * nh + hd, 0, 0))],
        out_shape=[jax.ShapeDtypeStruct((b * s, nh * d), F32)] * 3 + [jax.ShapeDtypeStruct((bh, 1, d), F32)],
        compiler_params=_params(("parallel", "parallel")),
    )(proj, proj, proj, lb.reshape(nh, 1, d), states, do, tri, psel, masks, tri_t, psel_t)


def _col_bcast_t(colvec):
    n = colvec.shape[0]
    return jnp.transpose(jnp.broadcast_to(colvec, (n, n)))[0:1, :]


def _swa_probs(qg, kb, bias, sink, valid, scale):
    logits = _dot(qg, kb, NT) * scale + bias
    logits = jnp.where(valid, logits, -jnp.inf)
    m = jnp.maximum(jnp.max(logits, axis=-1, keepdims=True), sink)
    e = jnp.exp(logits - m)
    es = jnp.exp(sink - m)
    den = jnp.sum(e, axis=-1, keepdims=True) + es
    return e / den, es / den


def _swa_valid(n):
    w = WINDOW
    row = lax.broadcasted_iota(jnp.int32, (w, 2 * w), 0)
    col = lax.broadcasted_iota(jnp.int32, (w, 2 * w), 1)
    dist = row + w - col
    return (dist >= 0) & (dist < w) & ((col >= w) | (n > 0))


def _swa_specs(b, g, s, d):
    w = WINDOW
    q_spec = pl.BlockSpec((1, 1, g, w, d), lambda h, bi, n: (bi, h, 0, n, 0))
    kp_spec = pl.BlockSpec((1, 1, w, d), lambda h, bi, n: (bi, h, jnp.maximum(n - 1, 0), 0))
    kc_spec = pl.BlockSpec((1, 1, w, d), lambda h, bi, n: (bi, h, n, 0))
    bias_spec = pl.BlockSpec((1, g, w, 2 * w), lambda h, bi, n: (h, 0, 0, 0))
    sink_spec = pl.BlockSpec(memory_space=pltpu.SMEM)
    return q_spec, kp_spec, kc_spec, bias_spec, sink_spec


def _swa_fwd(q, k, v, sinks, bias, name):
    b, kvh, g, s, d = q.shape
    w = WINDOW
    scale = d ** -0.5
    q_spec, kp_spec, kc_spec, bias_spec, sink_spec = _swa_specs(b, g, s, d)

    def body(q_ref, kp_ref, kc_ref, vp_ref, vc_ref, bias_ref, sink_ref, o_ref):
        h, n = pl.program_id(0), pl.program_id(2)
        valid = _swa_valid(n)
        kb = jnp.concatenate([kp_ref[0, 0], kc_ref[0, 0]], axis=0)
        vb = jnp.concatenate([vp_ref[0, 0], vc_ref[0, 0]], axis=0)
        for gi in range(g):
            p, _ = _swa_probs(q_ref[0, 0, gi], kb, bias_ref[0, gi], sink_ref[h * g + gi], valid, scale)
            o_ref[0, 0, gi] = _dot(p, vb)

    return pl.pallas_call(
        body, name=name, grid=(kvh, b, s // w),
        in_specs=[q_spec, kp_spec, kc_spec, kp_spec, kc_spec, bias_spec, sink_spec], out_specs=q_spec,
        out_shape=jax.ShapeDtypeStruct(q.shape, F32), compiler_params=_params(("parallel", "parallel", "arbitrary")),
    )(q, k, k, v, v, bias, sinks)


def _swa_bwd(q, k, v, sinks, bias, do, name):
    b, kvh, g, s, d = q.shape
    w = WINDOW
    scale = d ** -0.5
    q_spec, kp_spec, kc_spec, bias_spec, sink_spec = _swa_specs(b, g, s, d)
    kv_acc = pl.BlockSpec((1, 1, s, d), lambda h, bi, n: (bi, h, 0, 0))
    dsink_spec = pl.BlockSpec((1, g, LANES), lambda h, bi, n: (h, 0, 0))

    def body(q_ref, kp_ref, kc_ref, vp_ref, vc_ref, bias_ref, sink_ref, do_ref, dq_ref, dk_ref, dv_ref, dbias_ref, dsink_ref):
        h, bi, n = pl.program_id(0), pl.program_id(1), pl.program_id(2)
        valid = _swa_valid(n)
        kb = jnp.concatenate([kp_ref[0, 0], kc_ref[0, 0]], axis=0)
        vb = jnp.concatenate([vp_ref[0, 0], vc_ref[0, 0]], axis=0)

        @pl.when(n == 0)
        def _():
            dk_ref[...] = jnp.zeros_like(dk_ref)
            dv_ref[...] = jnp.zeros_like(dv_ref)

        @pl.when((n == 0) & (bi == 0))
        def _():
            dbias_ref[...] = jnp.zeros_like(dbias_ref)
            dsink_ref[...] = jnp.zeros_like(dsink_ref)

        dkb = jnp.zeros((2 * w, d), F32)
        dvb = jnp.zeros((2 * w, d), F32)
        for gi in range(g):
            qg, dog = q_ref[0, 0, gi], do_ref[0, 0, gi]
            p, ps = _swa_probs(qg, kb, bias_ref[0, gi], sink_ref[h * g + gi], valid, scale)
            dp = _dot(dog, vb, NT)
            delta = jnp.sum(p * dp, axis=-1, keepdims=True)
            dl = p * (dp - delta)
            dq_ref[0, 0, gi] = _dot(dl, kb) * scale
            dkb = dkb + _dot(dl, qg, TN) * scale
            dvb = dvb + _dot(p, dog, TN)
            dbias_ref[0, gi] += dl
            dsink_ref[0, gi:gi + 1, :] += jnp.broadcast_to(jnp.sum(-ps * delta, axis=0, keepdims=True), (1, LANES))

        c0 = pl.multiple_of(n * w, w)
        dk_ref[0, 0, pl.ds(c0, w), :] += dkb[w:]
        dv_ref[0, 0, pl.ds(c0, w), :] += dvb[w:]

        @pl.when(n > 0)
        def _():
            p0 = pl.multiple_of((n - 1) * w, w)
            dk_ref[0, 0, pl.ds(p0, w), :] += dkb[:w]
            dv_ref[0, 0, pl.ds(p0, w), :] += dvb[:w]

    return pl.pallas_call(
        body, name=name, grid=(kvh, b, s // w),
        in_specs=[q_spec, kp_spec, kc_spec, kp_spec, kc_spec, bias_spec, sink_spec, q_spec],
        out_specs=[q_spec, kv_acc, kv_acc, bias_spec, dsink_spec],
        out_shape=[jax.ShapeDtypeStruct(q.shape, F32), jax.ShapeDtypeStruct(k.shape, F32), jax.ShapeDtypeStruct(k.shape, F32),
                   jax.ShapeDtypeStruct(bias.shape, F32), jax.ShapeDtypeStruct((kvh, g, LANES), F32)],
        compiler_params=_params(("arbitrary", "arbitrary", "arbitrary")),
    )(q, k, k, v, v, bias, sinks, do)


def _t5_bias(rel_bias):
    t = np.arange(WINDOW)[:, None]
    s = np.arange(2 * WINDOW)[None, :]
    dist = t + WINDOW - s
    max_exact = N_BUCKETS // 2
    large = max_exact + (np.log(np.maximum(dist, max_exact) / max_exact) / math.log(MAX_DISTANCE / max_exact)
                         * (N_BUCKETS - max_exact)).astype(np.int32)
    large = np.minimum(large, N_BUCKETS - 1)
    bucket = np.where(dist < max_exact, np.maximum(dist, 0), large).astype(np.int32)
    onehot = jnp.asarray(np.eye(N_BUCKETS, dtype=np.float32)[bucket])
    bias = jnp.einsum("tsb,bh->hts", onehot, rel_bias.astype(F32), precision=lax.Precision.HIGHEST)
    return bias.reshape(SW_KV_HEADS, SW_GROUP, WINDOW, 2 * WINDOW)


CONV_W = 3


def _shift_down(x, k):
    row = lax.broadcasted_iota(jnp.int32, x.shape, 0)
    return jnp.where(row >= k, pltpu.roll(x, k, axis=0), 0.0)


def _shift_up(x, k):
    n = x.shape[0]
    row = lax.broadcasted_iota(jnp.int32, x.shape, 0)
    return jnp.where(row < n - k, pltpu.roll(x, n - k, axis=0), 0.0)


def _conv3(u, w, bvec):
    return w[0:1] * _shift_down(u, 2) + w[1:2] * _shift_down(u, 1) + w[2:3] * u + bvec


def _convglu_fwd(u, w, bvec, name, out_dtype=F32):
    b, s, f2 = u.shape
    f = f2 // 2
    tc = _pick(f, 256)
    nt = f // tc

    def body(ug_ref, uu_ref, wg_ref, wu_ref, bg_ref, bu_ref, o_ref):
        cg = _conv3(ug_ref[0], wg_ref[...], bg_ref[...])
        cu = _conv3(uu_ref[0], wu_ref[...], bu_ref[...])
        o_ref[0] = (_silu(cg) * cu).astype(out_dtype)

    ug = pl.BlockSpec((1, s, tc), lambda j, bi: (bi, 0, j))
    uu = pl.BlockSpec((1, s, tc), lambda j, bi: (bi, 0, j + nt))
    wg = pl.BlockSpec((CONV_W, tc), lambda j, bi: (0, j))
    wu = pl.BlockSpec((CONV_W, tc), lambda j, bi: (0, j + nt))
    bg = pl.BlockSpec((1, tc), lambda j, bi: (0, j))
    bu = pl.BlockSpec((1, tc), lambda j, bi: (0, j + nt))
    bv = bvec.reshape(1, f2)
    return pl.pallas_call(body, name=name, grid=(nt, b), in_specs=[ug, uu, wg, wu, bg, bu], out_specs=ug,
                          out_shape=jax.ShapeDtypeStruct((b, s, f), out_dtype),
                          compiler_params=_params(("parallel", "parallel")))(u, u, w, w, bv, bv)


def _convglu_bwd(u, w, bvec, dact, name):
    b, s, f2 = u.shape
    f = f2 // 2
    tc = LANES
    nt = f // tc

    def taps(dc, uv):
        rows = [jnp.sum(dc * _shift_down(uv, 2), axis=0, keepdims=True), jnp.sum(dc * _shift_down(uv, 1), axis=0, keepdims=True),
                jnp.sum(dc * uv, axis=0, keepdims=True), jnp.sum(dc, axis=0, keepdims=True)]
        return jnp.concatenate(rows + [jnp.zeros((4, tc), F32)], axis=0)

    def back(dc, wv):
        return wv[2:3] * dc + wv[1:2] * _shift_up(dc, 1) + wv[0:1] * _shift_up(dc, 2)

    def body(ug_ref, uu_ref, wg_ref, wu_ref, bg_ref, bu_ref, da_ref, dug_ref, duu_ref, dwg_ref, dwu_ref):
        ugv, uuv, da = ug_ref[0], uu_ref[0], da_ref[0]
        cg = _conv3(ugv, wg_ref[...], bg_ref[...])
        cu = _conv3(uuv, wu_ref[...], bu_ref[...])
        sg = jax.nn.sigmoid(cg)
        dcu = da * (cg * sg)
        dcg = da * cu * (sg * (1.0 + cg * (1.0 - sg)))
        dug_ref[0] = back(dcg, wg_ref[...])
        duu_ref[0] = back(dcu, wu_ref[...])

        @pl.when(pl.program_id(1) == 0)
        def _():
            dwg_ref[...] = jnp.zeros_like(dwg_ref)
            dwu_ref[...] = jnp.zeros_like(dwu_ref)

        dwg_ref[...] += taps(dcg, ugv)
        dwu_ref[...] += taps(dcu, uuv)

    ug = pl.BlockSpec((1, s, tc), lambda j, bi: (bi, 0, j))
    uu = pl.BlockSpec((1, s, tc), lambda j, bi: (bi, 0, j + nt))
    wg = pl.BlockSpec((CONV_W, tc), lambda j, bi: (0, j))
    wu = pl.BlockSpec((CONV_W, tc), lambda j, bi: (0, j + nt))
    bg = pl.BlockSpec((1, tc), lambda j, bi: (0, j))
    bu = pl.BlockSpec((1, tc), lambda j, bi: (0, j + nt))
    acc = pl.BlockSpec((8, tc), lambda j, bi: (0, j))
    bv = bvec.reshape(1, f2)
    return pl.pallas_call(
        body, name=name, grid=(nt, b), in_specs=[ug, uu, wg, wu, bg, bu, ug], out_specs=[ug, ug, acc, acc],
        out_shape=[jax.ShapeDtypeStruct((b, s, f), F32)] * 2 + [jax.ShapeDtypeStruct((8, f), F32)] * 2,
        compiler_params=_params(("parallel", "arbitrary")),
    )(u, u, w, w, bv, bv, dact)


def _adamw(w, g, m, v, name):
    r, c = w.shape
    tr = _row_tile(r, c)
    c1 = 1.0 - ADAM_B1 ** ADAM_STEP
    c2 = 1.0 - ADAM_B2 ** ADAM_STEP

    def body(w_ref, g_ref, m_ref, v_ref, d_ref, mo_ref, vo_ref):
        gv = g_ref[...]
        mn = ADAM_B1 * m_ref[...] + (1.0 - ADAM_B1) * gv
        vn = ADAM_B2 * v_ref[...] + (1.0 - ADAM_B2) * (gv * gv)
        d_ref[...] = -ADAM_LR * ((mn / c1) / (jnp.sqrt(vn / c2) + ADAM_EPS) + ADAM_WD * w_ref[...])
        mo_ref[...] = mn
        vo_ref[...] = vn

    blk = pl.BlockSpec((tr, c), lambda i: (i, 0))
    return pl.pallas_call(body, name=name, grid=(r // tr,), in_specs=[blk] * 4, out_specs=[blk] * 3,
                          out_shape=[jax.ShapeDtypeStruct((r, c), F32)] * 3, compiler_params=_params(("parallel",)))(w, g, m, v)


MESH = pl.DeviceIdType.MESH
ANY = pl.BlockSpec(memory_space=pl.ANY)


def _position():
    return lax.axis_index("x"), lax.axis_index("y"), lax.axis_index("c")


def _all_gather(x, name):
    r, c = x.shape

    def body(x_ref, out_ref, send_sems, recv_sems, local_sem):
        mx, my, mc = _position()
        me, sibling = (mx, my, mc), (mx, my, 1 - mc)
        chips = [(1 - mx, my), (mx, 1 - my), (1 - mx, 1 - my)]

        def slot(px, py, pc):
            return out_ref.at[4 * px + 2 * py + pc]

        def copy(k, block, to, src=None):
            return pltpu.make_async_remote_copy(
                src_ref=slot(*block) if src is None else src, dst_ref=slot(*block),
                send_sem=send_sems.at[k], recv_sem=recv_sems.at[k], device_id=to, device_id_type=MESH)

        mine = pltpu.make_async_copy(x_ref, slot(*me), local_sem.at[0])
        mine.start()
        first = [copy(0, me, sibling, src=x_ref)]
        first += [copy(1 + j, me, (*chip, mc), src=x_ref) for j, chip in enumerate(chips)]
        for cp in first:
            cp.start()
        passed = [copy(4 + j, (*chip, mc), sibling) for j, chip in enumerate(chips)]
        for j, chip in enumerate(chips):
            copy(1 + j, (*chip, mc), me).wait_recv()
            passed[j].start()
        copy(0, sibling, me).wait_recv()
        for j, chip in enumerate(chips):
            copy(4 + j, (*chip, 1 - mc), me).wait_recv()
        for cp in first + passed:
            cp.wait_send()
        mine.wait()

    return pl.pallas_call(
        body, name=name, out_shape=jax.ShapeDtypeStruct((N_DEV, r, c), x.dtype), in_specs=[ANY], out_specs=ANY,
        scratch_shapes=[pltpu.SemaphoreType.DMA((7,)), pltpu.SemaphoreType.DMA((7,)), pltpu.SemaphoreType.DMA((1,))],
    )(x)


def _dev_rows(ref, dev, a):
    return ref.at[:, pl.ds(pl.multiple_of(dev * a, 16), a), :]


def _all_gather_rows(shards, name):
    nt = len(shards)

    def body(*refs):
        x_refs, out_refs = refs[:nt], refs[nt:2 * nt]
        send_sems, recv_sems, local_sems = refs[2 * nt:]
        mx, my, mc = _position()
        me, sibling = (mx, my, mc), (mx, my, 1 - mc)
        chips = [(1 - mx, my), (mx, 1 - my), (1 - mx, 1 - my)]

        def slot(t, px, py, pc):
            return _dev_rows(out_refs[t], 4 * px + 2 * py + pc, shards[t].shape[1])

        def copy(t, k, block, to, src=None):
            return pltpu.make_async_remote_copy(
                src_ref=slot(t, *block) if src is None else src, dst_ref=slot(t, *block),
                send_sem=send_sems.at[7 * t + k], recv_sem=recv_sems.at[7 * t + k], device_id=to, device_id_type=MESH)

        mine = [pltpu.make_async_copy(x_refs[t], slot(t, *me), local_sems.at[t]) for t in range(nt)]
        first = [copy(t, 0, me, sibling, src=x_refs[t]) for t in range(nt)]
        first += [copy(t, 1 + j, me, (*chip, mc), src=x_refs[t]) for j, chip in enumerate(chips) for t in range(nt)]
        for cp in mine + first:
            cp.start()
        passed = []
        for j, chip in enumerate(chips):
            for t in range(nt):
                copy(t, 1 + j, (*chip, mc), me).wait_recv()
                fwd = copy(t, 4 + j, (*chip, mc), sibling)
                fwd.start()
                passed.append(fwd)
        for t in range(nt):
            copy(t, 0, sibling, me).wait_recv()
        for j, chip in enumerate(chips):
            for t in range(nt):
                copy(t, 4 + j, (*chip, 1 - mc), me).wait_recv()
        for cp in first + passed:
            cp.wait_send()
        for cp in mine:
            cp.wait()

    out_shape = [jax.ShapeDtypeStruct((x.shape[0], N_DEV * x.shape[1], x.shape[2]), x.dtype) for x in shards]
    return pl.pallas_call(
        body, name=name, out_shape=out_shape, in_specs=[ANY] * nt, out_specs=[ANY] * nt,
        scratch_shapes=[pltpu.SemaphoreType.DMA((7 * nt,)), pltpu.SemaphoreType.DMA((7 * nt,)), pltpu.SemaphoreType.DMA((nt,))],
    )(*shards)


def _rs_pair(gs, name):
    nt = len(gs)

    def body(*refs):
        g_refs, a_refs = refs[:nt], refs[nt:2 * nt]
        send_sems, recv_sems = refs[2 * nt:]
        mx, my, mc = _position()
        copies = [pltpu.make_async_remote_copy(
            src_ref=_dev_rows(g_refs[t], 2 * j + 1 - mc, gs[t].shape[1] // N_DEV), dst_ref=a_refs[t].at[j],
            send_sem=send_sems.at[4 * t + j], recv_sem=recv_sems.at[4 * t + j],
            device_id=(mx, my, 1 - mc), device_id_type=MESH) for t in range(nt) for j in range(4)]
        for cp in copies:
            cp.start()
        for cp in copies:
            cp.wait()

    out_shape = [jax.ShapeDtypeStruct((4, g.shape[0], g.shape[1] // N_DEV, g.shape[2]), g.dtype) for g in gs]
    return pl.pallas_call(
        body, name=name, out_shape=out_shape, in_specs=[ANY] * nt, out_specs=[ANY] * nt,
        scratch_shapes=[pltpu.SemaphoreType.DMA((4 * nt,)), pltpu.SemaphoreType.DMA((4 * nt,))],
    )(*gs)


def _rs_chips(ps, name):
    nt = len(ps)

    def body(*refs):
        p_refs, b_refs = refs[:nt], refs[nt:2 * nt]
        send_sems, recv_sems = refs[2 * nt:]
        mx, my, mc = _position()
        chips = [(1 - mx, my), (mx, 1 - my), (1 - mx, 1 - my)]
        copies = [pltpu.make_async_remote_copy(
            src_ref=p_refs[t].at[2 * cx + cy], dst_ref=b_refs[t].at[k],
            send_sem=send_sems.at[3 * t + k], recv_sem=recv_sems.at[3 * t + k],
            device_id=(cx, cy, mc), device_id_type=MESH) for t in range(nt) for k, (cx, cy) in enumerate(chips)]
        for cp in copies:
            cp.start()
        for cp in copies:
            cp.wait()

    out_shape = [jax.ShapeDtypeStruct((3,) + p.shape[1:], p.dtype) for p in ps]
    return pl.pallas_call(
        body, name=name, out_shape=out_shape, in_specs=[ANY] * nt, out_specs=[ANY] * nt,
        scratch_shapes=[pltpu.SemaphoreType.DMA((3 * nt,)), pltpu.SemaphoreType.DMA((3 * nt,))],
    )(*ps)


def _div_tile(a, b):
    best = 16
    for t in range(16, a + 1, 16):
        if a % t == 0 and t * b * 4 <= 2 * 1024 * 1024:
            best = t
    return best


def _rs_add_pair(g, a, core, name):
    l, a8, b = g.shape
    rows = a8 // N_DEV
    ta = _div_tile(rows, b)

    def body(core_ref, g_ref, a_ref, p_ref):
        p_ref[...] = (g_ref[...] + a_ref[...]).astype(BF16)

    grid_spec = pltpu.PrefetchScalarGridSpec(
        num_scalar_prefetch=1, grid=(4, l, rows // ta),
        in_specs=[pl.BlockSpec((1, 1, ta, b), lambda j, li, i, core_ref: (li, 2 * j + core_ref[0], i, 0)),
                  pl.BlockSpec((1, 1, ta, b), lambda j, li, i, core_ref: (j, li, i, 0))],
        out_specs=pl.BlockSpec((1, 1, ta, b), lambda j, li, i, core_ref: (j, li, i, 0)))
    return pl.pallas_call(body, name=name, grid_spec=grid_spec, out_shape=jax.ShapeDtypeStruct((4, l, rows, b), BF16),
                          compiler_params=_params(("parallel", "parallel", "parallel")))(core, g.reshape(l, N_DEV, rows, b), a)


def _rs_final(g, a, bsum, where, name):
    l, a8, b = g.shape
    rows = a8 // N_DEV
    ta = _div_tile(rows, b)

    def body(where_ref, g_ref, a_ref, b_ref, o_ref):
        own = g_ref[0, 0] + a_ref[0, 0]
        o_ref[0] = ((own + b_ref[0, 0].astype(F32)) + b_ref[1, 0].astype(F32)) + b_ref[2, 0].astype(F32)

    grid_spec = pltpu.PrefetchScalarGridSpec(
        num_scalar_prefetch=1, grid=(l, rows // ta),
        in_specs=[pl.BlockSpec((1, 1, ta, b), lambda li, i, w_ref: (li, w_ref[0], i, 0)),
                  pl.BlockSpec((1, 1, ta, b), lambda li, i, w_ref: (w_ref[1], li, i, 0)),
                  pl.BlockSpec((3, 1, ta, b), lambda li, i, w_ref: (0, li, i, 0))],
        out_specs=pl.BlockSpec((1, ta, b), lambda li, i, w_ref: (li, i, 0)))
    return pl.pallas_call(body, name=name, grid_spec=grid_spec, out_shape=jax.ShapeDtypeStruct((l, rows, b), F32),
                          compiler_params=_params(("parallel", "parallel")))(where, g.reshape(l, N_DEV, rows, b), a, bsum)


def _sum_devices(x, name):
    _, r, c = x.shape

    def body(x_ref, o_ref):
        acc = x_ref[0]
        for d in range(1, N_DEV):
            acc = acc + x_ref[d]
        o_ref[...] = acc

    return pl.pallas_call(body, name=name, out_shape=jax.ShapeDtypeStruct((r, c), F32))(x)


BIG = (("ab_w_in", "col"), ("ab_w_out", "row"), ("c_w_in", "col"), ("c_w_out", "row"),
       ("ffn_up", "col"), ("ffn_down", "row"), ("ple_gate", "row"), ("ple_proj", "col"))
KIND = dict(BIG)


def _row_block(shard, kind):
    return shard.transpose(0, 2, 1) if kind == "col" else shard


def _pad_rows(flat):
    pad = -flat.shape[0] % (8 * LANES)
    return jnp.pad(flat, (0, pad)).reshape(-1, LANES)


def _heads_out(x, b, s, nh, d):
    return x.reshape(b, s, nh, d).transpose(0, 2, 1, 3).reshape(b * nh, s, d)


def _heads_in(x, b, s, nh, d):
    return x.reshape(b, nh, s, d).transpose(0, 2, 1, 3).reshape(b * s, nh * d)


def _lower_bounds(logits):
    c = jnp.cumsum(jax.nn.softmax(logits.astype(F32), axis=0), axis=0)
    return c - c[0]


SB_COLS = tuple(k * SB_WIDTH // LANES for k in range(3))
HG_COLS = tuple((3 * SB_WIDTH + k * HG_W) // LANES for k in range(3))
HG_GATE_COL = (3 * SB_WIDTH + 3 * HG_W) // LANES
HG_OUT_COL = SB_WIDTH // LANES


def _forward_backward(x, p, target, W, P):
    b, s, dm = x.shape
    n = b * s
    h = x.reshape(n, dm)
    lbs, lb_vjp = jax.vjp(_lower_bounds, P["hg_lb_logits"])
    bias, bias_vjp = jax.vjp(_t5_bias, P["rel_bias"])
    saved = []
    gw = {name: lax.empty(W[name].shape, F32) for name, _ in BIG}

    def times_w(a, name, l, tag, res=None):
        return _mm(a, W[name], "nt" if KIND[name] == "col" else "nn", tag, res=res, layer=l)

    def times_wt(dy, name, l, tag):
        return _mm(dy, W[name], "nn" if KIND[name] == "col" else "nt", tag, layer=l)

    def grad_w(a, dy, name, l, tag):
        lhs, rhs = (dy, a) if KIND[name] == "col" else (a, dy)
        gw[name] = _mm(lhs, rhs, "tn", tag, into=(gw[name], l, 0))

    for i in range(DEPTH):
        j = i // 2
        r = {"h0": h}
        hn = _rms_fwd(h, P["mix_norm"][i], f"mix_norm_f{i}", BF16)
        r["hn"] = hn
        if i % 2 == 0:
            proj = times_w(hn, "ab_w_in", j, f"ab_in_f{i}")
            oa, lta = _sb_fwd(proj, SB_COLS, b, s, f"sb_f{i}")
            ob, st = _hg_fwd(proj, HG_COLS, lbs[j].reshape(HG_HEADS, HG_DK), b, s, f"hg_f{i}")
            obg = _gnorm_fwd(ob, proj, HG_GATE_COL, P["hg_out_norm"][j], f"hg_norm_f{i}")
            cat = jnp.concatenate([oa, obg], axis=1).astype(BF16)
            h = times_w(cat, "ab_w_out", j, f"ab_out_f{i}", res=h)
            r.update(lta=lta, proj=proj, ob=ob, st=st, cat=cat)
        else:
            proj = times_w(hn, "c_w_in", j, f"c_in_f{i}")
            nq = SW_HEADS * SW_DIM
            nkv = SW_KV_HEADS * SW_DIM
            q = _heads_out(proj[:, :nq], b, s, SW_HEADS, SW_DIM).reshape(-1, SW_DIM)
            k = _heads_out(proj[:, nq:nq + nkv], b, s, SW_KV_HEADS, SW_DIM).reshape(-1, SW_DIM)
            v = _heads_out(proj[:, nq + nkv:], b, s, SW_KV_HEADS, SW_DIM).reshape(b, SW_KV_HEADS, s, SW_DIM)
            qn = _rms_fwd(q, P["q_norm"][j], f"q_norm_f{i}").reshape(b, SW_KV_HEADS, SW_GROUP, s, SW_DIM)
            kn = _rms_fwd(k, P["k_norm"][j], f"k_norm_f{i}").reshape(b, SW_KV_HEADS, s, SW_DIM)
            o = _swa_fwd(qn, kn, v, P["sinks"][j], bias, f"swa_f{i}")
            o2 = _heads_in(o.reshape(b * SW_HEADS, s, SW_DIM), b, s, SW_HEADS, SW_DIM).astype(BF16)
            h = times_w(o2, "c_w_out", j, f"c_out_f{i}", res=h)
            r.update(q=q, k=k, v=v, qn=qn, kn=kn, o2=o2)
        r["h1"] = h
        hn2 = _rms_fwd(h, P["ffn_norm"][i], f"ffn_norm_f{i}", BF16)
        u = times_w(hn2, "ffn_up", i, f"ffn_up_f{i}").reshape(b, s, 2 * D_FF)
        act = _convglu_fwd(u, W["ffn_conv"][i], P["ffn_conv_b"][i], f"conv_f{i}", BF16).reshape(n, D_FF)
        h = times_w(act, "ffn_down", i, f"ffn_down_f{i}", res=h)
        r.update(hn2=hn2, u=u, act=act, h2=h)
        hn3 = _rms_fwd(h, P["ple_norm"][i], f"ple_norm_f{i}", BF16)
        z = times_w(hn3, "ple_gate", i, f"ple_gate_f{i}")
        pi = p[i].reshape(n, PLE_DIM)
        e = times_w(pi, "ple_proj", i, f"ple_proj_f{i}")
        h = _sigmul_fwd(z, e, h, f"ple_f{i}")
        r.update(hn3=hn3, z=z, e=e, pi=pi)
        saved.append(r)

    loss, dh = _loss_fwd(h, target.reshape(n, dm), "loss")

    gconv = [None] * DEPTH
    gp = {name: [None] * P[name].shape[0] for name in ("mix_norm", "hg_out_norm", "q_norm", "k_norm", "sinks",
                                                        "ffn_norm", "ffn_conv_b", "ple_norm")}
    dlbs = [None] * (DEPTH // 2)
    dbias = jnp.zeros_like(bias)

    for i in reversed(range(DEPTH)):
        j = i // 2
        r = saved[i]
        dz, de = _sigmul_bwd(r["z"], r["e"], dh, f"ple_b{i}")
        grad_w(r["pi"], de, "ple_proj", i, f"ple_proj_g{i}")
        grad_w(r["hn3"], dz, "ple_gate", i, f"ple_gate_g{i}")
        dhn3 = times_wt(dz, "ple_gate", i, f"ple_gate_b{i}")
        dh, gp["ple_norm"][i] = _rms_bwd(r["h2"], P["ple_norm"][i], dhn3, f"ple_norm_b{i}", res=dh)

        dact = times_wt(dh, "ffn_down", i, f"ffn_down_b{i}").reshape(b, s, D_FF)
        grad_w(r["act"], dh, "ffn_down", i, f"ffn_down_g{i}")
        dug, duu, ag, au = _convglu_bwd(r["u"], W["ffn_conv"][i], P["ffn_conv_b"][i], dact, f"conv_b{i}")
        gconv[i] = jnp.concatenate([ag[:CONV_W], au[:CONV_W]], axis=-1)
        gp["ffn_conv_b"][i] = jnp.concatenate([ag[CONV_W], au[CONV_W]], axis=-1)
        dhn2 = None
        for half, dpart in enumerate((dug.reshape(n, D_FF), duu.reshape(n, D_FF))):
            gw["ffn_up"] = _mm(dpart, r["hn2"], "tn", f"ffn_up_g{i}_{half}", into=(gw["ffn_up"], i, half * D_FF))
            dhn2 = _mm(dpart, W["ffn_up"], "nn", f"ffn_up_b{i}_{half}", layer=i, b_rows=(half * D_FF, D_FF), res=dhn2)
        dh, gp["ffn_norm"][i] = _rms_bwd(r["h1"], P["ffn_norm"][i], dhn2, f"ffn_norm_b{i}", res=dh)

        if i % 2 == 0:
            dcat = times_wt(dh, "ab_w_out", j, f"ab_out_b{i}")
            grad_w(r["cat"], dh, "ab_w_out", j, f"ab_out_g{i}")
            dob, dgb, gp["hg_out_norm"][j] = _gnorm_bwd(r["ob"], r["proj"], HG_GATE_COL, P["hg_out_norm"][j], dcat, HG_OUT_COL,
                                                        f"hg_norm_b{i}")
            dqb, dfb, dib, dlb = _hg_bwd(r["proj"], HG_COLS, lbs[j].reshape(HG_HEADS, HG_DK), r["st"], dob, b, s, f"hg_b{i}")
            dlbs[j] = dlb.reshape(b, HG_W).sum(axis=0)
            dqa, dka, dva = _sb_bwd(r["proj"], SB_COLS, r["lta"], dcat, 0, b, s, f"sb_b{i}")
            dproj = jnp.concatenate([dqa, dka, dva, dqb, dfb, dib, dgb], axis=1)
            grad_w(r["hn"], dproj, "ab_w_in", j, f"ab_in_g{i}")
            dhn = times_wt(dproj, "ab_w_in", j, f"ab_in_b{i}")
        else:
            do2 = times_wt(dh, "c_w_out", j, f"c_out_b{i}")
            grad_w(r["o2"], dh, "c_w_out", j, f"c_out_g{i}")
            do = _heads_out(do2, b, s, SW_HEADS, SW_DIM).reshape(b, SW_KV_HEADS, SW_GROUP, s, SW_DIM)
            dqn, dkn, dv, dbias_i, dsink = _swa_bwd(r["qn"], r["kn"], r["v"], P["sinks"][j], bias, do, f"swa_b{i}")
            dbias = dbias + dbias_i
            gp["sinks"][j] = dsink[:, :, 0].reshape(SW_HEADS)
            dq, gp["q_norm"][j] = _rms_bwd(r["q"], P["q_norm"][j], dqn.reshape(-1, SW_DIM), f"q_norm_b{i}")
            dk, gp["k_norm"][j] = _rms_bwd(r["k"], P["k_norm"][j], dkn.reshape(-1, SW_DIM), f"k_norm_b{i}")
            dproj = jnp.concatenate([_heads_in(dq.reshape(b * SW_HEADS, s, SW_DIM), b, s, SW_HEADS, SW_DIM),
                                     _heads_in(dk.reshape(b * SW_KV_HEADS, s, SW_DIM), b, s, SW_KV_HEADS, SW_DIM),
                                     _heads_in(dv.reshape(b * SW_KV_HEADS, s, SW_DIM), b, s, SW_KV_HEADS, SW_DIM)], axis=1)
            grad_w(r["hn"], dproj, "c_w_in", j, f"c_in_g{i}")
            dhn = times_wt(dproj, "c_w_in", j, f"c_in_b{i}")
        dh, gp["mix_norm"][i] = _rms_bwd(r["h0"], P["mix_norm"][i], dhn, f"mix_norm_b{i}", res=dh)

    gp = {name: jnp.stack(v) for name, v in gp.items()}
    gp["hg_lb_logits"] = lb_vjp(jnp.stack(dlbs))[0]
    gp["rel_bias"] = bias_vjp(dbias)[0]
    return loss[0, 0], dh.reshape(b, s, dm), gw, jnp.stack(gconv), gp


WEIGHTS = ("mix_norm", "ab_w_in", "hg_lb_logits", "hg_out_norm", "ab_w_out", "c_w_in", "q_norm", "k_norm", "sinks", "rel_bias",
           "c_w_out", "ffn_norm", "ffn_up", "ffn_conv", "ffn_conv_b", "ffn_down", "ple_norm", "ple_gate", "ple_proj")
SMALL = ("mix_norm", "hg_lb_logits", "hg_out_norm", "q_norm", "k_norm", "sinks", "rel_bias", "ffn_norm", "ffn_conv_b", "ple_norm")


def _step(x, p, target, w, m, v):
    names = [name for name, _ in BIG]
    mx, my, mc = _position()
    dev = 4 * mx + 2 * my + mc

    blocks = [_row_block(w[name], KIND[name]).astype(BF16) for name in names]
    full = dict(zip(names, _all_gather_rows(blocks, "gather_weights")))
    nl, taps, cs = w["ffn_conv"].shape
    conv_all = _all_gather(_pad_rows(w["ffn_conv"].reshape(-1)), "gather_conv").reshape(N_DEV, -1)[:, :nl * taps * cs]
    full["ffn_conv"] = conv_all.reshape(N_DEV, nl, taps, cs).transpose(1, 2, 0, 3).reshape(nl, taps, N_DEV * cs)

    small = {name: w[name] for name in SMALL}
    loss, grad_x, gw, gconv, gp = _forward_backward(x, p, target, full, small)

    core = jnp.reshape(mc, (1,)).astype(jnp.int32)
    where = jnp.stack([dev, 2 * mx + my]).astype(jnp.int32)
    parts = [gw[name] for name in names]
    from_sibling = _rs_pair(parts, "reduce_pair")
    chip_sums = [_rs_add_pair(g, a, core, f"reduce_pair_add_{name}") for name, g, a in zip(names, parts, from_sibling)]
    from_chips = _rs_chips(chip_sums, "reduce_chips")
    grads = {name: _row_block(_rs_final(g, a, bs, where, f"reduce_final_{name}"), KIND[name])
             for name, g, a, bs in zip(names, parts, from_sibling, from_chips)}

    flat_small = jnp.concatenate([gp[name].reshape(-1) for name in SMALL] + [gconv.reshape(-1), loss.reshape(1)])
    small_sum = _sum_devices(_all_gather(_pad_rows(flat_small), "gather_small"), "sum_small").reshape(-1)
    e0 = 0
    for name in SMALL:
        cnt = math.prod(w[name].shape)
        grads[name] = small_sum[e0:e0 + cnt].reshape(w[name].shape)
        e0 += cnt
    gconv_sum = small_sum[e0:e0 + gconv.size].reshape(gconv.shape)
    grads["ffn_conv"] = lax.dynamic_slice_in_dim(gconv_sum, dev * cs, cs, axis=2)
    loss = small_sum[e0 + gconv.size]

    deltas, new_m, new_v = {}, {}, {}
    for name in WEIGHTS:
        shape = w[name].shape
        view = (-1, shape[-1]) if len(shape) > 1 else (1, -1)
        d_, m_, v_ = _adamw(w[name].reshape(view), grads[name].reshape(view), m[name].reshape(view), v[name].reshape(view), f"adamw_{name}")
        deltas[name], new_m[name], new_v[name] = d_.reshape(shape), m_.reshape(shape), v_.reshape(shape)
    return (loss, grad_x, *[grads[k] for k in WEIGHTS], *[deltas[k] for k in WEIGHTS],
            *[new_m[k] for k in WEIGHTS], *[new_v[k] for k in WEIGHTS])


def kernel(x, p, mix_norm, ab_w_in, hg_lb_logits, hg_out_norm, ab_w_out, c_w_in, q_norm, k_norm, sinks, rel_bias, c_w_out, ffn_norm, ffn_up, ffn_conv, ffn_conv_b, ffn_down, ple_norm, ple_gate, ple_proj, loss_target, m_mix_norm, m_ab_w_in, m_hg_lb_logits, m_hg_out_norm, m_ab_w_out, m_c_w_in, m_q_norm, m_k_norm, m_sinks, m_rel_bias, m_c_w_out, m_ffn_norm, m_ffn_up, m_ffn_conv, m_ffn_conv_b, m_ffn_down, m_ple_norm, m_ple_gate, m_ple_proj, v_mix_norm, v_ab_w_in, v_hg_lb_logits, v_hg_out_norm, v_ab_w_out, v_c_w_in, v_q_norm, v_k_norm, v_sinks, v_rel_bias, v_c_w_out, v_ffn_norm, v_ffn_up, v_ffn_conv, v_ffn_conv_b, v_ffn_down, v_ple_norm, v_ple_gate, v_ple_proj):
    w = dict(zip(WEIGHTS, (mix_norm, ab_w_in, hg_lb_logits, hg_out_norm, ab_w_out, c_w_in, q_norm, k_norm, sinks, rel_bias, c_w_out,
                           ffn_norm, ffn_up, ffn_conv, ffn_conv_b, ffn_down, ple_norm, ple_gate, ple_proj)))
    m = dict(zip(WEIGHTS, (m_mix_norm, m_ab_w_in, m_hg_lb_logits, m_hg_out_norm, m_ab_w_out, m_c_w_in, m_q_norm, m_k_norm, m_sinks,
                           m_rel_bias, m_c_w_out, m_ffn_norm, m_ffn_up, m_ffn_conv, m_ffn_conv_b, m_ffn_down, m_ple_norm, m_ple_gate,
                           m_ple_proj)))
    v = dict(zip(WEIGHTS, (v_mix_norm, v_ab_w_in, v_hg_lb_logits, v_hg_out_norm, v_ab_w_out, v_c_w_in, v_q_norm, v_k_norm, v_sinks,
                           v_rel_bias, v_c_w_out, v_ffn_norm, v_ffn_up, v_ffn_conv, v_ffn_conv_b, v_ffn_down, v_ple_norm, v_ple_gate,
                           v_ple_proj)))
    return _step(x, p, loss_target, w, m, v)
```

```python
import functools
import math

import numpy as np
import jax
import jax.numpy as jnp
from jax import lax
from jax.experimental import pallas as pl
from jax.experimental.pallas import tpu as pltpu

F32 = jnp.float32
BF16 = jnp.bfloat16

D_MODEL = 1024
DEPTH = 4
PLE_DIM = 256
EPS = 1e-6
SB_HEADS, SB_DIM = 8, 64
SB_WIDTH = SB_HEADS * SB_DIM
HG_HEADS, HG_DK, HG_DV = 4, 128, 128
HG_W = HG_HEADS * HG_DK
AB_IN = 3 * SB_WIDTH + 4 * HG_W
SW_HEADS, SW_KV_HEADS, SW_DIM = 16, 4, 64
SW_GROUP = SW_HEADS // SW_KV_HEADS
WINDOW = 128
C_IN = (SW_HEADS + 2 * SW_KV_HEADS) * SW_DIM
N_BUCKETS, MAX_DISTANCE = 32, 128
D_FF = 2816
N_DEV = 8

ADAM_LR, ADAM_B1, ADAM_B2, ADAM_EPS, ADAM_WD, ADAM_STEP = 0.001, 0.9, 0.999, 1e-08, 0.01, 10

LANES = 128
VMEM_LIMIT = 48 * 1024 * 1024

NN = (((1,), (0,)), ((), ()))
NT = (((1,), (1,)), ((), ()))
TN = (((0,), (0,)), ((), ()))


MXU_DTYPE = BF16


def _bf(x):
    return x.astype(MXU_DTYPE)


def _dot(a, b, dims=NN):
    return lax.dot_general(_bf(a), _bf(b), dims, preferred_element_type=F32)


def _split3(x):
    x1 = _bf(x)
    r = x - x1.astype(F32)
    x2 = _bf(r)
    x3 = _bf(r - x2.astype(F32))
    return x1, x2, x3


def _dot_exact_lhs01(m, x, terms=3):
    parts = _split3(x)[:terms]
    out = lax.dot_general(m, parts[0], NN, preferred_element_type=F32)
    for p_ in parts[1:]:
        out = out + lax.dot_general(m, p_, NN, preferred_element_type=F32)
    return out


def _dot_exact_rhs01(x, m, terms=2):
    parts = _split3(x)[:terms]
    out = lax.dot_general(parts[0], m, NN, preferred_element_type=F32)
    for p_ in parts[1:]:
        out = out + lax.dot_general(p_, m, NN, preferred_element_type=F32)
    return out


def _pick(n, target):
    best = None
    for t in range(LANES, target + 1, LANES):
        if n % t == 0:
            best = t
    return best or n


def _params(sem=None):
    return pltpu.CompilerParams(dimension_semantics=sem, vmem_limit_bytes=VMEM_LIMIT)


MM_ROWS, MM_ROWS_TN, MM_COLS, MM_DEPTH, MM_DEPTH_TN = 1024, 1408, 1408, 2048, 1024

def _mm(a, b, mode, name, res=None, out_dtype=F32, layer=None, b_rows=None, into=None):
    bshape = b.shape if layer is None else b.shape[1:]
    if b_rows is not None:
        assert mode == "nn"
        bshape = (b_rows[1], bshape[1])
    if mode == "nn":
        (M, K), (K2, N) = a.shape, bshape
    elif mode == "nt":
        (M, K), (N, K2) = a.shape, bshape
    else:
        (K, M), (K2, N) = a.shape, bshape
    assert K == K2, (a.shape, b.shape, mode)
    tm, tn = _pick(M, MM_ROWS_TN if mode == "tn" else MM_ROWS), _pick(N, MM_COLS)
    tk = _pick(K, MM_DEPTH_TN if mode == "tn" else MM_DEPTH)
    nk = K // tk
    k_off = 0 if b_rows is None else b_rows[0] // tk
    assert b_rows is None or b_rows[0] % tk == 0
    dims = {"nn": NN, "nt": NT, "tn": TN}[mode]
    a_spec = pl.BlockSpec((tk, tm), lambda i, j, k: (k, i)) if mode == "tn" else pl.BlockSpec((tm, tk), lambda i, j, k: (i, k))
    if layer is None:
        b_spec = pl.BlockSpec((tn, tk), lambda i, j, k: (j, k)) if mode == "nt" else pl.BlockSpec((tk, tn), lambda i, j, k: (k, j))
    elif mode == "nt":
        b_spec = pl.BlockSpec((None, tn, tk), lambda i, j, k: (layer, j, k))
    else:
        b_spec = pl.BlockSpec((None, tk, tn), lambda i, j, k: (layer, k + k_off, j))
    o_spec = pl.BlockSpec((tm, tn), lambda i, j, k: (i, j))
    has_res = res is not None

    def finish(acc, r_ref, o_ref):
        if has_res:
            acc = acc + r_ref[...]
        o_ref[...] = acc.astype(out_dtype)

    def body(*refs):
        a_ref, b_ref = refs[0], refs[1]
        r_ref = refs[2] if has_res else None
        o_ref = refs[-1] if nk == 1 else refs[-2]
        part = _dot(a_ref[...], b_ref[...], dims)
        if nk == 1:
            finish(part, r_ref, o_ref)
            return
        acc_ref = refs[-1]
        k = pl.program_id(2)

        @pl.when(k == 0)
        def _():
            acc_ref[...] = part

        @pl.when((k > 0) & (k < nk - 1))
        def _():
            acc_ref[...] += part

        @pl.when(k == nk - 1)
        def _():
            finish(acc_ref[...] + part, r_ref, o_ref)

    in_specs = [a_spec, b_spec] + ([o_spec] if has_res else [])
    args = (a, b) + ((res,) if has_res else ())
    out_shape, aliases = jax.ShapeDtypeStruct((M, N), out_dtype), {}
    if into is not None:
        stack, slot, row = into
        assert stack.shape[2] == N and row % tm == 0 and row + M <= stack.shape[1] and stack.dtype == out_dtype and not has_res
        in_specs = in_specs + [pl.BlockSpec(memory_space=pl.ANY)]
        args = args + (stack,)
        o_spec = pl.BlockSpec((None, tm, tn), lambda i, j, k: (slot, i + row // tm, j))
        out_shape, aliases = jax.ShapeDtypeStruct(stack.shape, out_dtype), {2: 0}

    def body_into(a_ref, b_ref, stack_ref, *rest):
        body(a_ref, b_ref, *rest)

    return pl.pallas_call(
        body if into is None else body_into, name=name, grid=(M // tm, N // tn, nk), in_specs=in_specs, out_specs=o_spec,
        out_shape=out_shape, scratch_shapes=[] if nk == 1 else [pltpu.VMEM((tm, tn), F32)], input_output_aliases=aliases,
        compiler_params=_params(("parallel", "parallel", "arbitrary")),
    )(*args)


def _row_tile(n, d):
    if n % 8:
        return n
    t = 8
    while t * 2 <= min(n, (512 * 1024) // d) and n % (t * 2) == 0:
        t *= 2
    return t


def _rms_fwd(x, g, name, out_dtype=F32):
    n, d = x.shape
    tm = _row_tile(n, d)

    def body(x_ref, g_ref, o_ref):
        xf = x_ref[...]
        r = lax.rsqrt(jnp.mean(xf * xf, axis=-1, keepdims=True) + EPS)
        o_ref[...] = (xf * r * g_ref[...]).astype(out_dtype)

    return pl.pallas_call(
        body, name=name, grid=(n // tm,),
        in_specs=[pl.BlockSpec((tm, d), lambda i: (i, 0)), pl.BlockSpec((1, d), lambda i: (0, 0))],
        out_specs=pl.BlockSpec((tm, d), lambda i: (i, 0)),
        out_shape=jax.ShapeDtypeStruct((n, d), out_dtype), compiler_params=_params(("parallel",)),
    )(x, g.reshape(1, d))


def _rms_bwd(x, g, dy, name, res=None):
    n, d = x.shape
    tm = _row_tile(n, d)
    has_res = res is not None

    def body(*refs):
        x_ref, g_ref, dy_ref = refs[:3]
        r_ref = refs[3] if has_res else None
        dx_ref, dg_ref = refs[-2:]
        xf = x_ref[...]
        r = lax.rsqrt(jnp.mean(xf * xf, axis=-1, keepdims=True) + EPS)
        xh = xf * r
        dyf = dy_ref[...].astype(F32)
        dxh = dyf * g_ref[...]
        dx = r * (dxh - xh * jnp.mean(dxh * xh, axis=-1, keepdims=True))
        if has_res:
            dx = dx + r_ref[...]
        dx_ref[...] = dx

        @pl.when(pl.program_id(0) == 0)
        def _():
            dg_ref[...] = jnp.zeros_like(dg_ref)

        dg_ref[...] += jnp.sum(dyf * xh, axis=0, keepdims=True)

    row = pl.BlockSpec((tm, d), lambda i: (i, 0))
    vec = pl.BlockSpec((1, d), lambda i: (0, 0))
    dx, dg = pl.pallas_call(
        body, name=name, grid=(n // tm,),
        in_specs=[row, vec, row] + ([row] if has_res else []),
        out_specs=[row, vec],
        out_shape=[jax.ShapeDtypeStruct((n, d), F32), jax.ShapeDtypeStruct((1, d), F32)],
        compiler_params=_params(("arbitrary",)),
    )(x, g.reshape(1, d), dy, *((res,) if has_res else ()))
    return dx, dg.reshape(d)


def _silu(x):
    return x * jax.nn.sigmoid(x)


def _gnorm_fwd(o, gate, gate_col, w, name):
    n, width = o.shape
    d = w.shape[0]
    tm = _row_tile(n, d)

    def body(o_ref, g_ref, w_ref, y_ref):
        of = o_ref[...]
        r = lax.rsqrt(jnp.mean(of * of, axis=-1, keepdims=True) + EPS)
        y_ref[...] = of * r * w_ref[...] * _silu(g_ref[...])

    row = pl.BlockSpec((tm, d), lambda i, h: (i, h))
    vec = pl.BlockSpec((1, d), lambda i, h: (0, 0))
    return pl.pallas_call(body, name=name, grid=(n // tm, width // d),
                          in_specs=[row, pl.BlockSpec((tm, d), lambda i, h: (i, gate_col + h)), vec], out_specs=row,
                          out_shape=jax.ShapeDtypeStruct((n, width), F32), compiler_params=_params(("parallel", "parallel")))(o, gate, w.reshape(1, d))


def _gnorm_bwd(o, gate, gate_col, w, dy, dy_col, name):
    n, width = o.shape
    d = w.shape[0]
    tm = _row_tile(n, d)

    def body(o_ref, g_ref, w_ref, dy_ref, do_ref, dgate_ref, dw_ref):
        of, gf, dyf = o_ref[...], g_ref[...], dy_ref[...]
        r = lax.rsqrt(jnp.mean(of * of, axis=-1, keepdims=True) + EPS)
        xh = of * r
        sg = jax.nn.sigmoid(gf)
        sil = gf * sg
        dnorm = dyf * sil
        dgate_ref[...] = dyf * xh * w_ref[...] * (sg * (1.0 + gf * (1.0 - sg)))
        dxh = dnorm * w_ref[...]
        do_ref[...] = r * (dxh - xh * jnp.mean(dxh * xh, axis=-1, keepdims=True))

        @pl.when((pl.program_id(0) == 0) & (pl.program_id(1) == 0))
        def _():
            dw_ref[...] = jnp.zeros_like(dw_ref)

        dw_ref[...] += jnp.sum(dnorm * xh, axis=0, keepdims=True)

    row = pl.BlockSpec((tm, d), lambda i, h: (i, h))
    vec = pl.BlockSpec((1, d), lambda i, h: (0, 0))
    do, dgate, dw = pl.pallas_call(
        body, name=name, grid=(n // tm, width // d),
        in_specs=[row, pl.BlockSpec((tm, d), lambda i, h: (i, gate_col + h)), vec, pl.BlockSpec((tm, d), lambda i, h: (i, dy_col + h))],
        out_specs=[row, row, vec],
        out_shape=[jax.ShapeDtypeStruct((n, width), F32)] * 2 + [jax.ShapeDtypeStruct((1, d), F32)],
        compiler_params=_params(("arbitrary", "arbitrary")),
    )(o, gate, w.reshape(1, d), dy)
    return do, dgate, dw.reshape(d)


def _sigmul_fwd(z, e, res, name):
    n, d = z.shape
    tm = _row_tile(n, d)

    def body(z_ref, e_ref, r_ref, o_ref):
        o_ref[...] = r_ref[...] + jax.nn.sigmoid(z_ref[...]) * e_ref[...]

    row = pl.BlockSpec((tm, d), lambda i: (i, 0))
    return pl.pallas_call(body, name=name, grid=(n // tm,), in_specs=[row] * 3, out_specs=row,
                          out_shape=jax.ShapeDtypeStruct((n, d), F32), compiler_params=_params(("parallel",)))(z, e, res)


def _sigmul_bwd(z, e, dy, name):
    n, d = z.shape
    tm = _row_tile(n, d)

    def body(z_ref, e_ref, dy_ref, dz_ref, de_ref):
        s = jax.nn.sigmoid(z_ref[...])
        dyf = dy_ref[...]
        dz_ref[...] = dyf * e_ref[...] * s * (1.0 - s)
        de_ref[...] = dyf * s

    row = pl.BlockSpec((tm, d), lambda i: (i, 0))
    return pl.pallas_call(body, name=name, grid=(n // tm,), in_specs=[row] * 3, out_specs=[row] * 2,
                          out_shape=[jax.ShapeDtypeStruct((n, d), F32)] * 2, compiler_params=_params(("parallel",)))(z, e, dy)


def _loss_fwd(y, target, name):
    n, d = y.shape
    tm = _row_tile(n, d)

    def body(y_ref, t_ref, l_ref, dy_ref):
        diff = y_ref[...] - t_ref[...]
        dy_ref[...] = diff * (1.0 / d)

        @pl.when(pl.program_id(0) == 0)
        def _():
            l_ref[...] = jnp.zeros_like(l_ref)

        part = jnp.sum(jnp.mean(diff * diff, axis=-1, keepdims=True), axis=0, keepdims=True)
        l_ref[...] += 0.5 * jnp.broadcast_to(part, l_ref.shape)

    row = pl.BlockSpec((tm, d), lambda i: (i, 0))
    vec = pl.BlockSpec((1, LANES), lambda i: (0, 0))
    return pl.pallas_call(body, name=name, grid=(n // tm,), in_specs=[row, row], out_specs=[vec, row],
                          out_shape=[jax.ShapeDtypeStruct((1, LANES), F32), jax.ShapeDtypeStruct((n, d), F32)],
                          compiler_params=_params(("arbitrary",)))(y, target)


SB_BLK = 128
SB_QBLK = 512


def _sb_logits(z, qi, kj, row, col):
    mask = (kj * SB_BLK + col) < (qi * SB_QBLK + row)
    sp = jnp.maximum(z, 0.0) + jnp.log1p(jnp.exp(-jnp.abs(z)))
    lk = jnp.where(mask, -sp, 0.0)
    return mask, lk, z - sp


SB_PAIRS = SB_WIDTH // LANES


def _sb_iotas():
    row = lax.broadcasted_iota(jnp.int32, (2 * SB_QBLK, SB_BLK), 0)
    row = jnp.where(row >= SB_QBLK, row - SB_QBLK, row)
    col = lax.broadcasted_iota(jnp.int32, (2 * SB_QBLK, SB_BLK), 1)
    return row, col, col[:SB_QBLK] < SB_DIM


def _sb_stack(x, first):
    return jnp.concatenate([jnp.where(first, x, 0.0), jnp.where(first, 0.0, x)], axis=0)


def _sb_unstack(y, first):
    return jnp.where(first, y[:SB_QBLK], y[SB_QBLK:])


def _sb_running(x, u):
    m = x.shape[0]
    hi = _bf(x)
    lo = _bf(x - hi.astype(F32))
    c = lax.dot_general(jnp.concatenate([hi, lo], axis=0), u, NN, preferred_element_type=F32)
    return c[:m] + c[m:]


def _sb_spec(s, col):
    return pl.BlockSpec((s, LANES), lambda e, pr: (e, col + pr))


def _sb_fwd(proj, cols, b, s, name):
    nq = s // SB_QBLK
    scale = SB_DIM ** -0.5

    def body(q_ref, k_ref, v_ref, o_ref, lt_ref):
        row, col, first = _sb_iotas()
        u_after = _bf(row[:SB_BLK] > col[:SB_BLK])

        def qloop(qi, _):
            q0 = pl.multiple_of(qi * SB_QBLK, SB_QBLK)
            q2 = _sb_stack(q_ref[pl.ds(q0, SB_QBLK), :], first)
            nkeys = (qi + 1) * (SB_QBLK // SB_BLK)

            def logits(kj):
                k0 = pl.multiple_of(kj * SB_BLK, SB_BLK)
                return _dot(q2, k_ref[pl.ds(k0, SB_BLK), :], NT) * scale

            def kloop(j, st):
                acc, carry, z = st
                kj = nkeys - 1 - j
                k0 = pl.multiple_of(kj * SB_BLK, SB_BLK)
                z_next = logits(jnp.maximum(kj - 1, 0))
                mask, lk, ls = _sb_logits(z, qi, kj, row, col)
                later = carry + _sb_running(lk, u_after)
                w = jnp.where(mask, jnp.exp(ls + later), 0.0)
                acc = acc + _sb_unstack(_dot(w, v_ref[pl.ds(k0, SB_BLK), :]), first)
                return acc, carry + jnp.sum(lk, axis=1, keepdims=True), z_next

            acc, carry, _ = lax.fori_loop(0, nkeys, kloop, (jnp.zeros((SB_QBLK, LANES), F32), jnp.zeros((2 * SB_QBLK, 1), F32), logits(nkeys - 1)))
            o_ref[pl.ds(q0, SB_QBLK), :] = acc
            lt_ref[pl.ds(q0, SB_QBLK), :] = _sb_unstack(jnp.broadcast_to(carry, (2 * SB_QBLK, LANES)), first)
            return 0

        lax.fori_loop(0, nq, qloop, 0)

    out = _sb_spec(s, 0)
    return pl.pallas_call(body, name=name, grid=(b, SB_PAIRS), in_specs=[_sb_spec(s, c) for c in cols], out_specs=[out, out],
                          out_shape=[jax.ShapeDtypeStruct((b * s, SB_WIDTH), F32)] * 2,
                          compiler_params=_params(("parallel", "parallel")))(proj, proj, proj)


def _sb_bwd(proj, cols, ltot, do, do_col, b, s, name):
    nq = s // SB_QBLK
    scale = SB_DIM ** -0.5

    def body(q_ref, k_ref, v_ref, lt_ref, do_ref, dq_ref, dk_ref, dv_ref):
        row, col, first = _sb_iotas()
        u_upto = _bf(row[:SB_BLK] <= col[:SB_BLK])
        u_before = _bf(row[:SB_BLK] < col[:SB_BLK])
        dk_ref[...] = jnp.zeros_like(dk_ref)
        dv_ref[...] = jnp.zeros_like(dv_ref)

        def qloop(qi, _):
            q0 = pl.multiple_of(qi * SB_QBLK, SB_QBLK)
            q2 = _sb_stack(q_ref[pl.ds(q0, SB_QBLK), :], first)
            nkeys = (qi + 1) * (SB_QBLK // SB_BLK)
            do2 = _sb_stack(do_ref[pl.ds(q0, SB_QBLK), :], first)
            lt2 = jnp.min(_sb_stack(lt_ref[pl.ds(q0, SB_QBLK), :], first), axis=1, keepdims=True)

            def logits(kj):
                k0 = pl.multiple_of(kj * SB_BLK, SB_BLK)
                return _dot(q2, k_ref[pl.ds(k0, SB_BLK), :], NT) * scale

            def kloop(kj, st):
                dq, cl, cg, z = st
                k0 = pl.multiple_of(kj * SB_BLK, SB_BLK)
                kb = k_ref[pl.ds(k0, SB_BLK), :]
                vb = v_ref[pl.ds(k0, SB_BLK), :]
                z_next = logits(jnp.minimum(kj + 1, nkeys - 1))
                mask, lk, ls = _sb_logits(z, qi, kj, row, col)
                later = lt2 - (cl + _sb_running(lk, u_upto))
                w = jnp.where(mask, jnp.exp(ls + later), 0.0)
                g = _dot(do2, vb, NT) * w
                dv_ref[pl.ds(k0, SB_BLK), :] += _dot(w, do2, TN)
                g_before = cg + _sb_running(g, u_before)
                sig = jnp.exp(ls)
                dz = jnp.where(mask, g * (1.0 - sig) - sig * g_before, 0.0) * scale
                dq = dq + _sb_unstack(_dot(dz, kb), first)
                dk_ref[pl.ds(k0, SB_BLK), :] += _dot(dz, q2, TN)
                return dq, cl + jnp.sum(lk, axis=1, keepdims=True), cg + jnp.sum(g, axis=1, keepdims=True), z_next

            z1 = jnp.zeros((2 * SB_QBLK, 1), F32)
            dq = lax.fori_loop(0, nkeys, kloop, (jnp.zeros((SB_QBLK, LANES), F32), z1, z1, logits(0)))[0]
            dq_ref[pl.ds(q0, SB_QBLK), :] = dq
            return 0

        lax.fori_loop(0, nq, qloop, 0)

    out = _sb_spec(s, 0)
    return pl.pallas_call(body, name=name, grid=(b, SB_PAIRS),
                          in_specs=[_sb_spec(s, c) for c in cols] + [out, _sb_spec(s, do_col)], out_specs=[out] * 3,
                          out_shape=[jax.ShapeDtypeStruct((b * s, SB_WIDTH), F32)] * 3,
                          compiler_params=_params(("parallel", "parallel")))(proj, proj, proj, ltot, do)


HG_CHUNK = 64
HG_GROUP = 2


def _hg_consts(c, r):
    levels = int(math.log2(c))
    t = np.arange(r)
    same = (t[:, None] // c) == (t[None, :] // c)
    tri = ((t[:, None] >= t[None, :]) & same).astype(np.float32)
    psel = np.zeros((levels, r, r), np.float32)
    masks = np.zeros((levels + 1, r, r), np.float32)
    for l in range(levels):
        n = c >> (l + 1)
        blk = t // (2 * n)
        psel[l, t, blk * 2 * n + n - 1] = 1.0
        upper = (t % (2 * n)) >= n
        masks[l] = (blk[:, None] == blk[None, :]) & upper[:, None] & (~upper)[None, :]
    masks[levels] = np.eye(r)
    psel = psel.reshape(levels * r, r)
    return levels, jnp.asarray(tri), jnp.asarray(psel), jnp.asarray(masks), jnp.asarray(tri.T.copy()), jnp.asarray(psel.T.copy())


def _hg_elem(qv, fv, lbv):
    sig = jax.nn.sigmoid(fv)
    lf = jnp.log(lbv + (1.0 - lbv) * sig)
    kk = (1.0 - lbv) * jax.nn.sigmoid(-fv)
    qf = qv * jax.nn.sigmoid(qv)
    return qf, kk, lf


def _col_bcast(rowvec):
    n = rowvec.shape[1]
    return jnp.transpose(jnp.broadcast_to(rowvec, (n, n)))


def _hg_within(qf, kk, lf, tri, psel, m_ref, c, levels):
    r = qf.shape[0]
    b = _dot_exact_lhs01(tri, lf)
    bls = [b[(g + 1) * c - 1:(g + 1) * c, :] for g in range(r // c)]
    blb = jnp.concatenate([jnp.broadcast_to(bl, (c, bl.shape[1])) for bl in bls], axis=0)
    eb = jnp.exp(b)
    qi = qf * eb
    bsel = _dot_exact_lhs01(psel, b)
    scores = jnp.where(m_ref[levels] > 0, _dot(qf, kk, NT), 0.0)
    lev = []
    for l in range(levels):
        bs = bsel[l * r:(l + 1) * r]
        eq = jnp.exp(jnp.minimum(b - bs, 0.0))
        ek = jnp.exp(jnp.minimum(bs - b, 0.0))
        ql, kl = qf * eq, kk * ek
        scores = scores + jnp.where(m_ref[l] > 0, _dot(ql, kl, NT), 0.0)
        lev.append((eq, ek, ql, kl))
    ebl = jnp.exp(blb - b)
    kd = kk * ebl
    decays = [_col_bcast(jnp.exp(bl)) for bl in bls]
    return eb, qi, scores, lev, ebl, kd, decays


def _hg_fwd(proj, cols, lb, b, s, name):
    nh, d = lb.shape
    bh = b * nh
    c = HG_CHUNK
    nc = s // c
    grp = math.gcd(HG_GROUP, nc)
    r = grp * c
    levels, tri, psel, masks, _, _ = _hg_consts(c, r)

    def body(q_ref, f_ref, i_ref, lb_ref, tri_ref, psel_ref, m_ref, o_ref, st_ref):
        lbv = jnp.broadcast_to(lb_ref[0], (r, d))
        tri_v, psel_v = _bf(tri_ref[...]), _bf(psel_ref[...])

        def group(gi, state):
            r0 = pl.multiple_of(gi * r, r)
            qf, kk, lf = _hg_elem(q_ref[pl.ds(r0, r), :], f_ref[pl.ds(r0, r), :], lbv)
            iv = i_ref[pl.ds(r0, r), :]
            _, qi, scores, _, _, kd, decays = _hg_within(qf, kk, lf, tri_v, psel_v, m_ref, c, levels)
            within = _dot(scores, iv)
            for g in range(grp):
                rows = slice(g * c, (g + 1) * c)
                st_ref[0, gi * grp + g] = state
                o_ref[pl.ds(r0 + g * c, c), :] = _dot(qi[rows], state) + within[rows]
                state = decays[g] * state + _dot(kd[rows], iv[rows], TN)
            return state

        lax.fori_loop(0, nc // grp, group, jnp.zeros((d, d), F32))

    full = lambda a: pl.BlockSpec(a.shape, lambda e, hd: (0,) * a.ndim)
    return pl.pallas_call(
        body, name=name, grid=(b, nh),
        in_specs=[_hg_seq(s, d, cols[0]), _hg_seq(s, d, cols[1]), _hg_seq(s, d, cols[2]),
                  pl.BlockSpec((1, 1, d), lambda e, hd: (hd, 0, 0)), full(tri), full(psel), full(masks)],
        out_specs=[_hg_seq(s, d, 0), pl.BlockSpec((1, nc, d, d), lambda e, hd: (e * nh + hd, 0, 0, 0))],
        out_shape=[jax.ShapeDtypeStruct((b * s, nh * d), F32), jax.ShapeDtypeStruct((bh, nc, d, d), F32)],
        compiler_params=_params(("parallel", "parallel")),
    )(proj, proj, proj, lb.reshape(nh, 1, d), tri, psel, masks)


def _hg_seq(s, d, col):
    return pl.BlockSpec((s, d), lambda e, hd: (e, col + hd))


def _hg_bwd(proj, cols, lb, states, do, b, s, name):
    nh, d = lb.shape
    bh = b * nh
    c = HG_CHUNK
    nc = s // c
    grp = math.gcd(HG_GROUP, nc)
    r = grp * c
    levels, tri, psel, masks, tri_t, psel_t = _hg_consts(c, r)

    def body(q_ref, f_ref, i_ref, lb_ref, st_ref, do_ref, tri_ref, psel_ref, m_ref, trit_ref, pselt_ref,
             dq_ref, df_ref, di_ref, dlb_ref):
        lbv = jnp.broadcast_to(lb_ref[0], (r, d))
        tri_v, psel_v = _bf(tri_ref[...]), _bf(psel_ref[...])
        trit_v, pselt_v = _bf(trit_ref[...]), _bf(pselt_ref[...])
        row_in_chunk = lax.broadcasted_iota(jnp.int32, (c, d), 0)

        def chunk(step, carry):
            ds_out, dlb = carry
            gi = nc // grp - 1 - step
            r0 = pl.multiple_of(gi * r, r)
            qv, fv, iv = q_ref[pl.ds(r0, r), :], f_ref[pl.ds(r0, r), :], i_ref[pl.ds(r0, r), :]
            dov = do_ref[pl.ds(r0, r), :]
            (qf, kk, lf), elem_vjp = jax.vjp(_hg_elem, qv, fv, lbv)
            eb, qi, scores, lev, ebl, kd, decays = _hg_within(qf, kk, lf, tri_v, psel_v, m_ref, c, levels)

            dscores = _dot(dov, iv, NT)
            di_within = _dot(scores, dov, TN)
            dqi_parts, dkd_parts, dbl_parts = [None] * grp, [None] * grp, [None] * grp
            for g in reversed(range(grp)):
                rows = slice(g * c, (g + 1) * c)
                state = st_ref[0, gi * grp + g]
                di_ref[pl.ds(r0 + g * c, c), :] = di_within[rows] + _dot(kd[rows], ds_out)
                dqi_parts[g] = _dot(dov[rows], state, NT)
                dkd_parts[g] = _dot(iv[rows], ds_out, NT)
                dbl = (jnp.sum(dkd_parts[g] * kd[rows], axis=0, keepdims=True)
                       + _col_bcast_t(jnp.sum(ds_out * decays[g] * state, axis=1, keepdims=True)))
                dbl_parts[g] = jnp.where(row_in_chunk == c - 1, dbl, 0.0)
                ds_out = decays[g] * ds_out + _dot(qi[rows], dov[rows], TN)
            ds_in = ds_out
            dqi = jnp.concatenate(dqi_parts, axis=0)
            dkd = jnp.concatenate(dkd_parts, axis=0)
            dqf = dqi * eb
            dkk = dkd * ebl
            db = dqi * qi - dkd * kd + jnp.concatenate(dbl_parts, axis=0)
            dsd = jnp.where(m_ref[levels] > 0, dscores, 0.0)
            dqf = dqf + _dot(dsd, kk)
            dkk = dkk + _dot(dsd, qf, TN)
            dbsel = []
            for l in range(levels):
                eq, ek, ql, kl = lev[l]
                dsl = jnp.where(m_ref[l] > 0, dscores, 0.0)
                dql = _dot(dsl, kl)
                dkl = _dot(dsl, ql, TN)
                dqf = dqf + dql * eq
                dkk = dkk + dkl * ek
                diff = dql * ql - dkl * kl
                db = db + diff
                dbsel.append(-diff)
            db = db + _dot_exact_lhs01(pselt_v, jnp.concatenate(dbsel, axis=0))
            dlf = _dot_exact_lhs01(trit_v, db)
            dq, df, dlb_c = elem_vjp((dqf, dkk, dlf))
            dq_ref[pl.ds(r0, r), :] = dq
            df_ref[pl.ds(r0, r), :] = df
            return ds_in, dlb + jnp.sum(dlb_c, axis=0, keepdims=True)

        _, dlb = lax.fori_loop(0, nc // grp, chunk, (jnp.zeros((d, d), F32), jnp.zeros((1, d), F32)))
        dlb_ref[0] = dlb

    seq = _hg_seq(s, d, 0)
    full = lambda a: pl.BlockSpec(a.shape, lambda e, hd: (0,) * a.ndim)
    return pl.pallas_call(
        body, name=name, grid=(b, nh),
        in_specs=[_hg_seq(s, d, cols[0]), _hg_seq(s, d, cols[1]), _hg_seq(s, d, cols[2]),
                  pl.BlockSpec((1, 1, d), lambda e, hd: (hd, 0, 0)),
                  pl.BlockSpec((1, nc, d, d), lambda e, hd: (e * nh + hd, 0, 0, 0)), seq,
                  full(tri), full(psel), full(masks), full(tri_t), full(psel_t)],
        out_specs=[seq, seq, seq, pl.BlockSpec((1, 1, d), lambda e, hd: (e * nh + hd, 0, 0))],
        out_shape=[jax.ShapeDtypeStruct((b * s, nh * d), F32)] * 3 + [jax.ShapeDtypeStruct((bh, 1, d), F32)],
        compiler_params=_params(("parallel", "parallel")),
    )(proj, proj, proj, lb.reshape(nh, 1, d), states, do, tri, psel, masks, tri_t, psel_t)


def _col_bcast_t(colvec):
    n = colvec.shape[0]
    return jnp.transpose(jnp.broadcast_to(colvec, (n, n)))[0:1, :]


def _swa_probs(qg, kb, bias, sink, valid, scale):
    logits = _dot(qg, kb, NT) * scale + bias
    logits = jnp.where(valid, logits, -jnp.inf)
    m = jnp.maximum(jnp.max(logits, axis=-1, keepdims=True), sink)
    e = jnp.exp(logits - m)
    es = jnp.exp(sink - m)
    den = jnp.sum(e, axis=-1, keepdims=True) + es
    return e / den, es / den


def _swa_valid(n):
    w = WINDOW
    row = lax.broadcasted_iota(jnp.int32, (w, 2 * w), 0)
    col = lax.broadcasted_iota(jnp.int32, (w, 2 * w), 1)
    dist = row + w - col
    return (dist >= 0) & (dist < w) & ((col >= w) | (n > 0))


def _swa_specs(b, g, s, d):
    w = WINDOW
    q_spec = pl.BlockSpec((1, 1, g, w, d), lambda h, bi, n: (bi, h, 0, n, 0))
    kp_spec = pl.BlockSpec((1, 1, w, d), lambda h, bi, n: (bi, h, jnp.maximum(n - 1, 0), 0))
    kc_spec = pl.BlockSpec((1, 1, w, d), lambda h, bi, n: (bi, h, n, 0))
    bias_spec = pl.BlockSpec((1, g, w, 2 * w), lambda h, bi, n: (h, 0, 0, 0))
    sink_spec = pl.BlockSpec(memory_space=pltpu.SMEM)
    return q_spec, kp_spec, kc_spec, bias_spec, sink_spec


def _swa_fwd(q, k, v, sinks, bias, name):
    b, kvh, g, s, d = q.shape
    w = WINDOW
    scale = d ** -0.5
    q_spec, kp_spec, kc_spec, bias_spec, sink_spec = _swa_specs(b, g, s, d)

    def body(q_ref, kp_ref, kc_ref, vp_ref, vc_ref, bias_ref, sink_ref, o_ref):
        h, n = pl.program_id(0), pl.program_id(2)
        valid = _swa_valid(n)
        kb = jnp.concatenate([kp_ref[0, 0], kc_ref[0, 0]], axis=0)
        vb = jnp.concatenate([vp_ref[0, 0], vc_ref[0, 0]], axis=0)
        for gi in range(g):
            p, _ = _swa_probs(q_ref[0, 0, gi], kb, bias_ref[0, gi], sink_ref[h * g + gi], valid, scale)
            o_ref[0, 0, gi] = _dot(p, vb)

    return pl.pallas_call(
        body, name=name, grid=(kvh, b, s // w),
        in_specs=[q_spec, kp_spec, kc_spec, kp_spec, kc_spec, bias_spec, sink_spec], out_specs=q_spec,
        out_shape=jax.ShapeDtypeStruct(q.shape, F32), compiler_params=_params(("parallel", "parallel", "arbitrary")),
    )(q, k, k, v, v, bias, sinks)


def _swa_bwd(q, k, v, sinks, bias, do, name):
    b, kvh, g, s, d = q.shape
    w = WINDOW
    scale = d ** -0.5
    q_spec, kp_spec, kc_spec, bias_spec, sink_spec = _swa_specs(b, g, s, d)
    kv_acc = pl.BlockSpec((1, 1, s, d), lambda h, bi, n: (bi, h, 0, 0))
    dsink_spec = pl.BlockSpec((1, g, LANES), lambda h, bi, n: (h, 0, 0))

    def body(q_ref, kp_ref, kc_ref, vp_ref, vc_ref, bias_ref, sink_ref, do_ref, dq_ref, dk_ref, dv_ref, dbias_ref, dsink_ref):
        h, bi, n = pl.program_id(0), pl.program_id(1), pl.program_id(2)
        valid = _swa_valid(n)
        kb = jnp.concatenate([kp_ref[0, 0], kc_ref[0, 0]], axis=0)
        vb = jnp.concatenate([vp_ref[0, 0], vc_ref[0, 0]], axis=0)

        @pl.when(n == 0)
        def _():
            dk_ref[...] = jnp.zeros_like(dk_ref)
            dv_ref[...] = jnp.zeros_like(dv_ref)

        @pl.when((n == 0) & (bi == 0))
        def _():
            dbias_ref[...] = jnp.zeros_like(dbias_ref)
            dsink_ref[...] = jnp.zeros_like(dsink_ref)

        dkb = jnp.zeros((2 * w, d), F32)
        dvb = jnp.zeros((2 * w, d), F32)
        for gi in range(g):
            qg, dog = q_ref[0, 0, gi], do_ref[0, 0, gi]
            p, ps = _swa_probs(qg, kb, bias_ref[0, gi], sink_ref[h * g + gi], valid, scale)
            dp = _dot(dog, vb, NT)
            delta = jnp.sum(p * dp, axis=-1, keepdims=True)
            dl = p * (dp - delta)
            dq_ref[0, 0, gi] = _dot(dl, kb) * scale
            dkb = dkb + _dot(dl, qg, TN) * scale
            dvb = dvb + _dot(p, dog, TN)
            dbias_ref[0, gi] += dl
            dsink_ref[0, gi:gi + 1, :] += jnp.broadcast_to(jnp.sum(-ps * delta, axis=0, keepdims=True), (1, LANES))

        c0 = pl.multiple_of(n * w, w)
        dk_ref[0, 0, pl.ds(c0, w), :] += dkb[w:]
        dv_ref[0, 0, pl.ds(c0, w), :] += dvb[w:]

        @pl.when(n > 0)
        def _():
            p0 = pl.multiple_of((n - 1) * w, w)
            dk_ref[0, 0, pl.ds(p0, w), :] += dkb[:w]
            dv_ref[0, 0, pl.ds(p0, w), :] += dvb[:w]

    return pl.pallas_call(
        body, name=name, grid=(kvh, b, s // w),
        in_specs=[q_spec, kp_spec, kc_spec, kp_spec, kc_spec, bias_spec, sink_spec, q_spec],
        out_specs=[q_spec, kv_acc, kv_acc, bias_spec, dsink_spec],
        out_shape=[jax.ShapeDtypeStruct(q.shape, F32), jax.ShapeDtypeStruct(k.shape, F32), jax.ShapeDtypeStruct(k.shape, F32),
                   jax.ShapeDtypeStruct(bias.shape, F32), jax.ShapeDtypeStruct((kvh, g, LANES), F32)],
        compiler_params=_params(("arbitrary", "arbitrary", "arbitrary")),
    )(q, k, k, v, v, bias, sinks, do)


def _t5_bias(rel_bias):
    t = np.arange(WINDOW)[:, None]
    s = np.arange(2 * WINDOW)[None, :]
    dist = t + WINDOW - s
    max_exact = N_BUCKETS // 2
    large = max_exact + (np.log(np.maximum(dist, max_exact) / max_exact) / math.log(MAX_DISTANCE / max_exact)
                         * (N_BUCKETS - max_exact)).astype(np.int32)
    large = np.minimum(large, N_BUCKETS - 1)
    bucket = np.where(dist < max_exact, np.maximum(dist, 0), large).astype(np.int32)
    onehot = jnp.asarray(np.eye(N_BUCKETS, dtype=np.float32)[bucket])
    bias = jnp.einsum("tsb,bh->hts", onehot, rel_bias.astype(F32), precision=lax.Precision.HIGHEST)
    return bias.reshape(SW_KV_HEADS, SW_GROUP, WINDOW, 2 * WINDOW)


CONV_W = 3


def _shift_down(x, k):
    row = lax.broadcasted_iota(jnp.int32, x.shape, 0)
    return jnp.where(row >= k, pltpu.roll(x, k, axis=0), 0.0)


def _shift_up(x, k):
    n = x.shape[0]
    row = lax.broadcasted_iota(jnp.int32, x.shape, 0)
    return jnp.where(row < n - k, pltpu.roll(x, n - k, axis=0), 0.0)


def _conv3(u, w, bvec):
    return w[0:1] * _shift_down(u, 2) + w[1:2] * _shift_down(u, 1) + w[2:3] * u + bvec


def _convglu_fwd(u, w, bvec, name, out_dtype=F32):
    b, s, f2 = u.shape
    f = f2 // 2
    tc = _pick(f, 256)
    nt = f // tc

    def body(ug_ref, uu_ref, wg_ref, wu_ref, bg_ref, bu_ref, o_ref):
        cg = _conv3(ug_ref[0], wg_ref[...], bg_ref[...])
        cu = _conv3(uu_ref[0], wu_ref[...], bu_ref[...])
        o_ref[0] = (_silu(cg) * cu).astype(out_dtype)

    ug = pl.BlockSpec((1, s, tc), lambda j, bi: (bi, 0, j))
    uu = pl.BlockSpec((1, s, tc), lambda j, bi: (bi, 0, j + nt))
    wg = pl.BlockSpec((CONV_W, tc), lambda j, bi: (0, j))
    wu = pl.BlockSpec((CONV_W, tc), lambda j, bi: (0, j + nt))
    bg = pl.BlockSpec((1, tc), lambda j, bi: (0, j))
    bu = pl.BlockSpec((1, tc), lambda j, bi: (0, j + nt))
    bv = bvec.reshape(1, f2)
    return pl.pallas_call(body, name=name, grid=(nt, b), in_specs=[ug, uu, wg, wu, bg, bu], out_specs=ug,
                          out_shape=jax.ShapeDtypeStruct((b, s, f), out_dtype),
                          compiler_params=_params(("parallel", "parallel")))(u, u, w, w, bv, bv)


def _convglu_bwd(u, w, bvec, dact, name):
    b, s, f2 = u.shape
    f = f2 // 2
    tc = LANES
    nt = f // tc

    def taps(dc, uv):
        rows = [jnp.sum(dc * _shift_down(uv, 2), axis=0, keepdims=True), jnp.sum(dc * _shift_down(uv, 1), axis=0, keepdims=True),
                jnp.sum(dc * uv, axis=0, keepdims=True), jnp.sum(dc, axis=0, keepdims=True)]
        return jnp.concatenate(rows + [jnp.zeros((4, tc), F32)], axis=0)

    def back(dc, wv):
        return wv[2:3] * dc + wv[1:2] * _shift_up(dc, 1) + wv[0:1] * _shift_up(dc, 2)

    def body(ug_ref, uu_ref, wg_ref, wu_ref, bg_ref, bu_ref, da_ref, dug_ref, duu_ref, dwg_ref, dwu_ref):
        ugv, uuv, da = ug_ref[0], uu_ref[0], da_ref[0]
        cg = _conv3(ugv, wg_ref[...], bg_ref[...])
        cu = _conv3(uuv, wu_ref[...], bu_ref[...])
        sg = jax.nn.sigmoid(cg)
        dcu = da * (cg * sg)
        dcg = da * cu * (sg * (1.0 + cg * (1.0 - sg)))
        dug_ref[0] = back(dcg, wg_ref[...])
        duu_ref[0] = back(dcu, wu_ref[...])

        @pl.when(pl.program_id(1) == 0)
        def _():
            dwg_ref[...] = jnp.zeros_like(dwg_ref)
            dwu_ref[...] = jnp.zeros_like(dwu_ref)

        dwg_ref[...] += taps(dcg, ugv)
        dwu_ref[...] += taps(dcu, uuv)

    ug = pl.BlockSpec((1, s, tc), lambda j, bi: (bi, 0, j))
    uu = pl.BlockSpec((1, s, tc), lambda j, bi: (bi, 0, j + nt))
    wg = pl.BlockSpec((CONV_W, tc), lambda j, bi: (0, j))
    wu = pl.BlockSpec((CONV_W, tc), lambda j, bi: (0, j + nt))
    bg = pl.BlockSpec((1, tc), lambda j, bi: (0, j))
    bu = pl.BlockSpec((1, tc), lambda j, bi: (0, j + nt))
    acc = pl.BlockSpec((8, tc), lambda j, bi: (0, j))
    bv = bvec.reshape(1, f2)
    return pl.pallas_call(
        body, name=name, grid=(nt, b), in_specs=[ug, uu, wg, wu, bg, bu, ug], out_specs=[ug, ug, acc, acc],
        out_shape=[jax.ShapeDtypeStruct((b, s, f), F32)] * 2 + [jax.ShapeDtypeStruct((8, f), F32)] * 2,
        compiler_params=_params(("parallel", "arbitrary")),
    )(u, u, w, w, bv, bv, dact)


def _adamw(w, g, m, v, name):
    r, c = w.shape
    tr = _row_tile(r, c)
    c1 = 1.0 - ADAM_B1 ** ADAM_STEP
    c2 = 1.0 - ADAM_B2 ** ADAM_STEP

    def body(w_ref, g_ref, m_ref, v_ref, d_ref, mo_ref, vo_ref):
        gv = g_ref[...]
        mn = ADAM_B1 * m_ref[...] + (1.0 - ADAM_B1) * gv
        vn = ADAM_B2 * v_ref[...] + (1.0 - ADAM_B2) * (gv * gv)
        d_ref[...] = -ADAM_LR * ((mn / c1) / (jnp.sqrt(vn / c2) + ADAM_EPS) + ADAM_WD * w_ref[...])
        mo_ref[...] = mn
        vo_ref[...] = vn

    blk = pl.BlockSpec((tr, c), lambda i: (i, 0))
    return pl.pallas_call(body, name=name, grid=(r // tr,), in_specs=[blk] * 4, out_specs=[blk] * 3,
                          out_shape=[jax.ShapeDtypeStruct((r, c), F32)] * 3, compiler_params=_params(("parallel",)))(w, g, m, v)


MESH = pl.DeviceIdType.MESH
ANY = pl.BlockSpec(memory_space=pl.ANY)


def _position():
    return lax.axis_index("x"), lax.axis_index("y"), lax.axis_index("c")


def _all_gather(x, name):
    r, c = x.shape

    def body(x_ref, out_ref, send_sems, recv_sems, local_sem):
        mx, my, mc = _position()
        me, sibling = (mx, my, mc), (mx, my, 1 - mc)
        chips = [(1 - mx, my), (mx, 1 - my), (1 - mx, 1 - my)]

        def slot(px, py, pc):
            return out_ref.at[4 * px + 2 * py + pc]

        def copy(k, block, to, src=None):
            return pltpu.make_async_remote_copy(
                src_ref=slot(*block) if src is None else src, dst_ref=slot(*block),
                send_sem=send_sems.at[k], recv_sem=recv_sems.at[k], device_id=to, device_id_type=MESH)

        mine = pltpu.make_async_copy(x_ref, slot(*me), local_sem.at[0])
        mine.start()
        first = [copy(0, me, sibling, src=x_ref)]
        first += [copy(1 + j, me, (*chip, mc), src=x_ref) for j, chip in enumerate(chips)]
        for cp in first:
            cp.start()
        passed = [copy(4 + j, (*chip, mc), sibling) for j, chip in enumerate(chips)]
        for j, chip in enumerate(chips):
            copy(1 + j, (*chip, mc), me).wait_recv()
            passed[j].start()
        copy(0, sibling, me).wait_recv()
        for j, chip in enumerate(chips):
            copy(4 + j, (*chip, 1 - mc), me).wait_recv()
        for cp in first + passed:
            cp.wait_send()
        mine.wait()

    return pl.pallas_call(
        body, name=name, out_shape=jax.ShapeDtypeStruct((N_DEV, r, c), x.dtype), in_specs=[ANY], out_specs=ANY,
        scratch_shapes=[pltpu.SemaphoreType.DMA((7,)), pltpu.SemaphoreType.DMA((7,)), pltpu.SemaphoreType.DMA((1,))],
    )(x)


def _dev_rows(ref, dev, a):
    return ref.at[:, pl.ds(pl.multiple_of(dev * a, 16), a), :]


def _all_gather_rows(shards, name):
    nt = len(shards)

    def body(*refs):
        x_refs, out_refs = refs[:nt], refs[nt:2 * nt]
        send_sems, recv_sems, local_sems = refs[2 * nt:]
        mx, my, mc = _position()
        me, sibling = (mx, my, mc), (mx, my, 1 - mc)
        chips = [(1 - mx, my), (mx, 1 - my), (1 - mx, 1 - my)]

        def slot(t, px, py, pc):
            return _dev_rows(out_refs[t], 4 * px + 2 * py + pc, shards[t].shape[1])

        def copy(t, k, block, to, src=None):
            return pltpu.make_async_remote_copy(
                src_ref=slot(t, *block) if src is None else src, dst_ref=slot(t, *block),
                send_sem=send_sems.at[7 * t + k], recv_sem=recv_sems.at[7 * t + k], device_id=to, device_id_type=MESH)

        mine = [pltpu.make_async_copy(x_refs[t], slot(t, *me), local_sems.at[t]) for t in range(nt)]
        first = [copy(t, 0, me, sibling, src=x_refs[t]) for t in range(nt)]
        first += [copy(t, 1 + j, me, (*chip, mc), src=x_refs[t]) for j, chip in enumerate(chips) for t in range(nt)]
        for cp in mine + first:
            cp.start()
        passed = []
        for j, chip in enumerate(chips):
            for t in range(nt):
                copy(t, 1 + j, (*chip, mc), me).wait_recv()
                fwd = copy(t, 4 + j, (*chip, mc), sibling)
                fwd.start()
                passed.append(fwd)
        for t in range(nt):
            copy(t, 0, sibling, me).wait_recv()
        for j, chip in enumerate(chips):
            for t in range(nt):
                copy(t, 4 + j, (*chip, 1 - mc), me).wait_recv()
        for cp in first + passed:
            cp.wait_send()
        for cp in mine:
            cp.wait()

    out_shape = [jax.ShapeDtypeStruct((x.shape[0], N_DEV * x.shape[1], x.shape[2]), x.dtype) for x in shards]
    return pl.pallas_call(
        body, name=name, out_shape=out_shape, in_specs=[ANY] * nt, out_specs=[ANY] * nt,
        scratch_shapes=[pltpu.SemaphoreType.DMA((7 * nt,)), pltpu.SemaphoreType.DMA((7 * nt,)), pltpu.SemaphoreType.DMA((nt,))],
    )(*shards)


def _rs_pair(gs, name):
    nt = len(gs)

    def body(*refs):
        g_refs, a_refs = refs[:nt], refs[nt:2 * nt]
        send_sems, recv_sems = refs[2 * nt:]
        mx, my, mc = _position()
        copies = [pltpu.make_async_remote_copy(
            src_ref=_dev_rows(g_refs[t], 2 * j + 1 - mc, gs[t].shape[1] // N_DEV), dst_ref=a_refs[t].at[j],
            send_sem=send_sems.at[4 * t + j], recv_sem=recv_sems.at[4 * t + j],
            device_id=(mx, my, 1 - mc), device_id_type=MESH) for t in range(nt) for j in range(4)]
        for cp in copies:
            cp.start()
        for cp in copies:
            cp.wait()

    out_shape = [jax.ShapeDtypeStruct((4, g.shape[0], g.shape[1] // N_DEV, g.shape[2]), g.dtype) for g in gs]
    return pl.pallas_call(
        body, name=name, out_shape=out_shape, in_specs=[ANY] * nt, out_specs=[ANY] * nt,
        scratch_shapes=[pltpu.SemaphoreType.DMA((4 * nt,)), pltpu.SemaphoreType.DMA((4 * nt,))],
    )(*gs)


def _rs_chips(ps, name):
    nt = len(ps)

    def body(*refs):
        p_refs, b_refs = refs[:nt], refs[nt:2 * nt]
        send_sems, recv_sems = refs[2 * nt:]
        mx, my, mc = _position()
        chips = [(1 - mx, my), (mx, 1 - my), (1 - mx, 1 - my)]
        copies = [pltpu.make_async_remote_copy(
            src_ref=p_refs[t].at[2 * cx + cy], dst_ref=b_refs[t].at[k],
            send_sem=send_sems.at[3 * t + k], recv_sem=recv_sems.at[3 * t + k],
            device_id=(cx, cy, mc), device_id_type=MESH) for t in range(nt) for k, (cx, cy) in enumerate(chips)]
        for cp in copies:
            cp.start()
        for cp in copies:
            cp.wait()

    out_shape = [jax.ShapeDtypeStruct((3,) + p.shape[1:], p.dtype) for p in ps]
    return pl.pallas_call(
        body, name=name, out_shape=out_shape, in_specs=[ANY] * nt, out_specs=[ANY] * nt,
        scratch_shapes=[pltpu.SemaphoreType.DMA((3 * nt,)), pltpu.SemaphoreType.DMA((3 * nt,))],
    )(*ps)


def _div_tile(a, b):
    best = 16
    for t in range(16, a + 1, 16):
        if a % t == 0 and t * b * 4 <= 2 * 1024 * 1024:
            best = t
    return best


def _rs_add_pair(g, a, core, name):
    l, a8, b = g.shape
    rows = a8 // N_DEV
    ta = _div_tile(rows, b)

    def body(core_ref, g_ref, a_ref, p_ref):
        p_ref[...] = (g_ref[...] + a_ref[...]).astype(BF16)

    grid_spec = pltpu.PrefetchScalarGridSpec(
        num_scalar_prefetch=1, grid=(4, l, rows // ta),
        in_specs=[pl.BlockSpec((1, 1, ta, b), lambda j, li, i, core_ref: (li, 2 * j + core_ref[0], i, 0)),
                  pl.BlockSpec((1, 1, ta, b), lambda j, li, i, core_ref: (j, li, i, 0))],
        out_specs=pl.BlockSpec((1, 1, ta, b), lambda j, li, i, core_ref: (j, li, i, 0)))
    return pl.pallas_call(body, name=name, grid_spec=grid_spec, out_shape=jax.ShapeDtypeStruct((4, l, rows, b), BF16),
                          compiler_params=_params(("parallel", "parallel", "parallel")))(core, g.reshape(l, N_DEV, rows, b), a)


def _rs_final(g, a, bsum, where, name):
    l, a8, b = g.shape
    rows = a8 // N_DEV
    ta = _div_tile(rows, b)

    def body(where_ref, g_ref, a_ref, b_ref, o_ref):
        own = g_ref[0, 0] + a_ref[0, 0]
        o_ref[0] = ((own + b_ref[0, 0].astype(F32)) + b_ref[1, 0].astype(F32)) + b_ref[2, 0].astype(F32)

    grid_spec = pltpu.PrefetchScalarGridSpec(
        num_scalar_prefetch=1, grid=(l, rows // ta),
        in_specs=[pl.BlockSpec((1, 1, ta, b), lambda li, i, w_ref: (li, w_ref[0], i, 0)),
                  pl.BlockSpec((1, 1, ta, b), lambda li, i, w_ref: (w_ref[1], li, i, 0)),
                  pl.BlockSpec((3, 1, ta, b), lambda li, i, w_ref: (0, li, i, 0))],
        out_specs=pl.BlockSpec((1, ta, b), lambda li, i, w_ref: (li, i, 0)))
    return pl.pallas_call(body, name=name, grid_spec=grid_spec, out_shape=jax.ShapeDtypeStruct((l, rows, b), F32),
                          compiler_params=_params(("parallel", "parallel")))(where, g.reshape(l, N_DEV, rows, b), a, bsum)


def _sum_devices(x, name):
    _, r, c = x.shape

    def body(x_ref, o_ref):
        acc = x_ref[0]
        for d in range(1, N_DEV):
            acc = acc + x_ref[d]
        o_ref[...] = acc

    return pl.pallas_call(body, name=name, out_shape=jax.ShapeDtypeStruct((r, c), F32))(x)


BIG = (("ab_w_in", "col"), ("ab_w_out", "row"), ("c_w_in", "col"), ("c_w_out", "row"),
       ("ffn_up", "col"), ("ffn_down", "row"), ("ple_gate", "row"), ("ple_proj", "col"))
KIND = dict(BIG)


def _row_block(shard, kind):
    return shard.transpose(0, 2, 1) if kind == "col" else shard


def _pad_rows(flat):
    pad = -flat.shape[0] % (8 * LANES)
    return jnp.pad(flat, (0, pad)).reshape(-1, LANES)


def _heads_out(x, b, s, nh, d):
    return x.reshape(b, s, nh, d).transpose(0, 2, 1, 3).reshape(b * nh, s, d)


def _heads_in(x, b, s, nh, d):
    return x.reshape(b, nh, s, d).transpose(0, 2, 1, 3).reshape(b * s, nh * d)


def _lower_bounds(logits):
    c = jnp.cumsum(jax.nn.softmax(logits.astype(F32), axis=0), axis=0)
    return c - c[0]


SB_COLS = tuple(k * SB_WIDTH // LANES for k in range(3))
HG_COLS = tuple((3 * SB_WIDTH + k * HG_W) // LANES for k in range(3))
HG_GATE_COL = (3 * SB_WIDTH + 3 * HG_W) // LANES
HG_OUT_COL = SB_WIDTH // LANES


def _forward_backward(x, p, target, W, P):
    b, s, dm = x.shape
    n = b * s
    h = x.reshape(n, dm)
    lbs, lb_vjp = jax.vjp(_lower_bounds, P["hg_lb_logits"])
    bias, bias_vjp = jax.vjp(_t5_bias, P["rel_bias"])
    saved = []
    gw = {name: lax.empty(W[name].shape, F32) for name, _ in BIG}

    def times_w(a, name, l, tag, res=None):
        return _mm(a, W[name], "nt" if KIND[name] == "col" else "nn", tag, res=res, layer=l)

    def times_wt(dy, name, l, tag):
        return _mm(dy, W[name], "nn" if KIND[name] == "col" else "nt", tag, layer=l)

    def grad_w(a, dy, name, l, tag):
        lhs, rhs = (dy, a) if KIND[name] == "col" else (a, dy)
        gw[name] = _mm(lhs, rhs, "tn", tag, into=(gw[name], l, 0))

    for i in range(DEPTH):
        j = i // 2
        r = {"h0": h}
        hn = _rms_fwd(h, P["mix_norm"][i], f"mix_norm_f{i}", BF16)
        r["hn"] = hn
        if i % 2 == 0:
            proj = times_w(hn, "ab_w_in", j, f"ab_in_f{i}")
            oa, lta = _sb_fwd(proj, SB_COLS, b, s, f"sb_f{i}")
            ob, st = _hg_fwd(proj, HG_COLS, lbs[j].reshape(HG_HEADS, HG_DK), b, s, f"hg_f{i}")
            obg = _gnorm_fwd(ob, proj, HG_GATE_COL, P["hg_out_norm"][j], f"hg_norm_f{i}")
            cat = jnp.concatenate([oa, obg], axis=1).astype(BF16)
            h = times_w(cat, "ab_w_out", j, f"ab_out_f{i}", res=h)
            r.update(lta=lta, proj=proj, ob=ob, st=st, cat=cat)
        else:
            proj = times_w(hn, "c_w_in", j, f"c_in_f{i}")
            nq = SW_HEADS * SW_DIM
            nkv = SW_KV_HEADS * SW_DIM
            q = _heads_out(proj[:, :nq], b, s, SW_HEADS, SW_DIM).reshape(-1, SW_DIM)
            k = _heads_out(proj[:, nq:nq + nkv], b, s, SW_KV_HEADS, SW_DIM).reshape(-1, SW_DIM)
            v = _heads_out(proj[:, nq + nkv:], b, s, SW_KV_HEADS, SW_DIM).reshape(b, SW_KV_HEADS, s, SW_DIM)
            qn = _rms_fwd(q, P["q_norm"][j], f"q_norm_f{i}").reshape(b, SW_KV_HEADS, SW_GROUP, s, SW_DIM)
            kn = _rms_fwd(k, P["k_norm"][j], f"k_norm_f{i}").reshape(b, SW_KV_HEADS, s, SW_DIM)
            o = _swa_fwd(qn, kn, v, P["sinks"][j], bias, f"swa_f{i}")
            o2 = _heads_in(o.reshape(b * SW_HEADS, s, SW_DIM), b, s, SW_HEADS, SW_DIM).astype(BF16)
            h = times_w(o2, "c_w_out", j, f"c_out_f{i}", res=h)
            r.update(q=q, k=k, v=v, qn=qn, kn=kn, o2=o2)
        r["h1"] = h
        hn2 = _rms_fwd(h, P["ffn_norm"][i], f"ffn_norm_f{i}", BF16)
        u = times_w(hn2, "ffn_up", i, f"ffn_up_f{i}").reshape(b, s, 2 * D_FF)
        act = _convglu_fwd(u, W["ffn_conv"][i], P["ffn_conv_b"][i], f"conv_f{i}", BF16).reshape(n, D_FF)
        h = times_w(act, "ffn_down", i, f"ffn_down_f{i}", res=h)
        r.update(hn2=hn2, u=u, act=act, h2=h)
        hn3 = _rms_fwd(h, P["ple_norm"][i], f"ple_norm_f{i}", BF16)
        z = times_w(hn3, "ple_gate", i, f"ple_gate_f{i}")
        pi = p[i].reshape(n, PLE_DIM)
        e = times_w(pi, "ple_proj", i, f"ple_proj_f{i}")
        h = _sigmul_fwd(z, e, h, f"ple_f{i}")
        r.update(hn3=hn3, z=z, e=e, pi=pi)
        saved.append(r)

    loss, dh = _loss_fwd(h, target.reshape(n, dm), "loss")

    gconv = [None] * DEPTH
    gp = {name: [None] * P[name].shape[0] for name in ("mix_norm", "hg_out_norm", "q_norm", "k_norm", "sinks",
                                                        "ffn_norm", "ffn_conv_b", "ple_norm")}
    dlbs = [None] * (DEPTH // 2)
    dbias = jnp.zeros_like(bias)

    for i in reversed(range(DEPTH)):
        j = i // 2
        r = saved[i]
        dz, de = _sigmul_bwd(r["z"], r["e"], dh, f"ple_b{i}")
        grad_w(r["pi"], de, "ple_proj", i, f"ple_proj_g{i}")
        grad_w(r["hn3"], dz, "ple_gate", i, f"ple_gate_g{i}")
        dhn3 = times_wt(dz, "ple_gate", i, f"ple_gate_b{i}")
        dh, gp["ple_norm"][i] = _rms_bwd(r["h2"], P["ple_norm"][i], dhn3, f"ple_norm_b{i}", res=dh)

        dact = times_wt(dh, "ffn_down", i, f"ffn_down_b{i}").reshape(b, s, D_FF)
        grad_w(r["act"], dh, "ffn_down", i, f"ffn_down_g{i}")
        dug, duu, ag, au = _convglu_bwd(r["u"], W["ffn_conv"][i], P["ffn_conv_b"][i], dact, f"conv_b{i}")
        gconv[i] = jnp.concatenate([ag[:CONV_W], au[:CONV_W]], axis=-1)
        gp["ffn_conv_b"][i] = jnp.concatenate([ag[CONV_W], au[CONV_W]], axis=-1)
        dhn2 = None
        for half, dpart in enumerate((dug.reshape(n, D_FF), duu.reshape(n, D_FF))):
            gw["ffn_up"] = _mm(dpart, r["hn2"], "tn", f"ffn_up_g{i}_{half}", into=(gw["ffn_up"], i, half * D_FF))
            dhn2 = _mm(dpart, W["ffn_up"], "nn", f"ffn_up_b{i}_{half}", layer=i, b_rows=(half * D_FF, D_FF), res=dhn2)
        dh, gp["ffn_norm"][i] = _rms_bwd(r["h1"], P["ffn_norm"][i], dhn2, f"ffn_norm_b{i}", res=dh)

        if i % 2 == 0:
            dcat = times_wt(dh, "ab_w_out", j, f"ab_out_b{i}")
            grad_w(r["cat"], dh, "ab_w_out", j, f"ab_out_g{i}")
            dob, dgb, gp["hg_out_norm"][j] = _gnorm_bwd(r["ob"], r["proj"], HG_GATE_COL, P["hg_out_norm"][j], dcat, HG_OUT_COL,
                                                        f"hg_norm_b{i}")
            dqb, dfb, dib, dlb = _hg_bwd(r["proj"], HG_COLS, lbs[j].reshape(HG_HEADS, HG_DK), r["st"], dob, b, s, f"hg_b{i}")
            dlbs[j] = dlb.reshape(b, HG_W).sum(axis=0)
            dqa, dka, dva = _sb_bwd(r["proj"], SB_COLS, r["lta"], dcat, 0, b, s, f"sb_b{i}")
            dproj = jnp.concatenate([dqa, dka, dva, dqb, dfb, dib, dgb], axis=1)
            grad_w(r["hn"], dproj, "ab_w_in", j, f"ab_in_g{i}")
            dhn = times_wt(dproj, "ab_w_in", j, f"ab_in_b{i}")
        else:
            do2 = times_wt(dh, "c_w_out", j, f"c_out_b{i}")
            grad_w(r["o2"], dh, "c_w_out", j, f"c_out_g{i}")
            do = _heads_out(do2, b, s, SW_HEADS, SW_DIM).reshape(b, SW_KV_HEADS, SW_GROUP, s, SW_DIM)
            dqn, dkn, dv, dbias_i, dsink = _swa_bwd(r["qn"], r["kn"], r["v"], P["sinks"][j], bias, do, f"swa_b{i}")
            dbias = dbias + dbias_i
            gp["sinks"][j] = dsink[:, :, 0].reshape(SW_HEADS)
            dq, gp["q_norm"][j] = _rms_bwd(r["q"], P["q_norm"][j], dqn.reshape(-1, SW_DIM), f"q_norm_b{i}")
            dk, gp["k_norm"][j] = _rms_bwd(r["k"], P["k_norm"][j], dkn.reshape(-1, SW_DIM), f"k_norm_b{i}")
            dproj = jnp.concatenate([_heads_in(dq.reshape(b * SW_HEADS, s, SW_DIM), b, s, SW_HEADS, SW_DIM),
                                     _heads_in(dk.reshape(b * SW_KV_HEADS, s, SW_DIM), b, s, SW_KV_HEADS, SW_DIM),
                                     _heads_in(dv.reshape(b * SW_KV_HEADS, s, SW_DIM), b, s, SW_KV_HEADS, SW_DIM)], axis=1)
            grad_w(r["hn"], dproj, "c_w_in", j, f"c_in_g{i}")
            dhn = times_wt(dproj, "c_w_in", j, f"c_in_b{i}")
        dh, gp["mix_norm"][i] = _rms_bwd(r["h0"], P["mix_norm"][i], dhn, f"mix_norm_b{i}", res=dh)

    gp = {name: jnp.stack(v) for name, v in gp.items()}
    gp["hg_lb_logits"] = lb_vjp(jnp.stack(dlbs))[0]
    gp["rel_bias"] = bias_vjp(dbias)[0]
    return loss[0, 0], dh.reshape(b, s, dm), gw, jnp.stack(gconv), gp


WEIGHTS = ("mix_norm", "ab_w_in", "hg_lb_logits", "hg_out_norm", "ab_w_out", "c_w_in", "q_norm", "k_norm", "sinks", "rel_bias",
           "c_w_out", "ffn_norm", "ffn_up", "ffn_conv", "ffn_conv_b", "ffn_down", "ple_norm", "ple_gate", "ple_proj")
SMALL = ("mix_norm", "hg_lb_logits", "hg_out_norm", "q_norm", "k_norm", "sinks", "rel_bias", "ffn_norm", "ffn_conv_b", "ple_norm")


def _step(x, p, target, w, m, v):
    names = [name for name, _ in BIG]
    mx, my, mc = _position()
    dev = 4 * mx + 2 * my + mc

    blocks = [_row_block(w[name], KIND[name]).astype(BF16) for name in names]
    full = dict(zip(names, _all_gather_rows(blocks, "gather_weights")))
    nl, taps, cs = w["ffn_conv"].shape
    conv_all = _all_gather(_pad_rows(w["ffn_conv"].reshape(-1)), "gather_conv").reshape(N_DEV, -1)[:, :nl * taps * cs]
    full["ffn_conv"] = conv_all.reshape(N_DEV, nl, taps, cs).transpose(1, 2, 0, 3).reshape(nl, taps, N_DEV * cs)

    small = {name: w[name] for name in SMALL}
    loss, grad_x, gw, gconv, gp = _forward_backward(x, p, target, full, small)

    core = jnp.reshape(mc, (1,)).astype(jnp.int32)
    where = jnp.stack([dev, 2 * mx + my]).astype(jnp.int32)
    parts = [gw[name] for name in names]
    from_sibling = _rs_pair(parts, "reduce_pair")
    chip_sums = [_rs_add_pair(g, a, core, f"reduce_pair_add_{name}") for name, g, a in zip(names, parts, from_sibling)]
    from_chips = _rs_chips(chip_sums, "reduce_chips")
    grads = {name: _row_block(_rs_final(g, a, bs, where, f"reduce_final_{name}"), KIND[name])
             for name, g, a, bs in zip(names, parts, from_sibling, from_chips)}

    flat_small = jnp.concatenate([gp[name].reshape(-1) for name in SMALL] + [gconv.reshape(-1), loss.reshape(1)])
    small_sum = _sum_devices(_all_gather(_pad_rows(flat_small), "gather_small"), "sum_small").reshape(-1)
    e0 = 0
    for name in SMALL:
        cnt = math.prod(w[name].shape)
        grads[name] = small_sum[e0:e0 + cnt].reshape(w[name].shape)
        e0 += cnt
    gconv_sum = small_sum[e0:e0 + gconv.size].reshape(gconv.shape)
    grads["ffn_conv"] = lax.dynamic_slice_in_dim(gconv_sum, dev * cs, cs, axis=2)
    loss = small_sum[e0 + gconv.size]

    deltas, new_m, new_v = {}, {}, {}
    for name in WEIGHTS:
        shape = w[name].shape
        view = (-1, shape[-1]) if len(shape) > 1 else (1, -1)
        d_, m_, v_ = _adamw(w[name].reshape(view), grads[name].reshape(view), m[name].reshape(view), v[name].reshape(view), f"adamw_{name}")
        deltas[name], new_m[name], new_v[name] = d_.reshape(shape), m_.reshape(shape), v_.reshape(shape)
    return (loss, grad_x, *[grads[k] for k in WEIGHTS], *[deltas[k] for k in WEIGHTS],
            *[new_m[k] for k in WEIGHTS], *[new_v[k] for k in WEIGHTS])


def kernel(x, p, mix_norm, ab_w_in, hg_lb_logits, hg_out_norm, ab_w_out, c_w_in, q_norm, k_norm, sinks, rel_bias, c_w_out, ffn_norm, ffn_up, ffn_conv, ffn_conv_b, ffn_down, ple_norm, ple_gate, ple_proj, loss_target, m_mix_norm, m_ab_w_in, m_hg_lb_logits, m_hg_out_norm, m_ab_w_out, m_c_w_in, m_q_norm, m_k_norm, m_sinks, m_rel_bias, m_c_w_out, m_ffn_norm, m_ffn_up, m_ffn_conv, m_ffn_conv_b, m_ffn_down, m_ple_norm, m_ple_gate, m_ple_proj, v_mix_norm, v_ab_w_in, v_hg_lb_logits, v_hg_out_norm, v_ab_w_out, v_c_w_in, v_q_norm, v_k_norm, v_sinks, v_rel_bias, v_c_w_out, v_ffn_norm, v_ffn_up, v_ffn_conv, v_ffn_conv_b, v_ffn_down, v_ple_norm, v_ple_gate, v_ple_proj):
    w = dict(zip(WEIGHTS, (mix_norm, ab_w_in, hg_lb_logits, hg_out_norm, ab_w_out, c_w_in, q_norm, k_norm, sinks, rel_bias, c_w_out,
                           ffn_norm, ffn_up, ffn_conv, ffn_conv_b, ffn_down, ple_norm, ple_gate, ple_proj)))
    m = dict(zip(WEIGHTS, (m_mix_norm, m_ab_w_in, m_hg_lb_logits, m_hg_out_norm, m_ab_w_out, m_c_w_in, m_q_norm, m_k_norm, m_sinks,
                           m_rel_bias, m_c_w_out, m_ffn_norm, m_ffn_up, m_ffn_conv, m_ffn_conv_b, m_ffn_down, m_ple_norm, m_ple_gate,
                           m_ple_proj)))
    v = dict(zip(WEIGHTS, (v_mix_norm, v_ab_w_in, v_hg_lb_logits, v_hg_out_norm, v_ab_w_out, v_c_w_in, v_q_norm, v_k_norm, v_sinks,
                           v_rel_bias, v_c_w_out, v_ffn_norm, v_ffn_up, v_ffn_conv, v_ffn_conv_b, v_ffn_down, v_ple_norm, v_ple_gate,
                           v_ple_proj)))
    return _step(x, p, loss_target, w, m, v)
```

```python
import functools
import math

import numpy as np
import jax
import jax.numpy as jnp
from jax import lax
from jax.experimental import pallas as pl
from jax.experimental.pallas import tpu as pltpu

F32 = jnp.float32
BF16 = jnp.bfloat16

D_MODEL = 1024
DEPTH = 4
PLE_DIM = 256
EPS = 1e-6
SB_HEADS, SB_DIM = 8, 64
SB_WIDTH = SB_HEADS * SB_DIM
HG_HEADS, HG_DK, HG_DV = 4, 128, 128
HG_W = HG_HEADS * HG_DK
AB_IN = 3 * SB_WIDTH + 4 * HG_W
SW_HEADS, SW_KV_HEADS, SW_DIM = 16, 4, 64
SW_GROUP = SW_HEADS // SW_KV_HEADS
WINDOW = 128
C_IN = (SW_HEADS + 2 * SW_KV_HEADS) * SW_DIM
N_BUCKETS, MAX_DISTANCE = 32, 128
D_FF = 2816
N_DEV = 8

ADAM_LR, ADAM_B1, ADAM_B2, ADAM_EPS, ADAM_WD, ADAM_STEP = 0.001, 0.9, 0.999, 1e-08, 0.01, 10

LANES = 128
VMEM_LIMIT = 48 * 1024 * 1024

NN = (((1,), (0,)), ((), ()))
NT = (((1,), (1,)), ((), ()))
TN = (((0,), (0,)), ((), ()))


MXU_DTYPE = BF16


def _bf(x):
    return x.astype(MXU_DTYPE)


def _dot(a, b, dims=NN):
    return lax.dot_general(_bf(a), _bf(b), dims, preferred_element_type=F32)


def _split3(x):
    x1 = _bf(x)
    r = x - x1.astype(F32)
    x2 = _bf(r)
    x3 = _bf(r - x2.astype(F32))
    return x1, x2, x3


def _dot_exact_lhs01(m, x, terms=3):
    parts = _split3(x)[:terms]
    out = lax.dot_general(m, parts[0], NN, preferred_element_type=F32)
    for p_ in parts[1:]:
        out = out + lax.dot_general(m, p_, NN, preferred_element_type=F32)
    return out


def _dot_exact_rhs01(x, m, terms=2):
    parts = _split3(x)[:terms]
    out = lax.dot_general(parts[0], m, NN, preferred_element_type=F32)
    for p_ in parts[1:]:
        out = out + lax.dot_general(p_, m, NN, preferred_element_type=F32)
    return out


def _pick(n, target):
    best = None
    for t in range(LANES, target + 1, LANES):
        if n % t == 0:
            best = t
    return best or n


def _params(sem=None):
    return pltpu.CompilerParams(dimension_semantics=sem, vmem_limit_bytes=VMEM_LIMIT)


MM_ROWS, MM_ROWS_TN, MM_COLS, MM_DEPTH, MM_DEPTH_TN = 1024, 1408, 1792, 2048, 1024

def _mm(a, b, mode, name, res=None, out_dtype=F32, layer=None, b_rows=None, into=None):
    bshape = b.shape if layer is None else b.shape[1:]
    if b_rows is not None:
        assert mode == "nn"
        bshape = (b_rows[1], bshape[1])
    if mode == "nn":
        (M, K), (K2, N) = a.shape, bshape
    elif mode == "nt":
        (M, K), (N, K2) = a.shape, bshape
    else:
        (K, M), (K2, N) = a.shape, bshape
    assert K == K2, (a.shape, b.shape, mode)
    tm, tn = _pick(M, MM_ROWS_TN if mode == "tn" else MM_ROWS), _pick(N, MM_COLS)
    tk = _pick(K, MM_DEPTH_TN if mode == "tn" else MM_DEPTH)
    nk = K // tk
    k_off = 0 if b_rows is None else b_rows[0] // tk
    assert b_rows is None or b_rows[0] % tk == 0
    dims = {"nn": NN, "nt": NT, "tn": TN}[mode]
    a_spec = pl.BlockSpec((tk, tm), lambda i, j, k: (k, i)) if mode == "tn" else pl.BlockSpec((tm, tk), lambda i, j, k: (i, k))
    if layer is None:
        b_spec = pl.BlockSpec((tn, tk), lambda i, j, k: (j, k)) if mode == "nt" else pl.BlockSpec((tk, tn), lambda i, j, k: (k, j))
    elif mode == "nt":
        b_spec = pl.BlockSpec((None, tn, tk), lambda i, j, k: (layer, j, k))
    else:
        b_spec = pl.BlockSpec((None, tk, tn), lambda i, j, k: (layer, k + k_off, j))
    o_spec = pl.BlockSpec((tm, tn), lambda i, j, k: (i, j))
    has_res = res is not None

    def finish(acc, r_ref, o_ref):
        if has_res:
            acc = acc + r_ref[...]
        o_ref[...] = acc.astype(out_dtype)

    def body(*refs):
        a_ref, b_ref = refs[0], refs[1]
        r_ref = refs[2] if has_res else None
        o_ref = refs[-1] if nk == 1 else refs[-2]
        part = _dot(a_ref[...], b_ref[...], dims)
        if nk == 1:
            finish(part, r_ref, o_ref)
            return
        acc_ref = refs[-1]
        k = pl.program_id(2)

        @pl.when(k == 0)
        def _():
            acc_ref[...] = part

        @pl.when((k > 0) & (k < nk - 1))
        def _():
            acc_ref[...] += part

        @pl.when(k == nk - 1)
        def _():
            finish(acc_ref[...] + part, r_ref, o_ref)

    in_specs = [a_spec, b_spec] + ([o_spec] if has_res else [])
    args = (a, b) + ((res,) if has_res else ())
    out_shape, aliases = jax.ShapeDtypeStruct((M, N), out_dtype), {}
    if into is not None:
        stack, slot, row = into
        assert stack.shape[2] == N and row % tm == 0 and row + M <= stack.shape[1] and stack.dtype == out_dtype and not has_res
        in_specs = in_specs + [pl.BlockSpec(memory_space=pl.ANY)]
        args = args + (stack,)
        o_spec = pl.BlockSpec((None, tm, tn), lambda i, j, k: (slot, i + row // tm, j))
        out_shape, aliases = jax.ShapeDtypeStruct(stack.shape, out_dtype), {2: 0}

    def body_into(a_ref, b_ref, stack_ref, *rest):
        body(a_ref, b_ref, *rest)

    return pl.pallas_call(
        body if into is None else body_into, name=name, grid=(M // tm, N // tn, nk), in_specs=in_specs, out_specs=o_spec,
        out_shape=out_shape, scratch_shapes=[] if nk == 1 else [pltpu.VMEM((tm, tn), F32)], input_output_aliases=aliases,
        compiler_params=_params(("parallel", "parallel", "arbitrary")),
    )(*args)


def _row_tile(n, d):
    if n % 8:
        return n
    t = 8
    while t * 2 <= min(n, (512 * 1024) // d) and n % (t * 2) == 0:
        t *= 2
    return t


def _rms_fwd(x, g, name, out_dtype=F32):
    n, d = x.shape
    tm = _row_tile(n, d)

    def body(x_ref, g_ref, o_ref):
        xf = x_ref[...]
        r = lax.rsqrt(jnp.mean(xf * xf, axis=-1, keepdims=True) + EPS)
        o_ref[...] = (xf * r * g_ref[...]).astype(out_dtype)

    return pl.pallas_call(
        body, name=name, grid=(n // tm,),
        in_specs=[pl.BlockSpec((tm, d), lambda i: (i, 0)), pl.BlockSpec((1, d), lambda i: (0, 0))],
        out_specs=pl.BlockSpec((tm, d), lambda i: (i, 0)),
        out_shape=jax.ShapeDtypeStruct((n, d), out_dtype), compiler_params=_params(("parallel",)),
    )(x, g.reshape(1, d))


def _rms_bwd(x, g, dy, name, res=None):
    n, d = x.shape
    tm = _row_tile(n, d)
    has_res = res is not None

    def body(*refs):
        x_ref, g_ref, dy_ref = refs[:3]
        r_ref = refs[3] if has_res else None
        dx_ref, dg_ref = refs[-2:]
        xf = x_ref[...]
        r = lax.rsqrt(jnp.mean(xf * xf, axis=-1, keepdims=True) + EPS)
        xh = xf * r
        dyf = dy_ref[...].astype(F32)
        dxh = dyf * g_ref[...]
        dx = r * (dxh - xh * jnp.mean(dxh * xh, axis=-1, keepdims=True))
        if has_res:
            dx = dx + r_ref[...]
        dx_ref[...] = dx

        @pl.when(pl.program_id(0) == 0)
        def _():
            dg_ref[...] = jnp.zeros_like(dg_ref)

        dg_ref[...] += jnp.sum(dyf * xh, axis=0, keepdims=True)

    row = pl.BlockSpec((tm, d), lambda i: (i, 0))
    vec = pl.BlockSpec((1, d), lambda i: (0, 0))
    dx, dg = pl.pallas_call(
        body, name=name, grid=(n // tm,),
        in_specs=[row, vec, row] + ([row] if has_res else []),
        out_specs=[row, vec],
        out_shape=[jax.ShapeDtypeStruct((n, d), F32), jax.ShapeDtypeStruct((1, d), F32)],
        compiler_params=_params(("arbitrary",)),
    )(x, g.reshape(1, d), dy, *((res,) if has_res else ()))
    return dx, dg.reshape(d)


def _silu(x):
    return x * jax.nn.sigmoid(x)


def _gnorm_fwd(o, gate, gate_col, w, name):
    n, width = o.shape
    d = w.shape[0]
    tm = _row_tile(n, d)

    def body(o_ref, g_ref, w_ref, y_ref):
        of = o_ref[...]
        r = lax.rsqrt(jnp.mean(of * of, axis=-1, keepdims=True) + EPS)
        y_ref[...] = of * r * w_ref[...] * _silu(g_ref[...])

    row = pl.BlockSpec((tm, d), lambda i, h: (i, h))
    vec = pl.BlockSpec((1, d), lambda i, h: (0, 0))
    return pl.pallas_call(body, name=name, grid=(n // tm, width // d),
                          in_specs=[row, pl.BlockSpec((tm, d), lambda i, h: (i, gate_col + h)), vec], out_specs=row,
                          out_shape=jax.ShapeDtypeStruct((n, width), F32), compiler_params=_params(("parallel", "parallel")))(o, gate, w.reshape(1, d))


def _gnorm_bwd(o, gate, gate_col, w, dy, dy_col, name):
    n, width = o.shape
    d = w.shape[0]
    tm = _row_tile(n, d)

    def body(o_ref, g_ref, w_ref, dy_ref, do_ref, dgate_ref, dw_ref):
        of, gf, dyf = o_ref[...], g_ref[...], dy_ref[...]
        r = lax.rsqrt(jnp.mean(of * of, axis=-1, keepdims=True) + EPS)
        xh = of * r
        sg = jax.nn.sigmoid(gf)
        sil = gf * sg
        dnorm = dyf * sil
        dgate_ref[...] = dyf * xh * w_ref[...] * (sg * (1.0 + gf * (1.0 - sg)))
        dxh = dnorm * w_ref[...]
        do_ref[...] = r * (dxh - xh * jnp.mean(dxh * xh, axis=-1, keepdims=True))

        @pl.when((pl.program_id(0) == 0) & (pl.program_id(1) == 0))
        def _():
            dw_ref[...] = jnp.zeros_like(dw_ref)

        dw_ref[...] += jnp.sum(dnorm * xh, axis=0, keepdims=True)

    row = pl.BlockSpec((tm, d), lambda i, h: (i, h))
    vec = pl.BlockSpec((1, d), lambda i, h: (0, 0))
    do, dgate, dw = pl.pallas_call(
        body, name=name, grid=(n // tm, width // d),
        in_specs=[row, pl.BlockSpec((tm, d), lambda i, h: (i, gate_col + h)), vec, pl.BlockSpec((tm, d), lambda i, h: (i, dy_col + h))],
        out_specs=[row, row, vec],
        out_shape=[jax.ShapeDtypeStruct((n, width), F32)] * 2 + [jax.ShapeDtypeStruct((1, d), F32)],
        compiler_params=_params(("arbitrary", "arbitrary")),
    )(o, gate, w.reshape(1, d), dy)
    return do, dgate, dw.reshape(d)


def _sigmul_fwd(z, e, res, name):
    n, d = z.shape
    tm = _row_tile(n, d)

    def body(z_ref, e_ref, r_ref, o_ref):
        o_ref[...] = r_ref[...] + jax.nn.sigmoid(z_ref[...]) * e_ref[...]

    row = pl.BlockSpec((tm, d), lambda i: (i, 0))
    return pl.pallas_call(body, name=name, grid=(n // tm,), in_specs=[row] * 3, out_specs=row,
                          out_shape=jax.ShapeDtypeStruct((n, d), F32), compiler_params=_params(("parallel",)))(z, e, res)


def _sigmul_bwd(z, e, dy, name):
    n, d = z.shape
    tm = _row_tile(n, d)

    def body(z_ref, e_ref, dy_ref, dz_ref, de_ref):
        s = jax.nn.sigmoid(z_ref[...])
        dyf = dy_ref[...]
        dz_ref[...] = dyf * e_ref[...] * s * (1.0 - s)
        de_ref[...] = dyf * s

    row = pl.BlockSpec((tm, d), lambda i: (i, 0))
    return pl.pallas_call(body, name=name, grid=(n // tm,), in_specs=[row] * 3, out_specs=[row] * 2,
                          out_shape=[jax.ShapeDtypeStruct((n, d), F32)] * 2, compiler_params=_params(("parallel",)))(z, e, dy)


def _loss_fwd(y, target, name):
    n, d = y.shape
    tm = _row_tile(n, d)

    def body(y_ref, t_ref, l_ref, dy_ref):
        diff = y_ref[...] - t_ref[...]
        dy_ref[...] = diff * (1.0 / d)

        @pl.when(pl.program_id(0) == 0)
        def _():
            l_ref[...] = jnp.zeros_like(l_ref)

        part = jnp.sum(jnp.mean(diff * diff, axis=-1, keepdims=True), axis=0, keepdims=True)
        l_ref[...] += 0.5 * jnp.broadcast_to(part, l_ref.shape)

    row = pl.BlockSpec((tm, d), lambda i: (i, 0))
    vec = pl.BlockSpec((1, LANES), lambda i: (0, 0))
    return pl.pallas_call(body, name=name, grid=(n // tm,), in_specs=[row, row], out_specs=[vec, row],
                          out_shape=[jax.ShapeDtypeStruct((1, LANES), F32), jax.ShapeDtypeStruct((n, d), F32)],
                          compiler_params=_params(("arbitrary",)))(y, target)


SB_BLK = 128
SB_QBLK = 512


def _sb_logits(z, qi, kj, row, col):
    mask = (kj * SB_BLK + col) < (qi * SB_QBLK + row)
    sp = jnp.maximum(z, 0.0) + jnp.log1p(jnp.exp(-jnp.abs(z)))
    lk = jnp.where(mask, -sp, 0.0)
    return mask, lk, z - sp


SB_PAIRS = SB_WIDTH // LANES


def _sb_iotas():
    row = lax.broadcasted_iota(jnp.int32, (2 * SB_QBLK, SB_BLK), 0)
    row = jnp.where(row >= SB_QBLK, row - SB_QBLK, row)
    col = lax.broadcasted_iota(jnp.int32, (2 * SB_QBLK, SB_BLK), 1)
    return row, col, col[:SB_QBLK] < SB_DIM


def _sb_stack(x, first):
    return jnp.concatenate([jnp.where(first, x, 0.0), jnp.where(first, 0.0, x)], axis=0)


def _sb_unstack(y, first):
    return jnp.where(first, y[:SB_QBLK], y[SB_QBLK:])


def _sb_running(x, u):
    m = x.shape[0]
    hi = _bf(x)
    lo = _bf(x - hi.astype(F32))
    c = lax.dot_general(jnp.concatenate([hi, lo], axis=0), u, NN, preferred_element_type=F32)
    return c[:m] + c[m:]


def _sb_spec(s, col):
    return pl.BlockSpec((s, LANES), lambda e, pr: (e, col + pr))


def _sb_fwd(proj, cols, b, s, name):
    nq = s // SB_QBLK
    scale = SB_DIM ** -0.5

    def body(q_ref, k_ref, v_ref, o_ref, lt_ref):
        row, col, first = _sb_iotas()
        u_after = _bf(row[:SB_BLK] > col[:SB_BLK])

        def qloop(qi, _):
            q0 = pl.multiple_of(qi * SB_QBLK, SB_QBLK)
            q2 = _sb_stack(q_ref[pl.ds(q0, SB_QBLK), :], first)
            nkeys = (qi + 1) * (SB_QBLK // SB_BLK)

            def logits(kj):
                k0 = pl.multiple_of(kj * SB_BLK, SB_BLK)
                return _dot(q2, k_ref[pl.ds(k0, SB_BLK), :], NT) * scale

            def kloop(j, st):
                acc, carry, z = st
                kj = nkeys - 1 - j
                k0 = pl.multiple_of(kj * SB_BLK, SB_BLK)
                z_next = logits(jnp.maximum(kj - 1, 0))
                mask, lk, ls = _sb_logits(z, qi, kj, row, col)
                later = carry + _sb_running(lk, u_after)
                w = jnp.where(mask, jnp.exp(ls + later), 0.0)
                acc = acc + _sb_unstack(_dot(w, v_ref[pl.ds(k0, SB_BLK), :]), first)
                return acc, carry + jnp.sum(lk, axis=1, keepdims=True), z_next

            acc, carry, _ = lax.fori_loop(0, nkeys, kloop, (jnp.zeros((SB_QBLK, LANES), F32), jnp.zeros((2 * SB_QBLK, 1), F32), logits(nkeys - 1)))
            o_ref[pl.ds(q0, SB_QBLK), :] = acc
            lt_ref[pl.ds(q0, SB_QBLK), :] = _sb_unstack(jnp.broadcast_to(carry, (2 * SB_QBLK, LANES)), first)
            return 0

        lax.fori_loop(0, nq, qloop, 0)

    out = _sb_spec(s, 0)
    return pl.pallas_call(body, name=name, grid=(b, SB_PAIRS), in_specs=[_sb_spec(s, c) for c in cols], out_specs=[out, out],
                          out_shape=[jax.ShapeDtypeStruct((b * s, SB_WIDTH), F32)] * 2,
                          compiler_params=_params(("parallel", "parallel")))(proj, proj, proj)


def _sb_bwd(proj, cols, ltot, do, do_col, b, s, name):
    nq = s // SB_QBLK
    scale = SB_DIM ** -0.5

    def body(q_ref, k_ref, v_ref, lt_ref, do_ref, dq_ref, dk_ref, dv_ref):
        row, col, first = _sb_iotas()
        u_upto = _bf(row[:SB_BLK] <= col[:SB_BLK])
        u_before = _bf(row[:SB_BLK] < col[:SB_BLK])
        dk_ref[...] = jnp.zeros_like(dk_ref)
        dv_ref[...] = jnp.zeros_like(dv_ref)

        def qloop(qi, _):
            q0 = pl.multiple_of(qi * SB_QBLK, SB_QBLK)
            q2 = _sb_stack(q_ref[pl.ds(q0, SB_QBLK), :], first)
            nkeys = (qi + 1) * (SB_QBLK // SB_BLK)
            do2 = _sb_stack(do_ref[pl.ds(q0, SB_QBLK), :], first)
            lt2 = jnp.min(_sb_stack(lt_ref[pl.ds(q0, SB_QBLK), :], first), axis=1, keepdims=True)

            def logits(kj):
                k0 = pl.multiple_of(kj * SB_BLK, SB_BLK)
                return _dot(q2, k_ref[pl.ds(k0, SB_BLK), :], NT) * scale

            def kloop(kj, st):
                dq, cl, cg, z = st
                k0 = pl.multiple_of(kj * SB_BLK, SB_BLK)
                kb = k_ref[pl.ds(k0, SB_BLK), :]
                vb = v_ref[pl.ds(k0, SB_BLK), :]
                z_next = logits(jnp.minimum(kj + 1, nkeys - 1))
                mask, lk, ls = _sb_logits(z, qi, kj, row, col)
                later = lt2 - (cl + _sb_running(lk, u_upto))
                w = jnp.where(mask, jnp.exp(ls + later), 0.0)
                g = _dot(do2, vb, NT) * w
                dv_ref[pl.ds(k0, SB_BLK), :] += _dot(w, do2, TN)
                g_before = cg + _sb_running(g, u_before)
                sig = jnp.exp(ls)
                dz = jnp.where(mask, g * (1.0 - sig) - sig * g_before, 0.0) * scale
                dq = dq + _sb_unstack(_dot(dz, kb), first)
                dk_ref[pl.ds(k0, SB_BLK), :] += _dot(dz, q2, TN)
                return dq, cl + jnp.sum(lk, axis=1, keepdims=True), cg + jnp.sum(g, axis=1, keepdims=True), z_next

            z1 = jnp.zeros((2 * SB_QBLK, 1), F32)
            dq = lax.fori_loop(0, nkeys, kloop, (jnp.zeros((SB_QBLK, LANES), F32), z1, z1, logits(0)))[0]
            dq_ref[pl.ds(q0, SB_QBLK), :] = dq
            return 0

        lax.fori_loop(0, nq, qloop, 0)

    out = _sb_spec(s, 0)
    return pl.pallas_call(body, name=name, grid=(b, SB_PAIRS),
                          in_specs=[_sb_spec(s, c) for c in cols] + [out, _sb_spec(s, do_col)], out_specs=[out] * 3,
                          out_shape=[jax.ShapeDtypeStruct((b * s, SB_WIDTH), F32)] * 3,
                          compiler_params=_params(("parallel", "parallel")))(proj, proj, proj, ltot, do)


HG_CHUNK = 64
HG_GROUP = 2


def _hg_consts(c, r):
    levels = int(math.log2(c))
    t = np.arange(r)
    same = (t[:, None] // c) == (t[None, :] // c)
    tri = ((t[:, None] >= t[None, :]) & same).astype(np.float32)
    psel = np.zeros((levels, r, r), np.float32)
    masks = np.zeros((levels + 1, r, r), np.float32)
    for l in range(levels):
        n = c >> (l + 1)
        blk = t // (2 * n)
        psel[l, t, blk * 2 * n + n - 1] = 1.0
        upper = (t % (2 * n)) >= n
        masks[l] = (blk[:, None] == blk[None, :]) & upper[:, None] & (~upper)[None, :]
    masks[levels] = np.eye(r)
    psel = psel.reshape(levels * r, r)
    return levels, jnp.asarray(tri), jnp.asarray(psel), jnp.asarray(masks), jnp.asarray(tri.T.copy()), jnp.asarray(psel.T.copy())


def _hg_elem(qv, fv, lbv):
    sig = jax.nn.sigmoid(fv)
    lf = jnp.log(lbv + (1.0 - lbv) * sig)
    kk = (1.0 - lbv) * jax.nn.sigmoid(-fv)
    qf = qv * jax.nn.sigmoid(qv)
    return qf, kk, lf


def _col_bcast(rowvec):
    n = rowvec.shape[1]
    return jnp.transpose(jnp.broadcast_to(rowvec, (n, n)))


def _hg_within(qf, kk, lf, tri, psel, m_ref, c, levels):
    r = qf.shape[0]
    b = _dot_exact_lhs01(tri, lf)
    bls = [b[(g + 1) * c - 1:(g + 1) * c, :] for g in range(r // c)]
    blb = jnp.concatenate([jnp.broadcast_to(bl, (c, bl.shape[1])) for bl in bls], axis=0)
    eb = jnp.exp(b)
    qi = qf * eb
    bsel = _dot_exact_lhs01(psel, b)
    scores = jnp.where(m_ref[levels] > 0, _dot(qf, kk, NT), 0.0)
    lev = []
    for l in range(levels):
        bs = bsel[l * r:(l + 1) * r]
        eq = jnp.exp(jnp.minimum(b - bs, 0.0))
        ek = jnp.exp(jnp.minimum(bs - b, 0.0))
        ql, kl = qf * eq, kk * ek
        scores = scores + jnp.where(m_ref[l] > 0, _dot(ql, kl, NT), 0.0)
        lev.append((eq, ek, ql, kl))
    ebl = jnp.exp(blb - b)
    kd = kk * ebl
    decays = [_col_bcast(jnp.exp(bl)) for bl in bls]
    return eb, qi, scores, lev, ebl, kd, decays


def _hg_fwd(proj, cols, lb, b, s, name):
    nh, d = lb.shape
    bh = b * nh
    c = HG_CHUNK
    nc = s // c
    grp = math.gcd(HG_GROUP, nc)
    r = grp * c
    levels, tri, psel, masks, _, _ = _hg_consts(c, r)

    def body(q_ref, f_ref, i_ref, lb_ref, tri_ref, psel_ref, m_ref, o_ref, st_ref):
        lbv = jnp.broadcast_to(lb_ref[0], (r, d))
        tri_v, psel_v = _bf(tri_ref[...]), _bf(psel_ref[...])

        def group(gi, state):
            r0 = pl.multiple_of(gi * r, r)
            qf, kk, lf = _hg_elem(q_ref[pl.ds(r0, r), :], f_ref[pl.ds(r0, r), :], lbv)
            iv = i_ref[pl.ds(r0, r), :]
            _, qi, scores, _, _, kd, decays = _hg_within(qf, kk, lf, tri_v, psel_v, m_ref, c, levels)
            within = _dot(scores, iv)
            for g in range(grp):
                rows = slice(g * c, (g + 1) * c)
                st_ref[0, gi * grp + g] = state
                o_ref[pl.ds(r0 + g * c, c), :] = _dot(qi[rows], state) + within[rows]
                state = decays[g] * state + _dot(kd[rows], iv[rows], TN)
            return state

        lax.fori_loop(0, nc // grp, group, jnp.zeros((d, d), F32))

    full = lambda a: pl.BlockSpec(a.shape, lambda e, hd: (0,) * a.ndim)
    return pl.pallas_call(
        body, name=name, grid=(b, nh),
        in_specs=[_hg_seq(s, d, cols[0]), _hg_seq(s, d, cols[1]), _hg_seq(s, d, cols[2]),
                  pl.BlockSpec((1, 1, d), lambda e, hd: (hd, 0, 0)), full(tri), full(psel), full(masks)],
        out_specs=[_hg_seq(s, d, 0), pl.BlockSpec((1, nc, d, d), lambda e, hd: (e * nh + hd, 0, 0, 0))],
        out_shape=[jax.ShapeDtypeStruct((b * s, nh * d), F32), jax.ShapeDtypeStruct((bh, nc, d, d), F32)],
        compiler_params=_params(("parallel", "parallel")),
    )(proj, proj, proj, lb.reshape(nh, 1, d), tri, psel, masks)


def _hg_seq(s, d, col):
    return pl.BlockSpec((s, d), lambda e, hd: (e, col + hd))


def _hg_bwd(proj, cols, lb, states, do, b, s, name):
    nh, d = lb.shape
    bh = b * nh
    c = HG_CHUNK
    nc = s // c
    grp = math.gcd(HG_GROUP, nc)
    r = grp * c
    levels, tri, psel, masks, tri_t, psel_t = _hg_consts(c, r)

    def body(q_ref, f_ref, i_ref, lb_ref, st_ref, do_ref, tri_ref, psel_ref, m_ref, trit_ref, pselt_ref,
             dq_ref, df_ref, di_ref, dlb_ref):
        lbv = jnp.broadcast_to(lb_ref[0], (r, d))
        tri_v, psel_v = _bf(tri_ref[...]), _bf(psel_ref[...])
        trit_v, pselt_v = _bf(trit_ref[...]), _bf(pselt_ref[...])
        row_in_chunk = lax.broadcasted_iota(jnp.int32, (c, d), 0)

        def chunk(step, carry):
            ds_out, dlb = carry
            gi = nc // grp - 1 - step
            r0 = pl.multiple_of(gi * r, r)
            qv, fv, iv = q_ref[pl.ds(r0, r), :], f_ref[pl.ds(r0, r), :], i_ref[pl.ds(r0, r), :]
            dov = do_ref[pl.ds(r0, r), :]
            (qf, kk, lf), elem_vjp = jax.vjp(_hg_elem, qv, fv, lbv)
            eb, qi, scores, lev, ebl, kd, decays = _hg_within(qf, kk, lf, tri_v, psel_v, m_ref, c, levels)

            dscores = _dot(dov, iv, NT)
            di_within = _dot(scores, dov, TN)
            dqi_parts, dkd_parts, dbl_parts = [None] * grp, [None] * grp, [None] * grp
            for g in reversed(range(grp)):
                rows = slice(g * c, (g + 1) * c)
                state = st_ref[0, gi * grp + g]
                di_ref[pl.ds(r0 + g * c, c), :] = di_within[rows] + _dot(kd[rows], ds_out)
                dqi_parts[g] = _dot(dov[rows], state, NT)
                dkd_parts[g] = _dot(iv[rows], ds_out, NT)
                dbl = (jnp.sum(dkd_parts[g] * kd[rows], axis=0, keepdims=True)
                       + _col_bcast_t(jnp.sum(ds_out * decays[g] * state, axis=1, keepdims=True)))
                dbl_parts[g] = jnp.where(row_in_chunk == c - 1, dbl, 0.0)
                ds_out = decays[g] * ds_out + _dot(qi[rows], dov[rows], TN)
            ds_in = ds_out
            dqi = jnp.concatenate(dqi_parts, axis=0)
            dkd = jnp.concatenate(dkd_parts, axis=0)
            dqf = dqi * eb
            dkk = dkd * ebl
            db = dqi * qi - dkd * kd + jnp.concatenate(dbl_parts, axis=0)
            dsd = jnp.where(m_ref[levels] > 0, dscores, 0.0)
            dqf = dqf + _dot(dsd, kk)
            dkk = dkk + _dot(dsd, qf, TN)
            dbsel = []
            for l in range(levels):
                eq, ek, ql, kl = lev[l]
                dsl = jnp.where(m_ref[l] > 0, dscores, 0.0)
                dql = _dot(dsl, kl)
                dkl = _dot(dsl, ql, TN)
                dqf = dqf + dql * eq
                dkk = dkk + dkl * ek
                diff = dql * ql - dkl * kl
                db = db + diff
                dbsel.append(-diff)
            db = db + _dot_exact_lhs01(pselt_v, jnp.concatenate(dbsel, axis=0))
            dlf = _dot_exact_lhs01(trit_v, db)
            dq, df, dlb_c = elem_vjp((dqf, dkk, dlf))
            dq_ref[pl.ds(r0, r), :] = dq
            df_ref[pl.ds(r0, r), :] = df
            return ds_in, dlb + jnp.sum(dlb_c, axis=0, keepdims=True)

        _, dlb = lax.fori_loop(0, nc // grp, chunk, (jnp.zeros((d, d), F32), jnp.zeros((1, d), F32)))
        dlb_ref[0] = dlb

    seq = _hg_seq(s, d, 0)
    full = lambda a: pl.BlockSpec(a.shape, lambda e, hd: (0,) * a.ndim)
    return pl.pallas_call(
        body, name=name, grid=(b, nh),
        in_specs=[_hg_seq(s, d, cols[0]), _hg_seq(s, d, cols[1]), _hg_seq(s, d, cols[2]),
                  pl.BlockSpec((1, 1, d), lambda e, hd: (hd, 0, 0)),
                  pl.BlockSpec((1, nc, d, d), lambda e, hd: (e * nh + hd, 0, 0, 0)), seq,
                  full(tri), full(psel), full(masks), full(tri_t), full(psel_t)],
        out_specs=[seq, seq, seq, pl.BlockSpec((1, 1, d), lambda e, hd: (e * nh + hd, 0, 0))],
        out_shape=[jax.ShapeDtypeStruct((b * s, nh * d), F32)] * 3 + [jax.ShapeDtypeStruct((bh, 1, d), F32)],
        compiler_params=_params(("parallel", "parallel")),
    )(proj, proj, proj, lb.reshape(nh, 1, d), states, do, tri, psel, masks, tri_t, psel_t)


def _col_bcast_t(colvec):
    n = colvec.shape[0]
    return jnp.transpose(jnp.broadcast_to(colvec, (n, n)))[0:1, :]


def _swa_probs(qg, kb, bias, sink, valid, scale):
    logits = _dot(qg, kb, NT) * scale + bias
    logits = jnp.where(valid, logits, -jnp.inf)
    m = jnp.maximum(jnp.max(logits, axis=-1, keepdims=True), sink)
    e = jnp.exp(logits - m)
    es = jnp.exp(sink - m)
    den = jnp.sum(e, axis=-1, keepdims=True) + es
    return e / den, es / den


def _swa_valid(n):
    w = WINDOW
    row = lax.broadcasted_iota(jnp.int32, (w, 2 * w), 0)
    col = lax.broadcasted_iota(jnp.int32, (w, 2 * w), 1)
    dist = row + w - col
    return (dist >= 0) & (dist < w) & ((col >= w) | (n > 0))


def _swa_specs(b, g, s, d):
    w = WINDOW
    q_spec = pl.BlockSpec((1, 1, g, w, d), lambda h, bi, n: (bi, h, 0, n, 0))
    kp_spec = pl.BlockSpec((1, 1, w, d), lambda h, bi, n: (bi, h, jnp.maximum(n - 1, 0), 0))
    kc_spec = pl.BlockSpec((1, 1, w, d), lambda h, bi, n: (bi, h, n, 0))
    bias_spec = pl.BlockSpec((1, g, w, 2 * w), lambda h, bi, n: (h, 0, 0, 0))
    sink_spec = pl.BlockSpec(memory_space=pltpu.SMEM)
    return q_spec, kp_spec, kc_spec, bias_spec, sink_spec


def _swa_fwd(q, k, v, sinks, bias, name):
    b, kvh, g, s, d = q.shape
    w = WINDOW
    scale = d ** -0.5
    q_spec, kp_spec, kc_spec, bias_spec, sink_spec = _swa_specs(b, g, s, d)

    def body(q_ref, kp_ref, kc_ref, vp_ref, vc_ref, bias_ref, sink_ref, o_ref):
        h, n = pl.program_id(0), pl.program_id(2)
        valid = _swa_valid(n)
        kb = jnp.concatenate([kp_ref[0, 0], kc_ref[0, 0]], axis=0)
        vb = jnp.concatenate([vp_ref[0, 0], vc_ref[0, 0]], axis=0)
        for gi in range(g):
            p, _ = _swa_probs(q_ref[0, 0, gi], kb, bias_ref[0, gi], sink_ref[h * g + gi], valid, scale)
            o_ref[0, 0, gi] = _dot(p, vb)

    return pl.pallas_call(
        body, name=name, grid=(kvh, b, s // w),
        in_specs=[q_spec, kp_spec, kc_spec, kp_spec, kc_spec, bias_spec, sink_spec], out_specs=q_spec,
        out_shape=jax.ShapeDtypeStruct(q.shape, F32), compiler_params=_params(("parallel", "parallel", "arbitrary")),
    )(q, k, k, v, v, bias, sinks)


def _swa_bwd(q, k, v, sinks, bias, do, name):
    b, kvh, g, s, d = q.shape
    w = WINDOW
    scale = d ** -0.5
    q_spec, kp_spec, kc_spec, bias_spec, sink_spec = _swa_specs(b, g, s, d)
    kv_acc = pl.BlockSpec((1, 1, s, d), lambda h, bi, n: (bi, h, 0, 0))
    dsink_spec = pl.BlockSpec((1, g, LANES), lambda h, bi, n: (h, 0, 0))

    def body(q_ref, kp_ref, kc_ref, vp_ref, vc_ref, bias_ref, sink_ref, do_ref, dq_ref, dk_ref, dv_ref, dbias_ref, dsink_ref):
        h, bi, n = pl.program_id(0), pl.program_id(1), pl.program_id(2)
        valid = _swa_valid(n)
        kb = jnp.concatenate([kp_ref[0, 0], kc_ref[0, 0]], axis=0)
        vb = jnp.concatenate([vp_ref[0, 0], vc_ref[0, 0]], axis=0)

        @pl.when(n == 0)
        def _():
            dk_ref[...] = jnp.zeros_like(dk_ref)
            dv_ref[...] = jnp.zeros_like(dv_ref)

        @pl.when((n == 0) & (bi == 0))
        def _():
            dbias_ref[...] = jnp.zeros_like(dbias_ref)
            dsink_ref[...] = jnp.zeros_like(dsink_ref)

        dkb = jnp.zeros((2 * w, d), F32)
        dvb = jnp.zeros((2 * w, d), F32)
        for gi in range(g):
            qg, dog = q_ref[0, 0, gi], do_ref[0, 0, gi]
            p, ps = _swa_probs(qg, kb, bias_ref[0, gi], sink_ref[h * g + gi], valid, scale)
            dp = _dot(dog, vb, NT)
            delta = jnp.sum(p * dp, axis=-1, keepdims=True)
            dl = p * (dp - delta)
            dq_ref[0, 0, gi] = _dot(dl, kb) * scale
            dkb = dkb + _dot(dl, qg, TN) * scale
            dvb = dvb + _dot(p, dog, TN)
            dbias_ref[0, gi] += dl
            dsink_ref[0, gi:gi + 1, :] += jnp.broadcast_to(jnp.sum(-ps * delta, axis=0, keepdims=True), (1, LANES))

        c0 = pl.multiple_of(n * w, w)
        dk_ref[0, 0, pl.ds(c0, w), :] += dkb[w:]
        dv_ref[0, 0, pl.ds(c0, w), :] += dvb[w:]

        @pl.when(n > 0)
        def _():
            p0 = pl.multiple_of((n - 1) * w, w)
            dk_ref[0, 0, pl.ds(p0, w), :] += dkb[:w]
            dv_ref[0, 0, pl.ds(p0, w), :] += dvb[:w]

    return pl.pallas_call(
        body, name=name, grid=(kvh, b, s // w),
        in_specs=[q_spec, kp_spec, kc_spec, kp_spec, kc_spec, bias_spec, sink_spec, q_spec],
        out_specs=[q_spec, kv_acc, kv_acc, bias_spec, dsink_spec],
        out_shape=[jax.ShapeDtypeStruct(q.shape, F32), jax.ShapeDtypeStruct(k.shape, F32), jax.ShapeDtypeStruct(k.shape, F32),
                   jax.ShapeDtypeStruct(bias.shape, F32), jax.ShapeDtypeStruct((kvh, g, LANES), F32)],
        compiler_params=_params(("arbitrary", "arbitrary", "arbitrary")),
    )(q, k, k, v, v, bias, sinks, do)


def _t5_bias(rel_bias):
    t = np.arange(WINDOW)[:, None]
    s = np.arange(2 * WINDOW)[None, :]
    dist = t + WINDOW - s
    max_exact = N_BUCKETS // 2
    large = max_exact + (np.log(np.maximum(dist, max_exact) / max_exact) / math.log(MAX_DISTANCE / max_exact)
                         * (N_BUCKETS - max_exact)).astype(np.int32)
    large = np.minimum(large, N_BUCKETS - 1)
    bucket = np.where(dist < max_exact, np.maximum(dist, 0), large).astype(np.int32)
    onehot = jnp.asarray(np.eye(N_BUCKETS, dtype=np.float32)[bucket])
    bias = jnp.einsum("tsb,bh->hts", onehot, rel_bias.astype(F32), precision=lax.Precision.HIGHEST)
    return bias.reshape(SW_KV_HEADS, SW_GROUP, WINDOW, 2 * WINDOW)


CONV_W = 3


def _shift_down(x, k):
    row = lax.broadcasted_iota(jnp.int32, x.shape, 0)
    return jnp.where(row >= k, pltpu.roll(x, k, axis=0), 0.0)


def _shift_up(x, k):
    n = x.shape[0]
    row = lax.broadcasted_iota(jnp.int32, x.shape, 0)
    return jnp.where(row < n - k, pltpu.roll(x, n - k, axis=0), 0.0)


def _conv3(u, w, bvec):
    return w[0:1] * _shift_down(u, 2) + w[1:2] * _shift_down(u, 1) + w[2:3] * u + bvec


def _convglu_fwd(u, w, bvec, name, out_dtype=F32):
    b, s, f2 = u.shape
    f = f2 // 2
    tc = _pick(f, 256)
    nt = f // tc

    def body(ug_ref, uu_ref, wg_ref, wu_ref, bg_ref, bu_ref, o_ref):
        cg = _conv3(ug_ref[0], wg_ref[...], bg_ref[...])
        cu = _conv3(uu_ref[0], wu_ref[...], bu_ref[...])
        o_ref[0] = (_silu(cg) * cu).astype(out_dtype)

    ug = pl.BlockSpec((1, s, tc), lambda j, bi: (bi, 0, j))
    uu = pl.BlockSpec((1, s, tc), lambda j, bi: (bi, 0, j + nt))
    wg = pl.BlockSpec((CONV_W, tc), lambda j, bi: (0, j))
    wu = pl.BlockSpec((CONV_W, tc), lambda j, bi: (0, j + nt))
    bg = pl.BlockSpec((1, tc), lambda j, bi: (0, j))
    bu = pl.BlockSpec((1, tc), lambda j, bi: (0, j + nt))
    bv = bvec.reshape(1, f2)
    return pl.pallas_call(body, name=name, grid=(nt, b), in_specs=[ug, uu, wg, wu, bg, bu], out_specs=ug,
                          out_shape=jax.ShapeDtypeStruct((b, s, f), out_dtype),
                          compiler_params=_params(("parallel", "parallel")))(u, u, w, w, bv, bv)


def _convglu_bwd(u, w, bvec, dact, name):
    b, s, f2 = u.shape
    f = f2 // 2
    tc = LANES
    nt = f // tc

    def taps(dc, uv):
        rows = [jnp.sum(dc * _shift_down(uv, 2), axis=0, keepdims=True), jnp.sum(dc * _shift_down(uv, 1), axis=0, keepdims=True),
                jnp.sum(dc * uv, axis=0, keepdims=True), jnp.sum(dc, axis=0, keepdims=True)]
        return jnp.concatenate(rows + [jnp.zeros((4, tc), F32)], axis=0)

    def back(dc, wv):
        return wv[2:3] * dc + wv[1:2] * _shift_up(dc, 1) + wv[0:1] * _shift_up(dc, 2)

    def body(ug_ref, uu_ref, wg_ref, wu_ref, bg_ref, bu_ref, da_ref, dug_ref, duu_ref, dwg_ref, dwu_ref):
        ugv, uuv, da = ug_ref[0], uu_ref[0], da_ref[0]
        cg = _conv3(ugv, wg_ref[...], bg_ref[...])
        cu = _conv3(uuv, wu_ref[...], bu_ref[...])
        sg = jax.nn.sigmoid(cg)
        dcu = da * (cg * sg)
        dcg = da * cu * (sg * (1.0 + cg * (1.0 - sg)))
        dug_ref[0] = back(dcg, wg_ref[...])
        duu_ref[0] = back(dcu, wu_ref[...])

        @pl.when(pl.program_id(1) == 0)
        def _():
            dwg_ref[...] = jnp.zeros_like(dwg_ref)
            dwu_ref[...] = jnp.zeros_like(dwu_ref)

        dwg_ref[...] += taps(dcg, ugv)
        dwu_ref[...] += taps(dcu, uuv)

    ug = pl.BlockSpec((1, s, tc), lambda j, bi: (bi, 0, j))
    uu = pl.BlockSpec((1, s, tc), lambda j, bi: (bi, 0, j + nt))
    wg = pl.BlockSpec((CONV_W, tc), lambda j, bi: (0, j))
    wu = pl.BlockSpec((CONV_W, tc), lambda j, bi: (0, j + nt))
    bg = pl.BlockSpec((1, tc), lambda j, bi: (0, j))
    bu = pl.BlockSpec((1, tc), lambda j, bi: (0, j + nt))
    acc = pl.BlockSpec((8, tc), lambda j, bi: (0, j))
    bv = bvec.reshape(1, f2)
    return pl.pallas_call(
        body, name=name, grid=(nt, b), in_specs=[ug, uu, wg, wu, bg, bu, ug], out_specs=[ug, ug, acc, acc],
        out_shape=[jax.ShapeDtypeStruct((b, s, f), F32)] * 2 + [jax.ShapeDtypeStruct((8, f), F32)] * 2,
        compiler_params=_params(("parallel", "arbitrary")),
    )(u, u, w, w, bv, bv, dact)


def _adamw(w, g, m, v, name):
    r, c = w.shape
    tr = _row_tile(r, c)
    c1 = 1.0 - ADAM_B1 ** ADAM_STEP
    c2 = 1.0 - ADAM_B2 ** ADAM_STEP

    def body(w_ref, g_ref, m_ref, v_ref, d_ref, mo_ref, vo_ref):
        gv = g_ref[...]
        mn = ADAM_B1 * m_ref[...] + (1.0 - ADAM_B1) * gv
        vn = ADAM_B2 * v_ref[...] + (1.0 - ADAM_B2) * (gv * gv)
        d_ref[...] = -ADAM_LR * ((mn / c1) / (jnp.sqrt(vn / c2) + ADAM_EPS) + ADAM_WD * w_ref[...])
        mo_ref[...] = mn
        vo_ref[...] = vn

    blk = pl.BlockSpec((tr, c), lambda i: (i, 0))
    return pl.pallas_call(body, name=name, grid=(r // tr,), in_specs=[blk] * 4, out_specs=[blk] * 3,
                          out_shape=[jax.ShapeDtypeStruct((r, c), F32)] * 3, compiler_params=_params(("parallel",)))(w, g, m, v)


MESH = pl.DeviceIdType.MESH
ANY = pl.BlockSpec(memory_space=pl.ANY)


def _position():
    return lax.axis_index("x"), lax.axis_index("y"), lax.axis_index("c")


def _all_gather(x, name):
    r, c = x.shape

    def body(x_ref, out_ref, send_sems, recv_sems, local_sem):
        mx, my, mc = _position()
        me, sibling = (mx, my, mc), (mx, my, 1 - mc)
        chips = [(1 - mx, my), (mx, 1 - my), (1 - mx, 1 - my)]

        def slot(px, py, pc):
            return out_ref.at[4 * px + 2 * py + pc]

        def copy(k, block, to, src=None):
            return pltpu.make_async_remote_copy(
                src_ref=slot(*block) if src is None else src, dst_ref=slot(*block),
                send_sem=send_sems.at[k], recv_sem=recv_sems.at[k], device_id=to, device_id_type=MESH)

        mine = pltpu.make_async_copy(x_ref, slot(*me), local_sem.at[0])
        mine.start()
        first = [copy(0, me, sibling, src=x_ref)]
        first += [copy(1 + j, me, (*chip, mc), src=x_ref) for j, chip in enumerate(chips)]
        for cp in first:
            cp.start()
        passed = [copy(4 + j, (*chip, mc), sibling) for j, chip in enumerate(chips)]
        for j, chip in enumerate(chips):
            copy(1 + j, (*chip, mc), me).wait_recv()
            passed[j].start()
        copy(0, sibling, me).wait_recv()
        for j, chip in enumerate(chips):
            copy(4 + j, (*chip, 1 - mc), me).wait_recv()
        for cp in first + passed:
            cp.wait_send()
        mine.wait()

    return pl.pallas_call(
        body, name=name, out_shape=jax.ShapeDtypeStruct((N_DEV, r, c), x.dtype), in_specs=[ANY], out_specs=ANY,
        scratch_shapes=[pltpu.SemaphoreType.DMA((7,)), pltpu.SemaphoreType.DMA((7,)), pltpu.SemaphoreType.DMA((1,))],
    )(x)


def _dev_rows(ref, dev, a):
    return ref.at[:, pl.ds(pl.multiple_of(dev * a, 16), a), :]


def _all_gather_rows(shards, name):
    nt = len(shards)

    def body(*refs):
        x_refs, out_refs = refs[:nt], refs[nt:2 * nt]
        send_sems, recv_sems, local_sems = refs[2 * nt:]
        mx, my, mc = _position()
        me, sibling = (mx, my, mc), (mx, my, 1 - mc)
        chips = [(1 - mx, my), (mx, 1 - my), (1 - mx, 1 - my)]

        def slot(t, px, py, pc):
            return _dev_rows(out_refs[t], 4 * px + 2 * py + pc, shards[t].shape[1])

        def copy(t, k, block, to, src=None):
            return pltpu.make_async_remote_copy(
                src_ref=slot(t, *block) if src is None else src, dst_ref=slot(t, *block),
                send_sem=send_sems.at[7 * t + k], recv_sem=recv_sems.at[7 * t + k], device_id=to, device_id_type=MESH)

        mine = [pltpu.make_async_copy(x_refs[t], slot(t, *me), local_sems.at[t]) for t in range(nt)]
        first = [copy(t, 0, me, sibling, src=x_refs[t]) for t in range(nt)]
        first += [copy(t, 1 + j, me, (*chip, mc), src=x_refs[t]) for j, chip in enumerate(chips) for t in range(nt)]
        for cp in mine + first:
            cp.start()
        passed = []
        for j, chip in enumerate(chips):
            for t in range(nt):
                copy(t, 1 + j, (*chip, mc), me).wait_recv()
                fwd = copy(t, 4 + j, (*chip, mc), sibling)
                fwd.start()
                passed.append(fwd)
        for t in range(nt):
            copy(t, 0, sibling, me).wait_recv()
        for j, chip in enumerate(chips):
            for t in range(nt):
                copy(t, 4 + j, (*chip, 1 - mc), me).wait_recv()
        for cp in first + passed:
            cp.wait_send()
        for cp in mine:
            cp.wait()

    out_shape = [jax.ShapeDtypeStruct((x.shape[0], N_DEV * x.shape[1], x.shape[2]), x.dtype) for x in shards]
    return pl.pallas_call(
        body, name=name, out_shape=out_shape, in_specs=[ANY] * nt, out_specs=[ANY] * nt,
        scratch_shapes=[pltpu.SemaphoreType.DMA((7 * nt,)), pltpu.SemaphoreType.DMA((7 * nt,)), pltpu.SemaphoreType.DMA((nt,))],
    )(*shards)


def _rs_pair(gs, name):
    nt = len(gs)

    def body(*refs):
        g_refs, a_refs = refs[:nt], refs[nt:2 * nt]
        send_sems, recv_sems = refs[2 * nt:]
        mx, my, mc = _position()
        copies = [pltpu.make_async_remote_copy(
            src_ref=_dev_rows(g_refs[t], 2 * j + 1 - mc, gs[t].shape[1] // N_DEV), dst_ref=a_refs[t].at[j],
            send_sem=send_sems.at[4 * t + j], recv_sem=recv_sems.at[4 * t + j],
            device_id=(mx, my, 1 - mc), device_id_type=MESH) for t in range(nt) for j in range(4)]
        for cp in copies:
            cp.start()
        for cp in copies:
            cp.wait()

    out_shape = [jax.ShapeDtypeStruct((4, g.shape[0], g.shape[1] // N_DEV, g.shape[2]), g.dtype) for g in gs]
    return pl.pallas_call(
        body, name=name, out_shape=out_shape, in_specs=[ANY] * nt, out_specs=[ANY] * nt,
        scratch_shapes=[pltpu.SemaphoreType.DMA((4 * nt,)), pltpu.SemaphoreType.DMA((4 * nt,))],
    )(*gs)


def _rs_chips(ps, name):
    nt = len(ps)

    def body(*refs):
        p_refs, b_refs = refs[:nt], refs[nt:2 * nt]
        send_sems, recv_sems = refs[2 * nt:]
        mx, my, mc = _position()
        chips = [(1 - mx, my), (mx, 1 - my), (1 - mx, 1 - my)]
        copies = [pltpu.make_async_remote_copy(
            src_ref=p_refs[t].at[2 * cx + cy], dst_ref=b_refs[t].at[k],
            send_sem=send_sems.at[3 * t + k], recv_sem=recv_sems.at[3 * t + k],
            device_id=(cx, cy, mc), device_id_type=MESH) for t in range(nt) for k, (cx, cy) in enumerate(chips)]
        for cp in copies:
            cp.start()
        for cp in copies:
            cp.wait()

    out_shape = [jax.ShapeDtypeStruct((3,) + p.shape[1:], p.dtype) for p in ps]
    return pl.pallas_call(
        body, name=name, out_shape=out_shape, in_specs=[ANY] * nt, out_specs=[ANY] * nt,
        scratch_shapes=[pltpu.SemaphoreType.DMA((3 * nt,)), pltpu.SemaphoreType.DMA((3 * nt,))],
    )(*ps)


def _div_tile(a, b):
    best = 16
    for t in range(16, a + 1, 16):
        if a % t == 0 and t * b * 4 <= 2 * 1024 * 1024:
            best = t
    return best


def _rs_add_pair(g, a, core, name):
    l, a8, b = g.shape
    rows = a8 // N_DEV
    ta = _div_tile(rows, b)

    def body(core_ref, g_ref, a_ref, p_ref):
        p_ref[...] = (g_ref[...] + a_ref[...]).astype(BF16)

    grid_spec = pltpu.PrefetchScalarGridSpec(
        num_scalar_prefetch=1, grid=(4, l, rows // ta),
        in_specs=[pl.BlockSpec((1, 1, ta, b), lambda j, li, i, core_ref: (li, 2 * j + core_ref[0], i, 0)),
                  pl.BlockSpec((1, 1, ta, b), lambda j, li, i, core_ref: (j, li, i, 0))],
        out_specs=pl.BlockSpec((1, 1, ta, b), lambda j, li, i, core_ref: (j, li, i, 0)))
    return pl.pallas_call(body, name=name, grid_spec=grid_spec, out_shape=jax.ShapeDtypeStruct((4, l, rows, b), BF16),
                          compiler_params=_params(("parallel", "parallel", "parallel")))(core, g.reshape(l, N_DEV, rows, b), a)


def _rs_final(g, a, bsum, where, name):
    l, a8, b = g.shape
    rows = a8 // N_DEV
    ta = _div_tile(rows, b)

    def body(where_ref, g_ref, a_ref, b_ref, o_ref):
        own = g_ref[0, 0] + a_ref[0, 0]
        o_ref[0] = ((own + b_ref[0, 0].astype(F32)) + b_ref[1, 0].astype(F32)) + b_ref[2, 0].astype(F32)

    grid_spec = pltpu.PrefetchScalarGridSpec(
        num_scalar_prefetch=1, grid=(l, rows // ta),
        in_specs=[pl.BlockSpec((1, 1, ta, b), lambda li, i, w_ref: (li, w_ref[0], i, 0)),
                  pl.BlockSpec((1, 1, ta, b), lambda li, i, w_ref: (w_ref[1], li, i, 0)),
                  pl.BlockSpec((3, 1, ta, b), lambda li, i, w_ref: (0, li, i, 0))],
        out_specs=pl.BlockSpec((1, ta, b), lambda li, i, w_ref: (li, i, 0)))
    return pl.pallas_call(body, name=name, grid_spec=grid_spec, out_shape=jax.ShapeDtypeStruct((l, rows, b), F32),
                          compiler_params=_params(("parallel", "parallel")))(where, g.reshape(l, N_DEV, rows, b), a, bsum)


def _sum_devices(x, name):
    _, r, c = x.shape

    def body(x_ref, o_ref):
        acc = x_ref[0]
        for d in range(1, N_DEV):
            acc = acc + x_ref[d]
        o_ref[...] = acc

    return pl.pallas_call(body, name=name, out_shape=jax.ShapeDtypeStruct((r, c), F32))(x)


BIG = (("ab_w_in", "col"), ("ab_w_out", "row"), ("c_w_in", "col"), ("c_w_out", "row"),
       ("ffn_up", "col"), ("ffn_down", "row"), ("ple_gate", "row"), ("ple_proj", "col"))
KIND = dict(BIG)


def _row_block(shard, kind):
    return shard.transpose(0, 2, 1) if kind == "col" else shard


def _pad_rows(flat):
    pad = -flat.shape[0] % (8 * LANES)
    return jnp.pad(flat, (0, pad)).reshape(-1, LANES)


def _heads_out(x, b, s, nh, d):
    return x.reshape(b, s, nh, d).transpose(0, 2, 1, 3).reshape(b * nh, s, d)


def _heads_in(x, b, s, nh, d):
    return x.reshape(b, nh, s, d).transpose(0, 2, 1, 3).reshape(b * s, nh * d)


def _lower_bounds(logits):
    c = jnp.cumsum(jax.nn.softmax(logits.astype(F32), axis=0), axis=0)
    return c - c[0]


SB_COLS = tuple(k * SB_WIDTH // LANES for k in range(3))
HG_COLS = tuple((3 * SB_WIDTH + k * HG_W) // LANES for k in range(3))
HG_GATE_COL = (3 * SB_WIDTH + 3 * HG_W) // LANES
HG_OUT_COL = SB_WIDTH // LANES


def _forward_backward(x, p, target, W, P):
    b, s, dm = x.shape
    n = b * s
    h = x.reshape(n, dm)
    lbs, lb_vjp = jax.vjp(_lower_bounds, P["hg_lb_logits"])
    bias, bias_vjp = jax.vjp(_t5_bias, P["rel_bias"])
    saved = []
    gw = {name: lax.empty(W[name].shape, F32) for name, _ in BIG}

    def times_w(a, name, l, tag, res=None):
        return _mm(a, W[name], "nt" if KIND[name] == "col" else "nn", tag, res=res, layer=l)

    def times_wt(dy, name, l, tag):
        return _mm(dy, W[name], "nn" if KIND[name] == "col" else "nt", tag, layer=l)

    def grad_w(a, dy, name, l, tag):
        lhs, rhs = (dy, a) if KIND[name] == "col" else (a, dy)
        gw[name] = _mm(lhs, rhs, "tn", tag, into=(gw[name], l, 0))

    for i in range(DEPTH):
        j = i // 2
        r = {"h0": h}
        hn = _rms_fwd(h, P["mix_norm"][i], f"mix_norm_f{i}", BF16)
        r["hn"] = hn
        if i % 2 == 0:
            proj = times_w(hn, "ab_w_in", j, f"ab_in_f{i}")
            oa, lta = _sb_fwd(proj, SB_COLS, b, s, f"sb_f{i}")
            ob, st = _hg_fwd(proj, HG_COLS, lbs[j].reshape(HG_HEADS, HG_DK), b, s, f"hg_f{i}")
            obg = _gnorm_fwd(ob, proj, HG_GATE_COL, P["hg_out_norm"][j], f"hg_norm_f{i}")
            cat = jnp.concatenate([oa, obg], axis=1).astype(BF16)
            h = times_w(cat, "ab_w_out", j, f"ab_out_f{i}", res=h)
            r.update(lta=lta, proj=proj, ob=ob, st=st, cat=cat)
        else:
            proj = times_w(hn, "c_w_in", j, f"c_in_f{i}")
            nq = SW_HEADS * SW_DIM
            nkv = SW_KV_HEADS * SW_DIM
            q = _heads_out(proj[:, :nq], b, s, SW_HEADS, SW_DIM).reshape(-1, SW_DIM)
            k = _heads_out(proj[:, nq:nq + nkv], b, s, SW_KV_HEADS, SW_DIM).reshape(-1, SW_DIM)
            v = _heads_out(proj[:, nq + nkv:], b, s, SW_KV_HEADS, SW_DIM).reshape(b, SW_KV_HEADS, s, SW_DIM)
            qn = _rms_fwd(q, P["q_norm"][j], f"q_norm_f{i}").reshape(b, SW_KV_HEADS, SW_GROUP, s, SW_DIM)
            kn = _rms_fwd(k, P["k_norm"][j], f"k_norm_f{i}").reshape(b, SW_KV_HEADS, s, SW_DIM)
            o = _swa_fwd(qn, kn, v, P["sinks"][j], bias, f"swa_f{i}")
            o2 = _heads_in(o.reshape(b * SW_HEADS, s, SW_DIM), b, s, SW_HEADS, SW_DIM).astype(BF16)
            h = times_w(o2, "c_w_out", j, f"c_out_f{i}", res=h)
            r.update(q=q, k=k, v=v, qn=qn, kn=kn, o2=o2)
        r["h1"] = h
        hn2 = _rms_fwd(h, P["ffn_norm"][i], f"ffn_norm_f{i}", BF16)
        u = times_w(hn2, "ffn_up", i, f"ffn_up_f{i}").reshape(b, s, 2 * D_FF)
        act = _convglu_fwd(u, W["ffn_conv"][i], P["ffn_conv_b"][i], f"conv_f{i}", BF16).reshape(n, D_FF)
        h = times_w(act, "ffn_down", i, f"ffn_down_f{i}", res=h)
        r.update(hn2=hn2, u=u, act=act, h2=h)
        hn3 = _rms_fwd(h, P["ple_norm"][i], f"ple_norm_f{i}", BF16)
        z = times_w(hn3, "ple_gate", i, f"ple_gate_f{i}")
        pi = p[i].reshape(n, PLE_DIM)
        e = times_w(pi, "ple_proj", i, f"ple_proj_f{i}")
        h = _sigmul_fwd(z, e, h, f"ple_f{i}")
        r.update(hn3=hn3, z=z, e=e, pi=pi)
        saved.append(r)

    loss, dh = _loss_fwd(h, target.reshape(n, dm), "loss")

    gconv = [None] * DEPTH
    gp = {name: [None] * P[name].shape[0] for name in ("mix_norm", "hg_out_norm", "q_norm", "k_norm", "sinks",
                                                        "ffn_norm", "ffn_conv_b", "ple_norm")}
    dlbs = [None] * (DEPTH // 2)
    dbias = jnp.zeros_like(bias)

    for i in reversed(range(DEPTH)):
        j = i // 2
        r = saved[i]
        dz, de = _sigmul_bwd(r["z"], r["e"], dh, f"ple_b{i}")
        grad_w(r["pi"], de, "ple_proj", i, f"ple_proj_g{i}")
        grad_w(r["hn3"], dz, "ple_gate", i, f"ple_gate_g{i}")
        dhn3 = times_wt(dz, "ple_gate", i, f"ple_gate_b{i}")
        dh, gp["ple_norm"][i] = _rms_bwd(r["h2"], P["ple_norm"][i], dhn3, f"ple_norm_b{i}", res=dh)

        dact = times_wt(dh, "ffn_down", i, f"ffn_down_b{i}").reshape(b, s, D_FF)
        grad_w(r["act"], dh, "ffn_down", i, f"ffn_down_g{i}")
        dug, duu, ag, au = _convglu_bwd(r["u"], W["ffn_conv"][i], P["ffn_conv_b"][i], dact, f"conv_b{i}")
        gconv[i] = jnp.concatenate([ag[:CONV_W], au[:CONV_W]], axis=-1)
        gp["ffn_conv_b"][i] = jnp.concatenate([ag[CONV_W], au[CONV_W]], axis=-1)
        dhn2 = None
        for half, dpart in enumerate((dug.reshape(n, D_FF), duu.reshape(n, D_FF))):
            gw["ffn_up"] = _mm(dpart, r["hn2"], "tn", f"ffn_up_g{i}_{half}", into=(gw["ffn_up"], i, half * D_FF))
            dhn2 = _mm(dpart, W["ffn_up"], "nn", f"ffn_up_b{i}_{half}", layer=i, b_rows=(half * D_FF, D_FF), res=dhn2)
        dh, gp["ffn_norm"][i] = _rms_bwd(r["h1"], P["ffn_norm"][i], dhn2, f"ffn_norm_b{i}", res=dh)

        if i % 2 == 0:
            dcat = times_wt(dh, "ab_w_out", j, f"ab_out_b{i}")
            grad_w(r["cat"], dh, "ab_w_out", j, f"ab_out_g{i}")
            dob, dgb, gp["hg_out_norm"][j] = _gnorm_bwd(r["ob"], r["proj"], HG_GATE_COL, P["hg_out_norm"][j], dcat, HG_OUT_COL,
                                                        f"hg_norm_b{i}")
            dqb, dfb, dib, dlb = _hg_bwd(r["proj"], HG_COLS, lbs[j].reshape(HG_HEADS, HG_DK), r["st"], dob, b, s, f"hg_b{i}")
            dlbs[j] = dlb.reshape(b, HG_W).sum(axis=0)
            dqa, dka, dva = _sb_bwd(r["proj"], SB_COLS, r["lta"], dcat, 0, b, s, f"sb_b{i}")
            dproj = jnp.concatenate([dqa, dka, dva, dqb, dfb, dib, dgb], axis=1)
            grad_w(r["hn"], dproj, "ab_w_in", j, f"ab_in_g{i}")
            dhn = times_wt(dproj, "ab_w_in", j, f"ab_in_b{i}")
        else:
            do2 = times_wt(dh, "c_w_out", j, f"c_out_b{i}")
            grad_w(r["o2"], dh, "c_w_out", j, f"c_out_g{i}")
            do = _heads_out(do2, b, s, SW_HEADS, SW_DIM).reshape(b, SW_KV_HEADS, SW_GROUP, s, SW_DIM)
            dqn, dkn, dv, dbias_i, dsink = _swa_bwd(r["qn"], r["kn"], r["v"], P["sinks"][j], bias, do, f"swa_b{i}")
            dbias = dbias + dbias_i
            gp["sinks"][j] = dsink[:, :, 0].reshape(SW_HEADS)
            dq, gp["q_norm"][j] = _rms_bwd(r["q"], P["q_norm"][j], dqn.reshape(-1, SW_DIM), f"q_norm_b{i}")
            dk, gp["k_norm"][j] = _rms_bwd(r["k"], P["k_norm"][j], dkn.reshape(-1, SW_DIM), f"k_norm_b{i}")
            dproj = jnp.concatenate([_heads_in(dq.reshape(b * SW_HEADS, s, SW_DIM), b, s, SW_HEADS, SW_DIM),
                                     _heads_in(dk.reshape(b * SW_KV_HEADS, s, SW_DIM), b, s, SW_KV_HEADS, SW_DIM),
                                     _heads_in(dv.reshape(b * SW_KV_HEADS, s, SW_DIM), b, s, SW_KV_HEADS, SW_DIM)], axis=1)
            grad_w(r["hn"], dproj, "c_w_in", j, f"c_in_g{i}")
            dhn = times_wt(dproj, "c_w_in", j, f"c_in_b{i}")
        dh, gp["mix_norm"][i] = _rms_bwd(r["h0"], P["mix_norm"][i], dhn, f"mix_norm_b{i}", res=dh)

    gp = {name: jnp.stack(v) for name, v in gp.items()}
    gp["hg_lb_logits"] = lb_vjp(jnp.stack(dlbs))[0]
    gp["rel_bias"] = bias_vjp(dbias)[0]
    return loss[0, 0], dh.reshape(b, s, dm), gw, jnp.stack(gconv), gp


WEIGHTS = ("mix_norm", "ab_w_in", "hg_lb_logits", "hg_out_norm", "ab_w_out", "c_w_in", "q_norm", "k_norm", "sinks", "rel_bias",
           "c_w_out", "ffn_norm", "ffn_up", "ffn_conv", "ffn_conv_b", "ffn_down", "ple_norm", "ple_gate", "ple_proj")
SMALL = ("mix_norm", "hg_lb_logits", "hg_out_norm", "q_norm", "k_norm", "sinks", "rel_bias", "ffn_norm", "ffn_conv_b", "ple_norm")


def _step(x, p, target, w, m, v):
    names = [name for name, _ in BIG]
    mx, my, mc = _position()
    dev = 4 * mx + 2 * my + mc

    blocks = [_row_block(w[name], KIND[name]).astype(BF16) for name in names]
    full = dict(zip(names, _all_gather_rows(blocks, "gather_weights")))
    nl, taps, cs = w["ffn_conv"].shape
    conv_all = _all_gather(_pad_rows(w["ffn_conv"].reshape(-1)), "gather_conv").reshape(N_DEV, -1)[:, :nl * taps * cs]
    full["ffn_conv"] = conv_all.reshape(N_DEV, nl, taps, cs).transpose(1, 2, 0, 3).reshape(nl, taps, N_DEV * cs)

    small = {name: w[name] for name in SMALL}
    loss, grad_x, gw, gconv, gp = _forward_backward(x, p, target, full, small)

    core = jnp.reshape(mc, (1,)).astype(jnp.int32)
    where = jnp.stack([dev, 2 * mx + my]).astype(jnp.int32)
    parts = [gw[name] for name in names]
    from_sibling = _rs_pair(parts, "reduce_pair")
    chip_sums = [_rs_add_pair(g, a, core, f"reduce_pair_add_{name}") for name, g, a in zip(names, parts, from_sibling)]
    from_chips = _rs_chips(chip_sums, "reduce_chips")
    grads = {name: _row_block(_rs_final(g, a, bs, where, f"reduce_final_{name}"), KIND[name])
             for name, g, a, bs in zip(names, parts, from_sibling, from_chips)}

    flat_small = jnp.concatenate([gp[name].reshape(-1) for name in SMALL] + [gconv.reshape(-1), loss.reshape(1)])
    small_sum = _sum_devices(_all_gather(_pad_rows(flat_small), "gather_small"), "sum_small").reshape(-1)
    e0 = 0
    for name in SMALL:
        cnt = math.prod(w[name].shape)
        grads[name] = small_sum[e0:e0 + cnt].reshape(w[name].shape)
        e0 += cnt
    gconv_sum = small_sum[e0:e0 + gconv.size].reshape(gconv.shape)
    grads["ffn_conv"] = lax.dynamic_slice_in_dim(gconv_sum, dev * cs, cs, axis=2)
    loss = small_sum[e0 + gconv.size]

    deltas, new_m, new_v = {}, {}, {}
    for name in WEIGHTS:
        shape = w[name].shape
        view = (-1, shape[-1]) if len(shape) > 1 else (1, -1)
        d_, m_, v_ = _adamw(w[name].reshape(view), grads[name].reshape(view), m[name].reshape(view), v[name].reshape(view), f"adamw_{name}")
        deltas[name], new_m[name], new_v[name] = d_.reshape(shape), m_.reshape(shape), v_.reshape(shape)
    return (loss, grad_x, *[grads[k] for k in WEIGHTS], *[deltas[k] for k in WEIGHTS],
            *[new_m[k] for k in WEIGHTS], *[new_v[k] for k in WEIGHTS])


def kernel(x, p, mix_norm, ab_w_in, hg_lb_logits, hg_out_norm, ab_w_out, c_w_in, q_norm, k_norm, sinks, rel_bias, c_w_out, ffn_norm, ffn_up, ffn_conv, ffn_conv_b, ffn_down, ple_norm, ple_gate, ple_proj, loss_target, m_mix_norm, m_ab_w_in, m_hg_lb_logits, m_hg_out_norm, m_ab_w_out, m_c_w_in, m_q_norm, m_k_norm, m_sinks, m_rel_bias, m_c_w_out, m_ffn_norm, m_ffn_up, m_ffn_conv, m_ffn_conv_b, m_ffn_down, m_ple_norm, m_ple_gate, m_ple_proj, v_mix_norm, v_ab_w_in, v_hg_lb_logits, v_hg_out_norm, v_ab_w_out, v_c_w_in, v_q_norm, v_k_norm, v_sinks, v_rel_bias, v_c_w_out, v_ffn_norm, v_ffn_up, v_ffn_conv, v_ffn_conv_b, v_ffn_down, v_ple_norm, v_ple_gate, v_ple_proj):
    w = dict(zip(WEIGHTS, (mix_norm, ab_w_in, hg_lb_logits, hg_out_norm, ab_w_out, c_w_in, q_norm, k_norm, sinks, rel_bias, c_w_out,
                           ffn_norm, ffn_up, ffn_conv, ffn_conv_b, ffn_down, ple_norm, ple_gate, ple_proj)))
    m = dict(zip(WEIGHTS, (m_mix_norm, m_ab_w_in, m_hg_lb_logits, m_hg_out_norm, m_ab_w_out, m_c_w_in, m_q_norm, m_k_norm, m_sinks,
                           m_rel_bias, m_c_w_out, m_ffn_norm, m_ffn_up, m_ffn_conv, m_ffn_conv_b, m_ffn_down, m_ple_norm, m_ple_gate,
                           m_ple_proj)))
    v = dict(zip(WEIGHTS, (v_mix_norm, v_ab_w_in, v_hg_lb_logits, v_hg_out_norm, v_ab_w_out, v_c_w_in, v_q_norm, v_k_norm, v_sinks,
                           v_rel_bias, v_c_w_out, v_ffn_norm, v_ffn_up, v_ffn_conv, v_ffn_conv_b, v_ffn_down, v_ple_norm, v_ple_gate,
                           v_ple_proj)))
    return _step(x, p, loss_target, w, m, v)
```

```python
import functools
import math

import numpy as np
import jax
import jax.numpy as jnp
from jax import lax
from jax.experimental import pallas as pl
from jax.experimental.pallas import tpu as pltpu

F32 = jnp.float32
BF16 = jnp.bfloat16

D_MODEL = 1024
DEPTH = 4
PLE_DIM = 256
EPS = 1e-6
SB_HEADS, SB_DIM = 8, 64
SB_WIDTH = SB_HEADS * SB_DIM
HG_HEADS, HG_DK, HG_DV = 4, 128, 128
HG_W = HG_HEADS * HG_DK
AB_IN = 3 * SB_WIDTH + 4 * HG_W
SW_HEADS, SW_KV_HEADS, SW_DIM = 16, 4, 64
SW_GROUP = SW_HEADS // SW_KV_HEADS
WINDOW = 128
C_IN = (SW_HEADS + 2 * SW_KV_HEADS) * SW_DIM
N_BUCKETS, MAX_DISTANCE = 32, 128
D_FF = 2816
N_DEV = 8

ADAM_LR, ADAM_B1, ADAM_B2, ADAM_EPS, ADAM_WD, ADAM_STEP = 0.001, 0.9, 0.999, 1e-08, 0.01, 10

LANES = 128
VMEM_LIMIT = 48 * 1024 * 1024

NN = (((1,), (0,)), ((), ()))
NT = (((1,), (1,)), ((), ()))
TN = (((0,), (0,)), ((), ()))


MXU_DTYPE = BF16


def _bf(x):
    return x.astype(MXU_DTYPE)


def _dot(a, b, dims=NN):
    return lax.dot_general(_bf(a), _bf(b), dims, preferred_element_type=F32)


def _split3(x):
    x1 = _bf(x)
    r = x - x1.astype(F32)
    x2 = _bf(r)
    x3 = _bf(r - x2.astype(F32))
    return x1, x2, x3


def _dot_exact_lhs01(m, x, terms=3):
    parts = _split3(x)[:terms]
    out = lax.dot_general(m, parts[0], NN, preferred_element_type=F32)
    for p_ in parts[1:]:
        out = out + lax.dot_general(m, p_, NN, preferred_element_type=F32)
    return out


def _dot_exact_rhs01(x, m, terms=2):
    parts = _split3(x)[:terms]
    out = lax.dot_general(parts[0], m, NN, preferred_element_type=F32)
    for p_ in parts[1:]:
        out = out + lax.dot_general(p_, m, NN, preferred_element_type=F32)
    return out


def _pick(n, target):
    best = None
    for t in range(LANES, target + 1, LANES):
        if n % t == 0:
            best = t
    return best or n


def _params(sem=None):
    return pltpu.CompilerParams(dimension_semantics=sem, vmem_limit_bytes=VMEM_LIMIT)


MM_ROWS, MM_ROWS_TN, MM_COLS, MM_DEPTH, MM_DEPTH_TN = 1024, 1408, 1792, 2048, 1024

def _mm(a, b, mode, name, res=None, out_dtype=F32, layer=None, b_rows=None, into=None):
    bshape = b.shape if layer is None else b.shape[1:]
    if b_rows is not None:
        assert mode == "nn"
        bshape = (b_rows[1], bshape[1])
    if mode == "nn":
        (M, K), (K2, N) = a.shape, bshape
    elif mode == "nt":
        (M, K), (N, K2) = a.shape, bshape
    else:
        (K, M), (K2, N) = a.shape, bshape
    assert K == K2, (a.shape, b.shape, mode)
    tm, tn = _pick(M, MM_ROWS_TN if mode == "tn" else MM_ROWS), _pick(N, MM_COLS)
    tk = _pick(K, MM_DEPTH_TN if mode == "tn" else MM_DEPTH)
    nk = K // tk
    k_off = 0 if b_rows is None else b_rows[0] // tk
    assert b_rows is None or b_rows[0] % tk == 0
    dims = {"nn": NN, "nt": NT, "tn": TN}[mode]
    a_spec = pl.BlockSpec((tk, tm), lambda i, j, k: (k, i)) if mode == "tn" else pl.BlockSpec((tm, tk), lambda i, j, k: (i, k))
    if layer is None:
        b_spec = pl.BlockSpec((tn, tk), lambda i, j, k: (j, k)) if mode == "nt" else pl.BlockSpec((tk, tn), lambda i, j, k: (k, j))
    elif mode == "nt":
        b_spec = pl.BlockSpec((None, tn, tk), lambda i, j, k: (layer, j, k))
    else:
        b_spec = pl.BlockSpec((None, tk, tn), lambda i, j, k: (layer, k + k_off, j))
    o_spec = pl.BlockSpec((tm, tn), lambda i, j, k: (i, j))
    has_res = res is not None

    def finish(acc, r_ref, o_ref):
        if has_res:
            acc = acc + r_ref[...]
        o_ref[...] = acc.astype(out_dtype)

    def body(*refs):
        a_ref, b_ref = refs[0], refs[1]
        r_ref = refs[2] if has_res else None
        o_ref = refs[-1] if nk == 1 else refs[-2]
        part = _dot(a_ref[...], b_ref[...], dims)
        if nk == 1:
            finish(part, r_ref, o_ref)
            return
        acc_ref = refs[-1]
        k = pl.program_id(2)

        @pl.when(k == 0)
        def _():
            acc_ref[...] = part

        @pl.when((k > 0) & (k < nk - 1))
        def _():
            acc_ref[...] += part

        @pl.when(k == nk - 1)
        def _():
            finish(acc_ref[...] + part, r_ref, o_ref)

    in_specs = [a_spec, b_spec] + ([o_spec] if has_res else [])
    args = (a, b) + ((res,) if has_res else ())
    out_shape, aliases = jax.ShapeDtypeStruct((M, N), out_dtype), {}
    if into is not None:
        stack, slot, row = into
        assert stack.shape[2] == N and row % tm == 0 and row + M <= stack.shape[1] and stack.dtype == out_dtype and not has_res
        in_specs = in_specs + [pl.BlockSpec(memory_space=pl.ANY)]
        args = args + (stack,)
        o_spec = pl.BlockSpec((None, tm, tn), lambda i, j, k: (slot, i + row // tm, j))
        out_shape, aliases = jax.ShapeDtypeStruct(stack.shape, out_dtype), {2: 0}

    def body_into(a_ref, b_ref, stack_ref, *rest):
        body(a_ref, b_ref, *rest)

    return pl.pallas_call(
        body if into is None else body_into, name=name, grid=(M // tm, N // tn, nk), in_specs=in_specs, out_specs=o_spec,
        out_shape=out_shape, scratch_shapes=[] if nk == 1 else [pltpu.VMEM((tm, tn), F32)], input_output_aliases=aliases,
        compiler_params=_params(("parallel", "parallel", "arbitrary")),
    )(*args)


def _row_tile(n, d):
    if n % 8:
        return n
    t = 8
    while t * 2 <= min(n, (512 * 1024) // d) and n % (t * 2) == 0:
        t *= 2
    return t


def _rms_fwd(x, g, name, out_dtype=F32):
    n, d = x.shape
    tm = _row_tile(n, d)

    def body(x_ref, g_ref, o_ref):
        xf = x_ref[...]
        r = lax.rsqrt(jnp.mean(xf * xf, axis=-1, keepdims=True) + EPS)
        o_ref[...] = (xf * r * g_ref[...]).astype(out_dtype)

    return pl.pallas_call(
        body, name=name, grid=(n // tm,),
        in_specs=[pl.BlockSpec((tm, d), lambda i: (i, 0)), pl.BlockSpec((1, d), lambda i: (0, 0))],
        out_specs=pl.BlockSpec((tm, d), lambda i: (i, 0)),
        out_shape=jax.ShapeDtypeStruct((n, d), out_dtype), compiler_params=_params(("parallel",)),
    )(x, g.reshape(1, d))


def _rms_bwd(x, g, dy, name, res=None):
    n, d = x.shape
    tm = _row_tile(n, d)
    has_res = res is not None

    def body(*refs):
        x_ref, g_ref, dy_ref = refs[:3]
        r_ref = refs[3] if has_res else None
        dx_ref, dg_ref = refs[-2:]
        xf = x_ref[...]
        r = lax.rsqrt(jnp.mean(xf * xf, axis=-1, keepdims=True) + EPS)
        xh = xf * r
        dyf = dy_ref[...].astype(F32)
        dxh = dyf * g_ref[...]
        dx = r * (dxh - xh * jnp.mean(dxh * xh, axis=-1, keepdims=True))
        if has_res:
            dx = dx + r_ref[...]
        dx_ref[...] = dx

        @pl.when(pl.program_id(0) == 0)
        def _():
            dg_ref[...] = jnp.zeros_like(dg_ref)

        dg_ref[...] += jnp.sum(dyf * xh, axis=0, keepdims=True)

    row = pl.BlockSpec((tm, d), lambda i: (i, 0))
    vec = pl.BlockSpec((1, d), lambda i: (0, 0))
    dx, dg = pl.pallas_call(
        body, name=name, grid=(n // tm,),
        in_specs=[row, vec, row] + ([row] if has_res else []),
        out_specs=[row, vec],
        out_shape=[jax.ShapeDtypeStruct((n, d), F32), jax.ShapeDtypeStruct((1, d), F32)],
        compiler_params=_params(("arbitrary",)),
    )(x, g.reshape(1, d), dy, *((res,) if has_res else ()))
    return dx, dg.reshape(d)


def _silu(x):
    return x * jax.nn.sigmoid(x)


def _gnorm_fwd(o, gate, gate_col, w, name):
    n, width = o.shape
    d = w.shape[0]
    tm = _row_tile(n, d)

    def body(o_ref, g_ref, w_ref, y_ref):
        of = o_ref[...]
        r = lax.rsqrt(jnp.mean(of * of, axis=-1, keepdims=True) + EPS)
        y_ref[...] = of * r * w_ref[...] * _silu(g_ref[...])

    row = pl.BlockSpec((tm, d), lambda i, h: (i, h))
    vec = pl.BlockSpec((1, d), lambda i, h: (0, 0))
    return pl.pallas_call(body, name=name, grid=(n // tm, width // d),
                          in_specs=[row, pl.BlockSpec((tm, d), lambda i, h: (i, gate_col + h)), vec], out_specs=row,
                          out_shape=jax.ShapeDtypeStruct((n, width), F32), compiler_params=_params(("parallel", "parallel")))(o, gate, w.reshape(1, d))


def _gnorm_bwd(o, gate, gate_col, w, dy, dy_col, name):
    n, width = o.shape
    d = w.shape[0]
    tm = _row_tile(n, d)

    def body(o_ref, g_ref, w_ref, dy_ref, do_ref, dgate_ref, dw_ref):
        of, gf, dyf = o_ref[...], g_ref[...], dy_ref[...]
        r = lax.rsqrt(jnp.mean(of * of, axis=-1, keepdims=True) + EPS)
        xh = of * r
        sg = jax.nn.sigmoid(gf)
        sil = gf * sg
        dnorm = dyf * sil
        dgate_ref[...] = dyf * xh * w_ref[...] * (sg * (1.0 + gf * (1.0 - sg)))
        dxh = dnorm * w_ref[...]
        do_ref[...] = r * (dxh - xh * jnp.mean(dxh * xh, axis=-1, keepdims=True))

        @pl.when((pl.program_id(0) == 0) & (pl.program_id(1) == 0))
        def _():
            dw_ref[...] = jnp.zeros_like(dw_ref)

        dw_ref[...] += jnp.sum(dnorm * xh, axis=0, keepdims=True)

    row = pl.BlockSpec((tm, d), lambda i, h: (i, h))
    vec = pl.BlockSpec((1, d), lambda i, h: (0, 0))
    do, dgate, dw = pl.pallas_call(
        body, name=name, grid=(n // tm, width // d),
        in_specs=[row, pl.BlockSpec((tm, d), lambda i, h: (i, gate_col + h)), vec, pl.BlockSpec((tm, d), lambda i, h: (i, dy_col + h))],
        out_specs=[row, row, vec],
        out_shape=[jax.ShapeDtypeStruct((n, width), F32)] * 2 + [jax.ShapeDtypeStruct((1, d), F32)],
        compiler_params=_params(("arbitrary", "arbitrary")),
    )(o, gate, w.reshape(1, d), dy)
    return do, dgate, dw.reshape(d)


def _sigmul_fwd(z, e, res, name, next_gain=None):
    n, d = z.shape
    tm = _row_tile(n, d)
    row = pl.BlockSpec((tm, d), lambda i: (i, 0))
    if next_gain is None:
        def body(z_ref, e_ref, r_ref, o_ref):
            o_ref[...] = r_ref[...] + jax.nn.sigmoid(z_ref[...]) * e_ref[...]

        return pl.pallas_call(body, name=name, grid=(n // tm,), in_specs=[row] * 3, out_specs=row,
                              out_shape=jax.ShapeDtypeStruct((n, d), F32), compiler_params=_params(("parallel",)))(z, e, res)

    def body_norm(z_ref, e_ref, r_ref, g_ref, o_ref, hn_ref):
        hv = r_ref[...] + jax.nn.sigmoid(z_ref[...]) * e_ref[...]
        o_ref[...] = hv
        r = lax.rsqrt(jnp.mean(hv * hv, axis=-1, keepdims=True) + EPS)
        hn_ref[...] = (hv * r * g_ref[...]).astype(BF16)

    vec = pl.BlockSpec((1, d), lambda i: (0, 0))
    return pl.pallas_call(body_norm, name=name, grid=(n // tm,), in_specs=[row] * 3 + [vec], out_specs=[row, row],
                          out_shape=[jax.ShapeDtypeStruct((n, d), F32), jax.ShapeDtypeStruct((n, d), BF16)],
                          compiler_params=_params(("parallel",)))(z, e, res, next_gain.reshape(1, d))


def _sigmul_bwd(z, e, dy, name):
    n, d = z.shape
    tm = _row_tile(n, d)

    def body(z_ref, e_ref, dy_ref, dz_ref, de_ref):
        s = jax.nn.sigmoid(z_ref[...])
        dyf = dy_ref[...]
        dz_ref[...] = dyf * e_ref[...] * s * (1.0 - s)
        de_ref[...] = dyf * s

    row = pl.BlockSpec((tm, d), lambda i: (i, 0))
    return pl.pallas_call(body, name=name, grid=(n // tm,), in_specs=[row] * 3, out_specs=[row] * 2,
                          out_shape=[jax.ShapeDtypeStruct((n, d), F32)] * 2, compiler_params=_params(("parallel",)))(z, e, dy)


def _loss_fwd(y, target, name):
    n, d = y.shape
    tm = _row_tile(n, d)

    def body(y_ref, t_ref, l_ref, dy_ref):
        diff = y_ref[...] - t_ref[...]
        dy_ref[...] = diff * (1.0 / d)

        @pl.when(pl.program_id(0) == 0)
        def _():
            l_ref[...] = jnp.zeros_like(l_ref)

        part = jnp.sum(jnp.mean(diff * diff, axis=-1, keepdims=True), axis=0, keepdims=True)
        l_ref[...] += 0.5 * jnp.broadcast_to(part, l_ref.shape)

    row = pl.BlockSpec((tm, d), lambda i: (i, 0))
    vec = pl.BlockSpec((1, LANES), lambda i: (0, 0))
    return pl.pallas_call(body, name=name, grid=(n // tm,), in_specs=[row, row], out_specs=[vec, row],
                          out_shape=[jax.ShapeDtypeStruct((1, LANES), F32), jax.ShapeDtypeStruct((n, d), F32)],
                          compiler_params=_params(("arbitrary",)))(y, target)


SB_BLK = 128
SB_QBLK = 512


def _sb_logits(z, qi, kj, row, col):
    mask = (kj * SB_BLK + col) < (qi * SB_QBLK + row)
    sp = jnp.maximum(z, 0.0) + jnp.log1p(jnp.exp(-jnp.abs(z)))
    lk = jnp.where(mask, -sp, 0.0)
    return mask, lk, z - sp


SB_PAIRS = SB_WIDTH // LANES


def _sb_iotas():
    row = lax.broadcasted_iota(jnp.int32, (2 * SB_QBLK, SB_BLK), 0)
    row = jnp.where(row >= SB_QBLK, row - SB_QBLK, row)
    col = lax.broadcasted_iota(jnp.int32, (2 * SB_QBLK, SB_BLK), 1)
    return row, col, col[:SB_QBLK] < SB_DIM


def _sb_stack(x, first):
    return jnp.concatenate([jnp.where(first, x, 0.0), jnp.where(first, 0.0, x)], axis=0)


def _sb_unstack(y, first):
    return jnp.where(first, y[:SB_QBLK], y[SB_QBLK:])


def _sb_running(x, u):
    m = x.shape[0]
    hi = _bf(x)
    lo = _bf(x - hi.astype(F32))
    c = lax.dot_general(jnp.concatenate([hi, lo], axis=0), u, NN, preferred_element_type=F32)
    return c[:m] + c[m:]


def _sb_spec(s, col):
    return pl.BlockSpec((s, LANES), lambda e, pr: (e, col + pr))


def _sb_fwd(proj, cols, b, s, name):
    nq = s // SB_QBLK
    scale = SB_DIM ** -0.5

    def body(q_ref, k_ref, v_ref, o_ref, lt_ref):
        row, col, first = _sb_iotas()
        u_after = _bf(row[:SB_BLK] > col[:SB_BLK])

        def qloop(qi, _):
            q0 = pl.multiple_of(qi * SB_QBLK, SB_QBLK)
            q2 = _sb_stack(q_ref[pl.ds(q0, SB_QBLK), :], first)
            nkeys = (qi + 1) * (SB_QBLK // SB_BLK)

            def logits(kj):
                k0 = pl.multiple_of(kj * SB_BLK, SB_BLK)
                return _dot(q2, k_ref[pl.ds(k0, SB_BLK), :], NT) * scale

            def kloop(j, st):
                acc, carry, z = st
                kj = nkeys - 1 - j
                k0 = pl.multiple_of(kj * SB_BLK, SB_BLK)
                z_next = logits(jnp.maximum(kj - 1, 0))
                mask, lk, ls = _sb_logits(z, qi, kj, row, col)
                later = carry + _sb_running(lk, u_after)
                w = jnp.where(mask, jnp.exp(ls + later), 0.0)
                acc = acc + _sb_unstack(_dot(w, v_ref[pl.ds(k0, SB_BLK), :]), first)
                return acc, carry + jnp.sum(lk, axis=1, keepdims=True), z_next

            acc, carry, _ = lax.fori_loop(0, nkeys, kloop, (jnp.zeros((SB_QBLK, LANES), F32), jnp.zeros((2 * SB_QBLK, 1), F32), logits(nkeys - 1)))
            o_ref[pl.ds(q0, SB_QBLK), :] = acc
            lt_ref[pl.ds(q0, SB_QBLK), :] = _sb_unstack(jnp.broadcast_to(carry, (2 * SB_QBLK, LANES)), first)
            return 0

        lax.fori_loop(0, nq, qloop, 0)

    out = _sb_spec(s, 0)
    return pl.pallas_call(body, name=name, grid=(b, SB_PAIRS), in_specs=[_sb_spec(s, c) for c in cols], out_specs=[out, out],
                          out_shape=[jax.ShapeDtypeStruct((b * s, SB_WIDTH), F32)] * 2,
                          compiler_params=_params(("parallel", "parallel")))(proj, proj, proj)


def _sb_bwd(proj, cols, ltot, do, do_col, b, s, name):
    nq = s // SB_QBLK
    scale = SB_DIM ** -0.5

    def body(q_ref, k_ref, v_ref, lt_ref, do_ref, dq_ref, dk_ref, dv_ref):
        row, col, first = _sb_iotas()
        u_upto = _bf(row[:SB_BLK] <= col[:SB_BLK])
        u_before = _bf(row[:SB_BLK] < col[:SB_BLK])
        dk_ref[...] = jnp.zeros_like(dk_ref)
        dv_ref[...] = jnp.zeros_like(dv_ref)

        def qloop(qi, _):
            q0 = pl.multiple_of(qi * SB_QBLK, SB_QBLK)
            q2 = _sb_stack(q_ref[pl.ds(q0, SB_QBLK), :], first)
            nkeys = (qi + 1) * (SB_QBLK // SB_BLK)
            do2 = _sb_stack(do_ref[pl.ds(q0, SB_QBLK), :], first)
            lt2 = jnp.min(_sb_stack(lt_ref[pl.ds(q0, SB_QBLK), :], first), axis=1, keepdims=True)

            def logits(kj):
                k0 = pl.multiple_of(kj * SB_BLK, SB_BLK)
                return _dot(q2, k_ref[pl.ds(k0, SB_BLK), :], NT) * scale

            def kloop(kj, st):
                dq, cl, cg, z = st
                k0 = pl.multiple_of(kj * SB_BLK, SB_BLK)
                kb = k_ref[pl.ds(k0, SB_BLK), :]
                vb = v_ref[pl.ds(k0, SB_BLK), :]
                z_next = logits(jnp.minimum(kj + 1, nkeys - 1))
                mask, lk, ls = _sb_logits(z, qi, kj, row, col)
                later = lt2 - (cl + _sb_running(lk, u_upto))
                w = jnp.where(mask, jnp.exp(ls + later), 0.0)
                g = _dot(do2, vb, NT) * w
                dv_ref[pl.ds(k0, SB_BLK), :] += _dot(w, do2, TN)
                g_before = cg + _sb_running(g, u_before)
                sig = jnp.exp(ls)
                dz = jnp.where(mask, g * (1.0 - sig) - sig * g_before, 0.0) * scale
                dq = dq + _sb_unstack(_dot(dz, kb), first)
                dk_ref[pl.ds(k0, SB_BLK), :] += _dot(dz, q2, TN)
                return dq, cl + jnp.sum(lk, axis=1, keepdims=True), cg + jnp.sum(g, axis=1, keepdims=True), z_next

            z1 = jnp.zeros((2 * SB_QBLK, 1), F32)
            dq = lax.fori_loop(0, nkeys, kloop, (jnp.zeros((SB_QBLK, LANES), F32), z1, z1, logits(0)))[0]
            dq_ref[pl.ds(q0, SB_QBLK), :] = dq
            return 0

        lax.fori_loop(0, nq, qloop, 0)

    out = _sb_spec(s, 0)
    return pl.pallas_call(body, name=name, grid=(b, SB_PAIRS),
                          in_specs=[_sb_spec(s, c) for c in cols] + [out, _sb_spec(s, do_col)], out_specs=[out] * 3,
                          out_shape=[jax.ShapeDtypeStruct((b * s, SB_WIDTH), F32)] * 3,
                          compiler_params=_params(("parallel", "parallel")))(proj, proj, proj, ltot, do)


HG_CHUNK = 64
HG_GROUP = 2


def _hg_consts(c, r):
    levels = int(math.log2(c))
    t = np.arange(r)
    same = (t[:, None] // c) == (t[None, :] // c)
    tri = ((t[:, None] >= t[None, :]) & same).astype(np.float32)
    psel = np.zeros((levels, r, r), np.float32)
    masks = np.zeros((levels + 1, r, r), np.float32)
    for l in range(levels):
        n = c >> (l + 1)
        blk = t // (2 * n)
        psel[l, t, blk * 2 * n + n - 1] = 1.0
        upper = (t % (2 * n)) >= n
        masks[l] = (blk[:, None] == blk[None, :]) & upper[:, None] & (~upper)[None, :]
    masks[levels] = np.eye(r)
    psel = psel.reshape(levels * r, r)
    return levels, jnp.asarray(tri), jnp.asarray(psel), jnp.asarray(masks), jnp.asarray(tri.T.copy()), jnp.asarray(psel.T.copy())


def _hg_elem(qv, fv, lbv):
    sig = jax.nn.sigmoid(fv)
    lf = jnp.log(lbv + (1.0 - lbv) * sig)
    kk = (1.0 - lbv) * jax.nn.sigmoid(-fv)
    qf = qv * jax.nn.sigmoid(qv)
    return qf, kk, lf


def _col_bcast(rowvec):
    n = rowvec.shape[1]
    return jnp.transpose(jnp.broadcast_to(rowvec, (n, n)))


def _hg_within(qf, kk, lf, tri, psel, m_ref, c, levels):
    r = qf.shape[0]
    b = _dot_exact_lhs01(tri, lf)
    bls = [b[(g + 1) * c - 1:(g + 1) * c, :] for g in range(r // c)]
    blb = jnp.concatenate([jnp.broadcast_to(bl, (c, bl.shape[1])) for bl in bls], axis=0)
    eb = jnp.exp(b)
    qi = qf * eb
    bsel = _dot_exact_lhs01(psel, b)
    scores = jnp.where(m_ref[levels] > 0, _dot(qf, kk, NT), 0.0)
    lev = []
    for l in range(levels):
        bs = bsel[l * r:(l + 1) * r]
        eq = jnp.exp(jnp.minimum(b - bs, 0.0))
        ek = jnp.exp(jnp.minimum(bs - b, 0.0))
        ql, kl = qf * eq, kk * ek
        scores = scores + jnp.where(m_ref[l] > 0, _dot(ql, kl, NT), 0.0)
        lev.append((eq, ek, ql, kl))
    ebl = jnp.exp(blb - b)
    kd = kk * ebl
    decays = [_col_bcast(jnp.exp(bl)) for bl in bls]
    return eb, qi, scores, lev, ebl, kd, decays


def _hg_fwd(proj, cols, lb, b, s, name):
    nh, d = lb.shape
    bh = b * nh
    c = HG_CHUNK
    nc = s // c
    grp = math.gcd(HG_GROUP, nc)
    r = grp * c
    levels, tri, psel, masks, _, _ = _hg_consts(c, r)

    def body(q_ref, f_ref, i_ref, lb_ref, tri_ref, psel_ref, m_ref, o_ref, st_ref):
        lbv = jnp.broadcast_to(lb_ref[0], (r, d))
        tri_v, psel_v = _bf(tri_ref[...]), _bf(psel_ref[...])

        def group(gi, state):
            r0 = pl.multiple_of(gi * r, r)
            qf, kk, lf = _hg_elem(q_ref[pl.ds(r0, r), :], f_ref[pl.ds(r0, r), :], lbv)
            iv = i_ref[pl.ds(r0, r), :]
            _, qi, scores, _, _, kd, decays = _hg_within(qf, kk, lf, tri_v, psel_v, m_ref, c, levels)
            within = _dot(scores, iv)
            for g in range(grp):
                rows = slice(g * c, (g + 1) * c)
                st_ref[0, gi * grp + g] = state
                o_ref[pl.ds(r0 + g * c, c), :] = _dot(qi[rows], state) + within[rows]
                state = decays[g] * state + _dot(kd[rows], iv[rows], TN)
            return state

        lax.fori_loop(0, nc // grp, group, jnp.zeros((d, d), F32))

    full = lambda a: pl.BlockSpec(a.shape, lambda e, hd: (0,) * a.ndim)
    return pl.pallas_call(
        body, name=name, grid=(b, nh),
        in_specs=[_hg_seq(s, d, cols[0]), _hg_seq(s, d, cols[1]), _hg_seq(s, d, cols[2]),
                  pl.BlockSpec((1, 1, d), lambda e, hd: (hd, 0, 0)), full(tri), full(psel), full(masks)],
        out_specs=[_hg_seq(s, d, 0), pl.BlockSpec((1, nc, d, d), lambda e, hd: (e * nh + hd, 0, 0, 0))],
        out_shape=[jax.ShapeDtypeStruct((b * s, nh * d), F32), jax.ShapeDtypeStruct((bh, nc, d, d), F32)],
        compiler_params=_params(("parallel", "parallel")),
    )(proj, proj, proj, lb.reshape(nh, 1, d), tri, psel, masks)


def _hg_seq(s, d, col):
    return pl.BlockSpec((s, d), lambda e, hd: (e, col + hd))


def _hg_bwd(proj, cols, lb, states, do, b, s, name):
    nh, d = lb.shape
    bh = b * nh
    c = HG_CHUNK
    nc = s // c
    grp = math.gcd(HG_GROUP, nc)
    r = grp * c
    levels, tri, psel, masks, tri_t, psel_t = _hg_consts(c, r)

    def body(q_ref, f_ref, i_ref, lb_ref, st_ref, do_ref, tri_ref, psel_ref, m_ref, trit_ref, pselt_ref,
             dq_ref, df_ref, di_ref, dlb_ref):
        lbv = jnp.broadcast_to(lb_ref[0], (r, d))
        tri_v, psel_v = _bf(tri_ref[...]), _bf(psel_ref[...])
        trit_v, pselt_v = _bf(trit_ref[...]), _bf(pselt_ref[...])
        row_in_chunk = lax.broadcasted_iota(jnp.int32, (c, d), 0)

        def chunk(step, carry):
            ds_out, dlb = carry
            gi = nc // grp - 1 - step
            r0 = pl.multiple_of(gi * r, r)
            qv, fv, iv = q_ref[pl.ds(r0, r), :], f_ref[pl.ds(r0, r), :], i_ref[pl.ds(r0, r), :]
            dov = do_ref[pl.ds(r0, r), :]
            (qf, kk, lf), elem_vjp = jax.vjp(_hg_elem, qv, fv, lbv)
            eb, qi, scores, lev, ebl, kd, decays = _hg_within(qf, kk, lf, tri_v, psel_v, m_ref, c, levels)

            dscores = _dot(dov, iv, NT)
            di_within = _dot(scores, dov, TN)
            dqi_parts, dkd_parts, dbl_parts = [None] * grp, [None] * grp, [None] * grp
            for g in reversed(range(grp)):
                rows = slice(g * c, (g + 1) * c)
                state = st_ref[0, gi * grp + g]
                di_ref[pl.ds(r0 + g * c, c), :] = di_within[rows] + _dot(kd[rows], ds_out)
                dqi_parts[g] = _dot(dov[rows], state, NT)
                dkd_parts[g] = _dot(iv[rows], ds_out, NT)
                dbl = (jnp.sum(dkd_parts[g] * kd[rows], axis=0, keepdims=True)
                       + _col_bcast_t(jnp.sum(ds_out * decays[g] * state, axis=1, keepdims=True)))
                dbl_parts[g] = jnp.where(row_in_chunk == c - 1, dbl, 0.0)
                ds_out = decays[g] * ds_out + _dot(qi[rows], dov[rows], TN)
            ds_in = ds_out
            dqi = jnp.concatenate(dqi_parts, axis=0)
            dkd = jnp.concatenate(dkd_parts, axis=0)
            dqf = dqi * eb
            dkk = dkd * ebl
            db = dqi * qi - dkd * kd + jnp.concatenate(dbl_parts, axis=0)
            dsd = jnp.where(m_ref[levels] > 0, dscores, 0.0)
            dqf = dqf + _dot(dsd, kk)
            dkk = dkk + _dot(dsd, qf, TN)
            dbsel = []
            for l in range(levels):
                eq, ek, ql, kl = lev[l]
                dsl = jnp.where(m_ref[l] > 0, dscores, 0.0)
                dql = _dot(dsl, kl)
                dkl = _dot(dsl, ql, TN)
                dqf = dqf + dql * eq
                dkk = dkk + dkl * ek
                diff = dql * ql - dkl * kl
                db = db + diff
                dbsel.append(-diff)
            db = db + _dot_exact_lhs01(pselt_v, jnp.concatenate(dbsel, axis=0))
            dlf = _dot_exact_lhs01(trit_v, db)
            dq, df, dlb_c = elem_vjp((dqf, dkk, dlf))
            dq_ref[pl.ds(r0, r), :] = dq
            df_ref[pl.ds(r0, r), :] = df
            return ds_in, dlb + jnp.sum(dlb_c, axis=0, keepdims=True)

        _, dlb = lax.fori_loop(0, nc // grp, chunk, (jnp.zeros((d, d), F32), jnp.zeros((1, d), F32)))
        dlb_ref[0] = dlb

    seq = _hg_seq(s, d, 0)
    full = lambda a: pl.BlockSpec(a.shape, lambda e, hd: (0,) * a.ndim)
    return pl.pallas_call(
        body, name=name, grid=(b, nh),
        in_specs=[_hg_seq(s, d, cols[0]), _hg_seq(s, d, cols[1]), _hg_seq(s, d, cols[2]),
                  pl.BlockSpec((1, 1, d), lambda e, hd: (hd, 0, 0)),
                  pl.BlockSpec((1, nc, d, d), lambda e, hd: (e * nh + hd, 0, 0, 0)), seq,
                  full(tri), full(psel), full(masks), full(tri_t), full(psel_t)],
        out_specs=[seq, seq, seq, pl.BlockSpec((1, 1, d), lambda e, hd: (e * nh + hd, 0, 0))],
        out_shape=[jax.ShapeDtypeStruct((b * s, nh * d), F32)] * 3 + [jax.ShapeDtypeStruct((bh, 1, d), F32)],
        compiler_params=_params(("parallel", "parallel")),
    )(proj, proj, proj, lb.reshape(nh, 1, d), states, do, tri, psel, masks, tri_t, psel_t)


def _col_bcast_t(colvec):
    n = colvec.shape[0]
    return jnp.transpose(jnp.broadcast_to(colvec, (n, n)))[0:1, :]


def _swa_probs(qg, kb, bias, sink, valid, scale):
    logits = _dot(qg, kb, NT) * scale + bias
    logits = jnp.where(valid, logits, -jnp.inf)
    m = jnp.maximum(jnp.max(logits, axis=-1, keepdims=True), sink)
    e = jnp.exp(logits - m)
    es = jnp.exp(sink - m)
    den = jnp.sum(e, axis=-1, keepdims=True) + es
    return e / den, es / den


def _swa_valid(n):
    w = WINDOW
    row = lax.broadcasted_iota(jnp.int32, (w, 2 * w), 0)
    col = lax.broadcasted_iota(jnp.int32, (w, 2 * w), 1)
    dist = row + w - col
    return (dist >= 0) & (dist < w) & ((col >= w) | (n > 0))


def _swa_specs(b, g, s, d):
    w = WINDOW
    q_spec = pl.BlockSpec((1, 1, g, w, d), lambda h, bi, n: (bi, h, 0, n, 0))
    kp_spec = pl.BlockSpec((1, 1, w, d), lambda h, bi, n: (bi, h, jnp.maximum(n - 1, 0), 0))
    kc_spec = pl.BlockSpec((1, 1, w, d), lambda h, bi, n: (bi, h, n, 0))
    bias_spec = pl.BlockSpec((1, g, w, 2 * w), lambda h, bi, n: (h, 0, 0, 0))
    sink_spec = pl.BlockSpec(memory_space=pltpu.SMEM)
    return q_spec, kp_spec, kc_spec, bias_spec, sink_spec


def _swa_fwd(q, k, v, sinks, bias, name):
    b, kvh, g, s, d = q.shape
    w = WINDOW
    scale = d ** -0.5
    q_spec, kp_spec, kc_spec, bias_spec, sink_spec = _swa_specs(b, g, s, d)

    def body(q_ref, kp_ref, kc_ref, vp_ref, vc_ref, bias_ref, sink_ref, o_ref):
        h, n = pl.program_id(0), pl.program_id(2)
        valid = _swa_valid(n)
        kb = jnp.concatenate([kp_ref[0, 0], kc_ref[0, 0]], axis=0)
        vb = jnp.concatenate([vp_ref[0, 0], vc_ref[0, 0]], axis=0)
        for gi in range(g):
            p, _ = _swa_probs(q_ref[0, 0, gi], kb, bias_ref[0, gi], sink_ref[h * g + gi], valid, scale)
            o_ref[0, 0, gi] = _dot(p, vb)

    return pl.pallas_call(
        body, name=name, grid=(kvh, b, s // w),
        in_specs=[q_spec, kp_spec, kc_spec, kp_spec, kc_spec, bias_spec, sink_spec], out_specs=q_spec,
        out_shape=jax.ShapeDtypeStruct(q.shape, F32), compiler_params=_params(("parallel", "parallel", "arbitrary")),
    )(q, k, k, v, v, bias, sinks)


def _swa_bwd(q, k, v, sinks, bias, do, name):
    b, kvh, g, s, d = q.shape
    w = WINDOW
    scale = d ** -0.5
    q_spec, kp_spec, kc_spec, bias_spec, sink_spec = _swa_specs(b, g, s, d)
    kv_acc = pl.BlockSpec((1, 1, s, d), lambda h, bi, n: (bi, h, 0, 0))
    dsink_spec = pl.BlockSpec((1, g, LANES), lambda h, bi, n: (h, 0, 0))

    def body(q_ref, kp_ref, kc_ref, vp_ref, vc_ref, bias_ref, sink_ref, do_ref, dq_ref, dk_ref, dv_ref, dbias_ref, dsink_ref):
        h, bi, n = pl.program_id(0), pl.program_id(1), pl.program_id(2)
        valid = _swa_valid(n)
        kb = jnp.concatenate([kp_ref[0, 0], kc_ref[0, 0]], axis=0)
        vb = jnp.concatenate([vp_ref[0, 0], vc_ref[0, 0]], axis=0)

        @pl.when(n == 0)
        def _():
            dk_ref[...] = jnp.zeros_like(dk_ref)
            dv_ref[...] = jnp.zeros_like(dv_ref)

        @pl.when((n == 0) & (bi == 0))
        def _():
            dbias_ref[...] = jnp.zeros_like(dbias_ref)
            dsink_ref[...] = jnp.zeros_like(dsink_ref)

        dkb = jnp.zeros((2 * w, d), F32)
        dvb = jnp.zeros((2 * w, d), F32)
        for gi in range(g):
            qg, dog = q_ref[0, 0, gi], do_ref[0, 0, gi]
            p, ps = _swa_probs(qg, kb, bias_ref[0, gi], sink_ref[h * g + gi], valid, scale)
            dp = _dot(dog, vb, NT)
            delta = jnp.sum(p * dp, axis=-1, keepdims=True)
            dl = p * (dp - delta)
            dq_ref[0, 0, gi] = _dot(dl, kb) * scale
            dkb = dkb + _dot(dl, qg, TN) * scale
            dvb = dvb + _dot(p, dog, TN)
            dbias_ref[0, gi] += dl
            dsink_ref[0, gi:gi + 1, :] += jnp.broadcast_to(jnp.sum(-ps * delta, axis=0, keepdims=True), (1, LANES))

        c0 = pl.multiple_of(n * w, w)
        dk_ref[0, 0, pl.ds(c0, w), :] += dkb[w:]
        dv_ref[0, 0, pl.ds(c0, w), :] += dvb[w:]

        @pl.when(n > 0)
        def _():
            p0 = pl.multiple_of((n - 1) * w, w)
            dk_ref[0, 0, pl.ds(p0, w), :] += dkb[:w]
            dv_ref[0, 0, pl.ds(p0, w), :] += dvb[:w]

    return pl.pallas_call(
        body, name=name, grid=(kvh, b, s // w),
        in_specs=[q_spec, kp_spec, kc_spec, kp_spec, kc_spec, bias_spec, sink_spec, q_spec],
        out_specs=[q_spec, kv_acc, kv_acc, bias_spec, dsink_spec],
        out_shape=[jax.ShapeDtypeStruct(q.shape, F32), jax.ShapeDtypeStruct(k.shape, F32), jax.ShapeDtypeStruct(k.shape, F32),
                   jax.ShapeDtypeStruct(bias.shape, F32), jax.ShapeDtypeStruct((kvh, g, LANES), F32)],
        compiler_params=_params(("arbitrary", "arbitrary", "arbitrary")),
    )(q, k, k, v, v, bias, sinks, do)


def _t5_bias(rel_bias):
    t = np.arange(WINDOW)[:, None]
    s = np.arange(2 * WINDOW)[None, :]
    dist = t + WINDOW - s
    max_exact = N_BUCKETS // 2
    large = max_exact + (np.log(np.maximum(dist, max_exact) / max_exact) / math.log(MAX_DISTANCE / max_exact)
                         * (N_BUCKETS - max_exact)).astype(np.int32)
    large = np.minimum(large, N_BUCKETS - 1)
    bucket = np.where(dist < max_exact, np.maximum(dist, 0), large).astype(np.int32)
    onehot = jnp.asarray(np.eye(N_BUCKETS, dtype=np.float32)[bucket])
    bias = jnp.einsum("tsb,bh->hts", onehot, rel_bias.astype(F32), precision=lax.Precision.HIGHEST)
    return bias.reshape(SW_KV_HEADS, SW_GROUP, WINDOW, 2 * WINDOW)


CONV_W = 3


def _shift_down(x, k):
    row = lax.broadcasted_iota(jnp.int32, x.shape, 0)
    return jnp.where(row >= k, pltpu.roll(x, k, axis=0), 0.0)


def _shift_up(x, k):
    n = x.shape[0]
    row = lax.broadcasted_iota(jnp.int32, x.shape, 0)
    return jnp.where(row < n - k, pltpu.roll(x, n - k, axis=0), 0.0)


def _conv3(u, w, bvec):
    return w[0:1] * _shift_down(u, 2) + w[1:2] * _shift_down(u, 1) + w[2:3] * u + bvec


def _convglu_fwd(u, w, bvec, name, out_dtype=F32):
    b, s, f2 = u.shape
    f = f2 // 2
    tc = _pick(f, 256)
    nt = f // tc

    def body(ug_ref, uu_ref, wg_ref, wu_ref, bg_ref, bu_ref, o_ref):
        cg = _conv3(ug_ref[0], wg_ref[...], bg_ref[...])
        cu = _conv3(uu_ref[0], wu_ref[...], bu_ref[...])
        o_ref[0] = (_silu(cg) * cu).astype(out_dtype)

    ug = pl.BlockSpec((1, s, tc), lambda j, bi: (bi, 0, j))
    uu = pl.BlockSpec((1, s, tc), lambda j, bi: (bi, 0, j + nt))
    wg = pl.BlockSpec((CONV_W, tc), lambda j, bi: (0, j))
    wu = pl.BlockSpec((CONV_W, tc), lambda j, bi: (0, j + nt))
    bg = pl.BlockSpec((1, tc), lambda j, bi: (0, j))
    bu = pl.BlockSpec((1, tc), lambda j, bi: (0, j + nt))
    bv = bvec.reshape(1, f2)
    return pl.pallas_call(body, name=name, grid=(nt, b), in_specs=[ug, uu, wg, wu, bg, bu], out_specs=ug,
                          out_shape=jax.ShapeDtypeStruct((b, s, f), out_dtype),
                          compiler_params=_params(("parallel", "parallel")))(u, u, w, w, bv, bv)


def _convglu_bwd(u, w, bvec, dact, name):
    b, s, f2 = u.shape
    f = f2 // 2
    tc = LANES
    nt = f // tc

    def taps(dc, uv):
        rows = [jnp.sum(dc * _shift_down(uv, 2), axis=0, keepdims=True), jnp.sum(dc * _shift_down(uv, 1), axis=0, keepdims=True),
                jnp.sum(dc * uv, axis=0, keepdims=True), jnp.sum(dc, axis=0, keepdims=True)]
        return jnp.concatenate(rows + [jnp.zeros((4, tc), F32)], axis=0)

    def back(dc, wv):
        return wv[2:3] * dc + wv[1:2] * _shift_up(dc, 1) + wv[0:1] * _shift_up(dc, 2)

    def body(ug_ref, uu_ref, wg_ref, wu_ref, bg_ref, bu_ref, da_ref, dug_ref, duu_ref, dwg_ref, dwu_ref):
        ugv, uuv, da = ug_ref[0], uu_ref[0], da_ref[0]
        cg = _conv3(ugv, wg_ref[...], bg_ref[...])
        cu = _conv3(uuv, wu_ref[...], bu_ref[...])
        sg = jax.nn.sigmoid(cg)
        dcu = da * (cg * sg)
        dcg = da * cu * (sg * (1.0 + cg * (1.0 - sg)))
        dug_ref[0] = back(dcg, wg_ref[...])
        duu_ref[0] = back(dcu, wu_ref[...])

        @pl.when(pl.program_id(1) == 0)
        def _():
            dwg_ref[...] = jnp.zeros_like(dwg_ref)
            dwu_ref[...] = jnp.zeros_like(dwu_ref)

        dwg_ref[...] += taps(dcg, ugv)
        dwu_ref[...] += taps(dcu, uuv)

    ug = pl.BlockSpec((1, s, tc), lambda j, bi: (bi, 0, j))
    uu = pl.BlockSpec((1, s, tc), lambda j, bi: (bi, 0, j + nt))
    wg = pl.BlockSpec((CONV_W, tc), lambda j, bi: (0, j))
    wu = pl.BlockSpec((CONV_W, tc), lambda j, bi: (0, j + nt))
    bg = pl.BlockSpec((1, tc), lambda j, bi: (0, j))
    bu = pl.BlockSpec((1, tc), lambda j, bi: (0, j + nt))
    acc = pl.BlockSpec((8, tc), lambda j, bi: (0, j))
    bv = bvec.reshape(1, f2)
    return pl.pallas_call(
        body, name=name, grid=(nt, b), in_specs=[ug, uu, wg, wu, bg, bu, ug], out_specs=[ug, ug, acc, acc],
        out_shape=[jax.ShapeDtypeStruct((b, s, f), F32)] * 2 + [jax.ShapeDtypeStruct((8, f), F32)] * 2,
        compiler_params=_params(("parallel", "arbitrary")),
    )(u, u, w, w, bv, bv, dact)


def _adamw(w, g, m, v, name):
    r, c = w.shape
    tr = _row_tile(r, c)
    c1 = 1.0 - ADAM_B1 ** ADAM_STEP
    c2 = 1.0 - ADAM_B2 ** ADAM_STEP

    def body(w_ref, g_ref, m_ref, v_ref, d_ref, mo_ref, vo_ref):
        gv = g_ref[...]
        mn = ADAM_B1 * m_ref[...] + (1.0 - ADAM_B1) * gv
        vn = ADAM_B2 * v_ref[...] + (1.0 - ADAM_B2) * (gv * gv)
        d_ref[...] = -ADAM_LR * ((mn / c1) / (jnp.sqrt(vn / c2) + ADAM_EPS) + ADAM_WD * w_ref[...])
        mo_ref[...] = mn
        vo_ref[...] = vn

    blk = pl.BlockSpec((tr, c), lambda i: (i, 0))
    return pl.pallas_call(body, name=name, grid=(r // tr,), in_specs=[blk] * 4, out_specs=[blk] * 3,
                          out_shape=[jax.ShapeDtypeStruct((r, c), F32)] * 3, compiler_params=_params(("parallel",)))(w, g, m, v)


MESH = pl.DeviceIdType.MESH
ANY = pl.BlockSpec(memory_space=pl.ANY)


def _position():
    return lax.axis_index("x"), lax.axis_index("y"), lax.axis_index("c")


def _all_gather(x, name):
    r, c = x.shape

    def body(x_ref, out_ref, send_sems, recv_sems, local_sem):
        mx, my, mc = _position()
        me, sibling = (mx, my, mc), (mx, my, 1 - mc)
        chips = [(1 - mx, my), (mx, 1 - my), (1 - mx, 1 - my)]

        def slot(px, py, pc):
            return out_ref.at[4 * px + 2 * py + pc]

        def copy(k, block, to, src=None):
            return pltpu.make_async_remote_copy(
                src_ref=slot(*block) if src is None else src, dst_ref=slot(*block),
                send_sem=send_sems.at[k], recv_sem=recv_sems.at[k], device_id=to, device_id_type=MESH)

        mine = pltpu.make_async_copy(x_ref, slot(*me), local_sem.at[0])
        mine.start()
        first = [copy(0, me, sibling, src=x_ref)]
        first += [copy(1 + j, me, (*chip, mc), src=x_ref) for j, chip in enumerate(chips)]
        for cp in first:
            cp.start()
        passed = [copy(4 + j, (*chip, mc), sibling) for j, chip in enumerate(chips)]
        for j, chip in enumerate(chips):
            copy(1 + j, (*chip, mc), me).wait_recv()
            passed[j].start()
        copy(0, sibling, me).wait_recv()
        for j, chip in enumerate(chips):
            copy(4 + j, (*chip, 1 - mc), me).wait_recv()
        for cp in first + passed:
            cp.wait_send()
        mine.wait()

    return pl.pallas_call(
        body, name=name, out_shape=jax.ShapeDtypeStruct((N_DEV, r, c), x.dtype), in_specs=[ANY], out_specs=ANY,
        scratch_shapes=[pltpu.SemaphoreType.DMA((7,)), pltpu.SemaphoreType.DMA((7,)), pltpu.SemaphoreType.DMA((1,))],
    )(x)


def _dev_rows(ref, dev, a):
    return ref.at[:, pl.ds(pl.multiple_of(dev * a, 16), a), :]


def _all_gather_rows(shards, name):
    nt = len(shards)

    def body(*refs):
        x_refs, out_refs = refs[:nt], refs[nt:2 * nt]
        send_sems, recv_sems, local_sems = refs[2 * nt:]
        mx, my, mc = _position()
        me, sibling = (mx, my, mc), (mx, my, 1 - mc)
        chips = [(1 - mx, my), (mx, 1 - my), (1 - mx, 1 - my)]

        def slot(t, px, py, pc):
            return _dev_rows(out_refs[t], 4 * px + 2 * py + pc, shards[t].shape[1])

        def copy(t, k, block, to, src=None):
            return pltpu.make_async_remote_copy(
                src_ref=slot(t, *block) if src is None else src, dst_ref=slot(t, *block),
                send_sem=send_sems.at[7 * t + k], recv_sem=recv_sems.at[7 * t + k], device_id=to, device_id_type=MESH)

        mine = [pltpu.make_async_copy(x_refs[t], slot(t, *me), local_sems.at[t]) for t in range(nt)]
        first = [copy(t, 0, me, sibling, src=x_refs[t]) for t in range(nt)]
        first += [copy(t, 1 + j, me, (*chip, mc), src=x_refs[t]) for j, chip in enumerate(chips) for t in range(nt)]
        for cp in mine + first:
            cp.start()
        passed = []
        for j, chip in enumerate(chips):
            for t in range(nt):
                copy(t, 1 + j, (*chip, mc), me).wait_recv()
                fwd = copy(t, 4 + j, (*chip, mc), sibling)
                fwd.start()
                passed.append(fwd)
        for t in range(nt):
            copy(t, 0, sibling, me).wait_recv()
        for j, chip in enumerate(chips):
            for t in range(nt):
                copy(t, 4 + j, (*chip, 1 - mc), me).wait_recv()
        for cp in first + passed:
            cp.wait_send()
        for cp in mine:
            cp.wait()

    out_shape = [jax.ShapeDtypeStruct((x.shape[0], N_DEV * x.shape[1], x.shape[2]), x.dtype) for x in shards]
    return pl.pallas_call(
        body, name=name, out_shape=out_shape, in_specs=[ANY] * nt, out_specs=[ANY] * nt,
        scratch_shapes=[pltpu.SemaphoreType.DMA((7 * nt,)), pltpu.SemaphoreType.DMA((7 * nt,)), pltpu.SemaphoreType.DMA((nt,))],
    )(*shards)


def _rs_pair(gs, name):
    nt = len(gs)

    def body(*refs):
        g_refs, a_refs = refs[:nt], refs[nt:2 * nt]
        send_sems, recv_sems = refs[2 * nt:]
        mx, my, mc = _position()
        copies = [pltpu.make_async_remote_copy(
            src_ref=_dev_rows(g_refs[t], 2 * j + 1 - mc, gs[t].shape[1] // N_DEV), dst_ref=a_refs[t].at[j],
            send_sem=send_sems.at[4 * t + j], recv_sem=recv_sems.at[4 * t + j],
            device_id=(mx, my, 1 - mc), device_id_type=MESH) for t in range(nt) for j in range(4)]
        for cp in copies:
            cp.start()
        for cp in copies:
            cp.wait()

    out_shape = [jax.ShapeDtypeStruct((4, g.shape[0], g.shape[1] // N_DEV, g.shape[2]), g.dtype) for g in gs]
    return pl.pallas_call(
        body, name=name, out_shape=out_shape, in_specs=[ANY] * nt, out_specs=[ANY] * nt,
        scratch_shapes=[pltpu.SemaphoreType.DMA((4 * nt,)), pltpu.SemaphoreType.DMA((4 * nt,))],
    )(*gs)


def _rs_chips(ps, name):
    nt = len(ps)

    def body(*refs):
        p_refs, b_refs = refs[:nt], refs[nt:2 * nt]
        send_sems, recv_sems = refs[2 * nt:]
        mx, my, mc = _position()
        chips = [(1 - mx, my), (mx, 1 - my), (1 - mx, 1 - my)]
        copies = [pltpu.make_async_remote_copy(
            src_ref=p_refs[t].at[2 * cx + cy], dst_ref=b_refs[t].at[k],
            send_sem=send_sems.at[3 * t + k], recv_sem=recv_sems.at[3 * t + k],
            device_id=(cx, cy, mc), device_id_type=MESH) for t in range(nt) for k, (cx, cy) in enumerate(chips)]
        for cp in copies:
            cp.start()
        for cp in copies:
            cp.wait()

    out_shape = [jax.ShapeDtypeStruct((3,) + p.shape[1:], p.dtype) for p in ps]
    return pl.pallas_call(
        body, name=name, out_shape=out_shape, in_specs=[ANY] * nt, out_specs=[ANY] * nt,
        scratch_shapes=[pltpu.SemaphoreType.DMA((3 * nt,)), pltpu.SemaphoreType.DMA((3 * nt,))],
    )(*ps)


def _div_tile(a, b):
    best = 16
    for t in range(16, a + 1, 16):
        if a % t == 0 and t * b * 4 <= 2 * 1024 * 1024:
            best = t
    return best


def _rs_add_pair(g, a, core, name):
    l, a8, b = g.shape
    rows = a8 // N_DEV
    ta = _div_tile(rows, b)

    def body(core_ref, g_ref, a_ref, p_ref):
        p_ref[...] = (g_ref[...] + a_ref[...]).astype(BF16)

    grid_spec = pltpu.PrefetchScalarGridSpec(
        num_scalar_prefetch=1, grid=(4, l, rows // ta),
        in_specs=[pl.BlockSpec((1, 1, ta, b), lambda j, li, i, core_ref: (li, 2 * j + core_ref[0], i, 0)),
                  pl.BlockSpec((1, 1, ta, b), lambda j, li, i, core_ref: (j, li, i, 0))],
        out_specs=pl.BlockSpec((1, 1, ta, b), lambda j, li, i, core_ref: (j, li, i, 0)))
    return pl.pallas_call(body, name=name, grid_spec=grid_spec, out_shape=jax.ShapeDtypeStruct((4, l, rows, b), BF16),
                          compiler_params=_params(("parallel", "parallel", "parallel")))(core, g.reshape(l, N_DEV, rows, b), a)


def _rs_final(g, a, bsum, where, name):
    l, a8, b = g.shape
    rows = a8 // N_DEV
    ta = _div_tile(rows, b)

    def body(where_ref, g_ref, a_ref, b_ref, o_ref):
        own = g_ref[0, 0] + a_ref[0, 0]
        o_ref[0] = ((own + b_ref[0, 0].astype(F32)) + b_ref[1, 0].astype(F32)) + b_ref[2, 0].astype(F32)

    grid_spec = pltpu.PrefetchScalarGridSpec(
        num_scalar_prefetch=1, grid=(l, rows // ta),
        in_specs=[pl.BlockSpec((1, 1, ta, b), lambda li, i, w_ref: (li, w_ref[0], i, 0)),
                  pl.BlockSpec((1, 1, ta, b), lambda li, i, w_ref: (w_ref[1], li, i, 0)),
                  pl.BlockSpec((3, 1, ta, b), lambda li, i, w_ref: (0, li, i, 0))],
        out_specs=pl.BlockSpec((1, ta, b), lambda li, i, w_ref: (li, i, 0)))
    return pl.pallas_call(body, name=name, grid_spec=grid_spec, out_shape=jax.ShapeDtypeStruct((l, rows, b), F32),
                          compiler_params=_params(("parallel", "parallel")))(where, g.reshape(l, N_DEV, rows, b), a, bsum)


def _sum_devices(x, name):
    _, r, c = x.shape

    def body(x_ref, o_ref):
        acc = x_ref[0]
        for d in range(1, N_DEV):
            acc = acc + x_ref[d]
        o_ref[...] = acc

    return pl.pallas_call(body, name=name, out_shape=jax.ShapeDtypeStruct((r, c), F32))(x)


BIG = (("ab_w_in", "col"), ("ab_w_out", "row"), ("c_w_in", "col"), ("c_w_out", "row"),
       ("ffn_up", "col"), ("ffn_down", "row"), ("ple_gate", "row"), ("ple_proj", "col"))
KIND = dict(BIG)


def _row_block(shard, kind):
    return shard.transpose(0, 2, 1) if kind == "col" else shard


def _pad_rows(flat):
    pad = -flat.shape[0] % (8 * LANES)
    return jnp.pad(flat, (0, pad)).reshape(-1, LANES)


def _heads_out(x, b, s, nh, d):
    return x.reshape(b, s, nh, d).transpose(0, 2, 1, 3).reshape(b * nh, s, d)


def _heads_in(x, b, s, nh, d):
    return x.reshape(b, nh, s, d).transpose(0, 2, 1, 3).reshape(b * s, nh * d)


def _lower_bounds(logits):
    c = jnp.cumsum(jax.nn.softmax(logits.astype(F32), axis=0), axis=0)
    return c - c[0]


SB_COLS = tuple(k * SB_WIDTH // LANES for k in range(3))
HG_COLS = tuple((3 * SB_WIDTH + k * HG_W) // LANES for k in range(3))
HG_GATE_COL = (3 * SB_WIDTH + 3 * HG_W) // LANES
HG_OUT_COL = SB_WIDTH // LANES


def _forward_backward(x, p, target, W, P):
    b, s, dm = x.shape
    n = b * s
    h = x.reshape(n, dm)
    lbs, lb_vjp = jax.vjp(_lower_bounds, P["hg_lb_logits"])
    bias, bias_vjp = jax.vjp(_t5_bias, P["rel_bias"])
    saved = []
    gw = {name: lax.empty(W[name].shape, F32) for name, _ in BIG}

    def times_w(a, name, l, tag, res=None):
        return _mm(a, W[name], "nt" if KIND[name] == "col" else "nn", tag, res=res, layer=l)

    def times_wt(dy, name, l, tag):
        return _mm(dy, W[name], "nn" if KIND[name] == "col" else "nt", tag, layer=l)

    def grad_w(a, dy, name, l, tag):
        lhs, rhs = (dy, a) if KIND[name] == "col" else (a, dy)
        gw[name] = _mm(lhs, rhs, "tn", tag, into=(gw[name], l, 0))

    hn_next = None
    for i in range(DEPTH):
        j = i // 2
        r = {"h0": h}
        hn = _rms_fwd(h, P["mix_norm"][i], f"mix_norm_f{i}", BF16) if hn_next is None else hn_next
        r["hn"] = hn
        if i % 2 == 0:
            proj = times_w(hn, "ab_w_in", j, f"ab_in_f{i}")
            oa, lta = _sb_fwd(proj, SB_COLS, b, s, f"sb_f{i}")
            ob, st = _hg_fwd(proj, HG_COLS, lbs[j].reshape(HG_HEADS, HG_DK), b, s, f"hg_f{i}")
            obg = _gnorm_fwd(ob, proj, HG_GATE_COL, P["hg_out_norm"][j], f"hg_norm_f{i}")
            cat = jnp.concatenate([oa, obg], axis=1).astype(BF16)
            h = times_w(cat, "ab_w_out", j, f"ab_out_f{i}", res=h)
            r.update(lta=lta, proj=proj, ob=ob, st=st, cat=cat)
        else:
            proj = times_w(hn, "c_w_in", j, f"c_in_f{i}")
            nq = SW_HEADS * SW_DIM
            nkv = SW_KV_HEADS * SW_DIM
            q = _heads_out(proj[:, :nq], b, s, SW_HEADS, SW_DIM).reshape(-1, SW_DIM)
            k = _heads_out(proj[:, nq:nq + nkv], b, s, SW_KV_HEADS, SW_DIM).reshape(-1, SW_DIM)
            v = _heads_out(proj[:, nq + nkv:], b, s, SW_KV_HEADS, SW_DIM).reshape(b, SW_KV_HEADS, s, SW_DIM)
            qn = _rms_fwd(q, P["q_norm"][j], f"q_norm_f{i}").reshape(b, SW_KV_HEADS, SW_GROUP, s, SW_DIM)
            kn = _rms_fwd(k, P["k_norm"][j], f"k_norm_f{i}").reshape(b, SW_KV_HEADS, s, SW_DIM)
            o = _swa_fwd(qn, kn, v, P["sinks"][j], bias, f"swa_f{i}")
            o2 = _heads_in(o.reshape(b * SW_HEADS, s, SW_DIM), b, s, SW_HEADS, SW_DIM).astype(BF16)
            h = times_w(o2, "c_w_out", j, f"c_out_f{i}", res=h)
            r.update(q=q, k=k, v=v, qn=qn, kn=kn, o2=o2)
        r["h1"] = h
        hn2 = _rms_fwd(h, P["ffn_norm"][i], f"ffn_norm_f{i}", BF16)
        u = times_w(hn2, "ffn_up", i, f"ffn_up_f{i}").reshape(b, s, 2 * D_FF)
        act = _convglu_fwd(u, W["ffn_conv"][i], P["ffn_conv_b"][i], f"conv_f{i}", BF16).reshape(n, D_FF)
        h = times_w(act, "ffn_down", i, f"ffn_down_f{i}", res=h)
        r.update(hn2=hn2, u=u, act=act, h2=h)
        hn3 = _rms_fwd(h, P["ple_norm"][i], f"ple_norm_f{i}", BF16)
        z = times_w(hn3, "ple_gate", i, f"ple_gate_f{i}")
        pi = p[i].reshape(n, PLE_DIM)
        e = times_w(pi, "ple_proj", i, f"ple_proj_f{i}")
        if i + 1 < DEPTH:
            h, hn_next = _sigmul_fwd(z, e, h, f"ple_f{i}", next_gain=P["mix_norm"][i + 1])
        else:
            h = _sigmul_fwd(z, e, h, f"ple_f{i}")
        r.update(hn3=hn3, z=z, e=e, pi=pi)
        saved.append(r)

    loss, dh = _loss_fwd(h, target.reshape(n, dm), "loss")

    gconv = [None] * DEPTH
    gp = {name: [None] * P[name].shape[0] for name in ("mix_norm", "hg_out_norm", "q_norm", "k_norm", "sinks",
                                                        "ffn_norm", "ffn_conv_b", "ple_norm")}
    dlbs = [None] * (DEPTH // 2)
    dbias = jnp.zeros_like(bias)

    for i in reversed(range(DEPTH)):
        j = i // 2
        r = saved[i]
        dz, de = _sigmul_bwd(r["z"], r["e"], dh, f"ple_b{i}")
        grad_w(r["pi"], de, "ple_proj", i, f"ple_proj_g{i}")
        grad_w(r["hn3"], dz, "ple_gate", i, f"ple_gate_g{i}")
        dhn3 = times_wt(dz, "ple_gate", i, f"ple_gate_b{i}")
        dh, gp["ple_norm"][i] = _rms_bwd(r["h2"], P["ple_norm"][i], dhn3, f"ple_norm_b{i}", res=dh)

        dact = times_wt(dh, "ffn_down", i, f"ffn_down_b{i}").reshape(b, s, D_FF)
        grad_w(r["act"], dh, "ffn_down", i, f"ffn_down_g{i}")
        dug, duu, ag, au = _convglu_bwd(r["u"], W["ffn_conv"][i], P["ffn_conv_b"][i], dact, f"conv_b{i}")
        gconv[i] = jnp.concatenate([ag[:CONV_W], au[:CONV_W]], axis=-1)
        gp["ffn_conv_b"][i] = jnp.concatenate([ag[CONV_W], au[CONV_W]], axis=-1)
        dhn2 = None
        for half, dpart in enumerate((dug.reshape(n, D_FF), duu.reshape(n, D_FF))):
            gw["ffn_up"] = _mm(dpart, r["hn2"], "tn", f"ffn_up_g{i}_{half}", into=(gw["ffn_up"], i, half * D_FF))
            dhn2 = _mm(dpart, W["ffn_up"], "nn", f"ffn_up_b{i}_{half}", layer=i, b_rows=(half * D_FF, D_FF), res=dhn2)
        dh, gp["ffn_norm"][i] = _rms_bwd(r["h1"], P["ffn_norm"][i], dhn2, f"ffn_norm_b{i}", res=dh)

        if i % 2 == 0:
            dcat = times_wt(dh, "ab_w_out", j, f"ab_out_b{i}")
            grad_w(r["cat"], dh, "ab_w_out", j, f"ab_out_g{i}")
            dob, dgb, gp["hg_out_norm"][j] = _gnorm_bwd(r["ob"], r["proj"], HG_GATE_COL, P["hg_out_norm"][j], dcat, HG_OUT_COL,
                                                        f"hg_norm_b{i}")
            dqb, dfb, dib, dlb = _hg_bwd(r["proj"], HG_COLS, lbs[j].reshape(HG_HEADS, HG_DK), r["st"], dob, b, s, f"hg_b{i}")
            dlbs[j] = dlb.reshape(b, HG_W).sum(axis=0)
            dqa, dka, dva = _sb_bwd(r["proj"], SB_COLS, r["lta"], dcat, 0, b, s, f"sb_b{i}")
            dproj = jnp.concatenate([dqa, dka, dva, dqb, dfb, dib, dgb], axis=1)
            grad_w(r["hn"], dproj, "ab_w_in", j, f"ab_in_g{i}")
            dhn = times_wt(dproj, "ab_w_in", j, f"ab_in_b{i}")
        else:
            do2 = times_wt(dh, "c_w_out", j, f"c_out_b{i}")
            grad_w(r["o2"], dh, "c_w_out", j, f"c_out_g{i}")
            do = _heads_out(do2, b, s, SW_HEADS, SW_DIM).reshape(b, SW_KV_HEADS, SW_GROUP, s, SW_DIM)
            dqn, dkn, dv, dbias_i, dsink = _swa_bwd(r["qn"], r["kn"], r["v"], P["sinks"][j], bias, do, f"swa_b{i}")
            dbias = dbias + dbias_i
            gp["sinks"][j] = dsink[:, :, 0].reshape(SW_HEADS)
            dq, gp["q_norm"][j] = _rms_bwd(r["q"], P["q_norm"][j], dqn.reshape(-1, SW_DIM), f"q_norm_b{i}")
            dk, gp["k_norm"][j] = _rms_bwd(r["k"], P["k_norm"][j], dkn.reshape(-1, SW_DIM), f"k_norm_b{i}")
            dproj = jnp.concatenate([_heads_in(dq.reshape(b * SW_HEADS, s, SW_DIM), b, s, SW_HEADS, SW_DIM),
                                     _heads_in(dk.reshape(b * SW_KV_HEADS, s, SW_DIM), b, s, SW_KV_HEADS, SW_DIM),
                                     _heads_in(dv.reshape(b * SW_KV_HEADS, s, SW_DIM), b, s, SW_KV_HEADS, SW_DIM)], axis=1)
            grad_w(r["hn"], dproj, "c_w_in", j, f"c_in_g{i}")
            dhn = times_wt(dproj, "c_w_in", j, f"c_in_b{i}")
        dh, gp["mix_norm"][i] = _rms_bwd(r["h0"], P["mix_norm"][i], dhn, f"mix_norm_b{i}", res=dh)

    gp = {name: jnp.stack(v) for name, v in gp.items()}
    gp["hg_lb_logits"] = lb_vjp(jnp.stack(dlbs))[0]
    gp["rel_bias"] = bias_vjp(dbias)[0]
    return loss[0, 0], dh.reshape(b, s, dm), gw, jnp.stack(gconv), gp


WEIGHTS = ("mix_norm", "ab_w_in", "hg_lb_logits", "hg_out_norm", "ab_w_out", "c_w_in", "q_norm", "k_norm", "sinks", "rel_bias",
           "c_w_out", "ffn_norm", "ffn_up", "ffn_conv", "ffn_conv_b", "ffn_down", "ple_norm", "ple_gate", "ple_proj")
SMALL = ("mix_norm", "hg_lb_logits", "hg_out_norm", "q_norm", "k_norm", "sinks", "rel_bias", "ffn_norm", "ffn_conv_b", "ple_norm")


def _step(x, p, target, w, m, v):
    names = [name for name, _ in BIG]
    mx, my, mc = _position()
    dev = 4 * mx + 2 * my + mc

    blocks = [_row_block(w[name], KIND[name]).astype(BF16) for name in names]
    full = dict(zip(names, _all_gather_rows(blocks, "gather_weights")))
    nl, taps, cs = w["ffn_conv"].shape
    conv_all = _all_gather(_pad_rows(w["ffn_conv"].reshape(-1)), "gather_conv").reshape(N_DEV, -1)[:, :nl * taps * cs]
    full["ffn_conv"] = conv_all.reshape(N_DEV, nl, taps, cs).transpose(1, 2, 0, 3).reshape(nl, taps, N_DEV * cs)

    small = {name: w[name] for name in SMALL}
    loss, grad_x, gw, gconv, gp = _forward_backward(x, p, target, full, small)

    core = jnp.reshape(mc, (1,)).astype(jnp.int32)
    where = jnp.stack([dev, 2 * mx + my]).astype(jnp.int32)
    parts = [gw[name] for name in names]
    from_sibling = _rs_pair(parts, "reduce_pair")
    chip_sums = [_rs_add_pair(g, a, core, f"reduce_pair_add_{name}") for name, g, a in zip(names, parts, from_sibling)]
    from_chips = _rs_chips(chip_sums, "reduce_chips")
    grads = {name: _row_block(_rs_final(g, a, bs, where, f"reduce_final_{name}"), KIND[name])
             for name, g, a, bs in zip(names, parts, from_sibling, from_chips)}

    flat_small = jnp.concatenate([gp[name].reshape(-1) for name in SMALL] + [gconv.reshape(-1), loss.reshape(1)])
    small_sum = _sum_devices(_all_gather(_pad_rows(flat_small), "gather_small"), "sum_small").reshape(-1)
    e0 = 0
    for name in SMALL:
        cnt = math.prod(w[name].shape)
        grads[name] = small_sum[e0:e0 + cnt].reshape(w[name].shape)
        e0 += cnt
    gconv_sum = small_sum[e0:e0 + gconv.size].reshape(gconv.shape)
    grads["ffn_conv"] = lax.dynamic_slice_in_dim(gconv_sum, dev * cs, cs, axis=2)
    loss = small_sum[e0 + gconv.size]

    deltas, new_m, new_v = {}, {}, {}
    for name in WEIGHTS:
        shape = w[name].shape
        view = (-1, shape[-1]) if len(shape) > 1 else (1, -1)
        d_, m_, v_ = _adamw(w[name].reshape(view), grads[name].reshape(view), m[name].reshape(view), v[name].reshape(view), f"adamw_{name}")
        deltas[name], new_m[name], new_v[name] = d_.reshape(shape), m_.reshape(shape), v_.reshape(shape)
    return (loss, grad_x, *[grads[k] for k in WEIGHTS], *[deltas[k] for k in WEIGHTS],
            *[new_m[k] for k in WEIGHTS], *[new_v[k] for k in WEIGHTS])


def kernel(x, p, mix_norm, ab_w_in, hg_lb_logits, hg_out_norm, ab_w_out, c_w_in, q_norm, k_norm, sinks, rel_bias, c_w_out, ffn_norm, ffn_up, ffn_conv, ffn_conv_b, ffn_down, ple_norm, ple_gate, ple_proj, loss_target, m_mix_norm, m_ab_w_in, m_hg_lb_logits, m_hg_out_norm, m_ab_w_out, m_c_w_in, m_q_norm, m_k_norm, m_sinks, m_rel_bias, m_c_w_out, m_ffn_norm, m_ffn_up, m_ffn_conv, m_ffn_conv_b, m_ffn_down, m_ple_norm, m_ple_gate, m_ple_proj, v_mix_norm, v_ab_w_in, v_hg_lb_logits, v_hg_out_norm, v_ab_w_out, v_c_w_in, v_q_norm, v_k_norm, v_sinks, v_rel_bias, v_c_w_out, v_ffn_norm, v_ffn_up, v_ffn_conv, v_ffn_conv_b, v_ffn_down, v_ple_norm, v_ple_gate, v_ple_proj):
    w = dict(zip(WEIGHTS, (mix_norm, ab_w_in, hg_lb_logits, hg_out_norm, ab_w_out, c_w_in, q_norm, k_norm, sinks, rel_bias, c_w_out,
                           ffn_norm, ffn_up, ffn_conv, ffn_conv_b, ffn_down, ple_norm, ple_gate, ple_proj)))
    m = dict(zip(WEIGHTS, (m_mix_norm, m_ab_w_in, m_hg_lb_logits, m_hg_out_norm, m_ab_w_out, m_c_w_in, m_q_norm, m_k_norm, m_sinks,
                           m_rel_bias, m_c_w_out, m_ffn_norm, m_ffn_up, m_ffn_conv, m_ffn_conv_b, m_ffn_down, m_ple_norm, m_ple_gate,
                           m_ple_proj)))
    v = dict(zip(WEIGHTS, (v_mix_norm, v_ab_w_in, v_hg_lb_logits, v_hg_out_norm, v_ab_w_out, v_c_w_in, v_q_norm, v_k_norm, v_sinks,
                           v_rel_bias, v_c_w_out, v_ffn_norm, v_ffn_up, v_ffn_conv, v_ffn_conv_b, v_ffn_down, v_ple_norm, v_ple_gate,
                           v_ple_proj)))
    return _step(x, p, loss_target, w, m, v)
```
